```python
import math
import jax, jax.numpy as jnp
from jax import lax
import numpy as np

D_MODEL = 1024
BATCH = 8
SEQ = 16384
DEPTH = 4

D_PLE = 256
SSD_WIDTH = 512
SSD_HEAD_DIM = 64
SSD_HEADS = SSD_WIDTH // SSD_HEAD_DIM
SSD_GROUPS = 2
SSD_STATE = 128
SSD_CONV = 4
SSD_CHUNK = 128
SSD_XBC = SSD_WIDTH + 2 * SSD_GROUPS * SSD_STATE
POOL_WINDOWS = (2, 4, 8, 16)
POOL_WIDTH = D_MODEL - SSD_WIDTH
POOL_GROUP = POOL_WIDTH // len(POOL_WINDOWS)
D_MIX = SSD_WIDTH + POOL_WIDTH
D_IN_PROJ = SSD_WIDTH + SSD_XBC + SSD_HEADS + POOL_WIDTH
D_FF = 2816
FFN_CONV = 3
EPS = 1e-6

kernel_name = 'hymba_ssd_pool_convffn_ple'


def rmsnorm(x, g):
    xf = x.astype(jnp.float32)
    y = xf * lax.rsqrt(jnp.mean(xf * xf, axis=-1, keepdims=True) + EPS)
    return (y * g.astype(jnp.float32)).astype(x.dtype)


def causal_dwconv(x, w, b):
    k_taps = w.shape[0]
    s = x.shape[1]
    xp = jnp.pad(x, ((0, 0), (k_taps - 1, 0), (0, 0)))
    out = b + xp[:, 0:s] * w[0]
    for k in range(1, k_taps):
        out = out + xp[:, k:k + s] * w[k]
    return out


def segsum_exp(a):
    q = a.shape[-1]
    cs = jnp.cumsum(a, axis=-1)
    diff = cs[..., :, None] - cs[..., None, :]
    mask = jnp.tril(jnp.ones((q, q), dtype=bool))
    return jnp.exp(jnp.where(mask, diff, -jnp.inf))


def ssd_scan(x, dt, A, B, C):
    b, l, h, p = x.shape
    g, n = B.shape[-2:]
    e = h // g
    q = SSD_CHUNK
    c = l // q
    X = (x * dt[..., None]).reshape(b, c, q, g, e, p)
    a = (dt * A).reshape(b, c, q, g, e).transpose(0, 3, 4, 1, 2)
    Bc = B.reshape(b, c, q, g, n)
    Cc = C.reshape(b, c, q, g, n)
    a_cs = jnp.cumsum(a, axis=-1)
    CB = jnp.einsum('bclgn,bcsgn->bgcls', Cc, Bc)
    M = CB[:, :, None] * segsum_exp(a)
    y_diag = jnp.einsum('bgecls,bcsgep->bclgep', M, X)
    decay_states = jnp.exp(a_cs[..., -1:] - a_cs).transpose(0, 3, 4, 1, 2)
    states = jnp.einsum('bclgn,bclgep->bcgepn', Bc, X * decay_states[..., None])
    chunk_decay = jnp.exp(a_cs[..., -1])

    def step(s, inp):
        dec, st = inp
        return s * dec[..., None, None] + st, s

    init = jnp.zeros((b, g, e, p, n), jnp.float32)
    _, prev = lax.scan(step, init, (jnp.moveaxis(chunk_decay, -1, 0), jnp.moveaxis(states, 1, 0)))
    y_off = jnp.einsum('bclgn,cbgepn->bclgep', Cc, prev) * jnp.exp(a_cs).transpose(0, 3, 4, 1, 2)[..., None]
    return (y_diag + y_off).reshape(b, l, h, p)


def ssd_branch(z, xbc, dt_raw, conv_w, conv_b, dt_bias, a_log, d_skip, norm_g):
    b, s, _ = z.shape
    f32 = jnp.float32
    xbc = jax.nn.silu(causal_dwconv(xbc, conv_w, conv_b))
    xs, Bm, Cm = jnp.split(xbc, [SSD_WIDTH, SSD_WIDTH + SSD_GROUPS * SSD_STATE], axis=-1)
    dt = jax.nn.softplus(dt_raw.astype(f32) + dt_bias.astype(f32))
    A = -jnp.exp(a_log.astype(f32))
    xh = xs.astype(f32).reshape(b, s, SSD_HEADS, SSD_HEAD_DIM)
    y = ssd_scan(xh, dt, A,
                 Bm.astype(f32).reshape(b, s, SSD_GROUPS, SSD_STATE),
                 Cm.astype(f32).reshape(b, s, SSD_GROUPS, SSD_STATE))
    y = y + xh * d_skip.astype(f32)[:, None]
    y = y.reshape(b, s, SSD_WIDTH) * jax.nn.silu(z.astype(f32))
    gs = SSD_WIDTH // SSD_GROUPS
    y = rmsnorm(y.reshape(b, s, SSD_GROUPS, gs), norm_g.reshape(SSD_GROUPS, gs))
    return y.reshape(b, s, SSD_WIDTH).astype(z.dtype)


def pool_branch(u, pool_w, pool_scale):
    b, s, _ = u.shape
    f32 = jnp.float32
    uf = u.astype(f32).reshape(b, s, len(POOL_WINDOWS), POOL_GROUP)
    cs = jnp.cumsum(uf, axis=1)
    pos = jnp.arange(1, s + 1, dtype=f32)
    outs = []
    for gi, w in enumerate(POOL_WINDOWS):
        c = cs[:, :, gi]
        lag = jnp.pad(c, ((0, 0), (w, 0), (0, 0)))[:, :s]
        mean = (c - lag) / jnp.minimum(pos, float(w))[:, None]
        outs.append(mean - uf[:, :, gi])
    pooled = jnp.stack(outs, axis=2)
    mixed = jnp.einsum('bsgc,gcd->bsgd', pooled, pool_w.astype(f32))
    return (mixed.reshape(b, s, POOL_WIDTH) * pool_scale.astype(f32)).astype(u.dtype)


def conv_ffn(h, w_up, conv_w, conv_b, w_down):
    up = causal_dwconv(h @ w_up, conv_w, conv_b)
    gate, val = jnp.split(up, 2, axis=-1)
    return (jax.nn.gelu(gate) * val) @ w_down


def _fwd_setup_inputs(seed: int = 0) -> dict:
    key = jax.random.key(seed)
    ks = jax.random.split(key, 24)
    f32 = jnp.float32
    L = DEPTH

    def nrm(k, shape, scale):
        return jax.random.normal(k, shape, f32) * scale

    def gain(k, shape):
        return 1.0 + 0.02 * jax.random.normal(k, shape, f32)

    dt0 = jnp.exp(jax.random.uniform(ks[5], (L, SSD_HEADS), f32, math.log(1e-3), math.log(1e-1)))
    return {
        'x': jax.random.normal(ks[0], (BATCH, SEQ, D_MODEL), f32),
        'p': jax.random.normal(ks[1], (DEPTH, BATCH, SEQ, D_PLE), f32),
        'mix_norm_g': gain(ks[2], (L, D_MODEL)),
        'w_in': nrm(ks[3], (L, D_MODEL, D_IN_PROJ), D_MODEL ** -0.5),
        'ssd_conv_w': nrm(ks[4], (L, SSD_CONV, SSD_XBC), SSD_CONV ** -0.5),
        'ssd_conv_b': nrm(ks[6], (L, SSD_XBC), 0.02),
        'ssd_dt_bias': dt0 + jnp.log(-jnp.expm1(-dt0)),
        'ssd_a_log': jnp.log(jax.random.uniform(ks[7], (L, SSD_HEADS), f32, 1.0, 16.0)),
        'ssd_d': gain(ks[8], (L, SSD_HEADS)),
        'ssd_norm_g': gain(ks[9], (L, SSD_WIDTH)),
        'pool_w': nrm(ks[10], (L, len(POOL_WINDOWS), POOL_GROUP, POOL_GROUP), POOL_GROUP ** -0.5),
        'pool_scale': gain(ks[11], (L, POOL_WIDTH)),
        'w_out': nrm(ks[12], (L, D_MIX, D_MODEL), D_MIX ** -0.5),
        'ffn_norm_g': gain(ks[13], (L, D_MODEL)),
        'ffn_w_up': nrm(ks[14], (L, D_MODEL, 2 * D_FF), D_MODEL ** -0.5),
        'ffn_conv_w': nrm(ks[15], (L, FFN_CONV, 2 * D_FF), FFN_CONV ** -0.5),
        'ffn_conv_b': nrm(ks[16], (L, 2 * D_FF), 0.02),
        'ffn_w_down': nrm(ks[17], (L, D_FF, D_MODEL), D_FF ** -0.5),
        'ple_norm_g': gain(ks[18], (L, D_MODEL)),
        'ple_w_gate': nrm(ks[19], (L, D_MODEL, D_MODEL), D_MODEL ** -0.5),
        'ple_w_proj': nrm(ks[20], (L, D_PLE, D_MODEL), D_PLE ** -0.5),
        'final_norm_g': gain(ks[21], (D_MODEL,)),
    }


def _fwd_reference(x, p, mix_norm_g, w_in, ssd_conv_w, ssd_conv_b, ssd_dt_bias, ssd_a_log, ssd_d, ssd_norm_g,
              pool_w, pool_scale, w_out, ffn_norm_g, ffn_w_up, ffn_conv_w, ffn_conv_b, ffn_w_down,
              ple_norm_g, ple_w_gate, ple_w_proj, final_norm_g):
    h = x
    splits = [SSD_WIDTH, SSD_WIDTH + SSD_XBC, SSD_WIDTH + SSD_XBC + SSD_HEADS]
    for i in range(DEPTH):
        proj = rmsnorm(h, mix_norm_g[i]) @ w_in[i]
        z, xbc, dt_raw, u = jnp.split(proj, splits, axis=-1)
        y_ssd = ssd_branch(z, xbc, dt_raw, ssd_conv_w[i], ssd_conv_b[i], ssd_dt_bias[i],
                           ssd_a_log[i], ssd_d[i], ssd_norm_g[i])
        y_pool = pool_branch(u, pool_w[i], pool_scale[i])
        h = h + jnp.concatenate([y_ssd, y_pool], axis=-1) @ w_out[i]
        h = h + conv_ffn(rmsnorm(h, ffn_norm_g[i]), ffn_w_up[i], ffn_conv_w[i], ffn_conv_b[i], ffn_w_down[i])
        h = h + (p[i] @ ple_w_proj[i]) * jax.nn.sigmoid(rmsnorm(h, ple_norm_g[i]) @ ple_w_gate[i])
    return rmsnorm(h, final_norm_g)


import jax as _jax
import jax.numpy as _jnp

TWIN_FORMAT = 'train_step'
FWD_PARAMS = ['x', 'p', 'mix_norm_g', 'w_in', 'ssd_conv_w', 'ssd_conv_b', 'ssd_dt_bias', 'ssd_a_log', 'ssd_d', 'ssd_norm_g', 'pool_w', 'pool_scale', 'w_out', 'ffn_norm_g', 'ffn_w_up', 'ffn_conv_w', 'ffn_conv_b', 'ffn_w_down', 'ple_norm_g', 'ple_w_gate', 'ple_w_proj', 'final_norm_g']
TWIN_WEIGHTS = ['mix_norm_g', 'w_in', 'ssd_conv_w', 'ssd_conv_b', 'ssd_dt_bias', 'ssd_a_log', 'ssd_d', 'ssd_norm_g', 'pool_w', 'pool_scale', 'w_out', 'ffn_norm_g', 'ffn_w_up', 'ffn_conv_w', 'ffn_conv_b', 'ffn_w_down', 'ple_norm_g', 'ple_w_gate', 'ple_w_proj', 'final_norm_g']
TWIN_DIFF_INPUT = 'x'
TWIN_INPUTS = ['x', 'p', 'mix_norm_g', 'w_in', 'ssd_conv_w', 'ssd_conv_b', 'ssd_dt_bias', 'ssd_a_log', 'ssd_d', 'ssd_norm_g', 'pool_w', 'pool_scale', 'w_out', 'ffn_norm_g', 'ffn_w_up', 'ffn_conv_w', 'ffn_conv_b', 'ffn_w_down', 'ple_norm_g', 'ple_w_gate', 'ple_w_proj', 'final_norm_g', 'loss_target', 'm_mix_norm_g', 'm_w_in', 'm_ssd_conv_w', 'm_ssd_conv_b', 'm_ssd_dt_bias', 'm_ssd_a_log', 'm_ssd_d', 'm_ssd_norm_g', 'm_pool_w', 'm_pool_scale', 'm_w_out', 'm_ffn_norm_g', 'm_ffn_w_up', 'm_ffn_conv_w', 'm_ffn_conv_b', 'm_ffn_w_down', 'm_ple_norm_g', 'm_ple_w_gate', 'm_ple_w_proj', 'm_final_norm_g', 'v_mix_norm_g', 'v_w_in', 'v_ssd_conv_w', 'v_ssd_conv_b', 'v_ssd_dt_bias', 'v_ssd_a_log', 'v_ssd_d', 'v_ssd_norm_g', 'v_pool_w', 'v_pool_scale', 'v_w_out', 'v_ffn_norm_g', 'v_ffn_w_up', 'v_ffn_conv_w', 'v_ffn_conv_b', 'v_ffn_w_down', 'v_ple_norm_g', 'v_ple_w_gate', 'v_ple_w_proj', 'v_final_norm_g']
TWIN_OUTPUTS = ['loss', 'grad_x', 'grad_mix_norm_g', 'grad_w_in', 'grad_ssd_conv_w', 'grad_ssd_conv_b', 'grad_ssd_dt_bias', 'grad_ssd_a_log', 'grad_ssd_d', 'grad_ssd_norm_g', 'grad_pool_w', 'grad_pool_scale', 'grad_w_out', 'grad_ffn_norm_g', 'grad_ffn_w_up', 'grad_ffn_conv_w', 'grad_ffn_conv_b', 'grad_ffn_w_down', 'grad_ple_norm_g', 'grad_ple_w_gate', 'grad_ple_w_proj', 'grad_final_norm_g', 'delta_mix_norm_g', 'delta_w_in', 'delta_ssd_conv_w', 'delta_ssd_conv_b', 'delta_ssd_dt_bias', 'delta_ssd_a_log', 'delta_ssd_d', 'delta_ssd_norm_g', 'delta_pool_w', 'delta_pool_scale', 'delta_w_out', 'delta_ffn_norm_g', 'delta_ffn_w_up', 'delta_ffn_conv_w', 'delta_ffn_conv_b', 'delta_ffn_w_down', 'delta_ple_norm_g', 'delta_ple_w_gate', 'delta_ple_w_proj', 'delta_final_norm_g', 'new_m_mix_norm_g', 'new_m_w_in', 'new_m_ssd_conv_w', 'new_m_ssd_conv_b', 'new_m_ssd_dt_bias', 'new_m_ssd_a_log', 'new_m_ssd_d', 'new_m_ssd_norm_g', 'new_m_pool_w', 'new_m_pool_scale', 'new_m_w_out', 'new_m_ffn_norm_g', 'new_m_ffn_w_up', 'new_m_ffn_conv_w', 'new_m_ffn_conv_b', 'new_m_ffn_w_down', 'new_m_ple_norm_g', 'new_m_ple_w_gate', 'new_m_ple_w_proj', 'new_m_final_norm_g', 'new_v_mix_norm_g', 'new_v_w_in', 'new_v_ssd_conv_w', 'new_v_ssd_conv_b', 'new_v_ssd_dt_bias', 'new_v_ssd_a_log', 'new_v_ssd_d', 'new_v_ssd_norm_g', 'new_v_pool_w', 'new_v_pool_scale', 'new_v_w_out', 'new_v_ffn_norm_g', 'new_v_ffn_w_up', 'new_v_ffn_conv_w', 'new_v_ffn_conv_b', 'new_v_ffn_w_down', 'new_v_ple_norm_g', 'new_v_ple_w_gate', 'new_v_ple_w_proj', 'new_v_final_norm_g']
TWIN_LEAF_KINDS = {'loss': 'loss', 'grad_x': 'grad_x', 'grad_mix_norm_g': 'grad_w', 'grad_w_in': 'grad_w', 'grad_ssd_conv_w': 'grad_w', 'grad_ssd_conv_b': 'grad_w', 'grad_ssd_dt_bias': 'grad_w', 'grad_ssd_a_log': 'grad_w', 'grad_ssd_d': 'grad_w', 'grad_ssd_norm_g': 'grad_w', 'grad_pool_w': 'grad_w', 'grad_pool_scale': 'grad_w', 'grad_w_out': 'grad_w', 'grad_ffn_norm_g': 'grad_w', 'grad_ffn_w_up': 'grad_w', 'grad_ffn_conv_w': 'grad_w', 'grad_ffn_conv_b': 'grad_w', 'grad_ffn_w_down': 'grad_w', 'grad_ple_norm_g': 'grad_w', 'grad_ple_w_gate': 'grad_w', 'grad_ple_w_proj': 'grad_w', 'grad_final_norm_g': 'grad_w', 'delta_mix_norm_g': 'delta_w', 'delta_w_in': 'delta_w', 'delta_ssd_conv_w': 'delta_w', 'delta_ssd_conv_b': 'delta_w', 'delta_ssd_dt_bias': 'delta_w', 'delta_ssd_a_log': 'delta_w', 'delta_ssd_d': 'delta_w', 'delta_ssd_norm_g': 'delta_w', 'delta_pool_w': 'delta_w', 'delta_pool_scale': 'delta_w', 'delta_w_out': 'delta_w', 'delta_ffn_norm_g': 'delta_w', 'delta_ffn_w_up': 'delta_w', 'delta_ffn_conv_w': 'delta_w', 'delta_ffn_conv_b': 'delta_w', 'delta_ffn_w_down': 'delta_w', 'delta_ple_norm_g': 'delta_w', 'delta_ple_w_gate': 'delta_w', 'delta_ple_w_proj': 'delta_w', 'delta_final_norm_g': 'delta_w', 'new_m_mix_norm_g': 'new_m', 'new_m_w_in': 'new_m', 'new_m_ssd_conv_w': 'new_m', 'new_m_ssd_conv_b': 'new_m', 'new_m_ssd_dt_bias': 'new_m', 'new_m_ssd_a_log': 'new_m', 'new_m_ssd_d': 'new_m', 'new_m_ssd_norm_g': 'new_m', 'new_m_pool_w': 'new_m', 'new_m_pool_scale': 'new_m', 'new_m_w_out': 'new_m', 'new_m_ffn_norm_g': 'new_m', 'new_m_ffn_w_up': 'new_m', 'new_m_ffn_conv_w': 'new_m', 'new_m_ffn_conv_b': 'new_m', 'new_m_ffn_w_down': 'new_m', 'new_m_ple_norm_g': 'new_m', 'new_m_ple_w_gate': 'new_m', 'new_m_ple_w_proj': 'new_m', 'new_m_final_norm_g': 'new_m', 'new_v_mix_norm_g': 'new_v', 'new_v_w_in': 'new_v', 'new_v_ssd_conv_w': 'new_v', 'new_v_ssd_conv_b': 'new_v', 'new_v_ssd_dt_bias': 'new_v', 'new_v_ssd_a_log': 'new_v', 'new_v_ssd_d': 'new_v', 'new_v_ssd_norm_g': 'new_v', 'new_v_pool_w': 'new_v', 'new_v_pool_scale': 'new_v', 'new_v_w_out': 'new_v', 'new_v_ffn_norm_g': 'new_v', 'new_v_ffn_w_up': 'new_v', 'new_v_ffn_conv_w': 'new_v', 'new_v_ffn_conv_b': 'new_v', 'new_v_ffn_w_down': 'new_v', 'new_v_ple_norm_g': 'new_v', 'new_v_ple_w_gate': 'new_v', 'new_v_ple_w_proj': 'new_v', 'new_v_final_norm_g': 'new_v'}


def _forward(args):
    return _fwd_reference(*[args[k] for k in FWD_PARAMS])


def _output_shape():
    def fwd():
        inp = _fwd_setup_inputs(0)
        return _fwd_reference(*[inp[k] for k in FWD_PARAMS])
    out = _jax.eval_shape(fwd)
    return out.shape, out.dtype

N_MICROBATCH = 1
ADAM_LR = 0.001
ADAM_B1 = 0.9
ADAM_B2 = 0.999
ADAM_EPS = 1e-08
ADAM_WD = 0.01
ADAM_STEP = 10
PER_EXAMPLE_BATCH_AXIS = {'x': 0, 'p': 1, 'loss_target': 0}
SHARED_INPUTS = []
_WEIGHT_DTYPES = {'mix_norm_g': _jnp.float32, 'w_in': _jnp.float32, 'ssd_conv_w': _jnp.float32, 'ssd_conv_b': _jnp.float32, 'ssd_dt_bias': _jnp.float32, 'ssd_a_log': _jnp.float32, 'ssd_d': _jnp.float32, 'ssd_norm_g': _jnp.float32, 'pool_w': _jnp.float32, 'pool_scale': _jnp.float32, 'w_out': _jnp.float32, 'ffn_norm_g': _jnp.float32, 'ffn_w_up': _jnp.float32, 'ffn_conv_w': _jnp.float32, 'ffn_conv_b': _jnp.float32, 'ffn_w_down': _jnp.float32, 'ple_norm_g': _jnp.float32, 'ple_w_gate': _jnp.float32, 'ple_w_proj': _jnp.float32, 'final_norm_g': _jnp.float32}
MOMENT_SCALE = {'mix_norm_g': 2.958672e-01, 'w_in': 2.054639e-01, 'ssd_conv_w': 1.813981e-01, 'ssd_conv_b': 2.472392e-01, 'ssd_dt_bias': 1.056819e+00, 'ssd_a_log': 8.096135e-01, 'ssd_d': 1.085964e+00, 'ssd_norm_g': 2.479799e-01, 'pool_w': 2.104712e-01, 'pool_scale': 2.069214e-01, 'w_out': 2.232242e-01, 'ffn_norm_g': 1.967125e-01, 'ffn_w_up': 8.260694e-02, 'ffn_conv_w': 8.295980e-02, 'ffn_conv_b': 8.146752e-02, 'ffn_w_down': 1.342416e-01, 'ple_norm_g': 4.395592e-02, 'ple_w_gate': 4.249281e-02, 'ple_w_proj': 1.086836e-01, 'final_norm_g': 1.282914e+02}


def _to_microbatches(a, axis):
    t = _jnp.moveaxis(a, axis, 0)
    t = t.reshape((N_MICROBATCH, t.shape[0] // N_MICROBATCH) + t.shape[1:])
    return _jnp.moveaxis(t, 1, axis + 1)


def setup_inputs(seed: int = 0) -> dict:
    inp = _fwd_setup_inputs(seed)
    key = _jax.random.fold_in(_jax.random.key(seed), 7919)
    shape, _ = _output_shape()
    out = dict(inp)
    out["loss_target"] = _jax.random.normal(_jax.random.fold_in(key, 0), shape, _jnp.float32)
    for i, name in enumerate(TWIN_WEIGHTS):
        w = inp[name].astype(_jnp.float32)
        if MOMENT_SCALE is None:
            s = _jnp.sqrt(_jnp.mean(_jnp.square(w)) + 1e-30)
        else:
            s = MOMENT_SCALE[name]
        km, kv = _jax.random.split(_jax.random.fold_in(key, i + 1))
        out[name] = w
        out["m_" + name] = s * _jax.random.normal(km, w.shape, _jnp.float32)
        out["v_" + name] = (s * s) * _jax.random.uniform(kv, w.shape, _jnp.float32, 0.5, 1.5)
    if N_MICROBATCH > 1:
        for name, axis in PER_EXAMPLE_BATCH_AXIS.items():
            out[name] = _to_microbatches(out[name], axis)
    return {'x': out['x'], 'p': out['p'], 'mix_norm_g': out['mix_norm_g'], 'w_in': out['w_in'], 'ssd_conv_w': out['ssd_conv_w'], 'ssd_conv_b': out['ssd_conv_b'], 'ssd_dt_bias': out['ssd_dt_bias'], 'ssd_a_log': out['ssd_a_log'], 'ssd_d': out['ssd_d'], 'ssd_norm_g': out['ssd_norm_g'], 'pool_w': out['pool_w'], 'pool_scale': out['pool_scale'], 'w_out': out['w_out'], 'ffn_norm_g': out['ffn_norm_g'], 'ffn_w_up': out['ffn_w_up'], 'ffn_conv_w': out['ffn_conv_w'], 'ffn_conv_b': out['ffn_conv_b'], 'ffn_w_down': out['ffn_w_down'], 'ple_norm_g': out['ple_norm_g'], 'ple_w_gate': out['ple_w_gate'], 'ple_w_proj': out['ple_w_proj'], 'final_norm_g': out['final_norm_g'], 'loss_target': out['loss_target'], 'm_mix_norm_g': out['m_mix_norm_g'], 'm_w_in': out['m_w_in'], 'm_ssd_conv_w': out['m_ssd_conv_w'], 'm_ssd_conv_b': out['m_ssd_conv_b'], 'm_ssd_dt_bias': out['m_ssd_dt_bias'], 'm_ssd_a_log': out['m_ssd_a_log'], 'm_ssd_d': out['m_ssd_d'], 'm_ssd_norm_g': out['m_ssd_norm_g'], 'm_pool_w': out['m_pool_w'], 'm_pool_scale': out['m_pool_scale'], 'm_w_out': out['m_w_out'], 'm_ffn_norm_g': out['m_ffn_norm_g'], 'm_ffn_w_up': out['m_ffn_w_up'], 'm_ffn_conv_w': out['m_ffn_conv_w'], 'm_ffn_conv_b': out['m_ffn_conv_b'], 'm_ffn_w_down': out['m_ffn_w_down'], 'm_ple_norm_g': out['m_ple_norm_g'], 'm_ple_w_gate': out['m_ple_w_gate'], 'm_ple_w_proj': out['m_ple_w_proj'], 'm_final_norm_g': out['m_final_norm_g'], 'v_mix_norm_g': out['v_mix_norm_g'], 'v_w_in': out['v_w_in'], 'v_ssd_conv_w': out['v_ssd_conv_w'], 'v_ssd_conv_b': out['v_ssd_conv_b'], 'v_ssd_dt_bias': out['v_ssd_dt_bias'], 'v_ssd_a_log': out['v_ssd_a_log'], 'v_ssd_d': out['v_ssd_d'], 'v_ssd_norm_g': out['v_ssd_norm_g'], 'v_pool_w': out['v_pool_w'], 'v_pool_scale': out['v_pool_scale'], 'v_w_out': out['v_w_out'], 'v_ffn_norm_g': out['v_ffn_norm_g'], 'v_ffn_w_up': out['v_ffn_w_up'], 'v_ffn_conv_w': out['v_ffn_conv_w'], 'v_ffn_conv_b': out['v_ffn_conv_b'], 'v_ffn_w_down': out['v_ffn_w_down'], 'v_ple_norm_g': out['v_ple_norm_g'], 'v_ple_w_gate': out['v_ple_w_gate'], 'v_ple_w_proj': out['v_ple_w_proj'], 'v_final_norm_g': out['v_final_norm_g']}


def _loss(weights, diff, rest, loss_target):
    with _jax.named_scope("forward"):
        args = {**rest, TWIN_DIFF_INPUT: diff, **{k: w.astype(_WEIGHT_DTYPES[k]) for k, w in weights.items()}}
        y = _forward(args)
    with _jax.named_scope("loss_head"):
        err = _jnp.square(y.astype(_jnp.float32) - loss_target)
        return 0.5 * _jnp.sum(_jnp.mean(err, axis=-1)) if err.ndim else 0.5 * err


def _adamw(w, g, m, v):
    m = ADAM_B1 * m + (1.0 - ADAM_B1) * g
    v = ADAM_B2 * v + (1.0 - ADAM_B2) * _jnp.square(g)
    m_hat = m / (1.0 - ADAM_B1 ** ADAM_STEP)
    v_hat = v / (1.0 - ADAM_B2 ** ADAM_STEP)
    delta = -ADAM_LR * (m_hat / (_jnp.sqrt(v_hat) + ADAM_EPS) + ADAM_WD * w)
    return delta, m, v


def reference(x, p, mix_norm_g, w_in, ssd_conv_w, ssd_conv_b, ssd_dt_bias, ssd_a_log, ssd_d, ssd_norm_g, pool_w, pool_scale, w_out, ffn_norm_g, ffn_w_up, ffn_conv_w, ffn_conv_b, ffn_w_down, ple_norm_g, ple_w_gate, ple_w_proj, final_norm_g, loss_target, m_mix_norm_g, m_w_in, m_ssd_conv_w, m_ssd_conv_b, m_ssd_dt_bias, m_ssd_a_log, m_ssd_d, m_ssd_norm_g, m_pool_w, m_pool_scale, m_w_out, m_ffn_norm_g, m_ffn_w_up, m_ffn_conv_w, m_ffn_conv_b, m_ffn_w_down, m_ple_norm_g, m_ple_w_gate, m_ple_w_proj, m_final_norm_g, v_mix_norm_g, v_w_in, v_ssd_conv_w, v_ssd_conv_b, v_ssd_dt_bias, v_ssd_a_log, v_ssd_d, v_ssd_norm_g, v_pool_w, v_pool_scale, v_w_out, v_ffn_norm_g, v_ffn_w_up, v_ffn_conv_w, v_ffn_conv_b, v_ffn_w_down, v_ple_norm_g, v_ple_w_gate, v_ple_w_proj, v_final_norm_g):
    given = dict(x=x, p=p, mix_norm_g=mix_norm_g, w_in=w_in, ssd_conv_w=ssd_conv_w, ssd_conv_b=ssd_conv_b, ssd_dt_bias=ssd_dt_bias, ssd_a_log=ssd_a_log, ssd_d=ssd_d, ssd_norm_g=ssd_norm_g, pool_w=pool_w, pool_scale=pool_scale, w_out=w_out, ffn_norm_g=ffn_norm_g, ffn_w_up=ffn_w_up, ffn_conv_w=ffn_conv_w, ffn_conv_b=ffn_conv_b, ffn_w_down=ffn_w_down, ple_norm_g=ple_norm_g, ple_w_gate=ple_w_gate, ple_w_proj=ple_w_proj, final_norm_g=final_norm_g, loss_target=loss_target, m_mix_norm_g=m_mix_norm_g, m_w_in=m_w_in, m_ssd_conv_w=m_ssd_conv_w, m_ssd_conv_b=m_ssd_conv_b, m_ssd_dt_bias=m_ssd_dt_bias, m_ssd_a_log=m_ssd_a_log, m_ssd_d=m_ssd_d, m_ssd_norm_g=m_ssd_norm_g, m_pool_w=m_pool_w, m_pool_scale=m_pool_scale, m_w_out=m_w_out, m_ffn_norm_g=m_ffn_norm_g, m_ffn_w_up=m_ffn_w_up, m_ffn_conv_w=m_ffn_conv_w, m_ffn_conv_b=m_ffn_conv_b, m_ffn_w_down=m_ffn_w_down, m_ple_norm_g=m_ple_norm_g, m_ple_w_gate=m_ple_w_gate, m_ple_w_proj=m_ple_w_proj, m_final_norm_g=m_final_norm_g, v_mix_norm_g=v_mix_norm_g, v_w_in=v_w_in, v_ssd_conv_w=v_ssd_conv_w, v_ssd_conv_b=v_ssd_conv_b, v_ssd_dt_bias=v_ssd_dt_bias, v_ssd_a_log=v_ssd_a_log, v_ssd_d=v_ssd_d, v_ssd_norm_g=v_ssd_norm_g, v_pool_w=v_pool_w, v_pool_scale=v_pool_scale, v_w_out=v_w_out, v_ffn_norm_g=v_ffn_norm_g, v_ffn_w_up=v_ffn_w_up, v_ffn_conv_w=v_ffn_conv_w, v_ffn_conv_b=v_ffn_conv_b, v_ffn_w_down=v_ffn_w_down, v_ple_norm_g=v_ple_norm_g, v_ple_w_gate=v_ple_w_gate, v_ple_w_proj=v_ple_w_proj, v_final_norm_g=v_final_norm_g)
    weights = {n: given[n] for n in TWIN_WEIGHTS}
    shared = {n: given[n] for n in SHARED_INPUTS}
    per_example = {n: given[n] for n in ['x', 'p']}
    grad_fn = _jax.value_and_grad(_loss, argnums=(0, 1))

    def one_microbatch(ex, loss_target):
        ex = dict(ex)
        diff = ex.pop(TWIN_DIFF_INPUT)
        return grad_fn(weights, diff, {**shared, **ex}, loss_target)

    if N_MICROBATCH == 1:
        loss, (grad_w, grad_x) = one_microbatch(per_example, given["loss_target"])
    else:
        def body(carry, xs):
            loss_sum, grad_sum = carry
            l_k, (gw_k, gx_k) = one_microbatch(xs[0], xs[1])
            with _jax.named_scope("update"):
                return (loss_sum + l_k, _jax.tree.map(_jnp.add, grad_sum, gw_k)), gx_k

        init = (_jnp.zeros((), _jnp.float32), _jax.tree.map(_jnp.zeros_like, weights))
        (loss, grad_w), grad_x = _jax.lax.scan(body, init, (per_example, given["loss_target"]))
    with _jax.named_scope("update"):
        delta_w, new_m, new_v = {}, {}, {}
        for n in TWIN_WEIGHTS:
            delta_w[n], new_m[n], new_v[n] = _adamw(weights[n], grad_w[n], given["m_" + n], given["v_" + n])
    return (loss, grad_x, *[grad_w[n] for n in TWIN_WEIGHTS], *[delta_w[n] for n in TWIN_WEIGHTS],
            *[new_m[n] for n in TWIN_WEIGHTS], *[new_v[n] for n in TWIN_WEIGHTS])
```

```python
import functools
import math

import jax
import jax.numpy as jnp
from jax import lax
from jax.experimental import pallas as pl
from jax.experimental.pallas import tpu as pltpu

f32, bf16 = jnp.float32, jnp.bfloat16
HI = lax.Precision.HIGHEST

D_MODEL = 1024
D_PLE = 256
DEPTH = 4
SSD_W = 512
HEADS = 8
HEAD_DIM = 64
NSTATE = 128
CHUNK = 128
SSD_CONV = 4
XBC = 1024
POOL_W = 512
POOL_G = 128
WINDOWS = (2, 4, 8, 16)
D_FF = 2816
FF_HALF = D_FF // 2
FFN_CONV = 3
D_IN = 2056
EPS = 1e-6
ADAM_LR, ADAM_B1, ADAM_B2, ADAM_EPS, ADAM_WD, ADAM_STEP = 0.001, 0.9, 0.999, 1e-08, 0.01, 10

LANES = 128
NPROJ = 2048 + LANES
HALO = 16
FHALO = 8
VMEM_LIMIT = 58 * 1024 * 1024
ROW_TILE = 256


def _dot(a, b):
    return jnp.dot(a, b, preferred_element_type=f32)


def _dot_nt(a, b):
    return lax.dot_general(a, b, (((1,), (1,)), ((), ())), preferred_element_type=f32)


def _dot_tn(a, b):
    return lax.dot_general(a, b, (((0,), (0,)), ((), ())), preferred_element_type=f32)


def _dot_hi(a, b):
    return jnp.dot(a, b, precision=HI, preferred_element_type=f32)


def _dot_nt_hi(a, b):
    return lax.dot_general(a, b, (((1,), (1,)), ((), ())), precision=HI, preferred_element_type=f32)


def _rms_fwd(x, g):
    r = lax.rsqrt(jnp.mean(x * x, axis=-1, keepdims=True) + EPS)
    xhat = x * r
    return xhat * g, xhat, r


def _rms_bwd(dy, xhat, r, g):
    dxhat = dy * g
    dx = r * (dxhat - xhat * jnp.mean(dxhat * xhat, axis=-1, keepdims=True))
    return dx, jnp.sum(dy * xhat, axis=0, keepdims=True)


def _sigmoid(x):
    return 1.0 / (1.0 + jnp.exp(-x))


_GELU_C = math.sqrt(2.0 / math.pi)


def _gelu_and_grad(x):
    x2 = x * x
    t = jnp.tanh(_GELU_C * (x + 0.044715 * x * x2))
    g = 0.5 * x * (1.0 + t)
    dg = 0.5 * (1.0 + t) + 0.5 * x * (1.0 - t * t) * _GELU_C * (1.0 + 3.0 * 0.044715 * x2)
    return g, dg


def _gelu(x):
    return 0.5 * x * (1.0 + jnp.tanh(_GELU_C * (x + 0.044715 * x * x * x)))


def _softplus(x):
    return jnp.maximum(x, 0.0) + jnp.log(1.0 + jnp.exp(-jnp.abs(x)))


def _cparams(sem=("arbitrary",)):
    return pltpu.CompilerParams(dimension_semantics=sem, vmem_limit_bytes=VMEM_LIMIT)


def _const_spec(shape):
    nd = len(shape)
    return pl.BlockSpec(shape, lambda *_: (0,) * nd, pipeline_mode=pl.Buffered(1))


def _row_tile(s):
    return min(ROW_TILE, s)


def _mix_in_fwd(h, g1, w_r):
    s = h.shape[0]
    t = _row_tile(s)

    def body(h_ref, g_ref, w_ref, zxu_ref, dtr_ref):
        hn, _, _ = _rms_fwd(h_ref[...], g_ref[...])
        proj = _dot(hn.astype(bf16), w_ref[...])
        zxu_ref[...] = proj[:, :2048].astype(bf16)
        dtr_ref[...] = proj[:, 2048:]

    return pl.pallas_call(
        body, name="mix_in_fwd", grid=(s // t,),
        out_shape=(jax.ShapeDtypeStruct((s, 2048), bf16), jax.ShapeDtypeStruct((s, LANES), f32)),
        in_specs=[pl.BlockSpec((t, D_MODEL), lambda i: (i, 0)), _const_spec((1, D_MODEL)), _const_spec((D_MODEL, NPROJ))],
        out_specs=(pl.BlockSpec((t, 2048), lambda i: (i, 0)), pl.BlockSpec((t, LANES), lambda i: (i, 0))),
        compiler_params=_cparams(),
    )(h, g1, w_r)


def _mix_in_bwd(d_zxu, d_dtr, h, dh1, g1, w_r):
    s = h.shape[0]
    t = _row_tile(s)
    n = s // t

    def body(dz_ref, dd_ref, h_ref, dh1_ref, g_ref, w_ref, dh_ref, dw_ref, dg_ref, acc):
        i = pl.program_id(0)

        @pl.when(i == 0)
        def _():
            acc[...] = jnp.zeros_like(acc)
            dg_ref[...] = jnp.zeros_like(dg_ref)

        g = g_ref[...]
        hn, xhat, r = _rms_fwd(h_ref[...], g)
        dproj = jnp.concatenate([dz_ref[...], dd_ref[...].astype(bf16)], axis=1)
        d_hn = _dot_nt(dproj, w_ref[...])
        acc[...] += _dot_tn(hn.astype(bf16), dproj)
        dx, dg = _rms_bwd(d_hn, xhat, r, g)
        dg_ref[...] += dg
        dh_ref[...] = dh1_ref[...] + dx

        @pl.when(i == n - 1)
        def _():
            pltpu.sync_copy(acc, dw_ref)

    return pl.pallas_call(
        body, name="mix_in_bwd", grid=(n,),
        out_shape=(jax.ShapeDtypeStruct((s, D_MODEL), f32), jax.ShapeDtypeStruct((D_MODEL, NPROJ), f32),
                   jax.ShapeDtypeStruct((1, D_MODEL), f32)),
        in_specs=[pl.BlockSpec((t, 2048), lambda i: (i, 0)), pl.BlockSpec((t, LANES), lambda i: (i, 0)),
                  pl.BlockSpec((t, D_MODEL), lambda i: (i, 0)), pl.BlockSpec((t, D_MODEL), lambda i: (i, 0)),
                  _const_spec((1, D_MODEL)), _const_spec((D_MODEL, NPROJ))],
        out_specs=(pl.BlockSpec((t, D_MODEL), lambda i: (i, 0)), pl.BlockSpec(memory_space=pl.ANY),
                   pl.BlockSpec((1, D_MODEL), lambda i: (0, 0))),
        scratch_shapes=[pltpu.VMEM((D_MODEL, NPROJ), f32)],
        compiler_params=_cparams(),
    )(d_zxu, d_dtr, h, dh1, g1, w_r)


def _iota2(shape, dim):
    return lax.broadcasted_iota(jnp.int32, shape, dim)


def _expand_mats():
    r = _iota2((LANES, HEADS * LANES), 0)
    c = _iota2((LANES, HEADS * LANES), 1)
    e_big = (lax.shift_right_logical(c, 7) == r).astype(f32)
    r2 = _iota2((LANES, SSD_W), 0)
    c2 = _iota2((LANES, SSD_W), 1)
    e_half = (lax.shift_right_logical(c2, 6) == r2).astype(f32)
    return e_big, e_half


def _ssd_pre(zx, dtr, xext, cw, cb, dtb, alog):
    c = cb + cw[0:1] * xext[pl.ds(HALO - 3, CHUNK), :]
    for k in range(1, SSD_CONV):
        c = c + cw[k:k + 1] * xext[pl.ds(HALO - 3 + k, CHUNK), :]
    sig_c = _sigmoid(c)
    xc = c * sig_c
    e_big, e_half = _expand_mats()
    dt = _softplus(dtr + dtb)
    a_neg = -jnp.exp(alog)
    a = dt * a_neg
    dt_w = _dot_hi(dt, e_half)
    xs = xc[:, :SSD_W]
    xd = xs * dt_w
    tril = (_iota2((CHUNK, CHUNK), 0) >= _iota2((CHUNK, CHUNK), 1))
    acs_b = _dot_hi(tril.astype(f32), _dot_hi(a, e_big))
    return dict(c=c, sig_c=sig_c, xc=xc, xs=xs, dt=dt, a_neg=a_neg, a=a, dt_w=dt_w, xd=xd, acs_b=acs_b,
                tril=tril, e_big=e_big, e_half=e_half)


def _pair_fwd(q, pp, s_in):
    g = pp // 2
    lo = _iota2((CHUNK, LANES), 1) < HEAD_DIM
    b_g = q["xc"][:, SSD_W + g * NSTATE:SSD_W + (g + 1) * NSTATE]
    c_g = q["xc"][:, SSD_W + 2 * NSTATE + g * NSTATE:SSD_W + 2 * NSTATE + (g + 1) * NSTATE]
    b16, c16 = b_g.astype(bf16), c_g.astype(bf16)
    gmat = _dot_nt(c16, b16)
    xp = q["xd"][:, pp * LANES:(pp + 1) * LANES]
    xp16 = xp.astype(bf16)
    ab0 = q["acs_b"][:, (2 * pp) * LANES:(2 * pp + 1) * LANES]
    ab1 = q["acs_b"][:, (2 * pp + 1) * LANES:(2 * pp + 2) * LANES]
    ls, ms, ys = [], [], []
    for ab in (ab0, ab1):
        lmat = jnp.exp(jnp.where(q["tril"], ab - ab.T, -jnp.inf))
        mmat = gmat * lmat
        ls.append(lmat)
        ms.append(mmat)
        ys.append(_dot(mmat.astype(bf16), xp16))
    y_diag = jnp.where(lo, ys[0], ys[1])
    ab_pair = jnp.where(lo, ab0, ab1)
    e_pair = jnp.exp(ab_pair)
    s16 = s_in.astype(bf16)
    y_off = _dot(c16, s16) * e_pair
    alast = ab_pair[CHUNK - 1:CHUNK, :]
    dec_pair = jnp.exp(alast - ab_pair)
    xdec = xp * dec_pair
    st = _dot_tn(b16, xdec.astype(bf16))
    cd_pair = jnp.exp(alast)
    s_out = s_in * cd_pair + st
    return dict(b16=b16, c16=c16, gmat=gmat, xp=xp, xp16=xp16, ls=ls, ms=ms, y=y_diag + y_off, y_off=y_off,
                e_pair=e_pair, dec_pair=dec_pair, xdec=xdec, cd_pair=cd_pair, s_out=s_out, s16=s16, lo=lo)


def _gate_norm_fwd(y_pre, z, ng):
    sz = _sigmoid(z)
    yg = y_pre * (z * sz)
    outs, stats = [], []
    half = SSD_W // 2
    for gi in range(2):
        o, xhat, r = _rms_fwd(yg[:, gi * half:(gi + 1) * half], ng[:, gi * half:(gi + 1) * half])
        outs.append(o)
        stats.append((xhat, r))
    return jnp.concatenate(outs, axis=1), sz, stats


def _pool_fwd(uext, u, row0, pw_ref, ps):
    pos = (row0 + _iota2((CHUNK, 1), 0) + 1).astype(f32)
    pooled, mixed, invs = [], [], []
    for gi, w in enumerate(WINDOWS):
        sl = slice(gi * POOL_G, (gi + 1) * POOL_G)
        acc = uext[pl.ds(HALO, CHUNK), sl]
        for j in range(1, w):
            acc = acc + uext[pl.ds(HALO - j, CHUNK), sl]
        den = jnp.minimum(pos, float(w))
        pg = acc / den - u[:, sl]
        pooled.append(pg)
        invs.append(den)
        mixed.append(_dot(pg.astype(bf16), pw_ref[gi]))
    mixed = jnp.concatenate(mixed, axis=1)
    return mixed * ps, pooled, mixed, invs


def _ssd_specs(s):
    nc = s // CHUNK
    hb = CHUNK // HALO
    return nc, hb


def _ssd_param_specs():
    return [_const_spec((SSD_CONV, XBC)), _const_spec((1, XBC)), _const_spec((1, LANES)), _const_spec((1, LANES)),
            _const_spec((1, SSD_W)), _const_spec((1, SSD_W)), _const_spec((4, POOL_G, POOL_G)), _const_spec((1, POOL_W))]


def _ssd_pool_fwd(zxu, dtr, prm):
    s = zxu.shape[0]
    nc, hb = _ssd_specs(s)

    def body(zx_ref, halo_ref, dtr_ref, cw_ref, cb_ref, dtb_ref, alog_ref, dsk_ref, ng_ref, pw_ref, ps_ref,
             ymix_ref, st_ref, state, xext, uext):
        i = pl.program_id(0)

        @pl.when(i == 0)
        def _():
            state[...] = jnp.zeros_like(state)

        zx = zx_ref[...].astype(f32)
        halo = jnp.where(i > 0, halo_ref[...].astype(f32), 0.0)
        xext[0:HALO, :] = halo[:, SSD_W:SSD_W + XBC]
        xext[HALO:, :] = zx[:, SSD_W:SSD_W + XBC]
        uext[0:HALO, :] = halo[:, SSD_W + XBC:]
        uext[HALO:, :] = zx[:, SSD_W + XBC:]
        q = _ssd_pre(zx, dtr_ref[...], xext, cw_ref[...], cb_ref[...], dtb_ref[...], alog_ref[...])
        ys = []
        for pp in range(4):
            s_in = state[pp]
            st_ref[0, pp] = s_in
            r = _pair_fwd(q, pp, s_in)
            state[pp] = r["s_out"]
            ys.append(r["y"])
        y_pre = jnp.concatenate(ys, axis=1) + q["xs"] * dsk_ref[...]
        y_ssd, _, _ = _gate_norm_fwd(y_pre, zx[:, :SSD_W], ng_ref[...])
        y_pool, _, _, _ = _pool_fwd(uext, zx[:, SSD_W + XBC:], i * CHUNK, pw_ref, ps_ref[...])
        ymix_ref[:, :SSD_W] = y_ssd.astype(bf16)
        ymix_ref[:, SSD_W:] = y_pool.astype(bf16)

    return pl.pallas_call(
        body, name="ssd_pool_fwd", grid=(nc,),
        out_shape=(jax.ShapeDtypeStruct((s, D_MODEL), bf16), jax.ShapeDtypeStruct((nc, 4, NSTATE, LANES), f32)),
        in_specs=[pl.BlockSpec((CHUNK, 2048), lambda i: (i, 0)),
                  pl.BlockSpec((HALO, 2048), lambda i: (jnp.maximum(i * hb - 1, 0), 0)),
                  pl.BlockSpec((CHUNK, LANES), lambda i: (i, 0))] + _ssd_param_specs(),
        out_specs=(pl.BlockSpec((CHUNK, D_MODEL), lambda i: (i, 0)),
                   pl.BlockSpec((1, 4, NSTATE, LANES), lambda i: (i, 0, 0, 0))),
        scratch_shapes=[pltpu.VMEM((4, NSTATE, LANES), f32), pltpu.VMEM((HALO + CHUNK, XBC), f32),
                        pltpu.VMEM((HALO + CHUNK, POOL_W), f32)],
        compiler_params=_cparams(),
    )(zxu, zxu, dtr, *prm)


def _ssd_pool_bwd(d_ymix, zxu, dtr, states, prm):
    s = zxu.shape[0]
    nc, hb = _ssd_specs(s)
    rev = lambda i: nc - 1 - i

    def body(dy_ref, zx_ref, halo_ref, dtr_ref, st_ref, cw_ref, cb_ref, dtb_ref, alog_ref, dsk_ref, ng_ref, pw_ref, ps_ref,
             dzx_ref, ddtr_ref, dcw_ref, dcb_ref, ddtb_ref, dalog_ref, ddsk_ref, dng_ref, dpw_ref, dps_ref,
             dstate, xext, uext, dxext, duext, cx, cu):
        i = pl.program_id(0)
        ci = nc - 1 - i

        @pl.when(i == 0)
        def _():
            dstate[...] = jnp.zeros_like(dstate)
            cx[...] = jnp.zeros_like(cx)
            cu[...] = jnp.zeros_like(cu)
            for r in (dcw_ref, dcb_ref, ddtb_ref, dalog_ref, ddsk_ref, dng_ref, dpw_ref, dps_ref):
                r[...] = jnp.zeros_like(r)

        zx = zx_ref[...].astype(f32)
        halo = jnp.where(ci > 0, halo_ref[...].astype(f32), 0.0)
        xext[0:HALO, :] = halo[:, SSD_W:SSD_W + XBC]
        xext[HALO:, :] = zx[:, SSD_W:SSD_W + XBC]
        uext[0:HALO, :] = halo[:, SSD_W + XBC:]
        uext[HALO:, :] = zx[:, SSD_W + XBC:]
        cw = cw_ref[...]
        q = _ssd_pre(zx, dtr_ref[...], xext, cw, cb_ref[...], dtb_ref[...], alog_ref[...])
        z = zx[:, :SSD_W]
        dy = dy_ref[...].astype(f32)
        d_yssd, d_ypool = dy[:, :SSD_W], dy[:, SSD_W:]

        pairs = [_pair_fwd(q, pp, st_ref[0, pp]) for pp in range(4)]
        dsk = dsk_ref[...]
        ng = ng_ref[...]
        y_pre = jnp.concatenate([r["y"] for r in pairs], axis=1) + q["xs"] * dsk
        _, sz, stats = _gate_norm_fwd(y_pre, z, ng)

        half = SSD_W // 2
        d_yg, d_ng = [], []
        for gi in range(2):
            xhat, r = stats[gi]
            dx, dg = _rms_bwd(d_yssd[:, gi * half:(gi + 1) * half], xhat, r, ng[:, gi * half:(gi + 1) * half])
            d_yg.append(dx)
            d_ng.append(dg)
        d_yg = jnp.concatenate(d_yg, axis=1)
        dng_ref[...] += jnp.concatenate(d_ng, axis=1)
        silu_z = z * sz
        d_ypre = d_yg * silu_z
        d_z = d_yg * y_pre * (sz * (1.0 + z * (1.0 - sz)))
        ddsk_ref[...] += jnp.sum(d_ypre * q["xs"], axis=0, keepdims=True)

        d_xd, v_blocks, r_blocks = [], [], []
        d_b = [None, None]
        d_c = [None, None]
        d_g = [None, None]
        last_row = _iota2((CHUNK, LANES), 0) == CHUNK - 1
        for pp in range(4):
            g = pp // 2
            r = pairs[pp]
            lo = r["lo"]
            dyp = d_ypre[:, pp * LANES:(pp + 1) * LANES]
            dyp16 = dyp.astype(bf16)
            ds_out = dstate[pp]
            ds16 = ds_out.astype(bf16)
            dye16 = (dyp * r["e_pair"]).astype(bf16)
            dstate[pp] = r["cd_pair"] * ds_out + _dot_tn(r["c16"], dye16)
            dc = _dot_nt(dye16, r["s16"])
            dxdec = _dot(r["b16"], ds16)
            db = _dot_nt(r["xdec"].astype(bf16), ds16)
            dxp = dxdec * r["dec_pair"]
            dxs = []
            for hh in range(2):
                msk = lo if hh == 0 else jnp.logical_not(lo)
                m16 = r["ms"][hh].astype(bf16)
                dxs.append(_dot_tn(m16, dyp16))
                dm = _dot_nt(jnp.where(msk, dyp, 0.0).astype(bf16), r["xp16"])
                wmat = dm * r["ms"][hh]
                v_blocks.append(wmat - wmat.T)
                dgh = dm * r["ls"][hh]
                d_g[g] = dgh if d_g[g] is None else d_g[g] + dgh
            dxp = dxp + jnp.where(lo, dxs[0], dxs[1])
            d_xd.append(dxp)
            t2 = dxdec * r["xdec"]
            rp = dyp * r["y_off"] - t2
            tail = jnp.sum(t2, axis=0, keepdims=True) + jnp.sum(ds_out * st_ref[0, pp] * r["cd_pair"], axis=0, keepdims=True)
            r_blocks.append(rp + jnp.where(last_row, tail, 0.0))
            d_b[g] = db if d_b[g] is None else d_b[g] + db
            d_c[g] = dc if d_c[g] is None else d_c[g] + dc
        for g in range(2):
            dg16 = d_g[g].astype(bf16)
            d_c[g] = d_c[g] + _dot(dg16, pairs[2 * g]["b16"])
            d_b[g] = d_b[g] + _dot_tn(dg16, pairs[2 * g]["c16"])
        d_xd = jnp.concatenate(d_xd, axis=1)
        d_acs = _dot_nt_hi(jnp.concatenate(v_blocks, axis=1), q["e_big"]) + _dot_nt_hi(jnp.concatenate(r_blocks, axis=1), q["e_half"])
        triu = (_iota2((CHUNK, CHUNK), 0) <= _iota2((CHUNK, CHUNK), 1)).astype(f32)
        d_a = _dot_hi(triu, d_acs)
        d_dt = d_a * q["a_neg"] + _dot_nt_hi(d_xd * q["xs"], q["e_half"])
        dalog_ref[...] += jnp.sum(d_a * q["dt"], axis=0, keepdims=True) * q["a_neg"]
        d_dtr = d_dt * _sigmoid(dtr_ref[...] + dtb_ref[...])
        ddtr_ref[...] = d_dtr
        ddtb_ref[...] += jnp.sum(d_dtr, axis=0, keepdims=True)
        d_xs = d_ypre * dsk + d_xd * q["dt_w"]

        d_xc = jnp.concatenate([d_xs, d_b[0], d_b[1], d_c[0], d_c[1]], axis=1)
        sc = q["sig_c"]
        d_conv = d_xc * (sc * (1.0 + q["c"] * (1.0 - sc)))
        dcb_ref[...] += jnp.sum(d_conv, axis=0, keepdims=True)
        dxext[...] = jnp.zeros_like(dxext)
        for k in range(SSD_CONV):
            dcw_ref[k:k + 1, :] += jnp.sum(d_conv * xext[pl.ds(HALO - 3 + k, CHUNK), :], axis=0, keepdims=True)
            dxext[pl.ds(HALO - 3 + k, CHUNK), :] += cw[k:k + 1] * d_conv
        dxext[pl.ds(CHUNK, HALO), :] += cx[...]
        cx[...] = dxext[0:HALO, :]

        ps = ps_ref[...]
        u = zx[:, SSD_W + XBC:]
        _, pooled, mixed, dens = _pool_fwd(uext, u, ci * CHUNK, pw_ref, ps)
        dps_ref[...] += jnp.sum(d_ypool * mixed, axis=0, keepdims=True)
        d_mixed = d_ypool * ps
        duext[...] = jnp.zeros_like(duext)
        for gi, w in enumerate(WINDOWS):
            sl = slice(gi * POOL_G, (gi + 1) * POOL_G)
            dm16 = d_mixed[:, sl].astype(bf16)
            dpw_ref[gi] += _dot_tn(pooled[gi].astype(bf16), dm16)
            d_pg = _dot_nt(dm16, pw_ref[gi])
            d_mean = d_pg / dens[gi]
            duext[pl.ds(HALO, CHUNK), sl] += d_mean - d_pg
            for j in range(1, w):
                duext[pl.ds(HALO - j, CHUNK), sl] += d_mean
        duext[pl.ds(CHUNK, HALO), :] += cu[...]
        cu[...] = duext[0:HALO, :]

        dzx_ref[:, :SSD_W] = d_z.astype(bf16)
        dzx_ref[:, SSD_W:SSD_W + XBC] = dxext[HALO:, :].astype(bf16)
        dzx_ref[:, SSD_W + XBC:] = duext[HALO:, :].astype(bf16)

    small = lambda shape: pl.BlockSpec(shape, lambda i: (0,) * len(shape))
    small_shapes = [(SSD_CONV, XBC), (1, XBC), (1, LANES), (1, LANES), (1, SSD_W), (1, SSD_W), (4, POOL_G, POOL_G), (1, POOL_W)]
    return pl.pallas_call(
        body, name="ssd_pool_bwd", grid=(nc,),
        out_shape=(jax.ShapeDtypeStruct((s, 2048), bf16), jax.ShapeDtypeStruct((s, LANES), f32))
        + tuple(jax.ShapeDtypeStruct(sh, f32) for sh in small_shapes),
        in_specs=[pl.BlockSpec((CHUNK, D_MODEL), lambda i: (rev(i), 0)),
                  pl.BlockSpec((CHUNK, 2048), lambda i: (rev(i), 0)),
                  pl.BlockSpec((HALO, 2048), lambda i: (jnp.maximum(rev(i) * hb - 1, 0), 0)),
                  pl.BlockSpec((CHUNK, LANES), lambda i: (rev(i), 0)),
                  pl.BlockSpec((1, 4, NSTATE, LANES), lambda i: (rev(i), 0, 0, 0))] + _ssd_param_specs(),
        out_specs=(pl.BlockSpec((CHUNK, 2048), lambda i: (rev(i), 0)), pl.BlockSpec((CHUNK, LANES), lambda i: (rev(i), 0)))
        + tuple(small(sh) for sh in small_shapes),
        scratch_shapes=[pltpu.VMEM((4, NSTATE, LANES), f32), pltpu.VMEM((HALO + CHUNK, XBC), f32),
                        pltpu.VMEM((HALO + CHUNK, POOL_W), f32), pltpu.VMEM((HALO + CHUNK, XBC), f32),
                        pltpu.VMEM((HALO + CHUNK, POOL_W), f32), pltpu.VMEM((HALO, XBC), f32), pltpu.VMEM((HALO, POOL_W), f32)],
        compiler_params=_cparams(),
    )(d_ymix, zxu, zxu, dtr, states, *prm)


FFN_BWD_TILE = 128


def _prev_halo_spec(t, width):
    hb = t // HALO
    return pl.BlockSpec((HALO, width), lambda i: (jnp.maximum(i * hb - 1, 0), 0))


def _ffn_half(hn16, j, wup_ref, cw_ref, cb_ref, up_scr, rows):
    up_scr[...] = _dot(hn16, wup_ref[j])
    cw = cw_ref[j]
    cv = cb_ref[j] + cw[0:1] * up_scr[pl.ds(HALO - 2, rows), :]
    for k in range(1, FFN_CONV):
        cv = cv + cw[k:k + 1] * up_scr[pl.ds(HALO - 2 + k, rows), :]
    return cv


def _out_ffn_ple_fwd(h, ymix, p_l, w_out, g2, wup_h, cw_h, cb_h, wdn_h, g3, w_gate, w_proj):
    s = h.shape[0]
    t = _row_tile(s)
    n = s // t

    def body(h_ref, hh_ref, ym_ref, ymh_ref, p_ref, wo_ref, g2_ref, wup_ref, cw_ref, cb_ref, wdn_ref, g3_ref, wg_ref, wp_ref,
             h1_ref, h2_ref, h3_ref, up_scr):
        i = pl.program_id(0)
        hh = jnp.where(i > 0, hh_ref[...], 0.0)
        ymh = jnp.where(i > 0, ymh_ref[...].astype(f32), 0.0)
        h_ext = jnp.concatenate([hh, h_ref[...]], axis=0)
        ym_ext = jnp.concatenate([ymh, ym_ref[...].astype(f32)], axis=0).astype(bf16)
        h1_ext = h_ext + _dot(ym_ext, wo_ref[...])
        hn2, _, _ = _rms_fwd(h1_ext, g2_ref[...])
        hn16 = hn2.astype(bf16)
        h1 = h1_ext[HALO:, :]
        acc = h1
        for j in range(2):
            cv = _ffn_half(hn16, j, wup_ref, cw_ref, cb_ref, up_scr, t)
            act = _gelu(cv[:, :FF_HALF]) * cv[:, FF_HALF:]
            acc = acc + _dot(act.astype(bf16), wdn_ref[j])
        h2 = acc
        hn3, _, _ = _rms_fwd(h2, g3_ref[...])
        gate = _sigmoid(_dot(hn3.astype(bf16), wg_ref[...]))
        pp = _dot(p_ref[...].astype(bf16), wp_ref[...])
        h1_ref[...] = h1
        h2_ref[...] = h2
        h3_ref[...] = h2 + pp * gate

    row = lambda w: pl.BlockSpec((t, w), lambda i: (i, 0))
    return pl.pallas_call(
        body, name="out_ffn_ple_fwd", grid=(n,),
        out_shape=tuple(jax.ShapeDtypeStruct((s, D_MODEL), f32) for _ in range(3)),
        in_specs=[row(D_MODEL), _prev_halo_spec(t, D_MODEL), row(D_MODEL), _prev_halo_spec(t, D_MODEL), row(D_PLE),
                  _const_spec((D_MODEL, D_MODEL)), _const_spec((1, D_MODEL)), _const_spec((2, D_MODEL, D_FF)),
                  _const_spec((2, FFN_CONV, D_FF)), _const_spec((2, 1, D_FF)), _const_spec((2, FF_HALF, D_MODEL)),
                  _const_spec((1, D_MODEL)), _const_spec((D_MODEL, D_MODEL)), _const_spec((D_PLE, D_MODEL))],
        out_specs=tuple(row(D_MODEL) for _ in range(3)),
        scratch_shapes=[pltpu.VMEM((HALO + t, D_FF), f32)],
        compiler_params=_cparams(),
    )(h, h, ymix, ymix, p_l, w_out, g2, wup_h, cw_h, cb_h, wdn_h, g3, w_gate, w_proj)


def _ple_bwd(dh3, h2, p_l, g3, w_gate, w_proj):
    s = h2.shape[0]
    t = _row_tile(s)
    n = s // t

    def body(dh3_ref, h2_ref, p_ref, g3_ref, wg_ref, wp_ref, dh2_ref, dwg_ref, dwp_ref, dg3_ref):
        i = pl.program_id(0)

        @pl.when(i == 0)
        def _():
            dwg_ref[...] = jnp.zeros_like(dwg_ref)
            dwp_ref[...] = jnp.zeros_like(dwp_ref)
            dg3_ref[...] = jnp.zeros_like(dg3_ref)

        g3 = g3_ref[...]
        dh3 = dh3_ref[...]
        hn3, xhat, r = _rms_fwd(h2_ref[...], g3)
        hn16 = hn3.astype(bf16)
        gate = _sigmoid(_dot(hn16, wg_ref[...]))
        p16 = p_ref[...].astype(bf16)
        pp = _dot(p16, wp_ref[...])
        d_pp = (dh3 * gate).astype(bf16)
        d_pre = (dh3 * pp * gate * (1.0 - gate)).astype(bf16)
        dwp_ref[...] += _dot_tn(p16, d_pp)
        dwg_ref[...] += _dot_tn(hn16, d_pre)
        dx, dg = _rms_bwd(_dot_nt(d_pre, wg_ref[...]), xhat, r, g3)
        dg3_ref[...] += dg
        dh2_ref[...] = dh3 + dx

    row = lambda w: pl.BlockSpec((t, w), lambda i: (i, 0))
    fixed = lambda shape: pl.BlockSpec(shape, lambda i: (0,) * len(shape))
    return pl.pallas_call(
        body, name="ple_bwd", grid=(n,),
        out_shape=(jax.ShapeDtypeStruct((s, D_MODEL), f32), jax.ShapeDtypeStruct((D_MODEL, D_MODEL), f32),
                   jax.ShapeDtypeStruct((D_PLE, D_MODEL), f32), jax.ShapeDtypeStruct((1, D_MODEL), f32)),
        in_specs=[row(D_MODEL), row(D_MODEL), row(D_PLE), _const_spec((1, D_MODEL)), _const_spec((D_MODEL, D_MODEL)),
                  _const_spec((D_PLE, D_MODEL))],
        out_specs=(row(D_MODEL), fixed((D_MODEL, D_MODEL)), fixed((D_PLE, D_MODEL)), fixed((1, D_MODEL))),
        compiler_params=_cparams(),
    )(dh3, h2, p_l, g3, w_gate, w_proj)


def _ffn_bwd(dh2, h1, g2, wup_h, cw_h, cb_h, wdn_h):
    s = h1.shape[0]
    t = min(FFN_BWD_TILE, s)
    n = s // t
    hb = t // HALO
    last_hb = s // HALO - 1

    def body(dh2_ref, dh2n_ref, h1_ref, h1p_ref, h1n_ref, g2_ref, wup_ref, cw_ref, cb_ref, wdn_ref,
             part_ref, dwup_ref, dwdn_ref, dcw_ref, dcb_ref, up_scr, dcv_scr, acc_up, acc_dn):
        j = pl.program_id(0)
        i = pl.program_id(1)

        @pl.when(i == 0)
        def _():
            acc_up[...] = jnp.zeros_like(acc_up)
            acc_dn[...] = jnp.zeros_like(acc_dn)
            dcw_ref[...] = jnp.zeros_like(dcw_ref)
            dcb_ref[...] = jnp.zeros_like(dcb_ref)

        h1p = jnp.where(i > 0, h1p_ref[...], 0.0)
        h1_ext = jnp.concatenate([h1p, h1_ref[...], h1n_ref[...]], axis=0)
        hn2, _, _ = _rms_fwd(h1_ext, g2_ref[...])
        hn16 = hn2.astype(bf16)
        up_scr[...] = _dot(hn16, wup_ref[0])
        cw = cw_ref[0]
        rows = t + HALO
        cv = cb_ref[0] + cw[0:1] * up_scr[pl.ds(HALO - 2, rows), :]
        for k in range(1, FFN_CONV):
            cv = cv + cw[k:k + 1] * up_scr[pl.ds(HALO - 2 + k, rows), :]
        dh2 = dh2_ref[...]
        dh2n = jnp.where(i < n - 1, dh2n_ref[...], 0.0)
        dh2_ext16 = jnp.concatenate([dh2, dh2n], axis=0).astype(bf16)
        d_act = _dot_nt(dh2_ext16, wdn_ref[0])
        gate, val = cv[:, :FF_HALF], cv[:, FF_HALF:]
        gl, dgl = _gelu_and_grad(gate)
        dcv_scr[:, :FF_HALF] = d_act * val * dgl
        dcv_scr[:, FF_HALF:] = d_act * gl
        act16 = (gl[:t] * val[:t]).astype(bf16)
        acc_dn[...] += _dot_tn(act16, dh2_ext16[:t])
        d_cv = dcv_scr[pl.ds(0, t), :]
        dcb_ref[0] += jnp.sum(d_cv, axis=0, keepdims=True)
        d_up = cw[2:3] * d_cv
        dcw_ref[0, 2:3, :] += jnp.sum(d_cv * up_scr[pl.ds(HALO, t), :], axis=0, keepdims=True)
        for k in range(FFN_CONV - 1):
            dcw_ref[0, k:k + 1, :] += jnp.sum(d_cv * up_scr[pl.ds(HALO - 2 + k, t), :], axis=0, keepdims=True)
            d_up = d_up + cw[k:k + 1] * dcv_scr[pl.ds(2 - k, t), :]
        d_up16 = d_up.astype(bf16)
        part_ref[0] = _dot_nt(d_up16, wup_ref[0])
        acc_up[...] += _dot_tn(hn16[HALO:HALO + t], d_up16)

        @pl.when(i == n - 1)
        def _():
            pltpu.sync_copy(acc_up, dwup_ref.at[j])
            pltpu.sync_copy(acc_dn, dwdn_ref.at[j])

    return pl.pallas_call(
        body, name="ffn_bwd", grid=(2, n),
        out_shape=(jax.ShapeDtypeStruct((2, s, D_MODEL), f32), jax.ShapeDtypeStruct((2, D_MODEL, D_FF), f32),
                   jax.ShapeDtypeStruct((2, FF_HALF, D_MODEL), f32), jax.ShapeDtypeStruct((2, FFN_CONV, D_FF), f32),
                   jax.ShapeDtypeStruct((2, 1, D_FF), f32)),
        in_specs=[pl.BlockSpec((t, D_MODEL), lambda j, i: (i, 0)),
                  pl.BlockSpec((HALO, D_MODEL), lambda j, i: (jnp.minimum((i + 1) * hb, last_hb), 0)),
                  pl.BlockSpec((t, D_MODEL), lambda j, i: (i, 0)),
                  pl.BlockSpec((HALO, D_MODEL), lambda j, i: (jnp.maximum(i * hb - 1, 0), 0)),
                  pl.BlockSpec((HALO, D_MODEL), lambda j, i: (jnp.minimum((i + 1) * hb, last_hb), 0)),
                  _const_spec((1, D_MODEL)),
                  pl.BlockSpec((1, D_MODEL, D_FF), lambda j, i: (j, 0, 0), pipeline_mode=pl.Buffered(1)),
                  pl.BlockSpec((1, FFN_CONV, D_FF), lambda j, i: (j, 0, 0)),
                  pl.BlockSpec((1, 1, D_FF), lambda j, i: (j, 0, 0)),
                  pl.BlockSpec((1, FF_HALF, D_MODEL), lambda j, i: (j, 0, 0), pipeline_mode=pl.Buffered(1))],
        out_specs=(pl.BlockSpec((1, t, D_MODEL), lambda j, i: (j, i, 0)), pl.BlockSpec(memory_space=pl.ANY),
                   pl.BlockSpec(memory_space=pl.ANY), pl.BlockSpec((1, FFN_CONV, D_FF), lambda j, i: (j, 0, 0)),
                   pl.BlockSpec((1, 1, D_FF), lambda j, i: (j, 0, 0))),
        scratch_shapes=[pltpu.VMEM((2 * HALO + t, D_FF), f32), pltpu.VMEM((HALO + t, D_FF), f32),
                        pltpu.VMEM((D_MODEL, D_FF), f32), pltpu.VMEM((FF_HALF, D_MODEL), f32)],
        compiler_params=_cparams(("arbitrary", "arbitrary")),
    )(dh2, dh2, h1, h1, h1, g2, wup_h, cw_h, cb_h, wdn_h)


def _out_bwd(dh2, parts, h1, ymix, g2, w_out):
    s = h1.shape[0]
    t = _row_tile(s)
    n = s // t

    def body(dh2_ref, part_ref, h1_ref, ym_ref, g2_ref, wo_ref, dh1_ref, dym_ref, dwo_ref, dg2_ref):
        i = pl.program_id(0)

        @pl.when(i == 0)
        def _():
            dwo_ref[...] = jnp.zeros_like(dwo_ref)
            dg2_ref[...] = jnp.zeros_like(dg2_ref)

        g2 = g2_ref[...]
        _, xhat, r = _rms_fwd(h1_ref[...], g2)
        dx, dg = _rms_bwd(part_ref[0] + part_ref[1], xhat, r, g2)
        dg2_ref[...] += dg
        dh1 = dh2_ref[...] + dx
        dh1_ref[...] = dh1
        dh16 = dh1.astype(bf16)
        dym_ref[...] = _dot_nt(dh16, wo_ref[...]).astype(bf16)
        dwo_ref[...] += _dot_tn(ym_ref[...], dh16)

    row = lambda w: pl.BlockSpec((t, w), lambda i: (i, 0))
    fixed = lambda shape: pl.BlockSpec(shape, lambda i: (0,) * len(shape))
    return pl.pallas_call(
        body, name="out_bwd", grid=(n,),
        out_shape=(jax.ShapeDtypeStruct((s, D_MODEL), f32), jax.ShapeDtypeStruct((s, D_MODEL), bf16),
                   jax.ShapeDtypeStruct((D_MODEL, D_MODEL), f32), jax.ShapeDtypeStruct((1, D_MODEL), f32)),
        in_specs=[row(D_MODEL), pl.BlockSpec((2, t, D_MODEL), lambda i: (0, i, 0)), row(D_MODEL), row(D_MODEL),
                  _const_spec((1, D_MODEL)), _const_spec((D_MODEL, D_MODEL))],
        out_specs=(row(D_MODEL), row(D_MODEL), fixed((D_MODEL, D_MODEL)), fixed((1, D_MODEL))),
        compiler_params=_cparams(),
    )(dh2, parts, h1, ymix, g2, w_out)


def _loss_head(h, target, gf):
    s = h.shape[0]
    t = _row_tile(s)

    def body(h_ref, t_ref, g_ref, dh_ref, dg_ref, loss_ref):
        i = pl.program_id(0)

        @pl.when(i == 0)
        def _():
            dg_ref[...] = jnp.zeros_like(dg_ref)
            loss_ref[...] = jnp.zeros_like(loss_ref)

        g = g_ref[...]
        y, xhat, r = _rms_fwd(h_ref[...], g)
        diff = y - t_ref[...]
        per_row = jnp.mean(diff * diff, axis=-1, keepdims=True)
        loss_ref[...] += 0.5 * jnp.sum(per_row, axis=0, keepdims=True)
        dx, dg = _rms_bwd(diff * (1.0 / D_MODEL), xhat, r, g)
        dg_ref[...] += dg
        dh_ref[...] = dx

    row = pl.BlockSpec((t, D_MODEL), lambda i: (i, 0))
    return pl.pallas_call(
        body, name="loss_head", grid=(s // t,),
        out_shape=(jax.ShapeDtypeStruct((s, D_MODEL), f32), jax.ShapeDtypeStruct((1, D_MODEL), f32),
                   jax.ShapeDtypeStruct((1, LANES), f32)),
        in_specs=[row, row, _const_spec((1, D_MODEL))],
        out_specs=(row, pl.BlockSpec((1, D_MODEL), lambda i: (0, 0)), pl.BlockSpec((1, LANES), lambda i: (0, 0))),
        compiler_params=_cparams(),
    )(h, target, gf)


def _prep_layer(w):
    w_in = w["w_in"]
    w_r = jnp.concatenate([w_in[:, :SSD_W + XBC], w_in[:, SSD_W + XBC + HEADS:], w_in[:, SSD_W + XBC:SSD_W + XBC + HEADS],
                           jnp.zeros((D_MODEL, LANES - HEADS), w_in.dtype)], axis=1)
    pad8 = lambda v: jnp.concatenate([v, jnp.zeros((LANES - HEADS,), f32)]).reshape(1, LANES)
    halves = lambda a: jnp.stack([jnp.concatenate([a[..., j * FF_HALF:(j + 1) * FF_HALF],
                                                   a[..., D_FF + j * FF_HALF:D_FF + (j + 1) * FF_HALF]], axis=-1) for j in range(2)])
    ssd_prm = (w["ssd_conv_w"], w["ssd_conv_b"].reshape(1, XBC), pad8(w["ssd_dt_bias"]), pad8(w["ssd_a_log"]),
               jnp.repeat(w["ssd_d"], HEAD_DIM).reshape(1, SSD_W), w["ssd_norm_g"].reshape(1, SSD_W),
               w["pool_w"], w["pool_scale"].reshape(1, POOL_W))
    return dict(
        g1=w["mix_norm_g"].reshape(1, D_MODEL), w_r=w_r, ssd=ssd_prm, w_out=w["w_out"], g2=w["ffn_norm_g"].reshape(1, D_MODEL),
        wup_h=halves(w["ffn_w_up"]), cw_h=halves(w["ffn_conv_w"]), cb_h=halves(w["ffn_conv_b"].reshape(1, 2 * D_FF)),
        wdn_h=w["ffn_w_down"].reshape(2, FF_HALF, D_MODEL), g3=w["ple_norm_g"].reshape(1, D_MODEL),
        w_gate=w["ple_w_gate"], w_proj=w["ple_w_proj"])


def _unhalve(a):
    return jnp.concatenate([a[0][..., :FF_HALF], a[1][..., :FF_HALF], a[0][..., FF_HALF:], a[1][..., FF_HALF:]], axis=-1)


def _device_step(x, p, target, layers, final_g):
    preps = [_prep_layer(w) for w in layers]
    saved = []
    h = x
    for l, q in enumerate(preps):
        zxu, dtr = _mix_in_fwd(h, q["g1"], q["w_r"])
        ymix, states = _ssd_pool_fwd(zxu, dtr, q["ssd"])
        h1, h2, h3 = _out_ffn_ple_fwd(h, ymix, p[l], q["w_out"], q["g2"], q["wup_h"], q["cw_h"], q["cb_h"], q["wdn_h"],
                                      q["g3"], q["w_gate"], q["w_proj"])
        saved.append((h, zxu, dtr, ymix, states, h1, h2))
        h = h3
    dh, d_gf, loss = _loss_head(h, target, final_g.reshape(1, D_MODEL))
    grads = [None] * len(preps)
    for l in reversed(range(len(preps))):
        q = preps[l]
        h0, zxu, dtr, ymix, states, h1, h2 = saved[l]
        dh2, d_wg, d_wp, d_g3 = _ple_bwd(dh, h2, p[l], q["g3"], q["w_gate"], q["w_proj"])
        parts, d_wup, d_wdn, d_cw, d_cb = _ffn_bwd(dh2, h1, q["g2"], q["wup_h"], q["cw_h"], q["cb_h"], q["wdn_h"])
        dh1, d_ymix, d_wo, d_g2 = _out_bwd(dh2, parts, h1, ymix, q["g2"], q["w_out"])
        (d_zxu, d_dtr, d_scw, d_scb, d_dtb, d_alog, d_dsk, d_ng, d_pw, d_ps) = _ssd_pool_bwd(d_ymix, zxu, dtr, states, q["ssd"])
        dh, d_wr, d_g1 = _mix_in_bwd(d_zxu, d_dtr, h0, dh1, q["g1"], q["w_r"])
        grads[l] = dict(
            mix_norm_g=d_g1.reshape(D_MODEL),
            w_in=jnp.concatenate([d_wr[:, :SSD_W + XBC], d_wr[:, 2048:2048 + HEADS], d_wr[:, SSD_W + XBC:2048]], axis=1),
            ssd_conv_w=d_scw, ssd_conv_b=d_scb.reshape(XBC), ssd_dt_bias=d_dtb[0, :HEADS], ssd_a_log=d_alog[0, :HEADS],
            ssd_d=jnp.sum(d_dsk.reshape(HEADS, HEAD_DIM), axis=1), ssd_norm_g=d_ng.reshape(SSD_W), pool_w=d_pw,
            pool_scale=d_ps.reshape(POOL_W), w_out=d_wo, ffn_norm_g=d_g2.reshape(D_MODEL), ffn_w_up=_unhalve(d_wup),
            ffn_conv_w=_unhalve(d_cw), ffn_conv_b=_unhalve(d_cb).reshape(2 * D_FF), ffn_w_down=d_wdn.reshape(D_FF, D_MODEL),
            ple_norm_g=d_g3.reshape(D_MODEL), ple_w_gate=d_wg, ple_w_proj=d_wp)
    return loss, dh, grads, d_gf.reshape(D_MODEL)


MESH = pl.DeviceIdType.MESH
COLS = 1024
ADD_ROWS = 256
N_CHIPS = 4
WEIGHT_NAMES = ["mix_norm_g", "w_in", "ssd_conv_w", "ssd_conv_b", "ssd_dt_bias", "ssd_a_log", "ssd_d", "ssd_norm_g", "pool_w",
                "pool_scale", "w_out", "ffn_norm_g", "ffn_w_up", "ffn_conv_w", "ffn_conv_b", "ffn_w_down", "ple_norm_g",
                "ple_w_gate", "ple_w_proj", "final_norm_g"]
SHARD_AXIS = {"w_in": 2, "ssd_conv_w": 2, "w_out": 1, "ffn_w_up": 2, "ffn_conv_w": 2, "ffn_w_down": 1, "ple_w_gate": 1,
              "ple_w_proj": 2}
MATMUL_SHARDED = ["w_in", "w_out", "ffn_w_up", "ffn_w_down", "ple_w_gate", "ple_w_proj"]
F32_SHARDED = ["ssd_conv_w", "ffn_conv_w"]
REPLICATED = [n for n in WEIGHT_NAMES if n not in SHARD_AXIS]


def _mesh_pos():
    return lax.axis_index("x"), lax.axis_index("y"), lax.axis_index("c")


def _hbm():
    return pl.BlockSpec(memory_space=pl.ANY)


def _all_gather_chips(xs, name):
    r, cols = xs.shape
    hr = r // 2

    def body(x_ref, out_ref, send_sems, recv_sems, local_sem):
        x, y, c = _mesh_pos()
        me = 2 * x + y
        sib = (x, y, 1 - c)
        flips = [(1 - x, y), (x, 1 - y), (1 - x, 1 - y)]

        def rows(chip, half):
            return out_ref.at[chip, pl.ds(half * hr, hr), :]

        def copy(k, src, dst, to):
            return pltpu.make_async_remote_copy(src_ref=src, dst_ref=dst, send_sem=send_sems.at[k], recv_sem=recv_sems.at[k],
                                                device_id=to, device_id_type=MESH)

        mine = pltpu.make_async_copy(x_ref, out_ref.at[me], local_sem)
        mine.start()
        first = [copy(j, x_ref.at[pl.ds(c * hr, hr), :], rows(me, c), (fx, fy, c)) for j, (fx, fy) in enumerate(flips)]
        for cp in first:
            cp.start()
        passed = []
        for j, (fx, fy) in enumerate(flips):
            blk = rows(2 * fx + fy, c)
            copy(j, blk, blk, (fx, fy, c)).wait_recv()
            fwd = copy(3 + j, blk, blk, sib)
            fwd.start()
            passed.append(fwd)
        for j, (fx, fy) in enumerate(flips):
            blk = rows(2 * fx + fy, 1 - c)
            copy(3 + j, blk, blk, sib).wait_recv()
        for cp in first + passed:
            cp.wait_send()
        mine.wait()

    return pl.pallas_call(
        body, name=name, out_shape=jax.ShapeDtypeStruct((N_CHIPS, r, cols), xs.dtype),
        in_specs=[_hbm()], out_specs=_hbm(),
        scratch_shapes=[pltpu.SemaphoreType.DMA((6,)), pltpu.SemaphoreType.DMA((6,)), pltpu.SemaphoreType.DMA],
    )(xs)


def _rs_pair_exchange(g):
    _, r, cols = g.shape
    hr = r // 2

    def body(g_ref, land_ref, send_sem, recv_sem):
        x, y, c = _mesh_pos()
        cp = pltpu.make_async_remote_copy(src_ref=g_ref.at[:, pl.ds((1 - c) * hr, hr), :], dst_ref=land_ref, send_sem=send_sem,
                                          recv_sem=recv_sem, device_id=(x, y, 1 - c), device_id_type=MESH)
        cp.start()
        cp.wait()

    return pl.pallas_call(
        body, name="rs_pair_exchange", out_shape=jax.ShapeDtypeStruct((N_CHIPS, hr, cols), g.dtype),
        in_specs=[_hbm()], out_specs=_hbm(), scratch_shapes=[pltpu.SemaphoreType.DMA, pltpu.SemaphoreType.DMA],
    )(g)


def _rs_pair_add(g, land, c_idx):
    _, r, cols = g.shape
    hr = r // 2
    nt = hr // ADD_ROWS

    def body(c_ref, g_ref, l_ref, o_ref):
        o_ref[...] = (g_ref[...] + l_ref[...]).astype(bf16)

    return pl.pallas_call(
        body, name="rs_pair_add", out_shape=jax.ShapeDtypeStruct((N_CHIPS, hr, cols), bf16),
        grid_spec=pltpu.PrefetchScalarGridSpec(
            num_scalar_prefetch=1, grid=(N_CHIPS, nt),
            in_specs=[pl.BlockSpec((1, ADD_ROWS, cols), lambda k, i, c_ref: (k, c_ref[0] * nt + i, 0)),
                      pl.BlockSpec((1, ADD_ROWS, cols), lambda k, i, c_ref: (k, i, 0))],
            out_specs=pl.BlockSpec((1, ADD_ROWS, cols), lambda k, i, c_ref: (k, i, 0))),
        compiler_params=_cparams(("arbitrary", "arbitrary")),
    )(c_idx, g, land)


def _rs_chip_exchange(part):
    _, hr, cols = part.shape

    def body(p_ref, land_ref, send_sems, recv_sems):
        x, y, c = _mesh_pos()
        flips = [(1 - x, y), (x, 1 - y), (1 - x, 1 - y)]
        cps = [pltpu.make_async_remote_copy(src_ref=p_ref.at[2 * fx + fy], dst_ref=land_ref.at[j], send_sem=send_sems.at[j],
                                            recv_sem=recv_sems.at[j], device_id=(fx, fy, c), device_id_type=MESH)
               for j, (fx, fy) in enumerate(flips)]
        for cp in cps:
            cp.start()
        for cp in cps:
            cp.wait()

    return pl.pallas_call(
        body, name="rs_chip_exchange", out_shape=jax.ShapeDtypeStruct((3, hr, cols), part.dtype),
        in_specs=[_hbm()], out_specs=_hbm(), scratch_shapes=[pltpu.SemaphoreType.DMA((3,)), pltpu.SemaphoreType.DMA((3,))],
    )(part)


def _rs_chip_add(part, land, me_idx):
    _, hr, cols = part.shape
    nt = hr // ADD_ROWS

    def body(me_ref, p_ref, l_ref, o_ref):
        o_ref[...] = ((p_ref[0].astype(f32) + l_ref[0].astype(f32)) + l_ref[1].astype(f32)) + l_ref[2].astype(f32)

    return pl.pallas_call(
        body, name="rs_chip_add", out_shape=jax.ShapeDtypeStruct((hr, cols), f32),
        grid_spec=pltpu.PrefetchScalarGridSpec(
            num_scalar_prefetch=1, grid=(nt,),
            in_specs=[pl.BlockSpec((1, ADD_ROWS, cols), lambda i, me_ref: (me_ref[0], i, 0)),
                      pl.BlockSpec((3, ADD_ROWS, cols), lambda i, me_ref: (0, i, 0))],
            out_specs=pl.BlockSpec((ADD_ROWS, cols), lambda i, me_ref: (i, 0))),
        compiler_params=_cparams(),
    )(me_idx, part, land)


def _rs_pair_share(red):
    hr, cols = red.shape

    def body(r_ref, out_ref, send_sem, recv_sem, local_sem):
        x, y, c = _mesh_pos()
        mine = pltpu.make_async_copy(r_ref, out_ref.at[pl.ds(c * hr, hr), :], local_sem)
        mine.start()
        cp = pltpu.make_async_remote_copy(src_ref=r_ref, dst_ref=out_ref.at[pl.ds(c * hr, hr), :], send_sem=send_sem,
                                          recv_sem=recv_sem, device_id=(x, y, 1 - c), device_id_type=MESH)
        cp.start()
        pltpu.make_async_remote_copy(src_ref=r_ref, dst_ref=out_ref.at[pl.ds((1 - c) * hr, hr), :], send_sem=send_sem,
                                     recv_sem=recv_sem, device_id=(x, y, 1 - c), device_id_type=MESH).wait_recv()
        cp.wait_send()
        mine.wait()

    return pl.pallas_call(
        body, name="rs_pair_share", out_shape=jax.ShapeDtypeStruct((2 * hr, cols), red.dtype),
        in_specs=[_hbm()], out_specs=_hbm(),
        scratch_shapes=[pltpu.SemaphoreType.DMA, pltpu.SemaphoreType.DMA, pltpu.SemaphoreType.DMA],
    )(red)


def _reduce_scatter(g):
    x, y, c = _mesh_pos()
    land = _rs_pair_exchange(g)
    part = _rs_pair_add(g, land, jnp.reshape(c, (1,)).astype(jnp.int32))
    land2 = _rs_chip_exchange(part)
    red = _rs_chip_add(part, land2, jnp.reshape(2 * x + y, (1,)).astype(jnp.int32))
    return _rs_pair_share(red)


def _pack(parts, row_multiple):
    flat = jnp.concatenate([a.reshape(-1) for a in parts])
    n = flat.shape[0]
    rows = -(-n // COLS)
    rows = -(-rows // row_multiple) * row_multiple
    return jnp.pad(flat, (0, rows * COLS - n)).reshape(rows, COLS)


def _unpack(flat, shapes):
    out, off = [], 0
    for shp in shapes:
        n = math.prod(shp)
        out.append(flat[off:off + n].reshape(shp))
        off += n
    return out


def _adamw(w, g, m, v, name):
    shape = w.shape
    cols = shape[-1]
    rows = math.prod(shape[:-1]) if len(shape) > 1 else 1
    tr = rows
    if rows > 512:
        tr = next(t for t in (512, 256, 128, 64, 32, 16, 8) if rows % t == 0)
    two_d = lambda a: a.reshape(rows, cols)

    def body(w_ref, g_ref, m_ref, v_ref, d_ref, nm_ref, nv_ref):
        gg = g_ref[...]
        nm = ADAM_B1 * m_ref[...] + (1.0 - ADAM_B1) * gg
        nv = ADAM_B2 * v_ref[...] + (1.0 - ADAM_B2) * (gg * gg)
        m_hat = nm / (1.0 - ADAM_B1 ** ADAM_STEP)
        v_hat = nv / (1.0 - ADAM_B2 ** ADAM_STEP)
        d_ref[...] = -ADAM_LR * (m_hat / (jnp.sqrt(v_hat) + ADAM_EPS) + ADAM_WD * w_ref[...])
        nm_ref[...] = nm
        nv_ref[...] = nv

    spec = pl.BlockSpec((tr, cols), lambda i: (i, 0))
    outs = pl.pallas_call(
        body, name="adamw_" + name, grid=(rows // tr,),
        out_shape=tuple(jax.ShapeDtypeStruct((rows, cols), f32) for _ in range(3)),
        in_specs=[spec] * 4, out_specs=(spec,) * 3, compiler_params=_cparams(),
    )(two_d(w), two_d(g), two_d(m), two_d(v))
    return tuple(o.reshape(shape) for o in outs)


def _chip_slice(a, axis, k):
    n = a.shape[axis] // N_CHIPS
    return lax.slice_in_dim(a, k * n, (k + 1) * n, axis=axis)


def _train_step(x, p, loss_target, w, m, v):
    as_pairs = lambda a: lax.bitcast_convert_type(a, bf16)
    wbuf = _pack([w[n].astype(bf16) for n in MATMUL_SHARDED] + [as_pairs(w[n]) for n in F32_SHARDED], 32)
    gathered = _all_gather_chips(wbuf, "gather_weights").reshape(N_CHIPS, -1)
    shapes = [w[n].shape for n in MATMUL_SHARDED] + [w[n].shape + (2,) for n in F32_SHARDED]
    per_chip = [_unpack(gathered[k], shapes) for k in range(N_CHIPS)]
    full = {}
    for idx, n in enumerate(MATMUL_SHARDED + F32_SHARDED):
        pieces = [per_chip[k][idx] for k in range(N_CHIPS)]
        if n in F32_SHARDED:
            pieces = [lax.bitcast_convert_type(a, f32) for a in pieces]
        full[n] = jnp.concatenate(pieces, axis=SHARD_AXIS[n])
    for n in REPLICATED:
        full[n] = w[n]
    full["pool_w"] = w["pool_w"].astype(bf16)
    layers = [{n: full[n][l] for n in WEIGHT_NAMES if n != "final_norm_g"} for l in range(DEPTH)]

    loss_part, grad_x, grads, d_gf = _device_step(x[0], p[:, 0], loss_target[0], layers, w["final_norm_g"])

    stacked = {n: jnp.stack([grads[l][n] for l in range(DEPTH)]) for n in grads[0]}
    stacked["final_norm_g"] = d_gf
    sharded = list(SHARD_AXIS)
    small = _pack([stacked[n] for n in REPLICATED], 32 * N_CHIPS)
    small_rows = small.shape[0] // N_CHIPS
    gbuf = jnp.stack([_pack([_chip_slice(stacked[n], SHARD_AXIS[n], k) for n in sharded]
                            + [small[k * small_rows:(k + 1) * small_rows]], 2 * ADD_ROWS) for k in range(N_CHIPS)])
    red = _reduce_scatter(gbuf).reshape(-1)
    red_parts = _unpack(red, [w[n].shape for n in sharded] + [(small_rows, COLS)])
    grad = dict(zip(sharded, red_parts[:-1]))
    small_all = _all_gather_chips(red_parts[-1], "gather_small_grads").reshape(-1)
    grad.update(zip(REPLICATED, _unpack(small_all, [w[n].shape for n in REPLICATED])))

    loss = lax.psum(loss_part[0, 0], ("x", "y", "c"))
    delta, new_m, new_v = {}, {}, {}
    for n in WEIGHT_NAMES:
        delta[n], new_m[n], new_v[n] = _adamw(w[n], grad[n], m[n], v[n], n)
    return (loss, grad_x[None], *[grad[n] for n in WEIGHT_NAMES], *[delta[n] for n in WEIGHT_NAMES],
            *[new_m[n] for n in WEIGHT_NAMES], *[new_v[n] for n in WEIGHT_NAMES])


def kernel(x, p, mix_norm_g, w_in, ssd_conv_w, ssd_conv_b, ssd_dt_bias, ssd_a_log, ssd_d, ssd_norm_g, pool_w, pool_scale, w_out, ffn_norm_g, ffn_w_up, ffn_conv_w, ffn_conv_b, ffn_w_down, ple_norm_g, ple_w_gate, ple_w_proj, final_norm_g, loss_target, m_mix_norm_g, m_w_in, m_ssd_conv_w, m_ssd_conv_b, m_ssd_dt_bias, m_ssd_a_log, m_ssd_d, m_ssd_norm_g, m_pool_w, m_pool_scale, m_w_out, m_ffn_norm_g, m_ffn_w_up, m_ffn_conv_w, m_ffn_conv_b, m_ffn_w_down, m_ple_norm_g, m_ple_w_gate, m_ple_w_proj, m_final_norm_g, v_mix_norm_g, v_w_in, v_ssd_conv_w, v_ssd_conv_b, v_ssd_dt_bias, v_ssd_a_log, v_ssd_d, v_ssd_norm_g, v_pool_w, v_pool_scale, v_w_out, v_ffn_norm_g, v_ffn_w_up, v_ffn_conv_w, v_ffn_conv_b, v_ffn_w_down, v_ple_norm_g, v_ple_w_gate, v_ple_w_proj, v_final_norm_g):
    w = dict(mix_norm_g=mix_norm_g, w_in=w_in, ssd_conv_w=ssd_conv_w, ssd_conv_b=ssd_conv_b, ssd_dt_bias=ssd_dt_bias, ssd_a_log=ssd_a_log, ssd_d=ssd_d, ssd_norm_g=ssd_norm_g, pool_w=pool_w, pool_scale=pool_scale, w_out=w_out, ffn_norm_g=ffn_norm_g, ffn_w_up=ffn_w_up, ffn_conv_w=ffn_conv_w, ffn_conv_b=ffn_conv_b, ffn_w_down=ffn_w_down, ple_norm_g=ple_norm_g, ple_w_gate=ple_w_gate, ple_w_proj=ple_w_proj, final_norm_g=final_norm_g)
    m = dict(mix_norm_g=m_mix_norm_g, w_in=m_w_in, ssd_conv_w=m_ssd_conv_w, ssd_conv_b=m_ssd_conv_b, ssd_dt_bias=m_ssd_dt_bias, ssd_a_log=m_ssd_a_log, ssd_d=m_ssd_d, ssd_norm_g=m_ssd_norm_g, pool_w=m_pool_w, pool_scale=m_pool_scale, w_out=m_w_out, ffn_norm_g=m_ffn_norm_g, ffn_w_up=m_ffn_w_up, ffn_conv_w=m_ffn_conv_w, ffn_conv_b=m_ffn_conv_b, ffn_w_down=m_ffn_w_down, ple_norm_g=m_ple_norm_g, ple_w_gate=m_ple_w_gate, ple_w_proj=m_ple_w_proj, final_norm_g=m_final_norm_g)
    v = dict(mix_norm_g=v_mix_norm_g, w_in=v_w_in, ssd_conv_w=v_ssd_conv_w, ssd_conv_b=v_ssd_conv_b, ssd_dt_bias=v_ssd_dt_bias, ssd_a_log=v_ssd_a_log, ssd_d=v_ssd_d, ssd_norm_g=v_ssd_norm_g, pool_w=v_pool_w, pool_scale=v_pool_scale, w_out=v_w_out, ffn_norm_g=v_ffn_norm_g, ffn_w_up=v_ffn_w_up, ffn_conv_w=v_ffn_conv_w, ffn_conv_b=v_ffn_conv_b, ffn_w_down=v_ffn_w_down, ple_norm_g=v_ple_norm_g, ple_w_gate=v_ple_w_gate, ple_w_proj=v_ple_w_proj, final_norm_g=v_final_norm_g)
    return _train_step(x, p, loss_target, w, m, v)
```

```python
import functools
import math

import jax
import jax.numpy as jnp
from jax import lax
from jax.experimental import pallas as pl
from jax.experimental.pallas import tpu as pltpu

f32, bf16 = jnp.float32, jnp.bfloat16
HI = lax.Precision.HIGHEST

D_MODEL = 1024
D_PLE = 256
DEPTH = 4
SSD_W = 512
HEADS = 8
HEAD_DIM = 64
NSTATE = 128
CHUNK = 128
SSD_CONV = 4
XBC = 1024
POOL_W = 512
POOL_G = 128
WINDOWS = (2, 4, 8, 16)
D_FF = 2816
FF_HALF = D_FF // 2
FFN_CONV = 3
D_IN = 2056
EPS = 1e-6
ADAM_LR, ADAM_B1, ADAM_B2, ADAM_EPS, ADAM_WD, ADAM_STEP = 0.001, 0.9, 0.999, 1e-08, 0.01, 10

LANES = 128
NPROJ = 2048 + LANES
HALO = 16
FHALO = 8
VMEM_LIMIT = 58 * 1024 * 1024
ROW_TILE = 256


def _dot(a, b):
    return jnp.dot(a, b, preferred_element_type=f32)


def _dot_nt(a, b):
    return lax.dot_general(a, b, (((1,), (1,)), ((), ())), preferred_element_type=f32)


def _dot_tn(a, b):
    return lax.dot_general(a, b, (((0,), (0,)), ((), ())), preferred_element_type=f32)


def _dot_hi(a, b):
    return jnp.dot(a, b, precision=HI, preferred_element_type=f32)


def _dot_nt_hi(a, b):
    return lax.dot_general(a, b, (((1,), (1,)), ((), ())), precision=HI, preferred_element_type=f32)


def _rms_fwd(x, g):
    r = lax.rsqrt(jnp.mean(x * x, axis=-1, keepdims=True) + EPS)
    xhat = x * r
    return xhat * g, xhat, r


def _rms_bwd(dy, xhat, r, g):
    dxhat = dy * g
    dx = r * (dxhat - xhat * jnp.mean(dxhat * xhat, axis=-1, keepdims=True))
    return dx, jnp.sum(dy * xhat, axis=0, keepdims=True)


def _sigmoid(x):
    return 1.0 / (1.0 + jnp.exp(-x))


_GELU_C = math.sqrt(2.0 / math.pi)


def _gelu_and_grad(x):
    x2 = x * x
    t = jnp.tanh(_GELU_C * (x + 0.044715 * x * x2))
    g = 0.5 * x * (1.0 + t)
    dg = 0.5 * (1.0 + t) + 0.5 * x * (1.0 - t * t) * _GELU_C * (1.0 + 3.0 * 0.044715 * x2)
    return g, dg


def _gelu(x):
    return 0.5 * x * (1.0 + jnp.tanh(_GELU_C * (x + 0.044715 * x * x * x)))


def _softplus(x):
    return jnp.maximum(x, 0.0) + jnp.log(1.0 + jnp.exp(-jnp.abs(x)))


def _cparams(sem=("arbitrary",)):
    return pltpu.CompilerParams(dimension_semantics=sem, vmem_limit_bytes=VMEM_LIMIT)


def _const_spec(shape):
    nd = len(shape)
    return pl.BlockSpec(shape, lambda *_: (0,) * nd, pipeline_mode=pl.Buffered(1))


WIDE_ROW_TILE = 512


def _row_tile(s, t=ROW_TILE):
    return min(t, s)


def _mix_in_fwd(h, g1, w_r):
    s = h.shape[0]
    t = _row_tile(s, WIDE_ROW_TILE)

    def body(h_ref, g_ref, w_ref, zxu_ref, dtr_ref):
        hn, _, _ = _rms_fwd(h_ref[...], g_ref[...])
        proj = _dot(hn.astype(bf16), w_ref[...])
        zxu_ref[...] = proj[:, :2048].astype(bf16)
        dtr_ref[...] = proj[:, 2048:]

    return pl.pallas_call(
        body, name="mix_in_fwd", grid=(s // t,),
        out_shape=(jax.ShapeDtypeStruct((s, 2048), bf16), jax.ShapeDtypeStruct((s, LANES), f32)),
        in_specs=[pl.BlockSpec((t, D_MODEL), lambda i: (i, 0)), _const_spec((1, D_MODEL)), _const_spec((D_MODEL, NPROJ))],
        out_specs=(pl.BlockSpec((t, 2048), lambda i: (i, 0)), pl.BlockSpec((t, LANES), lambda i: (i, 0))),
        compiler_params=_cparams(),
    )(h, g1, w_r)


def _mix_in_bwd(d_zxu, d_dtr, h, dh1, g1, w_r):
    s = h.shape[0]
    t = _row_tile(s, WIDE_ROW_TILE)
    n = s // t

    def body(dz_ref, dd_ref, h_ref, dh1_ref, g_ref, w_ref, dh_ref, dw_ref, dg_ref, acc):
        i = pl.program_id(0)

        @pl.when(i == 0)
        def _():
            acc[...] = jnp.zeros_like(acc)
            dg_ref[...] = jnp.zeros_like(dg_ref)

        g = g_ref[...]
        hn, xhat, r = _rms_fwd(h_ref[...], g)
        dproj = jnp.concatenate([dz_ref[...], dd_ref[...].astype(bf16)], axis=1)
        d_hn = _dot_nt(dproj, w_ref[...])
        acc[...] += _dot_tn(hn.astype(bf16), dproj)
        dx, dg = _rms_bwd(d_hn, xhat, r, g)
        dg_ref[...] += dg
        dh_ref[...] = dh1_ref[...] + dx

        @pl.when(i == n - 1)
        def _():
            pltpu.sync_copy(acc, dw_ref)

    return pl.pallas_call(
        body, name="mix_in_bwd", grid=(n,),
        out_shape=(jax.ShapeDtypeStruct((s, D_MODEL), f32), jax.ShapeDtypeStruct((D_MODEL, NPROJ), f32),
                   jax.ShapeDtypeStruct((1, D_MODEL), f32)),
        in_specs=[pl.BlockSpec((t, 2048), lambda i: (i, 0)), pl.BlockSpec((t, LANES), lambda i: (i, 0)),
                  pl.BlockSpec((t, D_MODEL), lambda i: (i, 0)), pl.BlockSpec((t, D_MODEL), lambda i: (i, 0)),
                  _const_spec((1, D_MODEL)), _const_spec((D_MODEL, NPROJ))],
        out_specs=(pl.BlockSpec((t, D_MODEL), lambda i: (i, 0)), pl.BlockSpec(memory_space=pl.ANY),
                   pl.BlockSpec((1, D_MODEL), lambda i: (0, 0))),
        scratch_shapes=[pltpu.VMEM((D_MODEL, NPROJ), f32)],
        compiler_params=_cparams(),
    )(d_zxu, d_dtr, h, dh1, g1, w_r)


def _iota2(shape, dim):
    return lax.broadcasted_iota(jnp.int32, shape, dim)


def _lane_bcast(a, h):
    return jnp.broadcast_to(a[:, h:h + 1], (a.shape[0], LANES))


def _to_columns(cols):
    lane = _iota2((cols[0].shape[0], LANES), 1)
    out = jnp.where(lane == 0, cols[0], 0.0)
    for h in range(1, len(cols)):
        out = out + jnp.where(lane == h, cols[h], 0.0)
    return out


def _ssd_pre(zx, dtr, xext, cw, cb, dtb, alog):
    c = cb + cw[0:1] * xext[pl.ds(HALO - 3, CHUNK), :]
    for k in range(1, SSD_CONV):
        c = c + cw[k:k + 1] * xext[pl.ds(HALO - 3 + k, CHUNK), :]
    sig_c = _sigmoid(c)
    xc = c * sig_c
    dt = _softplus(dtr + dtb)
    a_neg = -jnp.exp(alog)
    a = dt * a_neg
    lo = _iota2((CHUNK, LANES), 1) < HEAD_DIM
    dt_w = jnp.concatenate([jnp.where(lo, _lane_bcast(dt, 2 * pp), _lane_bcast(dt, 2 * pp + 1)) for pp in range(4)], axis=1)
    xs = xc[:, :SSD_W]
    xd = xs * dt_w
    tril = (_iota2((CHUNK, CHUNK), 0) >= _iota2((CHUNK, CHUNK), 1))
    acs = _dot_hi(tril.astype(f32), a)
    acs_b = [_lane_bcast(acs, h) for h in range(HEADS)]
    return dict(c=c, sig_c=sig_c, xc=xc, xs=xs, dt=dt, a_neg=a_neg, a=a, dt_w=dt_w, xd=xd, acs_b=acs_b, tril=tril, lo=lo)


def _pair_fwd(q, pp, s_in):
    g = pp // 2
    lo = q["lo"]
    b_g = q["xc"][:, SSD_W + g * NSTATE:SSD_W + (g + 1) * NSTATE]
    c_g = q["xc"][:, SSD_W + 2 * NSTATE + g * NSTATE:SSD_W + 2 * NSTATE + (g + 1) * NSTATE]
    b16, c16 = b_g.astype(bf16), c_g.astype(bf16)
    gmat = _dot_nt(c16, b16)
    xp = q["xd"][:, pp * LANES:(pp + 1) * LANES]
    xp16 = xp.astype(bf16)
    ab0 = q["acs_b"][2 * pp]
    ab1 = q["acs_b"][2 * pp + 1]
    ls, ms, ys = [], [], []
    for ab in (ab0, ab1):
        lmat = jnp.exp(jnp.where(q["tril"], ab - ab.T, -jnp.inf))
        mmat = gmat * lmat
        ls.append(lmat)
        ms.append(mmat)
        ys.append(_dot(mmat.astype(bf16), xp16))
    y_diag = jnp.where(lo, ys[0], ys[1])
    ab_pair = jnp.where(lo, ab0, ab1)
    e_pair = jnp.exp(ab_pair)
    s16 = s_in.astype(bf16)
    y_off = _dot(c16, s16) * e_pair
    alast = ab_pair[CHUNK - 1:CHUNK, :]
    dec_pair = jnp.exp(alast - ab_pair)
    xdec = xp * dec_pair
    st = _dot_tn(b16, xdec.astype(bf16))
    cd_pair = jnp.exp(alast)
    s_out = s_in * cd_pair + st
    return dict(b16=b16, c16=c16, gmat=gmat, xp=xp, xp16=xp16, ls=ls, ms=ms, y=y_diag + y_off, y_off=y_off,
                e_pair=e_pair, dec_pair=dec_pair, xdec=xdec, cd_pair=cd_pair, s_out=s_out, s16=s16, lo=lo)


def _gate_norm_fwd(y_pre, z, ng):
    sz = _sigmoid(z)
    yg = y_pre * (z * sz)
    outs, stats = [], []
    half = SSD_W // 2
    for gi in range(2):
        o, xhat, r = _rms_fwd(yg[:, gi * half:(gi + 1) * half], ng[:, gi * half:(gi + 1) * half])
        outs.append(o)
        stats.append((xhat, r))
    return jnp.concatenate(outs, axis=1), sz, stats


def _pool_fwd(uext, u, row0, pw_ref, ps):
    pos = (row0 + _iota2((CHUNK, 1), 0) + 1).astype(f32)
    pooled, mixed, invs = [], [], []
    for gi, w in enumerate(WINDOWS):
        sl = slice(gi * POOL_G, (gi + 1) * POOL_G)
        acc = uext[pl.ds(HALO, CHUNK), sl]
        for j in range(1, w):
            acc = acc + uext[pl.ds(HALO - j, CHUNK), sl]
        den = jnp.minimum(pos, float(w))
        pg = acc / den - u[:, sl]
        pooled.append(pg)
        invs.append(den)
        mixed.append(_dot(pg.astype(bf16), pw_ref[gi]))
    mixed = jnp.concatenate(mixed, axis=1)
    return mixed * ps, pooled, mixed, invs


def _ssd_specs(s):
    nc = s // CHUNK
    hb = CHUNK // HALO
    return nc, hb


def _ssd_param_specs():
    return [_const_spec((SSD_CONV, XBC)), _const_spec((1, XBC)), _const_spec((1, LANES)), _const_spec((1, LANES)),
            _const_spec((1, SSD_W)), _const_spec((1, SSD_W)), _const_spec((4, POOL_G, POOL_G)), _const_spec((1, POOL_W))]


def _ssd_pool_fwd(zxu, dtr, prm):
    s = zxu.shape[0]
    nc, hb = _ssd_specs(s)

    def body(zx_ref, halo_ref, dtr_ref, cw_ref, cb_ref, dtb_ref, alog_ref, dsk_ref, ng_ref, pw_ref, ps_ref,
             ymix_ref, st_ref, state, xext, uext):
        i = pl.program_id(0)

        @pl.when(i == 0)
        def _():
            state[...] = jnp.zeros_like(state)

        zx = zx_ref[...].astype(f32)
        halo = jnp.where(i > 0, halo_ref[...].astype(f32), 0.0)
        xext[0:HALO, :] = halo[:, SSD_W:SSD_W + XBC]
        xext[HALO:, :] = zx[:, SSD_W:SSD_W + XBC]
        uext[0:HALO, :] = halo[:, SSD_W + XBC:]
        uext[HALO:, :] = zx[:, SSD_W + XBC:]
        q = _ssd_pre(zx, dtr_ref[...], xext, cw_ref[...], cb_ref[...], dtb_ref[...], alog_ref[...])
        ys = []
        for pp in range(4):
            s_in = state[pp]
            st_ref[0, pp] = s_in
            r = _pair_fwd(q, pp, s_in)
            state[pp] = r["s_out"]
            ys.append(r["y"])
        y_pre = jnp.concatenate(ys, axis=1) + q["xs"] * dsk_ref[...]
        y_ssd, _, _ = _gate_norm_fwd(y_pre, zx[:, :SSD_W], ng_ref[...])
        y_pool, _, _, _ = _pool_fwd(uext, zx[:, SSD_W + XBC:], i * CHUNK, pw_ref, ps_ref[...])
        ymix_ref[:, :SSD_W] = y_ssd.astype(bf16)
        ymix_ref[:, SSD_W:] = y_pool.astype(bf16)

    return pl.pallas_call(
        body, name="ssd_pool_fwd", grid=(nc,),
        out_shape=(jax.ShapeDtypeStruct((s, D_MODEL), bf16), jax.ShapeDtypeStruct((nc, 4, NSTATE, LANES), f32)),
        in_specs=[pl.BlockSpec((CHUNK, 2048), lambda i: (i, 0)),
                  pl.BlockSpec((HALO, 2048), lambda i: (jnp.maximum(i * hb - 1, 0), 0)),
                  pl.BlockSpec((CHUNK, LANES), lambda i: (i, 0))] + _ssd_param_specs(),
        out_specs=(pl.BlockSpec((CHUNK, D_MODEL), lambda i: (i, 0)),
                   pl.BlockSpec((1, 4, NSTATE, LANES), lambda i: (i, 0, 0, 0))),
        scratch_shapes=[pltpu.VMEM((4, NSTATE, LANES), f32), pltpu.VMEM((HALO + CHUNK, XBC), f32),
                        pltpu.VMEM((HALO + CHUNK, POOL_W), f32)],
        compiler_params=_cparams(),
    )(zxu, zxu, dtr, *prm)


def _ssd_pool_bwd(d_ymix, zxu, dtr, states, prm):
    s = zxu.shape[0]
    nc, hb = _ssd_specs(s)
    rev = lambda i: nc - 1 - i

    def body(dy_ref, zx_ref, halo_ref, dtr_ref, st_ref, cw_ref, cb_ref, dtb_ref, alog_ref, dsk_ref, ng_ref, pw_ref, ps_ref,
             dzx_ref, ddtr_ref, dcw_ref, dcb_ref, ddtb_ref, dalog_ref, ddsk_ref, dng_ref, dpw_ref, dps_ref,
             dstate, xext, uext, dxext, duext, cx, cu):
        i = pl.program_id(0)
        ci = nc - 1 - i

        @pl.when(i == 0)
        def _():
            dstate[...] = jnp.zeros_like(dstate)
            cx[...] = jnp.zeros_like(cx)
            cu[...] = jnp.zeros_like(cu)
            for r in (dcw_ref, dcb_ref, ddtb_ref, dalog_ref, ddsk_ref, dng_ref, dpw_ref, dps_ref):
                r[...] = jnp.zeros_like(r)

        zx = zx_ref[...].astype(f32)
        halo = jnp.where(ci > 0, halo_ref[...].astype(f32), 0.0)
        xext[0:HALO, :] = halo[:, SSD_W:SSD_W + XBC]
        xext[HALO:, :] = zx[:, SSD_W:SSD_W + XBC]
        uext[0:HALO, :] = halo[:, SSD_W + XBC:]
        uext[HALO:, :] = zx[:, SSD_W + XBC:]
        cw = cw_ref[...]
        q = _ssd_pre(zx, dtr_ref[...], xext, cw, cb_ref[...], dtb_ref[...], alog_ref[...])
        z = zx[:, :SSD_W]
        dy = dy_ref[...].astype(f32)
        d_yssd, d_ypool = dy[:, :SSD_W], dy[:, SSD_W:]

        pairs = [_pair_fwd(q, pp, st_ref[0, pp]) for pp in range(4)]
        dsk = dsk_ref[...]
        ng = ng_ref[...]
        y_pre = jnp.concatenate([r["y"] for r in pairs], axis=1) + q["xs"] * dsk
        _, sz, stats = _gate_norm_fwd(y_pre, z, ng)

        half = SSD_W // 2
        d_yg, d_ng = [], []
        for gi in range(2):
            xhat, r = stats[gi]
            dx, dg = _rms_bwd(d_yssd[:, gi * half:(gi + 1) * half], xhat, r, ng[:, gi * half:(gi + 1) * half])
            d_yg.append(dx)
            d_ng.append(dg)
        d_yg = jnp.concatenate(d_yg, axis=1)
        dng_ref[...] += jnp.concatenate(d_ng, axis=1)
        silu_z = z * sz
        d_ypre = d_yg * silu_z
        d_z = d_yg * y_pre * (sz * (1.0 + z * (1.0 - sz)))
        ddsk_ref[...] += jnp.sum(d_ypre * q["xs"], axis=0, keepdims=True)

        d_xd, acs_cols, dt_cols = [], [], []
        d_b = [None, None]
        d_c = [None, None]
        d_g = [None, None]
        last_row = _iota2((CHUNK, LANES), 0) == CHUNK - 1
        for pp in range(4):
            g = pp // 2
            r = pairs[pp]
            lo = r["lo"]
            dyp = d_ypre[:, pp * LANES:(pp + 1) * LANES]
            dyp16 = dyp.astype(bf16)
            ds_out = dstate[pp]
            ds16 = ds_out.astype(bf16)
            dye16 = (dyp * r["e_pair"]).astype(bf16)
            dstate[pp] = r["cd_pair"] * ds_out + _dot_tn(r["c16"], dye16)
            dc = _dot_nt(dye16, r["s16"])
            dxdec = _dot(r["b16"], ds16)
            db = _dot_nt(r["xdec"].astype(bf16), ds16)
            dxp = dxdec * r["dec_pair"]
            t2 = dxdec * r["xdec"]
            tail = jnp.sum(t2, axis=0, keepdims=True) + jnp.sum(ds_out * st_ref[0, pp] * r["cd_pair"], axis=0, keepdims=True)
            rp = dyp * r["y_off"] - t2 + jnp.where(last_row, tail, 0.0)
            dxs = []
            for hh in range(2):
                msk = lo if hh == 0 else jnp.logical_not(lo)
                m16 = r["ms"][hh].astype(bf16)
                dxs.append(_dot_tn(m16, dyp16))
                dm = _dot_nt(jnp.where(msk, dyp, 0.0).astype(bf16), r["xp16"])
                wmat = dm * r["ms"][hh]
                acs_cols.append(jnp.sum(wmat - wmat.T + jnp.where(msk, rp, 0.0), axis=1, keepdims=True))
                dgh = dm * r["ls"][hh]
                d_g[g] = dgh if d_g[g] is None else d_g[g] + dgh
            dxp = dxp + jnp.where(lo, dxs[0], dxs[1])
            d_xd.append(dxp)
            xprod = dxp * q["xs"][:, pp * LANES:(pp + 1) * LANES]
            dt_cols.append(jnp.sum(jnp.where(lo, xprod, 0.0), axis=1, keepdims=True))
            dt_cols.append(jnp.sum(jnp.where(lo, 0.0, xprod), axis=1, keepdims=True))
            d_b[g] = db if d_b[g] is None else d_b[g] + db
            d_c[g] = dc if d_c[g] is None else d_c[g] + dc
        for g in range(2):
            dg16 = d_g[g].astype(bf16)
            d_c[g] = d_c[g] + _dot(dg16, pairs[2 * g]["b16"])
            d_b[g] = d_b[g] + _dot_tn(dg16, pairs[2 * g]["c16"])
        d_xd = jnp.concatenate(d_xd, axis=1)
        triu = (_iota2((CHUNK, CHUNK), 0) <= _iota2((CHUNK, CHUNK), 1)).astype(f32)
        d_a = _dot_hi(triu, _to_columns(acs_cols))
        d_dt = d_a * q["a_neg"] + _to_columns(dt_cols)
        dalog_ref[...] += jnp.sum(d_a * q["dt"], axis=0, keepdims=True) * q["a_neg"]
        d_dtr = d_dt * _sigmoid(dtr_ref[...] + dtb_ref[...])
        ddtr_ref[...] = d_dtr
        ddtb_ref[...] += jnp.sum(d_dtr, axis=0, keepdims=True)
        d_xs = d_ypre * dsk + d_xd * q["dt_w"]

        d_xc = jnp.concatenate([d_xs, d_b[0], d_b[1], d_c[0], d_c[1]], axis=1)
        sc = q["sig_c"]
        d_conv = d_xc * (sc * (1.0 + q["c"] * (1.0 - sc)))
        dcb_ref[...] += jnp.sum(d_conv, axis=0, keepdims=True)
        dxext[...] = jnp.zeros_like(dxext)
        for k in range(SSD_CONV):
            dcw_ref[k:k + 1, :] += jnp.sum(d_conv * xext[pl.ds(HALO - 3 + k, CHUNK), :], axis=0, keepdims=True)
            dxext[pl.ds(HALO - 3 + k, CHUNK), :] += cw[k:k + 1] * d_conv
        dxext[pl.ds(CHUNK, HALO), :] += cx[...]
        cx[...] = dxext[0:HALO, :]

        ps = ps_ref[...]
        u = zx[:, SSD_W + XBC:]
        _, pooled, mixed, dens = _pool_fwd(uext, u, ci * CHUNK, pw_ref, ps)
        dps_ref[...] += jnp.sum(d_ypool * mixed, axis=0, keepdims=True)
        d_mixed = d_ypool * ps
        duext[...] = jnp.zeros_like(duext)
        for gi, w in enumerate(WINDOWS):
            sl = slice(gi * POOL_G, (gi + 1) * POOL_G)
            dm16 = d_mixed[:, sl].astype(bf16)
            dpw_ref[gi] += _dot_tn(pooled[gi].astype(bf16), dm16)
            d_pg = _dot_nt(dm16, pw_ref[gi])
            d_mean = d_pg / dens[gi]
            duext[pl.ds(HALO, CHUNK), sl] += d_mean - d_pg
            for j in range(1, w):
                duext[pl.ds(HALO - j, CHUNK), sl] += d_mean
        duext[pl.ds(CHUNK, HALO), :] += cu[...]
        cu[...] = duext[0:HALO, :]

        dzx_ref[:, :SSD_W] = d_z.astype(bf16)
        dzx_ref[:, SSD_W:SSD_W + XBC] = dxext[HALO:, :].astype(bf16)
        dzx_ref[:, SSD_W + XBC:] = duext[HALO:, :].astype(bf16)

    small = lambda shape: pl.BlockSpec(shape, lambda i: (0,) * len(shape))
    small_shapes = [(SSD_CONV, XBC), (1, XBC), (1, LANES), (1, LANES), (1, SSD_W), (1, SSD_W), (4, POOL_G, POOL_G), (1, POOL_W)]
    return pl.pallas_call(
        body, name="ssd_pool_bwd", grid=(nc,),
        out_shape=(jax.ShapeDtypeStruct((s, 2048), bf16), jax.ShapeDtypeStruct((s, LANES), f32))
        + tuple(jax.ShapeDtypeStruct(sh, f32) for sh in small_shapes),
        in_specs=[pl.BlockSpec((CHUNK, D_MODEL), lambda i: (rev(i), 0)),
                  pl.BlockSpec((CHUNK, 2048), lambda i: (rev(i), 0)),
                  pl.BlockSpec((HALO, 2048), lambda i: (jnp.maximum(rev(i) * hb - 1, 0), 0)),
                  pl.BlockSpec((CHUNK, LANES), lambda i: (rev(i), 0)),
                  pl.BlockSpec((1, 4, NSTATE, LANES), lambda i: (rev(i), 0, 0, 0))] + _ssd_param_specs(),
        out_specs=(pl.BlockSpec((CHUNK, 2048), lambda i: (rev(i), 0)), pl.BlockSpec((CHUNK, LANES), lambda i: (rev(i), 0)))
        + tuple(small(sh) for sh in small_shapes),
        scratch_shapes=[pltpu.VMEM((4, NSTATE, LANES), f32), pltpu.VMEM((HALO + CHUNK, XBC), f32),
                        pltpu.VMEM((HALO + CHUNK, POOL_W), f32), pltpu.VMEM((HALO + CHUNK, XBC), f32),
                        pltpu.VMEM((HALO + CHUNK, POOL_W), f32), pltpu.VMEM((HALO, XBC), f32), pltpu.VMEM((HALO, POOL_W), f32)],
        compiler_params=_cparams(),
    )(d_ymix, zxu, zxu, dtr, states, *prm)


FFN_BWD_TILE = 256


def _prev_halo_spec(t, width):
    hb = t // HALO
    return pl.BlockSpec((HALO, width), lambda i: (jnp.maximum(i * hb - 1, 0), 0))


def _ffn_half(hn16, j, wup_ref, cw_ref, cb_ref, up_scr, rows):
    up_scr[...] = _dot(hn16, wup_ref[j])
    cw = cw_ref[j]
    cv = cb_ref[j] + cw[0:1] * up_scr[pl.ds(HALO - 2, rows), :]
    for k in range(1, FFN_CONV):
        cv = cv + cw[k:k + 1] * up_scr[pl.ds(HALO - 2 + k, rows), :]
    return cv


def _out_ffn_ple_fwd(h, ymix, p_l, w_out, g2, wup_h, cw_h, cb_h, wdn_h, g3, w_gate, w_proj):
    s = h.shape[0]
    t = _row_tile(s)
    n = s // t

    def body(h_ref, hh_ref, ym_ref, ymh_ref, p_ref, wo_ref, g2_ref, wup_ref, cw_ref, cb_ref, wdn_ref, g3_ref, wg_ref, wp_ref,
             h1_ref, h2_ref, h3_ref, up_scr):
        i = pl.program_id(0)
        hh = jnp.where(i > 0, hh_ref[...], 0.0)
        ymh = jnp.where(i > 0, ymh_ref[...].astype(f32), 0.0)
        h_ext = jnp.concatenate([hh, h_ref[...]], axis=0)
        ym_ext = jnp.concatenate([ymh, ym_ref[...].astype(f32)], axis=0).astype(bf16)
        h1_ext = h_ext + _dot(ym_ext, wo_ref[...])
        hn2, _, _ = _rms_fwd(h1_ext, g2_ref[...])
        hn16 = hn2.astype(bf16)
        h1 = h1_ext[HALO:, :]
        acc = h1
        for j in range(2):
            cv = _ffn_half(hn16, j, wup_ref, cw_ref, cb_ref, up_scr, t)
            act = _gelu(cv[:, :FF_HALF]) * cv[:, FF_HALF:]
            acc = acc + _dot(act.astype(bf16), wdn_ref[j])
        h2 = acc
        hn3, _, _ = _rms_fwd(h2, g3_ref[...])
        gate = _sigmoid(_dot(hn3.astype(bf16), wg_ref[...]))
        pp = _dot(p_ref[...].astype(bf16), wp_ref[...])
        h1_ref[...] = h1
        h2_ref[...] = h2
        h3_ref[...] = h2 + pp * gate

    row = lambda w: pl.BlockSpec((t, w), lambda i: (i, 0))
    return pl.pallas_call(
        body, name="out_ffn_ple_fwd", grid=(n,),
        out_shape=tuple(jax.ShapeDtypeStruct((s, D_MODEL), f32) for _ in range(3)),
        in_specs=[row(D_MODEL), _prev_halo_spec(t, D_MODEL), row(D_MODEL), _prev_halo_spec(t, D_MODEL), row(D_PLE),
                  _const_spec((D_MODEL, D_MODEL)), _const_spec((1, D_MODEL)), _const_spec((2, D_MODEL, D_FF)),
                  _const_spec((2, FFN_CONV, D_FF)), _const_spec((2, 1, D_FF)), _const_spec((2, FF_HALF, D_MODEL)),
                  _const_spec((1, D_MODEL)), _const_spec((D_MODEL, D_MODEL)), _const_spec((D_PLE, D_MODEL))],
        out_specs=tuple(row(D_MODEL) for _ in range(3)),
        scratch_shapes=[pltpu.VMEM((HALO + t, D_FF), f32)],
        compiler_params=_cparams(),
    )(h, h, ymix, ymix, p_l, w_out, g2, wup_h, cw_h, cb_h, wdn_h, g3, w_gate, w_proj)


def _ple_bwd(dh3, h2, p_l, g3, w_gate, w_proj):
    s = h2.shape[0]
    t = _row_tile(s, WIDE_ROW_TILE)
    n = s // t

    def body(dh3_ref, h2_ref, p_ref, g3_ref, wg_ref, wp_ref, dh2_ref, dwg_ref, dwp_ref, dg3_ref):
        i = pl.program_id(0)

        @pl.when(i == 0)
        def _():
            dwg_ref[...] = jnp.zeros_like(dwg_ref)
            dwp_ref[...] = jnp.zeros_like(dwp_ref)
            dg3_ref[...] = jnp.zeros_like(dg3_ref)

        g3 = g3_ref[...]
        dh3 = dh3_ref[...]
        hn3, xhat, r = _rms_fwd(h2_ref[...], g3)
        hn16 = hn3.astype(bf16)
        gate = _sigmoid(_dot(hn16, wg_ref[...]))
        p16 = p_ref[...].astype(bf16)
        pp = _dot(p16, wp_ref[...])
        d_pp = (dh3 * gate).astype(bf16)
        d_pre = (dh3 * pp * gate * (1.0 - gate)).astype(bf16)
        dwp_ref[...] += _dot_tn(p16, d_pp)
        dwg_ref[...] += _dot_tn(hn16, d_pre)
        dx, dg = _rms_bwd(_dot_nt(d_pre, wg_ref[...]), xhat, r, g3)
        dg3_ref[...] += dg
        dh2_ref[...] = dh3 + dx

    row = lambda w: pl.BlockSpec((t, w), lambda i: (i, 0))
    fixed = lambda shape: pl.BlockSpec(shape, lambda i: (0,) * len(shape))
    return pl.pallas_call(
        body, name="ple_bwd", grid=(n,),
        out_shape=(jax.ShapeDtypeStruct((s, D_MODEL), f32), jax.ShapeDtypeStruct((D_MODEL, D_MODEL), f32),
                   jax.ShapeDtypeStruct((D_PLE, D_MODEL), f32), jax.ShapeDtypeStruct((1, D_MODEL), f32)),
        in_specs=[row(D_MODEL), row(D_MODEL), row(D_PLE), _const_spec((1, D_MODEL)), _const_spec((D_MODEL, D_MODEL)),
                  _const_spec((D_PLE, D_MODEL))],
        out_specs=(row(D_MODEL), fixed((D_MODEL, D_MODEL)), fixed((D_PLE, D_MODEL)), fixed((1, D_MODEL))),
        compiler_params=_cparams(),
    )(dh3, h2, p_l, g3, w_gate, w_proj)


def _ffn_bwd(dh2, h1, g2, wup_h, cw_h, cb_h, wdn_h):
    s = h1.shape[0]
    t = min(FFN_BWD_TILE, s)
    n = s // t
    hb = t // HALO
    last_hb = s // HALO - 1

    def body(dh2_ref, dh2n_ref, h1_ref, h1p_ref, h1n_ref, g2_ref, wup_ref, cw_ref, cb_ref, wdn_ref,
             part_ref, dwup_ref, dwdn_ref, dcw_ref, dcb_ref, up_scr, dcv_scr, acc_up, acc_dn):
        j = pl.program_id(0)
        i = pl.program_id(1)

        @pl.when(i == 0)
        def _():
            acc_up[...] = jnp.zeros_like(acc_up)
            acc_dn[...] = jnp.zeros_like(acc_dn)
            dcw_ref[...] = jnp.zeros_like(dcw_ref)
            dcb_ref[...] = jnp.zeros_like(dcb_ref)

        h1p = jnp.where(i > 0, h1p_ref[...], 0.0)
        h1_ext = jnp.concatenate([h1p, h1_ref[...], h1n_ref[...]], axis=0)
        hn2, _, _ = _rms_fwd(h1_ext, g2_ref[...])
        hn16 = hn2.astype(bf16)
        up_scr[...] = _dot(hn16, wup_ref[0])
        cw = cw_ref[0]
        rows = t + HALO
        cv = cb_ref[0] + cw[0:1] * up_scr[pl.ds(HALO - 2, rows), :]
        for k in range(1, FFN_CONV):
            cv = cv + cw[k:k + 1] * up_scr[pl.ds(HALO - 2 + k, rows), :]
        dh2 = dh2_ref[...]
        dh2n = jnp.where(i < n - 1, dh2n_ref[...], 0.0)
        dh2_ext16 = jnp.concatenate([dh2, dh2n], axis=0).astype(bf16)
        d_act = _dot_nt(dh2_ext16, wdn_ref[0])
        gate, val = cv[:, :FF_HALF], cv[:, FF_HALF:]
        gl, dgl = _gelu_and_grad(gate)
        dcv_scr[:, :FF_HALF] = d_act * val * dgl
        dcv_scr[:, FF_HALF:] = d_act * gl
        act16 = (gl[:t] * val[:t]).astype(bf16)
        acc_dn[...] += _dot_tn(act16, dh2_ext16[:t])
        d_cv = dcv_scr[pl.ds(0, t), :]
        dcb_ref[0] += jnp.sum(d_cv, axis=0, keepdims=True)
        d_up = cw[2:3] * d_cv
        dcw_ref[0, 2:3, :] += jnp.sum(d_cv * up_scr[pl.ds(HALO, t), :], axis=0, keepdims=True)
        for k in range(FFN_CONV - 1):
            dcw_ref[0, k:k + 1, :] += jnp.sum(d_cv * up_scr[pl.ds(HALO - 2 + k, t), :], axis=0, keepdims=True)
            d_up = d_up + cw[k:k + 1] * dcv_scr[pl.ds(2 - k, t), :]
        d_up16 = d_up.astype(bf16)
        part_ref[0] = _dot_nt(d_up16, wup_ref[0])
        acc_up[...] += _dot_tn(hn16[HALO:HALO + t], d_up16)

        @pl.when(i == n - 1)
        def _():
            pltpu.sync_copy(acc_up, dwup_ref.at[j])
            pltpu.sync_copy(acc_dn, dwdn_ref.at[j])

    return pl.pallas_call(
        body, name="ffn_bwd", grid=(2, n),
        out_shape=(jax.ShapeDtypeStruct((2, s, D_MODEL), f32), jax.ShapeDtypeStruct((2, D_MODEL, D_FF), f32),
                   jax.ShapeDtypeStruct((2, FF_HALF, D_MODEL), f32), jax.ShapeDtypeStruct((2, FFN_CONV, D_FF), f32),
                   jax.ShapeDtypeStruct((2, 1, D_FF), f32)),
        in_specs=[pl.BlockSpec((t, D_MODEL), lambda j, i: (i, 0)),
                  pl.BlockSpec((HALO, D_MODEL), lambda j, i: (jnp.minimum((i + 1) * hb, last_hb), 0)),
                  pl.BlockSpec((t, D_MODEL), lambda j, i: (i, 0)),
                  pl.BlockSpec((HALO, D_MODEL), lambda j, i: (jnp.maximum(i * hb - 1, 0), 0)),
                  pl.BlockSpec((HALO, D_MODEL), lambda j, i: (jnp.minimum((i + 1) * hb, last_hb), 0)),
                  _const_spec((1, D_MODEL)),
                  pl.BlockSpec((1, D_MODEL, D_FF), lambda j, i: (j, 0, 0), pipeline_mode=pl.Buffered(1)),
                  pl.BlockSpec((1, FFN_CONV, D_FF), lambda j, i: (j, 0, 0)),
                  pl.BlockSpec((1, 1, D_FF), lambda j, i: (j, 0, 0)),
                  pl.BlockSpec((1, FF_HALF, D_MODEL), lambda j, i: (j, 0, 0), pipeline_mode=pl.Buffered(1))],
        out_specs=(pl.BlockSpec((1, t, D_MODEL), lambda j, i: (j, i, 0)), pl.BlockSpec(memory_space=pl.ANY),
                   pl.BlockSpec(memory_space=pl.ANY), pl.BlockSpec((1, FFN_CONV, D_FF), lambda j, i: (j, 0, 0)),
                   pl.BlockSpec((1, 1, D_FF), lambda j, i: (j, 0, 0))),
        scratch_shapes=[pltpu.VMEM((2 * HALO + t, D_FF), f32), pltpu.VMEM((HALO + t, D_FF), f32),
                        pltpu.VMEM((D_MODEL, D_FF), f32), pltpu.VMEM((FF_HALF, D_MODEL), f32)],
        compiler_params=_cparams(("arbitrary", "arbitrary")),
    )(dh2, dh2, h1, h1, h1, g2, wup_h, cw_h, cb_h, wdn_h)


def _out_bwd(dh2, parts, h1, ymix, g2, w_out):
    s = h1.shape[0]
    t = _row_tile(s, WIDE_ROW_TILE)
    n = s // t

    def body(dh2_ref, part_ref, h1_ref, ym_ref, g2_ref, wo_ref, dh1_ref, dym_ref, dwo_ref, dg2_ref):
        i = pl.program_id(0)

        @pl.when(i == 0)
        def _():
            dwo_ref[...] = jnp.zeros_like(dwo_ref)
            dg2_ref[...] = jnp.zeros_like(dg2_ref)

        g2 = g2_ref[...]
        _, xhat, r = _rms_fwd(h1_ref[...], g2)
        dx, dg = _rms_bwd(part_ref[0] + part_ref[1], xhat, r, g2)
        dg2_ref[...] += dg
        dh1 = dh2_ref[...] + dx
        dh1_ref[...] = dh1
        dh16 = dh1.astype(bf16)
        dym_ref[...] = _dot_nt(dh16, wo_ref[...]).astype(bf16)
        dwo_ref[...] += _dot_tn(ym_ref[...], dh16)

    row = lambda w: pl.BlockSpec((t, w), lambda i: (i, 0))
    fixed = lambda shape: pl.BlockSpec(shape, lambda i: (0,) * len(shape))
    return pl.pallas_call(
        body, name="out_bwd", grid=(n,),
        out_shape=(jax.ShapeDtypeStruct((s, D_MODEL), f32), jax.ShapeDtypeStruct((s, D_MODEL), bf16),
                   jax.ShapeDtypeStruct((D_MODEL, D_MODEL), f32), jax.ShapeDtypeStruct((1, D_MODEL), f32)),
        in_specs=[row(D_MODEL), pl.BlockSpec((2, t, D_MODEL), lambda i: (0, i, 0)), row(D_MODEL), row(D_MODEL),
                  _const_spec((1, D_MODEL)), _const_spec((D_MODEL, D_MODEL))],
        out_specs=(row(D_MODEL), row(D_MODEL), fixed((D_MODEL, D_MODEL)), fixed((1, D_MODEL))),
        compiler_params=_cparams(),
    )(dh2, parts, h1, ymix, g2, w_out)


def _loss_head(h, target, gf):
    s = h.shape[0]
    t = _row_tile(s, WIDE_ROW_TILE)

    def body(h_ref, t_ref, g_ref, dh_ref, dg_ref, loss_ref):
        i = pl.program_id(0)

        @pl.when(i == 0)
        def _():
            dg_ref[...] = jnp.zeros_like(dg_ref)
            loss_ref[...] = jnp.zeros_like(loss_ref)

        g = g_ref[...]
        y, xhat, r = _rms_fwd(h_ref[...], g)
        diff = y - t_ref[...]
        per_row = jnp.mean(diff * diff, axis=-1, keepdims=True)
        loss_ref[...] += 0.5 * jnp.sum(per_row, axis=0, keepdims=True)
        dx, dg = _rms_bwd(diff * (1.0 / D_MODEL), xhat, r, g)
        dg_ref[...] += dg
        dh_ref[...] = dx

    row = pl.BlockSpec((t, D_MODEL), lambda i: (i, 0))
    return pl.pallas_call(
        body, name="loss_head", grid=(s // t,),
        out_shape=(jax.ShapeDtypeStruct((s, D_MODEL), f32), jax.ShapeDtypeStruct((1, D_MODEL), f32),
                   jax.ShapeDtypeStruct((1, LANES), f32)),
        in_specs=[row, row, _const_spec((1, D_MODEL))],
        out_specs=(row, pl.BlockSpec((1, D_MODEL), lambda i: (0, 0)), pl.BlockSpec((1, LANES), lambda i: (0, 0))),
        compiler_params=_cparams(),
    )(h, target, gf)


def _prep_layer(w):
    w_in = w["w_in"]
    w_r = jnp.concatenate([w_in[:, :SSD_W + XBC], w_in[:, SSD_W + XBC + HEADS:], w_in[:, SSD_W + XBC:SSD_W + XBC + HEADS],
                           jnp.zeros((D_MODEL, LANES - HEADS), w_in.dtype)], axis=1)
    pad8 = lambda v: jnp.concatenate([v, jnp.zeros((LANES - HEADS,), f32)]).reshape(1, LANES)
    halves = lambda a: jnp.stack([jnp.concatenate([a[..., j * FF_HALF:(j + 1) * FF_HALF],
                                                   a[..., D_FF + j * FF_HALF:D_FF + (j + 1) * FF_HALF]], axis=-1) for j in range(2)])
    ssd_prm = (w["ssd_conv_w"], w["ssd_conv_b"].reshape(1, XBC), pad8(w["ssd_dt_bias"]), pad8(w["ssd_a_log"]),
               jnp.repeat(w["ssd_d"], HEAD_DIM).reshape(1, SSD_W), w["ssd_norm_g"].reshape(1, SSD_W),
               w["pool_w"], w["pool_scale"].reshape(1, POOL_W))
    return dict(
        g1=w["mix_norm_g"].reshape(1, D_MODEL), w_r=w_r, ssd=ssd_prm, w_out=w["w_out"], g2=w["ffn_norm_g"].reshape(1, D_MODEL),
        wup_h=halves(w["ffn_w_up"]), cw_h=halves(w["ffn_conv_w"]), cb_h=halves(w["ffn_conv_b"].reshape(1, 2 * D_FF)),
        wdn_h=w["ffn_w_down"].reshape(2, FF_HALF, D_MODEL), g3=w["ple_norm_g"].reshape(1, D_MODEL),
        w_gate=w["ple_w_gate"], w_proj=w["ple_w_proj"])


def _unhalve(a):
    return jnp.concatenate([a[0][..., :FF_HALF], a[1][..., :FF_HALF], a[0][..., FF_HALF:], a[1][..., FF_HALF:]], axis=-1)


def _device_step(x, p, target, layers, final_g):
    preps = [_prep_layer(w) for w in layers]
    saved = []
    h = x
    for l, q in enumerate(preps):
        zxu, dtr = _mix_in_fwd(h, q["g1"], q["w_r"])
        ymix, states = _ssd_pool_fwd(zxu, dtr, q["ssd"])
        h1, h2, h3 = _out_ffn_ple_fwd(h, ymix, p[l], q["w_out"], q["g2"], q["wup_h"], q["cw_h"], q["cb_h"], q["wdn_h"],
                                      q["g3"], q["w_gate"], q["w_proj"])
        saved.append((h, zxu, dtr, ymix, states, h1, h2))
        h = h3
    dh, d_gf, loss = _loss_head(h, target, final_g.reshape(1, D_MODEL))
    grads = [None] * len(preps)
    for l in reversed(range(len(preps))):
        q = preps[l]
        h0, zxu, dtr, ymix, states, h1, h2 = saved[l]
        dh2, d_wg, d_wp, d_g3 = _ple_bwd(dh, h2, p[l], q["g3"], q["w_gate"], q["w_proj"])
        parts, d_wup, d_wdn, d_cw, d_cb = _ffn_bwd(dh2, h1, q["g2"], q["wup_h"], q["cw_h"], q["cb_h"], q["wdn_h"])
        dh1, d_ymix, d_wo, d_g2 = _out_bwd(dh2, parts, h1, ymix, q["g2"], q["w_out"])
        (d_zxu, d_dtr, d_scw, d_scb, d_dtb, d_alog, d_dsk, d_ng, d_pw, d_ps) = _ssd_pool_bwd(d_ymix, zxu, dtr, states, q["ssd"])
        dh, d_wr, d_g1 = _mix_in_bwd(d_zxu, d_dtr, h0, dh1, q["g1"], q["w_r"])
        grads[l] = dict(
            mix_norm_g=d_g1.reshape(D_MODEL),
            w_in=jnp.concatenate([d_wr[:, :SSD_W + XBC], d_wr[:, 2048:2048 + HEADS], d_wr[:, SSD_W + XBC:2048]], axis=1),
            ssd_conv_w=d_scw, ssd_conv_b=d_scb.reshape(XBC), ssd_dt_bias=d_dtb[0, :HEADS], ssd_a_log=d_alog[0, :HEADS],
            ssd_d=jnp.sum(d_dsk.reshape(HEADS, HEAD_DIM), axis=1), ssd_norm_g=d_ng.reshape(SSD_W), pool_w=d_pw,
            pool_scale=d_ps.reshape(POOL_W), w_out=d_wo, ffn_norm_g=d_g2.reshape(D_MODEL), ffn_w_up=_unhalve(d_wup),
            ffn_conv_w=_unhalve(d_cw), ffn_conv_b=_unhalve(d_cb).reshape(2 * D_FF), ffn_w_down=d_wdn.reshape(D_FF, D_MODEL),
            ple_norm_g=d_g3.reshape(D_MODEL), ple_w_gate=d_wg, ple_w_proj=d_wp)
    return loss, dh, grads, d_gf.reshape(D_MODEL)


MESH = pl.DeviceIdType.MESH
COLS = 1024
ADD_ROWS = 256
N_CHIPS = 4
WEIGHT_NAMES = ["mix_norm_g", "w_in", "ssd_conv_w", "ssd_conv_b", "ssd_dt_bias", "ssd_a_log", "ssd_d", "ssd_norm_g", "pool_w",
                "pool_scale", "w_out", "ffn_norm_g", "ffn_w_up", "ffn_conv_w", "ffn_conv_b", "ffn_w_down", "ple_norm_g",
                "ple_w_gate", "ple_w_proj", "final_norm_g"]
SHARD_AXIS = {"w_in": 2, "ssd_conv_w": 2, "w_out": 1, "ffn_w_up": 2, "ffn_conv_w": 2, "ffn_w_down": 1, "ple_w_gate": 1,
              "ple_w_proj": 2}
MATMUL_SHARDED = ["w_in", "w_out", "ffn_w_up", "ffn_w_down", "ple_w_gate", "ple_w_proj"]
F32_SHARDED = ["ssd_conv_w", "ffn_conv_w"]
REPLICATED = [n for n in WEIGHT_NAMES if n not in SHARD_AXIS]


def _mesh_pos():
    return lax.axis_index("x"), lax.axis_index("y"), lax.axis_index("c")


def _hbm():
    return pl.BlockSpec(memory_space=pl.ANY)


def _all_gather_chips(xs, name):
    r, cols = xs.shape
    hr = r // 2

    def body(x_ref, out_ref, send_sems, recv_sems):
        x, y, c = _mesh_pos()
        me = 2 * x + y
        sib = (x, y, 1 - c)
        flips = [(1 - x, y), (x, 1 - y), (1 - x, 1 - y)]

        def rows(chip, half):
            return out_ref.at[chip, pl.ds(half * hr, hr), :]

        def copy(k, src, dst, to):
            return pltpu.make_async_remote_copy(src_ref=src, dst_ref=dst, send_sem=send_sems.at[k], recv_sem=recv_sems.at[k],
                                                device_id=to, device_id_type=MESH)

        first = [copy(j, x_ref.at[pl.ds(c * hr, hr), :], rows(me, c), (fx, fy, c)) for j, (fx, fy) in enumerate(flips)]
        for cp in first:
            cp.start()
        passed = []
        for j, (fx, fy) in enumerate(flips):
            blk = rows(2 * fx + fy, c)
            copy(j, blk, blk, (fx, fy, c)).wait_recv()
            fwd = copy(3 + j, blk, blk, sib)
            fwd.start()
            passed.append(fwd)
        for j, (fx, fy) in enumerate(flips):
            blk = rows(2 * fx + fy, 1 - c)
            copy(3 + j, blk, blk, sib).wait_recv()
        for cp in first + passed:
            cp.wait_send()

    gathered = pl.pallas_call(
        body, name=name, out_shape=jax.ShapeDtypeStruct((N_CHIPS, r, cols), xs.dtype),
        in_specs=[_hbm()], out_specs=_hbm(),
        scratch_shapes=[pltpu.SemaphoreType.DMA((6,)), pltpu.SemaphoreType.DMA((6,))],
    )(xs)
    me = 2 * lax.axis_index("x") + lax.axis_index("y")
    return lax.dynamic_update_slice(gathered, xs[None], (me, 0, 0))


def _rs_pair_exchange(g):
    _, r, cols = g.shape
    hr = r // 2

    def body(g_ref, land_ref, send_sem, recv_sem):
        x, y, c = _mesh_pos()
        cp = pltpu.make_async_remote_copy(src_ref=g_ref.at[:, pl.ds((1 - c) * hr, hr), :], dst_ref=land_ref, send_sem=send_sem,
                                          recv_sem=recv_sem, device_id=(x, y, 1 - c), device_id_type=MESH)
        cp.start()
        cp.wait()

    return pl.pallas_call(
        body, name="rs_pair_exchange", out_shape=jax.ShapeDtypeStruct((N_CHIPS, hr, cols), g.dtype),
        in_specs=[_hbm()], out_specs=_hbm(), scratch_shapes=[pltpu.SemaphoreType.DMA, pltpu.SemaphoreType.DMA],
    )(g)


def _rs_pair_add(g, land, c_idx):
    _, r, cols = g.shape
    hr = r // 2
    nt = hr // ADD_ROWS

    def body(c_ref, g_ref, l_ref, o_ref):
        o_ref[...] = (g_ref[...] + l_ref[...]).astype(bf16)

    return pl.pallas_call(
        body, name="rs_pair_add", out_shape=jax.ShapeDtypeStruct((N_CHIPS, hr, cols), bf16),
        grid_spec=pltpu.PrefetchScalarGridSpec(
            num_scalar_prefetch=1, grid=(N_CHIPS, nt),
            in_specs=[pl.BlockSpec((1, ADD_ROWS, cols), lambda k, i, c_ref: (k, c_ref[0] * nt + i, 0)),
                      pl.BlockSpec((1, ADD_ROWS, cols), lambda k, i, c_ref: (k, i, 0))],
            out_specs=pl.BlockSpec((1, ADD_ROWS, cols), lambda k, i, c_ref: (k, i, 0))),
        compiler_params=_cparams(("arbitrary", "arbitrary")),
    )(c_idx, g, land)


def _rs_chip_exchange(part):
    _, hr, cols = part.shape

    def body(p_ref, land_ref, send_sems, recv_sems):
        x, y, c = _mesh_pos()
        flips = [(1 - x, y), (x, 1 - y), (1 - x, 1 - y)]
        cps = [pltpu.make_async_remote_copy(src_ref=p_ref.at[2 * fx + fy], dst_ref=land_ref.at[j], send_sem=send_sems.at[j],
                                            recv_sem=recv_sems.at[j], device_id=(fx, fy, c), device_id_type=MESH)
               for j, (fx, fy) in enumerate(flips)]
        for cp in cps:
            cp.start()
        for cp in cps:
            cp.wait()

    return pl.pallas_call(
        body, name="rs_chip_exchange", out_shape=jax.ShapeDtypeStruct((3, hr, cols), part.dtype),
        in_specs=[_hbm()], out_specs=_hbm(), scratch_shapes=[pltpu.SemaphoreType.DMA((3,)), pltpu.SemaphoreType.DMA((3,))],
    )(part)


def _rs_chip_add(part, land, me_idx):
    _, hr, cols = part.shape
    nt = hr // ADD_ROWS

    def body(me_ref, p_ref, l_ref, o_ref):
        o_ref[...] = ((p_ref[0].astype(f32) + l_ref[0].astype(f32)) + l_ref[1].astype(f32)) + l_ref[2].astype(f32)

    return pl.pallas_call(
        body, name="rs_chip_add", out_shape=jax.ShapeDtypeStruct((hr, cols), f32),
        grid_spec=pltpu.PrefetchScalarGridSpec(
            num_scalar_prefetch=1, grid=(nt,),
            in_specs=[pl.BlockSpec((1, ADD_ROWS, cols), lambda i, me_ref: (me_ref[0], i, 0)),
                      pl.BlockSpec((3, ADD_ROWS, cols), lambda i, me_ref: (0, i, 0))],
            out_specs=pl.BlockSpec((ADD_ROWS, cols), lambda i, me_ref: (i, 0))),
        compiler_params=_cparams(),
    )(me_idx, part, land)


def _rs_pair_share(red):
    hr, cols = red.shape

    def body(r_ref, out_ref, send_sem, recv_sem):
        x, y, c = _mesh_pos()
        cp = pltpu.make_async_remote_copy(src_ref=r_ref, dst_ref=out_ref.at[pl.ds(c * hr, hr), :], send_sem=send_sem,
                                          recv_sem=recv_sem, device_id=(x, y, 1 - c), device_id_type=MESH)
        cp.start()
        pltpu.make_async_remote_copy(src_ref=r_ref, dst_ref=out_ref.at[pl.ds((1 - c) * hr, hr), :], send_sem=send_sem,
                                     recv_sem=recv_sem, device_id=(x, y, 1 - c), device_id_type=MESH).wait_recv()
        cp.wait_send()

    both = pl.pallas_call(
        body, name="rs_pair_share", out_shape=jax.ShapeDtypeStruct((2 * hr, cols), red.dtype),
        in_specs=[_hbm()], out_specs=_hbm(), scratch_shapes=[pltpu.SemaphoreType.DMA, pltpu.SemaphoreType.DMA],
    )(red)
    return lax.dynamic_update_slice(both, red, (lax.axis_index("c") * hr, 0))


def _reduce_scatter(g):
    x, y, c = _mesh_pos()
    land = _rs_pair_exchange(g)
    part = _rs_pair_add(g, land, jnp.reshape(c, (1,)).astype(jnp.int32))
    land2 = _rs_chip_exchange(part)
    red = _rs_chip_add(part, land2, jnp.reshape(2 * x + y, (1,)).astype(jnp.int32))
    return _rs_pair_share(red)


def _pack(parts, row_multiple):
    flat = jnp.concatenate([a.reshape(-1) for a in parts])
    n = flat.shape[0]
    rows = -(-n // COLS)
    rows = -(-rows // row_multiple) * row_multiple
    return jnp.pad(flat, (0, rows * COLS - n)).reshape(rows, COLS)


def _unpack(flat, shapes):
    out, off = [], 0
    for shp in shapes:
        n = math.prod(shp)
        out.append(flat[off:off + n].reshape(shp))
        off += n
    return out


def _adamw(w, g, m, v, name):
    shape = w.shape
    cols = shape[-1]
    rows = math.prod(shape[:-1]) if len(shape) > 1 else 1
    tr = rows
    if rows > 512:
        tr = next(t for t in (512, 256, 128, 64, 32, 16, 8) if rows % t == 0)
    two_d = lambda a: a.reshape(rows, cols)

    def body(w_ref, g_ref, m_ref, v_ref, d_ref, nm_ref, nv_ref):
        gg = g_ref[...]
        nm = ADAM_B1 * m_ref[...] + (1.0 - ADAM_B1) * gg
        nv = ADAM_B2 * v_ref[...] + (1.0 - ADAM_B2) * (gg * gg)
        m_hat = nm / (1.0 - ADAM_B1 ** ADAM_STEP)
        v_hat = nv / (1.0 - ADAM_B2 ** ADAM_STEP)
        d_ref[...] = -ADAM_LR * (m_hat / (jnp.sqrt(v_hat) + ADAM_EPS) + ADAM_WD * w_ref[...])
        nm_ref[...] = nm
        nv_ref[...] = nv

    spec = pl.BlockSpec((tr, cols), lambda i: (i, 0))
    outs = pl.pallas_call(
        body, name="adamw_" + name, grid=(rows // tr,),
        out_shape=tuple(jax.ShapeDtypeStruct((rows, cols), f32) for _ in range(3)),
        in_specs=[spec] * 4, out_specs=(spec,) * 3, compiler_params=_cparams(),
    )(two_d(w), two_d(g), two_d(m), two_d(v))
    return tuple(o.reshape(shape) for o in outs)


def _chip_slice(a, axis, k):
    n = a.shape[axis] // N_CHIPS
    return lax.slice_in_dim(a, k * n, (k + 1) * n, axis=axis)


def _train_step(x, p, loss_target, w, m, v):
    as_pairs = lambda a: lax.bitcast_convert_type(a, bf16)
    wbuf = _pack([w[n].astype(bf16) for n in MATMUL_SHARDED] + [as_pairs(w[n]) for n in F32_SHARDED], 32)
    gathered = _all_gather_chips(wbuf, "gather_weights").reshape(N_CHIPS, -1)
    shapes = [w[n].shape for n in MATMUL_SHARDED] + [w[n].shape + (2,) for n in F32_SHARDED]
    per_chip = [_unpack(gathered[k], shapes) for k in range(N_CHIPS)]
    full = {}
    for idx, n in enumerate(MATMUL_SHARDED + F32_SHARDED):
        pieces = [per_chip[k][idx] for k in range(N_CHIPS)]
        if n in F32_SHARDED:
            pieces = [lax.bitcast_convert_type(a, f32) for a in pieces]
        full[n] = jnp.concatenate(pieces, axis=SHARD_AXIS[n])
    for n in REPLICATED:
        full[n] = w[n]
    full["pool_w"] = w["pool_w"].astype(bf16)
    layers = [{n: full[n][l] for n in WEIGHT_NAMES if n != "final_norm_g"} for l in range(DEPTH)]

    loss_part, grad_x, grads, d_gf = _device_step(x[0], p[:, 0], loss_target[0], layers, w["final_norm_g"])

    sharded = list(SHARD_AXIS)
    small = _pack([d_gf if n == "final_norm_g" else jnp.stack([grads[l][n] for l in range(DEPTH)]) for n in REPLICATED],
                  32 * N_CHIPS)
    small_rows = small.shape[0] // N_CHIPS
    gbuf = jnp.stack([_pack([_chip_slice(grads[l][n], SHARD_AXIS[n] - 1, k) for n in sharded for l in range(DEPTH)]
                            + [small[k * small_rows:(k + 1) * small_rows]], 2 * ADD_ROWS) for k in range(N_CHIPS)])
    red = _reduce_scatter(gbuf).reshape(-1)
    red_parts = _unpack(red, [w[n].shape for n in sharded] + [(small_rows, COLS)])
    grad = dict(zip(sharded, red_parts[:-1]))
    small_all = _all_gather_chips(red_parts[-1], "gather_small_grads").reshape(-1)
    grad.update(zip(REPLICATED, _unpack(small_all, [w[n].shape for n in REPLICATED])))

    loss = lax.psum(loss_part[0, 0], ("x", "y", "c"))
    delta, new_m, new_v = {}, {}, {}
    for n in WEIGHT_NAMES:
        delta[n], new_m[n], new_v[n] = _adamw(w[n], grad[n], m[n], v[n], n)
    return (loss, grad_x[None], *[grad[n] for n in WEIGHT_NAMES], *[delta[n] for n in WEIGHT_NAMES],
            *[new_m[n] for n in WEIGHT_NAMES], *[new_v[n] for n in WEIGHT_NAMES])


def kernel(x, p, mix_norm_g, w_in, ssd_conv_w, ssd_conv_b, ssd_dt_bias, ssd_a_log, ssd_d, ssd_norm_g, pool_w, pool_scale, w_out, ffn_norm_g, ffn_w_up, ffn_conv_w, ffn_conv_b, ffn_w_down, ple_norm_g, ple_w_gate, ple_w_proj, final_norm_g, loss_target, m_mix_norm_g, m_w_in, m_ssd_conv_w, m_ssd_conv_b, m_ssd_dt_bias, m_ssd_a_log, m_ssd_d, m_ssd_norm_g, m_pool_w, m_pool_scale, m_w_out, m_ffn_norm_g, m_ffn_w_up, m_ffn_conv_w, m_ffn_conv_b, m_ffn_w_down, m_ple_norm_g, m_ple_w_gate, m_ple_w_proj, m_final_norm_g, v_mix_norm_g, v_w_in, v_ssd_conv_w, v_ssd_conv_b, v_ssd_dt_bias, v_ssd_a_log, v_ssd_d, v_ssd_norm_g, v_pool_w, v_pool_scale, v_w_out, v_ffn_norm_g, v_ffn_w_up, v_ffn_conv_w, v_ffn_conv_b, v_ffn_w_down, v_ple_norm_g, v_ple_w_gate, v_ple_w_proj, v_final_norm_g):
    w = dict(mix_norm_g=mix_norm_g, w_in=w_in, ssd_conv_w=ssd_conv_w, ssd_conv_b=ssd_conv_b, ssd_dt_bias=ssd_dt_bias, ssd_a_log=ssd_a_log, ssd_d=ssd_d, ssd_norm_g=ssd_norm_g, pool_w=pool_w, pool_scale=pool_scale, w_out=w_out, ffn_norm_g=ffn_norm_g, ffn_w_up=ffn_w_up, ffn_conv_w=ffn_conv_w, ffn_conv_b=ffn_conv_b, ffn_w_down=ffn_w_down, ple_norm_g=ple_norm_g, ple_w_gate=ple_w_gate, ple_w_proj=ple_w_proj, final_norm_g=final_norm_g)
    m = dict(mix_norm_g=m_mix_norm_g, w_in=m_w_in, ssd_conv_w=m_ssd_conv_w, ssd_conv_b=m_ssd_conv_b, ssd_dt_bias=m_ssd_dt_bias, ssd_a_log=m_ssd_a_log, ssd_d=m_ssd_d, ssd_norm_g=m_ssd_norm_g, pool_w=m_pool_w, pool_scale=m_pool_scale, w_out=m_w_out, ffn_norm_g=m_ffn_norm_g, ffn_w_up=m_ffn_w_up, ffn_conv_w=m_ffn_conv_w, ffn_conv_b=m_ffn_conv_b, ffn_w_down=m_ffn_w_down, ple_norm_g=m_ple_norm_g, ple_w_gate=m_ple_w_gate, ple_w_proj=m_ple_w_proj, final_norm_g=m_final_norm_g)
    v = dict(mix_norm_g=v_mix_norm_g, w_in=v_w_in, ssd_conv_w=v_ssd_conv_w, ssd_conv_b=v_ssd_conv_b, ssd_dt_bias=v_ssd_dt_bias, ssd_a_log=v_ssd_a_log, ssd_d=v_ssd_d, ssd_norm_g=v_ssd_norm_g, pool_w=v_pool_w, pool_scale=v_pool_scale, w_out=v_w_out, ffn_norm_g=v_ffn_norm_g, ffn_w_up=v_ffn_w_up, ffn_conv_w=v_ffn_conv_w, ffn_conv_b=v_ffn_conv_b, ffn_w_down=v_ffn_w_down, ple_norm_g=v_ple_norm_g, ple_w_gate=v_ple_w_gate, ple_w_proj=v_ple_w_proj, final_norm_g=v_final_norm_g)
    return _train_step(x, p, loss_target, w, m, v)
```

```python
import functools
import math

import jax
import jax.numpy as jnp
from jax import lax
from jax.experimental import pallas as pl
from jax.experimental.pallas import tpu as pltpu

f32, bf16 = jnp.float32, jnp.bfloat16
HI = lax.Precision.HIGHEST

D_MODEL = 1024
D_PLE = 256
DEPTH = 4
SSD_W = 512
HEADS = 8
HEAD_DIM = 64
NSTATE = 128
CHUNK = 128
SSD_CONV = 4
XBC = 1024
POOL_W = 512
POOL_G = 128
WINDOWS = (2, 4, 8, 16)
D_FF = 2816
FF_HALF = D_FF // 2
FFN_CONV = 3
D_IN = 2056
EPS = 1e-6
ADAM_LR, ADAM_B1, ADAM_B2, ADAM_EPS, ADAM_WD, ADAM_STEP = 0.001, 0.9, 0.999, 1e-08, 0.01, 10

LANES = 128
NPROJ = 2048 + LANES
HALO = 16
FHALO = 8
VMEM_LIMIT = 58 * 1024 * 1024
ROW_TILE = 256


def _dot(a, b):
    return jnp.dot(a, b, preferred_element_type=f32)


def _dot_nt(a, b):
    return lax.dot_general(a, b, (((1,), (1,)), ((), ())), preferred_element_type=f32)


def _dot_tn(a, b):
    return lax.dot_general(a, b, (((0,), (0,)), ((), ())), preferred_element_type=f32)


def _dot_hi(a, b):
    return jnp.dot(a, b, precision=HI, preferred_element_type=f32)


def _dot_nt_hi(a, b):
    return lax.dot_general(a, b, (((1,), (1,)), ((), ())), precision=HI, preferred_element_type=f32)


def _rms_fwd(x, g):
    r = lax.rsqrt(jnp.mean(x * x, axis=-1, keepdims=True) + EPS)
    xhat = x * r
    return xhat * g, xhat, r


def _rms_bwd(dy, xhat, r, g):
    dxhat = dy * g
    dx = r * (dxhat - xhat * jnp.mean(dxhat * xhat, axis=-1, keepdims=True))
    return dx, jnp.sum(dy * xhat, axis=0, keepdims=True)


def _sigmoid(x):
    return 1.0 / (1.0 + jnp.exp(-x))


_GELU_C = math.sqrt(2.0 / math.pi)


def _gelu_and_grad(x):
    x2 = x * x
    t = jnp.tanh(_GELU_C * (x + 0.044715 * x * x2))
    g = 0.5 * x * (1.0 + t)
    dg = 0.5 * (1.0 + t) + 0.5 * x * (1.0 - t * t) * _GELU_C * (1.0 + 3.0 * 0.044715 * x2)
    return g, dg


def _gelu(x):
    return 0.5 * x * (1.0 + jnp.tanh(_GELU_C * (x + 0.044715 * x * x * x)))


def _softplus(x):
    return jnp.maximum(x, 0.0) + jnp.log(1.0 + jnp.exp(-jnp.abs(x)))


def _cparams(sem=("arbitrary",)):
    return pltpu.CompilerParams(dimension_semantics=sem, vmem_limit_bytes=VMEM_LIMIT)


def _const_spec(shape):
    nd = len(shape)
    return pl.BlockSpec(shape, lambda *_: (0,) * nd, pipeline_mode=pl.Buffered(1))


WIDE_ROW_TILE = 512


def _row_tile(s, t=ROW_TILE):
    return min(t, s)


def _mix_in_fwd(h, g1, w_r):
    s = h.shape[0]
    t = _row_tile(s, WIDE_ROW_TILE)

    def body(h_ref, g_ref, w_ref, zxu_ref, dtr_ref):
        hn, _, _ = _rms_fwd(h_ref[...], g_ref[...])
        proj = _dot(hn.astype(bf16), w_ref[...])
        zxu_ref[...] = proj[:, :2048].astype(bf16)
        dtr_ref[...] = proj[:, 2048:]

    return pl.pallas_call(
        body, name="mix_in_fwd", grid=(s // t,),
        out_shape=(jax.ShapeDtypeStruct((s, 2048), bf16), jax.ShapeDtypeStruct((s, LANES), f32)),
        in_specs=[pl.BlockSpec((t, D_MODEL), lambda i: (i, 0)), _const_spec((1, D_MODEL)), _const_spec((D_MODEL, NPROJ))],
        out_specs=(pl.BlockSpec((t, 2048), lambda i: (i, 0)), pl.BlockSpec((t, LANES), lambda i: (i, 0))),
        compiler_params=_cparams(),
    )(h, g1, w_r)


def _mix_in_bwd(d_zxu, d_dtr, h, dh1, g1, w_r):
    s = h.shape[0]
    t = _row_tile(s, WIDE_ROW_TILE)
    n = s // t

    def body(dz_ref, dd_ref, h_ref, dh1_ref, g_ref, w_ref, dh_ref, dw_ref, dg_ref, acc):
        i = pl.program_id(0)

        @pl.when(i == 0)
        def _():
            acc[...] = jnp.zeros_like(acc)
            dg_ref[...] = jnp.zeros_like(dg_ref)

        g = g_ref[...]
        hn, xhat, r = _rms_fwd(h_ref[...], g)
        dproj = jnp.concatenate([dz_ref[...], dd_ref[...].astype(bf16)], axis=1)
        d_hn = _dot_nt(dproj, w_ref[...])
        acc[...] += _dot_tn(hn.astype(bf16), dproj)
        dx, dg = _rms_bwd(d_hn, xhat, r, g)
        dg_ref[...] += dg
        dh_ref[...] = dh1_ref[...] + dx

        @pl.when(i == n - 1)
        def _():
            pltpu.sync_copy(acc, dw_ref)

    return pl.pallas_call(
        body, name="mix_in_bwd", grid=(n,),
        out_shape=(jax.ShapeDtypeStruct((s, D_MODEL), f32), jax.ShapeDtypeStruct((D_MODEL, NPROJ), f32),
                   jax.ShapeDtypeStruct((1, D_MODEL), f32)),
        in_specs=[pl.BlockSpec((t, 2048), lambda i: (i, 0)), pl.BlockSpec((t, LANES), lambda i: (i, 0)),
                  pl.BlockSpec((t, D_MODEL), lambda i: (i, 0)), pl.BlockSpec((t, D_MODEL), lambda i: (i, 0)),
                  _const_spec((1, D_MODEL)), _const_spec((D_MODEL, NPROJ))],
        out_specs=(pl.BlockSpec((t, D_MODEL), lambda i: (i, 0)), pl.BlockSpec(memory_space=pl.ANY),
                   pl.BlockSpec((1, D_MODEL), lambda i: (0, 0))),
        scratch_shapes=[pltpu.VMEM((D_MODEL, NPROJ), f32)],
        compiler_params=_cparams(),
    )(d_zxu, d_dtr, h, dh1, g1, w_r)


def _iota2(shape, dim):
    return lax.broadcasted_iota(jnp.int32, shape, dim)


def _lane_bcast(a, h):
    return jnp.broadcast_to(a[:, h:h + 1], (a.shape[0], LANES))


def _to_columns(cols):
    lane = _iota2((cols[0].shape[0], LANES), 1)
    out = jnp.where(lane == 0, cols[0], 0.0)
    for h in range(1, len(cols)):
        out = out + jnp.where(lane == h, cols[h], 0.0)
    return out


def _ssd_pre(zx, dtr, xext, cw, cb, dtb, alog):
    c = cb + cw[0:1] * xext[pl.ds(HALO - 3, CHUNK), :]
    for k in range(1, SSD_CONV):
        c = c + cw[k:k + 1] * xext[pl.ds(HALO - 3 + k, CHUNK), :]
    sig_c = _sigmoid(c)
    xc = c * sig_c
    dt = _softplus(dtr + dtb)
    a_neg = -jnp.exp(alog)
    a = dt * a_neg
    lo = _iota2((CHUNK, LANES), 1) < HEAD_DIM
    dt_w = jnp.concatenate([jnp.where(lo, _lane_bcast(dt, 2 * pp), _lane_bcast(dt, 2 * pp + 1)) for pp in range(4)], axis=1)
    xs = xc[:, :SSD_W]
    xd = xs * dt_w
    tril = (_iota2((CHUNK, CHUNK), 0) >= _iota2((CHUNK, CHUNK), 1))
    acs = _dot_hi(tril.astype(f32), a)
    acs_b = [_lane_bcast(acs, h) for h in range(HEADS)]
    return dict(c=c, sig_c=sig_c, xc=xc, xs=xs, dt=dt, a_neg=a_neg, a=a, dt_w=dt_w, xd=xd, acs_b=acs_b, tril=tril, lo=lo)


def _pair_fwd(q, pp, s_in):
    g = pp // 2
    lo = q["lo"]
    b_g = q["xc"][:, SSD_W + g * NSTATE:SSD_W + (g + 1) * NSTATE]
    c_g = q["xc"][:, SSD_W + 2 * NSTATE + g * NSTATE:SSD_W + 2 * NSTATE + (g + 1) * NSTATE]
    b16, c16 = b_g.astype(bf16), c_g.astype(bf16)
    gmat = _dot_nt(c16, b16)
    xp = q["xd"][:, pp * LANES:(pp + 1) * LANES]
    xp16 = xp.astype(bf16)
    ab0 = q["acs_b"][2 * pp]
    ab1 = q["acs_b"][2 * pp + 1]
    ls, ms, ys = [], [], []
    for ab in (ab0, ab1):
        lmat = jnp.exp(jnp.where(q["tril"], ab - ab.T, -jnp.inf))
        mmat = gmat * lmat
        ls.append(lmat)
        ms.append(mmat)
        ys.append(_dot(mmat.astype(bf16), xp16))
    y_diag = jnp.where(lo, ys[0], ys[1])
    ab_pair = jnp.where(lo, ab0, ab1)
    e_pair = jnp.exp(ab_pair)
    s16 = s_in.astype(bf16)
    y_off = _dot(c16, s16) * e_pair
    alast = ab_pair[CHUNK - 1:CHUNK, :]
    dec_pair = jnp.exp(alast - ab_pair)
    xdec = xp * dec_pair
    st = _dot_tn(b16, xdec.astype(bf16))
    cd_pair = jnp.exp(alast)
    s_out = s_in * cd_pair + st
    return dict(b16=b16, c16=c16, gmat=gmat, xp=xp, xp16=xp16, ls=ls, ms=ms, y=y_diag + y_off, y_off=y_off,
                e_pair=e_pair, dec_pair=dec_pair, xdec=xdec, cd_pair=cd_pair, s_out=s_out, s16=s16, lo=lo)


def _gate_norm_fwd(y_pre, z, ng):
    sz = _sigmoid(z)
    yg = y_pre * (z * sz)
    outs, stats = [], []
    half = SSD_W // 2
    for gi in range(2):
        o, xhat, r = _rms_fwd(yg[:, gi * half:(gi + 1) * half], ng[:, gi * half:(gi + 1) * half])
        outs.append(o)
        stats.append((xhat, r))
    return jnp.concatenate(outs, axis=1), sz, stats


def _pool_fwd(uext, u, row0, pw_ref, ps):
    pos = (row0 + _iota2((CHUNK, 1), 0) + 1).astype(f32)
    pooled, mixed, invs = [], [], []
    for gi, w in enumerate(WINDOWS):
        sl = slice(gi * POOL_G, (gi + 1) * POOL_G)
        acc = uext[pl.ds(HALO, CHUNK), sl]
        for j in range(1, w):
            acc = acc + uext[pl.ds(HALO - j, CHUNK), sl]
        den = jnp.minimum(pos, float(w))
        pg = acc / den - u[:, sl]
        pooled.append(pg)
        invs.append(den)
        mixed.append(_dot(pg.astype(bf16), pw_ref[gi]))
    mixed = jnp.concatenate(mixed, axis=1)
    return mixed * ps, pooled, mixed, invs


def _ssd_specs(s):
    nc = s // CHUNK
    hb = CHUNK // HALO
    return nc, hb


def _ssd_param_specs():
    return [_const_spec((SSD_CONV, XBC)), _const_spec((1, XBC)), _const_spec((1, LANES)), _const_spec((1, LANES)),
            _const_spec((1, SSD_W)), _const_spec((1, SSD_W)), _const_spec((4, POOL_G, POOL_G)), _const_spec((1, POOL_W))]


def _ssd_pool_fwd(zxu, dtr, prm):
    s = zxu.shape[0]
    nc, hb = _ssd_specs(s)

    def body(zx_ref, halo_ref, dtr_ref, cw_ref, cb_ref, dtb_ref, alog_ref, dsk_ref, ng_ref, pw_ref, ps_ref,
             ymix_ref, st_ref, state, xext, uext):
        i = pl.program_id(0)

        @pl.when(i == 0)
        def _():
            state[...] = jnp.zeros_like(state)

        zx = zx_ref[...].astype(f32)
        halo = jnp.where(i > 0, halo_ref[...].astype(f32), 0.0)
        xext[0:HALO, :] = halo[:, SSD_W:SSD_W + XBC]
        xext[HALO:, :] = zx[:, SSD_W:SSD_W + XBC]
        uext[0:HALO, :] = halo[:, SSD_W + XBC:]
        uext[HALO:, :] = zx[:, SSD_W + XBC:]
        q = _ssd_pre(zx, dtr_ref[...], xext, cw_ref[...], cb_ref[...], dtb_ref[...], alog_ref[...])
        ys = []
        for pp in range(4):
            s_in = state[pp]
            st_ref[0, pp] = s_in
            r = _pair_fwd(q, pp, s_in)
            state[pp] = r["s_out"]
            ys.append(r["y"])
        y_pre = jnp.concatenate(ys, axis=1) + q["xs"] * dsk_ref[...]
        y_ssd, _, _ = _gate_norm_fwd(y_pre, zx[:, :SSD_W], ng_ref[...])
        y_pool, _, _, _ = _pool_fwd(uext, zx[:, SSD_W + XBC:], i * CHUNK, pw_ref, ps_ref[...])
        ymix_ref[:, :SSD_W] = y_ssd.astype(bf16)
        ymix_ref[:, SSD_W:] = y_pool.astype(bf16)

    return pl.pallas_call(
        body, name="ssd_pool_fwd", grid=(nc,),
        out_shape=(jax.ShapeDtypeStruct((s, D_MODEL), bf16), jax.ShapeDtypeStruct((nc, 4, NSTATE, LANES), f32)),
        in_specs=[pl.BlockSpec((CHUNK, 2048), lambda i: (i, 0)),
                  pl.BlockSpec((HALO, 2048), lambda i: (jnp.maximum(i * hb - 1, 0), 0)),
                  pl.BlockSpec((CHUNK, LANES), lambda i: (i, 0))] + _ssd_param_specs(),
        out_specs=(pl.BlockSpec((CHUNK, D_MODEL), lambda i: (i, 0)),
                   pl.BlockSpec((1, 4, NSTATE, LANES), lambda i: (i, 0, 0, 0))),
        scratch_shapes=[pltpu.VMEM((4, NSTATE, LANES), f32), pltpu.VMEM((HALO + CHUNK, XBC), f32),
                        pltpu.VMEM((HALO + CHUNK, POOL_W), f32)],
        compiler_params=_cparams(),
    )(zxu, zxu, dtr, *prm)


def _ssd_pool_bwd(d_ymix, zxu, dtr, states, prm):
    s = zxu.shape[0]
    nc, hb = _ssd_specs(s)
    rev = lambda i: nc - 1 - i

    def body(dy_ref, zx_ref, halo_ref, dtr_ref, st_ref, cw_ref, cb_ref, dtb_ref, alog_ref, dsk_ref, ng_ref, pw_ref, ps_ref,
             dzx_ref, ddtr_ref, dcw_ref, dcb_ref, ddtb_ref, dalog_ref, ddsk_ref, dng_ref, dpw_ref, dps_ref,
             dstate, xext, uext, dxext, duext, cx, cu):
        i = pl.program_id(0)
        ci = nc - 1 - i

        @pl.when(i == 0)
        def _():
            dstate[...] = jnp.zeros_like(dstate)
            cx[...] = jnp.zeros_like(cx)
            cu[...] = jnp.zeros_like(cu)
            for r in (dcw_ref, dcb_ref, ddtb_ref, dalog_ref, ddsk_ref, dng_ref, dpw_ref, dps_ref):
                r[...] = jnp.zeros_like(r)

        zx = zx_ref[...].astype(f32)
        halo = jnp.where(ci > 0, halo_ref[...].astype(f32), 0.0)
        xext[0:HALO, :] = halo[:, SSD_W:SSD_W + XBC]
        xext[HALO:, :] = zx[:, SSD_W:SSD_W + XBC]
        uext[0:HALO, :] = halo[:, SSD_W + XBC:]
        uext[HALO:, :] = zx[:, SSD_W + XBC:]
        cw = cw_ref[...]
        q = _ssd_pre(zx, dtr_ref[...], xext, cw, cb_ref[...], dtb_ref[...], alog_ref[...])
        z = zx[:, :SSD_W]
        dy = dy_ref[...].astype(f32)
        d_yssd, d_ypool = dy[:, :SSD_W], dy[:, SSD_W:]

        pairs = [_pair_fwd(q, pp, st_ref[0, pp]) for pp in range(4)]
        dsk = dsk_ref[...]
        ng = ng_ref[...]
        y_pre = jnp.concatenate([r["y"] for r in pairs], axis=1) + q["xs"] * dsk
        _, sz, stats = _gate_norm_fwd(y_pre, z, ng)

        half = SSD_W // 2
        d_yg, d_ng = [], []
        for gi in range(2):
            xhat, r = stats[gi]
            dx, dg = _rms_bwd(d_yssd[:, gi * half:(gi + 1) * half], xhat, r, ng[:, gi * half:(gi + 1) * half])
            d_yg.append(dx)
            d_ng.append(dg)
        d_yg = jnp.concatenate(d_yg, axis=1)
        dng_ref[...] += jnp.concatenate(d_ng, axis=1)
        silu_z = z * sz
        d_ypre = d_yg * silu_z
        d_z = d_yg * y_pre * (sz * (1.0 + z * (1.0 - sz)))
        ddsk_ref[...] += jnp.sum(d_ypre * q["xs"], axis=0, keepdims=True)

        d_xd, acs_cols, dt_cols = [], [], []
        d_b = [None, None]
        d_c = [None, None]
        d_g = [None, None]
        last_row = _iota2((CHUNK, LANES), 0) == CHUNK - 1
        for pp in range(4):
            g = pp // 2
            r = pairs[pp]
            lo = r["lo"]
            dyp = d_ypre[:, pp * LANES:(pp + 1) * LANES]
            dyp16 = dyp.astype(bf16)
            ds_out = dstate[pp]
            ds16 = ds_out.astype(bf16)
            dye16 = (dyp * r["e_pair"]).astype(bf16)
            dstate[pp] = r["cd_pair"] * ds_out + _dot_tn(r["c16"], dye16)
            dc = _dot_nt(dye16, r["s16"])
            dxdec = _dot(r["b16"], ds16)
            db = _dot_nt(r["xdec"].astype(bf16), ds16)
            dxp = dxdec * r["dec_pair"]
            t2 = dxdec * r["xdec"]
            tail = jnp.sum(t2, axis=0, keepdims=True) + jnp.sum(ds_out * st_ref[0, pp] * r["cd_pair"], axis=0, keepdims=True)
            rp = dyp * r["y_off"] - t2 + jnp.where(last_row, tail, 0.0)
            dxs = []
            for hh in range(2):
                msk = lo if hh == 0 else jnp.logical_not(lo)
                m16 = r["ms"][hh].astype(bf16)
                dxs.append(_dot_tn(m16, dyp16))
                dm = _dot_nt(jnp.where(msk, dyp, 0.0).astype(bf16), r["xp16"])
                wmat = dm * r["ms"][hh]
                acs_cols.append(jnp.sum(wmat - wmat.T + jnp.where(msk, rp, 0.0), axis=1, keepdims=True))
                dgh = dm * r["ls"][hh]
                d_g[g] = dgh if d_g[g] is None else d_g[g] + dgh
            dxp = dxp + jnp.where(lo, dxs[0], dxs[1])
            d_xd.append(dxp)
            xprod = dxp * q["xs"][:, pp * LANES:(pp + 1) * LANES]
            dt_cols.append(jnp.sum(jnp.where(lo, xprod, 0.0), axis=1, keepdims=True))
            dt_cols.append(jnp.sum(jnp.where(lo, 0.0, xprod), axis=1, keepdims=True))
            d_b[g] = db if d_b[g] is None else d_b[g] + db
            d_c[g] = dc if d_c[g] is None else d_c[g] + dc
        for g in range(2):
            dg16 = d_g[g].astype(bf16)
            d_c[g] = d_c[g] + _dot(dg16, pairs[2 * g]["b16"])
            d_b[g] = d_b[g] + _dot_tn(dg16, pairs[2 * g]["c16"])
        d_xd = jnp.concatenate(d_xd, axis=1)
        triu = (_iota2((CHUNK, CHUNK), 0) <= _iota2((CHUNK, CHUNK), 1)).astype(f32)
        d_a = _dot_hi(triu, _to_columns(acs_cols))
        d_dt = d_a * q["a_neg"] + _to_columns(dt_cols)
        dalog_ref[...] += jnp.sum(d_a * q["dt"], axis=0, keepdims=True) * q["a_neg"]
        d_dtr = d_dt * _sigmoid(dtr_ref[...] + dtb_ref[...])
        ddtr_ref[...] = d_dtr
        ddtb_ref[...] += jnp.sum(d_dtr, axis=0, keepdims=True)
        d_xs = d_ypre * dsk + d_xd * q["dt_w"]

        d_xc = jnp.concatenate([d_xs, d_b[0], d_b[1], d_c[0], d_c[1]], axis=1)
        sc = q["sig_c"]
        d_conv = d_xc * (sc * (1.0 + q["c"] * (1.0 - sc)))
        dcb_ref[...] += jnp.sum(d_conv, axis=0, keepdims=True)
        dxext[...] = jnp.zeros_like(dxext)
        for k in range(SSD_CONV):
            dcw_ref[k:k + 1, :] += jnp.sum(d_conv * xext[pl.ds(HALO - 3 + k, CHUNK), :], axis=0, keepdims=True)
            dxext[pl.ds(HALO - 3 + k, CHUNK), :] += cw[k:k + 1] * d_conv
        dxext[pl.ds(CHUNK, HALO), :] += cx[...]
        cx[...] = dxext[0:HALO, :]

        ps = ps_ref[...]
        u = zx[:, SSD_W + XBC:]
        _, pooled, mixed, dens = _pool_fwd(uext, u, ci * CHUNK, pw_ref, ps)
        dps_ref[...] += jnp.sum(d_ypool * mixed, axis=0, keepdims=True)
        d_mixed = d_ypool * ps
        duext[...] = jnp.zeros_like(duext)
        for gi, w in enumerate(WINDOWS):
            sl = slice(gi * POOL_G, (gi + 1) * POOL_G)
            dm16 = d_mixed[:, sl].astype(bf16)
            dpw_ref[gi] += _dot_tn(pooled[gi].astype(bf16), dm16)
            d_pg = _dot_nt(dm16, pw_ref[gi])
            d_mean = d_pg / dens[gi]
            duext[pl.ds(HALO, CHUNK), sl] += d_mean - d_pg
            for j in range(1, w):
                duext[pl.ds(HALO - j, CHUNK), sl] += d_mean
        duext[pl.ds(CHUNK, HALO), :] += cu[...]
        cu[...] = duext[0:HALO, :]

        dzx_ref[:, :SSD_W] = d_z.astype(bf16)
        dzx_ref[:, SSD_W:SSD_W + XBC] = dxext[HALO:, :].astype(bf16)
        dzx_ref[:, SSD_W + XBC:] = duext[HALO:, :].astype(bf16)

    small = lambda shape: pl.BlockSpec(shape, lambda i: (0,) * len(shape))
    small_shapes = [(SSD_CONV, XBC), (1, XBC), (1, LANES), (1, LANES), (1, SSD_W), (1, SSD_W), (4, POOL_G, POOL_G), (1, POOL_W)]
    return pl.pallas_call(
        body, name="ssd_pool_bwd", grid=(nc,),
        out_shape=(jax.ShapeDtypeStruct((s, 2048), bf16), jax.ShapeDtypeStruct((s, LANES), f32))
        + tuple(jax.ShapeDtypeStruct(sh, f32) for sh in small_shapes),
        in_specs=[pl.BlockSpec((CHUNK, D_MODEL), lambda i: (rev(i), 0)),
                  pl.BlockSpec((CHUNK, 2048), lambda i: (rev(i), 0)),
                  pl.BlockSpec((HALO, 2048), lambda i: (jnp.maximum(rev(i) * hb - 1, 0), 0)),
                  pl.BlockSpec((CHUNK, LANES), lambda i: (rev(i), 0)),
                  pl.BlockSpec((1, 4, NSTATE, LANES), lambda i: (rev(i), 0, 0, 0))] + _ssd_param_specs(),
        out_specs=(pl.BlockSpec((CHUNK, 2048), lambda i: (rev(i), 0)), pl.BlockSpec((CHUNK, LANES), lambda i: (rev(i), 0)))
        + tuple(small(sh) for sh in small_shapes),
        scratch_shapes=[pltpu.VMEM((4, NSTATE, LANES), f32), pltpu.VMEM((HALO + CHUNK, XBC), f32),
                        pltpu.VMEM((HALO + CHUNK, POOL_W), f32), pltpu.VMEM((HALO + CHUNK, XBC), f32),
                        pltpu.VMEM((HALO + CHUNK, POOL_W), f32), pltpu.VMEM((HALO, XBC), f32), pltpu.VMEM((HALO, POOL_W), f32)],
        compiler_params=_cparams(),
    )(d_ymix, zxu, zxu, dtr, states, *prm)


FFN_BWD_TILE = 256


def _prev_halo_spec(t, width):
    hb = t // HALO
    return pl.BlockSpec((HALO, width), lambda i: (jnp.maximum(i * hb - 1, 0), 0))


def _ffn_half(hn16, j, wup_ref, cw_ref, cb_ref, up_scr, rows):
    up_scr[...] = _dot(hn16, wup_ref[j])
    cw = cw_ref[j]
    cv = cb_ref[j] + cw[0:1] * up_scr[pl.ds(HALO - 2, rows), :]
    for k in range(1, FFN_CONV):
        cv = cv + cw[k:k + 1] * up_scr[pl.ds(HALO - 2 + k, rows), :]
    return cv


def _out_ffn_ple_fwd(h, ymix, p_l, w_out, g2, wup_h, cw_h, cb_h, wdn_h, g3, w_gate, w_proj):
    s = h.shape[0]
    t = _row_tile(s)
    n = s // t

    def body(h_ref, hh_ref, ym_ref, ymh_ref, p_ref, wo_ref, g2_ref, wup_ref, cw_ref, cb_ref, wdn_ref, g3_ref, wg_ref, wp_ref,
             h1_ref, h2_ref, h3_ref, up_scr):
        i = pl.program_id(0)
        hh = jnp.where(i > 0, hh_ref[...], 0.0)
        ymh = jnp.where(i > 0, ymh_ref[...].astype(f32), 0.0)
        h_ext = jnp.concatenate([hh, h_ref[...]], axis=0)
        ym_ext = jnp.concatenate([ymh, ym_ref[...].astype(f32)], axis=0).astype(bf16)
        h1_ext = h_ext + _dot(ym_ext, wo_ref[...])
        hn2, _, _ = _rms_fwd(h1_ext, g2_ref[...])
        hn16 = hn2.astype(bf16)
        h1 = h1_ext[HALO:, :]
        acc = h1
        for j in range(2):
            cv = _ffn_half(hn16, j, wup_ref, cw_ref, cb_ref, up_scr, t)
            act = _gelu(cv[:, :FF_HALF]) * cv[:, FF_HALF:]
            acc = acc + _dot(act.astype(bf16), wdn_ref[j])
        h2 = acc
        hn3, _, _ = _rms_fwd(h2, g3_ref[...])
        gate = _sigmoid(_dot(hn3.astype(bf16), wg_ref[...]))
        pp = _dot(p_ref[...].astype(bf16), wp_ref[...])
        h1_ref[...] = h1
        h2_ref[...] = h2
        h3_ref[...] = h2 + pp * gate

    row = lambda w: pl.BlockSpec((t, w), lambda i: (i, 0))
    return pl.pallas_call(
        body, name="out_ffn_ple_fwd", grid=(n,),
        out_shape=tuple(jax.ShapeDtypeStruct((s, D_MODEL), f32) for _ in range(3)),
        in_specs=[row(D_MODEL), _prev_halo_spec(t, D_MODEL), row(D_MODEL), _prev_halo_spec(t, D_MODEL), row(D_PLE),
                  _const_spec((D_MODEL, D_MODEL)), _const_spec((1, D_MODEL)), _const_spec((2, D_MODEL, D_FF)),
                  _const_spec((2, FFN_CONV, D_FF)), _const_spec((2, 1, D_FF)), _const_spec((2, FF_HALF, D_MODEL)),
                  _const_spec((1, D_MODEL)), _const_spec((D_MODEL, D_MODEL)), _const_spec((D_PLE, D_MODEL))],
        out_specs=tuple(row(D_MODEL) for _ in range(3)),
        scratch_shapes=[pltpu.VMEM((HALO + t, D_FF), f32)],
        compiler_params=_cparams(),
    )(h, h, ymix, ymix, p_l, w_out, g2, wup_h, cw_h, cb_h, wdn_h, g3, w_gate, w_proj)


def _ple_bwd(dh3, h2, p_l, g3, w_gate, w_proj):
    s = h2.shape[0]
    t = _row_tile(s, WIDE_ROW_TILE)
    n = s // t

    def body(dh3_ref, h2_ref, p_ref, g3_ref, wg_ref, wp_ref, dh2_ref, dwg_ref, dwp_ref, dg3_ref):
        i = pl.program_id(0)

        @pl.when(i == 0)
        def _():
            dwg_ref[...] = jnp.zeros_like(dwg_ref)
            dwp_ref[...] = jnp.zeros_like(dwp_ref)
            dg3_ref[...] = jnp.zeros_like(dg3_ref)

        g3 = g3_ref[...]
        dh3 = dh3_ref[...]
        hn3, xhat, r = _rms_fwd(h2_ref[...], g3)
        hn16 = hn3.astype(bf16)
        gate = _sigmoid(_dot(hn16, wg_ref[...]))
        p16 = p_ref[...].astype(bf16)
        pp = _dot(p16, wp_ref[...])
        d_pp = (dh3 * gate).astype(bf16)
        d_pre = (dh3 * pp * gate * (1.0 - gate)).astype(bf16)
        dwp_ref[...] += _dot_tn(p16, d_pp)
        dwg_ref[...] += _dot_tn(hn16, d_pre)
        dx, dg = _rms_bwd(_dot_nt(d_pre, wg_ref[...]), xhat, r, g3)
        dg3_ref[...] += dg
        dh2_ref[...] = dh3 + dx

    row = lambda w: pl.BlockSpec((t, w), lambda i: (i, 0))
    fixed = lambda shape: pl.BlockSpec(shape, lambda i: (0,) * len(shape))
    return pl.pallas_call(
        body, name="ple_bwd", grid=(n,),
        out_shape=(jax.ShapeDtypeStruct((s, D_MODEL), f32), jax.ShapeDtypeStruct((D_MODEL, D_MODEL), f32),
                   jax.ShapeDtypeStruct((D_PLE, D_MODEL), f32), jax.ShapeDtypeStruct((1, D_MODEL), f32)),
        in_specs=[row(D_MODEL), row(D_MODEL), row(D_PLE), _const_spec((1, D_MODEL)), _const_spec((D_MODEL, D_MODEL)),
                  _const_spec((D_PLE, D_MODEL))],
        out_specs=(row(D_MODEL), fixed((D_MODEL, D_MODEL)), fixed((D_PLE, D_MODEL)), fixed((1, D_MODEL))),
        compiler_params=_cparams(),
    )(dh3, h2, p_l, g3, w_gate, w_proj)


def _ffn_bwd(dh2, h1, g2, wup_h, cw_h, cb_h, wdn_h):
    s = h1.shape[0]
    t = min(FFN_BWD_TILE, s)
    n = s // t
    hb = t // HALO
    last_hb = s // HALO - 1

    def body(dh2_ref, dh2n_ref, h1_ref, h1p_ref, h1n_ref, g2_ref, wup_ref, cw_ref, cb_ref, wdn_ref,
             part_ref, dwup_ref, dwdn_ref, dcw_ref, dcb_ref, up_scr, dcv_scr, acc_up, acc_dn):
        j = pl.program_id(0)
        i = pl.program_id(1)

        @pl.when(i == 0)
        def _():
            acc_up[...] = jnp.zeros_like(acc_up)
            acc_dn[...] = jnp.zeros_like(acc_dn)
            dcw_ref[...] = jnp.zeros_like(dcw_ref)
            dcb_ref[...] = jnp.zeros_like(dcb_ref)

        h1p = jnp.where(i > 0, h1p_ref[...], 0.0)
        h1_ext = jnp.concatenate([h1p, h1_ref[...], h1n_ref[...]], axis=0)
        hn2, _, _ = _rms_fwd(h1_ext, g2_ref[...])
        hn16 = hn2.astype(bf16)
        up_scr[...] = _dot(hn16, wup_ref[0])
        cw = cw_ref[0]
        rows = t + HALO
        cv = cb_ref[0] + cw[0:1] * up_scr[pl.ds(HALO - 2, rows), :]
        for k in range(1, FFN_CONV):
            cv = cv + cw[k:k + 1] * up_scr[pl.ds(HALO - 2 + k, rows), :]
        dh2 = dh2_ref[...]
        dh2n = jnp.where(i < n - 1, dh2n_ref[...], 0.0)
        dh2_ext16 = jnp.concatenate([dh2, dh2n], axis=0).astype(bf16)
        d_act = _dot_nt(dh2_ext16, wdn_ref[0])
        gate, val = cv[:, :FF_HALF], cv[:, FF_HALF:]
        gl, dgl = _gelu_and_grad(gate)
        dcv_scr[:, :FF_HALF] = d_act * val * dgl
        dcv_scr[:, FF_HALF:] = d_act * gl
        act16 = (gl[:t] * val[:t]).astype(bf16)
        acc_dn[...] += _dot_tn(act16, dh2_ext16[:t])
        d_cv = dcv_scr[pl.ds(0, t), :]
        dcb_ref[0] += jnp.sum(d_cv, axis=0, keepdims=True)
        d_up = cw[2:3] * d_cv
        dcw_ref[0, 2:3, :] += jnp.sum(d_cv * up_scr[pl.ds(HALO, t), :], axis=0, keepdims=True)
        for k in range(FFN_CONV - 1):
            dcw_ref[0, k:k + 1, :] += jnp.sum(d_cv * up_scr[pl.ds(HALO - 2 + k, t), :], axis=0, keepdims=True)
            d_up = d_up + cw[k:k + 1] * dcv_scr[pl.ds(2 - k, t), :]
        d_up16 = d_up.astype(bf16)
        part_ref[0] = _dot_nt(d_up16, wup_ref[0])
        acc_up[...] += _dot_tn(hn16[HALO:HALO + t], d_up16)

        @pl.when(i == n - 1)
        def _():
            pltpu.sync_copy(acc_up, dwup_ref.at[j])
            pltpu.sync_copy(acc_dn, dwdn_ref.at[j])

    return pl.pallas_call(
        body, name="ffn_bwd", grid=(2, n),
        out_shape=(jax.ShapeDtypeStruct((2, s, D_MODEL), f32), jax.ShapeDtypeStruct((2, D_MODEL, D_FF), f32),
                   jax.ShapeDtypeStruct((2, FF_HALF, D_MODEL), f32), jax.ShapeDtypeStruct((2, FFN_CONV, D_FF), f32),
                   jax.ShapeDtypeStruct((2, 1, D_FF), f32)),
        in_specs=[pl.BlockSpec((t, D_MODEL), lambda j, i: (i, 0)),
                  pl.BlockSpec((HALO, D_MODEL), lambda j, i: (jnp.minimum((i + 1) * hb, last_hb), 0)),
                  pl.BlockSpec((t, D_MODEL), lambda j, i: (i, 0)),
                  pl.BlockSpec((HALO, D_MODEL), lambda j, i: (jnp.maximum(i * hb - 1, 0), 0)),
                  pl.BlockSpec((HALO, D_MODEL), lambda j, i: (jnp.minimum((i + 1) * hb, last_hb), 0)),
                  _const_spec((1, D_MODEL)),
                  pl.BlockSpec((1, D_MODEL, D_FF), lambda j, i: (j, 0, 0), pipeline_mode=pl.Buffered(1)),
                  pl.BlockSpec((1, FFN_CONV, D_FF), lambda j, i: (j, 0, 0)),
                  pl.BlockSpec((1, 1, D_FF), lambda j, i: (j, 0, 0)),
                  pl.BlockSpec((1, FF_HALF, D_MODEL), lambda j, i: (j, 0, 0), pipeline_mode=pl.Buffered(1))],
        out_specs=(pl.BlockSpec((1, t, D_MODEL), lambda j, i: (j, i, 0)), pl.BlockSpec(memory_space=pl.ANY),
                   pl.BlockSpec(memory_space=pl.ANY), pl.BlockSpec((1, FFN_CONV, D_FF), lambda j, i: (j, 0, 0)),
                   pl.BlockSpec((1, 1, D_FF), lambda j, i: (j, 0, 0))),
        scratch_shapes=[pltpu.VMEM((2 * HALO + t, D_FF), f32), pltpu.VMEM((HALO + t, D_FF), f32),
                        pltpu.VMEM((D_MODEL, D_FF), f32), pltpu.VMEM((FF_HALF, D_MODEL), f32)],
        compiler_params=_cparams(("arbitrary", "arbitrary")),
    )(dh2, dh2, h1, h1, h1, g2, wup_h, cw_h, cb_h, wdn_h)


def _out_bwd(dh2, parts, h1, ymix, g2, w_out):
    s = h1.shape[0]
    t = _row_tile(s, WIDE_ROW_TILE)
    n = s // t

    def body(dh2_ref, part_ref, h1_ref, ym_ref, g2_ref, wo_ref, dh1_ref, dym_ref, dwo_ref, dg2_ref):
        i = pl.program_id(0)

        @pl.when(i == 0)
        def _():
            dwo_ref[...] = jnp.zeros_like(dwo_ref)
            dg2_ref[...] = jnp.zeros_like(dg2_ref)

        g2 = g2_ref[...]
        _, xhat, r = _rms_fwd(h1_ref[...], g2)
        dx, dg = _rms_bwd(part_ref[0] + part_ref[1], xhat, r, g2)
        dg2_ref[...] += dg
        dh1 = dh2_ref[...] + dx
        dh1_ref[...] = dh1
        dh16 = dh1.astype(bf16)
        dym_ref[...] = _dot_nt(dh16, wo_ref[...]).astype(bf16)
        dwo_ref[...] += _dot_tn(ym_ref[...], dh16)

    row = lambda w: pl.BlockSpec((t, w), lambda i: (i, 0))
    fixed = lambda shape: pl.BlockSpec(shape, lambda i: (0,) * len(shape))
    return pl.pallas_call(
        body, name="out_bwd", grid=(n,),
        out_shape=(jax.ShapeDtypeStruct((s, D_MODEL), f32), jax.ShapeDtypeStruct((s, D_MODEL), bf16),
                   jax.ShapeDtypeStruct((D_MODEL, D_MODEL), f32), jax.ShapeDtypeStruct((1, D_MODEL), f32)),
        in_specs=[row(D_MODEL), pl.BlockSpec((2, t, D_MODEL), lambda i: (0, i, 0)), row(D_MODEL), row(D_MODEL),
                  _const_spec((1, D_MODEL)), _const_spec((D_MODEL, D_MODEL))],
        out_specs=(row(D_MODEL), row(D_MODEL), fixed((D_MODEL, D_MODEL)), fixed((1, D_MODEL))),
        compiler_params=_cparams(),
    )(dh2, parts, h1, ymix, g2, w_out)


def _loss_head(h, target, gf):
    s = h.shape[0]
    t = _row_tile(s, WIDE_ROW_TILE)

    def body(h_ref, t_ref, g_ref, dh_ref, dg_ref, loss_ref):
        i = pl.program_id(0)

        @pl.when(i == 0)
        def _():
            dg_ref[...] = jnp.zeros_like(dg_ref)
            loss_ref[...] = jnp.zeros_like(loss_ref)

        g = g_ref[...]
        y, xhat, r = _rms_fwd(h_ref[...], g)
        diff = y - t_ref[...]
        per_row = jnp.mean(diff * diff, axis=-1, keepdims=True)
        loss_ref[...] += 0.5 * jnp.sum(per_row, axis=0, keepdims=True)
        dx, dg = _rms_bwd(diff * (1.0 / D_MODEL), xhat, r, g)
        dg_ref[...] += dg
        dh_ref[...] = dx

    row = pl.BlockSpec((t, D_MODEL), lambda i: (i, 0))
    return pl.pallas_call(
        body, name="loss_head", grid=(s // t,),
        out_shape=(jax.ShapeDtypeStruct((s, D_MODEL), f32), jax.ShapeDtypeStruct((1, D_MODEL), f32),
                   jax.ShapeDtypeStruct((1, LANES), f32)),
        in_specs=[row, row, _const_spec((1, D_MODEL))],
        out_specs=(row, pl.BlockSpec((1, D_MODEL), lambda i: (0, 0)), pl.BlockSpec((1, LANES), lambda i: (0, 0))),
        compiler_params=_cparams(),
    )(h, target, gf)


def _prep_layer(w):
    w_in = w["w_in"]
    w_r = jnp.concatenate([w_in[:, :SSD_W + XBC], w_in[:, SSD_W + XBC + HEADS:], w_in[:, SSD_W + XBC:SSD_W + XBC + HEADS],
                           jnp.zeros((D_MODEL, LANES - HEADS), w_in.dtype)], axis=1)
    pad8 = lambda v: jnp.concatenate([v, jnp.zeros((LANES - HEADS,), f32)]).reshape(1, LANES)
    halves = _halves
    ssd_prm = (w["ssd_conv_w"], w["ssd_conv_b"].reshape(1, XBC), pad8(w["ssd_dt_bias"]), pad8(w["ssd_a_log"]),
               jnp.repeat(w["ssd_d"], HEAD_DIM).reshape(1, SSD_W), w["ssd_norm_g"].reshape(1, SSD_W),
               w["pool_w"], w["pool_scale"].reshape(1, POOL_W))
    return dict(
        g1=w["mix_norm_g"].reshape(1, D_MODEL), w_r=w_r, ssd=ssd_prm, w_out=w["w_out"], g2=w["ffn_norm_g"].reshape(1, D_MODEL),
        wup_h=w["wup_h"], cw_h=halves(w["ffn_conv_w"]), cb_h=halves(w["ffn_conv_b"].reshape(1, 2 * D_FF)),
        wdn_h=w["wdn_h"], g3=w["ple_norm_g"].reshape(1, D_MODEL), w_gate=w["ple_w_gate"], w_proj=w["ple_w_proj"])


def _halves(a):
    return jnp.stack([jnp.concatenate([a[..., j * FF_HALF:(j + 1) * FF_HALF],
                                       a[..., D_FF + j * FF_HALF:D_FF + (j + 1) * FF_HALF]], axis=-1) for j in range(2)])


def _unhalve(a):
    return jnp.concatenate([a[0][..., :FF_HALF], a[1][..., :FF_HALF], a[0][..., FF_HALF:], a[1][..., FF_HALF:]], axis=-1)


def _device_step(x, p, target, layers, final_g):
    preps = [_prep_layer(w) for w in layers]
    saved = []
    h = x
    for l, q in enumerate(preps):
        zxu, dtr = _mix_in_fwd(h, q["g1"], q["w_r"])
        ymix, states = _ssd_pool_fwd(zxu, dtr, q["ssd"])
        h1, h2, h3 = _out_ffn_ple_fwd(h, ymix, p[l], q["w_out"], q["g2"], q["wup_h"], q["cw_h"], q["cb_h"], q["wdn_h"],
                                      q["g3"], q["w_gate"], q["w_proj"])
        saved.append((h, zxu, dtr, ymix, states, h1, h2))
        h = h3
    dh, d_gf, loss = _loss_head(h, target, final_g.reshape(1, D_MODEL))
    grads = [None] * len(preps)
    for l in reversed(range(len(preps))):
        q = preps[l]
        h0, zxu, dtr, ymix, states, h1, h2 = saved[l]
        dh2, d_wg, d_wp, d_g3 = _ple_bwd(dh, h2, p[l], q["g3"], q["w_gate"], q["w_proj"])
        parts, d_wup, d_wdn, d_cw, d_cb = _ffn_bwd(dh2, h1, q["g2"], q["wup_h"], q["cw_h"], q["cb_h"], q["wdn_h"])
        dh1, d_ymix, d_wo, d_g2 = _out_bwd(dh2, parts, h1, ymix, q["g2"], q["w_out"])
        (d_zxu, d_dtr, d_scw, d_scb, d_dtb, d_alog, d_dsk, d_ng, d_pw, d_ps) = _ssd_pool_bwd(d_ymix, zxu, dtr, states, q["ssd"])
        dh, d_wr, d_g1 = _mix_in_bwd(d_zxu, d_dtr, h0, dh1, q["g1"], q["w_r"])
        grads[l] = dict(
            mix_norm_g=d_g1.reshape(D_MODEL),
            w_in=jnp.concatenate([d_wr[:, :SSD_W + XBC], d_wr[:, 2048:2048 + HEADS], d_wr[:, SSD_W + XBC:2048]], axis=1),
            ssd_conv_w=d_scw, ssd_conv_b=d_scb.reshape(XBC), ssd_dt_bias=d_dtb[0, :HEADS], ssd_a_log=d_alog[0, :HEADS],
            ssd_d=jnp.sum(d_dsk.reshape(HEADS, HEAD_DIM), axis=1), ssd_norm_g=d_ng.reshape(SSD_W), pool_w=d_pw,
            pool_scale=d_ps.reshape(POOL_W), w_out=d_wo, ffn_norm_g=d_g2.reshape(D_MODEL), wup_h=d_wup,
            ffn_conv_w=_unhalve(d_cw), ffn_conv_b=_unhalve(d_cb).reshape(2 * D_FF), wdn_h=d_wdn,
            ple_norm_g=d_g3.reshape(D_MODEL), ple_w_gate=d_wg, ple_w_proj=d_wp)
    return loss, dh, grads, d_gf.reshape(D_MODEL)


MESH = pl.DeviceIdType.MESH
COLS = 1024
N_CHIPS = 4
WEIGHT_NAMES = ["mix_norm_g", "w_in", "ssd_conv_w", "ssd_conv_b", "ssd_dt_bias", "ssd_a_log", "ssd_d", "ssd_norm_g", "pool_w",
                "pool_scale", "w_out", "ffn_norm_g", "ffn_w_up", "ffn_conv_w", "ffn_conv_b", "ffn_w_down", "ple_norm_g",
                "ple_w_gate", "ple_w_proj", "final_norm_g"]
SHARD_AXIS = {"w_in": 2, "ssd_conv_w": 2, "w_out": 1, "ffn_w_up": 2, "ffn_conv_w": 2, "ffn_w_down": 1, "ple_w_gate": 1,
              "ple_w_proj": 2}
MATMUL_SHARDED = ["w_in", "w_out", "ffn_w_up", "ffn_w_down", "ple_w_gate", "ple_w_proj"]
F32_SHARDED = ["ssd_conv_w", "ffn_conv_w"]
REPLICATED = [n for n in WEIGHT_NAMES if n not in SHARD_AXIS]


def _mesh_pos():
    return lax.axis_index("x"), lax.axis_index("y"), lax.axis_index("c")


def _hbm():
    return pl.BlockSpec(memory_space=pl.ANY)


def _all_gather_chips(arrays, name):
    n = len(arrays)
    half_rows = [a.shape[0] // 2 for a in arrays]

    def body(*refs):
        x_refs, out_refs, (send_sems, recv_sems) = refs[:n], refs[n:2 * n], refs[2 * n:]
        x, y, c = _mesh_pos()
        me = 2 * x + y
        sib = (x, y, 1 - c)
        flips = [(1 - x, y), (x, 1 - y), (1 - x, 1 - y)]

        def rows(a, chip, half):
            return out_refs[a].at[chip, pl.ds(half * half_rows[a], half_rows[a]), :]

        def copy(k, src, dst, to):
            return pltpu.make_async_remote_copy(src_ref=src, dst_ref=dst, send_sem=send_sems.at[k], recv_sem=recv_sems.at[k],
                                                device_id=to, device_id_type=MESH)

        first = [copy(6 * a + j, x_refs[a].at[pl.ds(c * half_rows[a], half_rows[a]), :], rows(a, me, c), (fx, fy, c))
                 for a in range(n) for j, (fx, fy) in enumerate(flips)]
        for cp in first:
            cp.start()
        passed = []
        for j, (fx, fy) in enumerate(flips):
            for a in range(n):
                blk = rows(a, 2 * fx + fy, c)
                copy(6 * a + j, blk, blk, (fx, fy, c)).wait_recv()
                fwd = copy(6 * a + 3 + j, blk, blk, sib)
                fwd.start()
                passed.append(fwd)
        for j, (fx, fy) in enumerate(flips):
            for a in range(n):
                blk = rows(a, 2 * fx + fy, 1 - c)
                copy(6 * a + 3 + j, blk, blk, sib).wait_recv()
        for cp in first + passed:
            cp.wait_send()

    gathered = pl.pallas_call(
        body, name=name, out_shape=[jax.ShapeDtypeStruct((N_CHIPS,) + a.shape, a.dtype) for a in arrays],
        in_specs=[_hbm()] * n, out_specs=[_hbm()] * n,
        scratch_shapes=[pltpu.SemaphoreType.DMA((6 * n,)), pltpu.SemaphoreType.DMA((6 * n,))],
    )(*arrays)
    me = 2 * lax.axis_index("x") + lax.axis_index("y")
    return [lax.dynamic_update_slice(o, a[None], (me, 0, 0)) for o, a in zip(gathered, arrays)]


def _rs_pair_exchange(gs):
    n = len(gs)

    def body(*refs):
        g_refs, land_refs, (send_sems, recv_sems) = refs[:n], refs[n:2 * n], refs[2 * n:]
        x, y, c = _mesh_pos()
        cps = []
        for a in range(n):
            hr = gs[a].shape[1] // 2
            cps.append(pltpu.make_async_remote_copy(
                src_ref=g_refs[a].at[:, pl.ds((1 - c) * hr, hr), :], dst_ref=land_refs[a], send_sem=send_sems.at[a],
                recv_sem=recv_sems.at[a], device_id=(x, y, 1 - c), device_id_type=MESH))
        for cp in cps:
            cp.start()
        for cp in cps:
            cp.wait()

    return pl.pallas_call(
        body, name="rs_pair_exchange",
        out_shape=[jax.ShapeDtypeStruct((N_CHIPS, g.shape[1] // 2, g.shape[2]), g.dtype) for g in gs],
        in_specs=[_hbm()] * n, out_specs=[_hbm()] * n,
        scratch_shapes=[pltpu.SemaphoreType.DMA((n,)), pltpu.SemaphoreType.DMA((n,))],
    )(*gs)


def _rs_pair_add(g, land, c_idx):
    _, r, cols = g.shape
    hr = r // 2

    def body(c_ref, g_ref, l_ref, o_ref):
        o_ref[...] = (g_ref[...] + l_ref[...]).astype(bf16)

    return pl.pallas_call(
        body, name="rs_pair_add", out_shape=jax.ShapeDtypeStruct((N_CHIPS, hr, cols), bf16),
        grid_spec=pltpu.PrefetchScalarGridSpec(
            num_scalar_prefetch=1, grid=(N_CHIPS,),
            in_specs=[pl.BlockSpec((1, hr, cols), lambda k, c_ref: (k, c_ref[0], 0)),
                      pl.BlockSpec((1, hr, cols), lambda k, c_ref: (k, 0, 0))],
            out_specs=pl.BlockSpec((1, hr, cols), lambda k, c_ref: (k, 0, 0))),
        compiler_params=_cparams(),
    )(c_idx, g, land)


def _rs_chip_exchange(parts):
    n = len(parts)

    def body(*refs):
        p_refs, land_refs, (send_sems, recv_sems) = refs[:n], refs[n:2 * n], refs[2 * n:]
        x, y, c = _mesh_pos()
        flips = [(1 - x, y), (x, 1 - y), (1 - x, 1 - y)]
        cps = [pltpu.make_async_remote_copy(src_ref=p_refs[a].at[2 * fx + fy], dst_ref=land_refs[a].at[j],
                                            send_sem=send_sems.at[3 * a + j], recv_sem=recv_sems.at[3 * a + j],
                                            device_id=(fx, fy, c), device_id_type=MESH)
               for a in range(n) for j, (fx, fy) in enumerate(flips)]
        for cp in cps:
            cp.start()
        for cp in cps:
            cp.wait()

    return pl.pallas_call(
        body, name="rs_chip_exchange", out_shape=[jax.ShapeDtypeStruct((3,) + p.shape[1:], p.dtype) for p in parts],
        in_specs=[_hbm()] * n, out_specs=[_hbm()] * n,
        scratch_shapes=[pltpu.SemaphoreType.DMA((3 * n,)), pltpu.SemaphoreType.DMA((3 * n,))],
    )(*parts)


def _rs_chip_add(part, land, me_idx):
    _, hr, cols = part.shape

    def body(me_ref, p_ref, l_ref, o_ref):
        o_ref[...] = ((p_ref[0].astype(f32) + l_ref[0].astype(f32)) + l_ref[1].astype(f32)) + l_ref[2].astype(f32)

    return pl.pallas_call(
        body, name="rs_chip_add", out_shape=jax.ShapeDtypeStruct((hr, cols), f32),
        grid_spec=pltpu.PrefetchScalarGridSpec(
            num_scalar_prefetch=1, grid=(1,),
            in_specs=[pl.BlockSpec((1, hr, cols), lambda i, me_ref: (me_ref[0], 0, 0)),
                      pl.BlockSpec((3, hr, cols), lambda i, me_ref: (0, 0, 0))],
            out_specs=pl.BlockSpec((hr, cols), lambda i, me_ref: (0, 0))),
        compiler_params=_cparams(),
    )(me_idx, part, land)


def _rs_pair_share(reds):
    n = len(reds)

    def body(*refs):
        r_refs, out_refs, (send_sems, recv_sems) = refs[:n], refs[n:2 * n], refs[2 * n:]
        x, y, c = _mesh_pos()

        def copy(a, half):
            hr = reds[a].shape[0]
            return pltpu.make_async_remote_copy(src_ref=r_refs[a], dst_ref=out_refs[a].at[pl.ds(half * hr, hr), :],
                                                send_sem=send_sems.at[a], recv_sem=recv_sems.at[a], device_id=(x, y, 1 - c),
                                                device_id_type=MESH)

        sends = [copy(a, c) for a in range(n)]
        for cp in sends:
            cp.start()
        for a in range(n):
            copy(a, 1 - c).wait_recv()
        for cp in sends:
            cp.wait_send()

    both = pl.pallas_call(
        body, name="rs_pair_share", out_shape=[jax.ShapeDtypeStruct((2 * r.shape[0], r.shape[1]), r.dtype) for r in reds],
        in_specs=[_hbm()] * n, out_specs=[_hbm()] * n,
        scratch_shapes=[pltpu.SemaphoreType.DMA((n,)), pltpu.SemaphoreType.DMA((n,))],
    )(*reds)
    c = lax.axis_index("c")
    return [lax.dynamic_update_slice(o, r, (c * r.shape[0], 0)) for o, r in zip(both, reds)]


def _reduce_scatter(gs):
    x, y, c = _mesh_pos()
    c_idx = jnp.reshape(c, (1,)).astype(jnp.int32)
    me_idx = jnp.reshape(2 * x + y, (1,)).astype(jnp.int32)
    lands = _rs_pair_exchange(gs)
    parts = [_rs_pair_add(g, land, c_idx) for g, land in zip(gs, lands)]
    lands2 = _rs_chip_exchange(parts)
    reds = [_rs_chip_add(part, land, me_idx) for part, land in zip(parts, lands2)]
    return _rs_pair_share(reds)


def _pack(parts, row_multiple):
    flat = jnp.concatenate([a.reshape(-1) for a in parts])
    n = flat.shape[0]
    rows = -(-n // COLS)
    rows = -(-rows // row_multiple) * row_multiple
    return jnp.pad(flat, (0, rows * COLS - n)).reshape(rows, COLS)


def _unpack(flat, shapes):
    out, off = [], 0
    for shp in shapes:
        n = math.prod(shp)
        out.append(flat[off:off + n].reshape(shp))
        off += n
    return out


def _adamw(w, g, m, v, name):
    shape = w.shape
    cols = shape[-1]
    rows = math.prod(shape[:-1]) if len(shape) > 1 else 1
    tr = rows
    if rows > 512:
        tr = next(t for t in (512, 256, 128, 64, 32, 16, 8) if rows % t == 0)
    two_d = lambda a: a.reshape(rows, cols)

    def body(w_ref, g_ref, m_ref, v_ref, d_ref, nm_ref, nv_ref):
        gg = g_ref[...]
        nm = ADAM_B1 * m_ref[...] + (1.0 - ADAM_B1) * gg
        nv = ADAM_B2 * v_ref[...] + (1.0 - ADAM_B2) * (gg * gg)
        m_hat = nm / (1.0 - ADAM_B1 ** ADAM_STEP)
        v_hat = nv / (1.0 - ADAM_B2 ** ADAM_STEP)
        d_ref[...] = -ADAM_LR * (m_hat / (jnp.sqrt(v_hat) + ADAM_EPS) + ADAM_WD * w_ref[...])
        nm_ref[...] = nm
        nv_ref[...] = nv

    spec = pl.BlockSpec((tr, cols), lambda i: (i, 0))
    outs = pl.pallas_call(
        body, name="adamw_" + name, grid=(rows // tr,),
        out_shape=tuple(jax.ShapeDtypeStruct((rows, cols), f32) for _ in range(3)),
        in_specs=[spec] * 4, out_specs=(spec,) * 3, compiler_params=_cparams(),
    )(two_d(w), two_d(g), two_d(m), two_d(v))
    return tuple(o.reshape(shape) for o in outs)


BIG_SHARDED = ["ffn_w_up", "ffn_w_down", "w_out", "ple_w_gate", "ple_w_proj", "w_in"]
SMALL_REDUCED = REPLICATED + F32_SHARDED


def _chip_major(a, axis):
    n = a.shape[axis] // N_CHIPS
    return jnp.moveaxis(a.reshape(a.shape[:axis] + (N_CHIPS, n) + a.shape[axis + 1:]), axis, 0)


def _train_step(x, p, loss_target, w, m, v):
    rows2d = lambda a: a.reshape(-1, a.shape[-1])
    conv_rows = 16
    pad_rows = lambda a: jnp.pad(a, ((0, conv_rows - a.shape[0]), (0, 0)))
    local = [rows2d(w[n].astype(bf16)) for n in MATMUL_SHARDED] + [pad_rows(rows2d(w[n])) for n in F32_SHARDED]
    gathered = dict(zip(MATMUL_SHARDED + F32_SHARDED, _all_gather_chips(local, "gather_weights")))

    def layer_rows(n, l):
        r = w[n].shape[1] if w[n].ndim == 3 else 1
        return gathered[n][:, l * r:(l + 1) * r, :]

    def col_cut(n, l):
        blk = layer_rows(n, l)
        return jnp.moveaxis(blk, 0, 1).reshape(blk.shape[1], N_CHIPS * blk.shape[2])

    layers = []
    for l in range(DEPTH):
        up4 = layer_rows("ffn_w_up", l)
        lw = {n: w[n][l] for n in REPLICATED if n != "final_norm_g"}
        lw["pool_w"] = lw["pool_w"].astype(bf16)
        lw.update(
            w_in=col_cut("w_in", l), ple_w_proj=col_cut("ple_w_proj", l),
            w_out=layer_rows("w_out", l).reshape(D_MODEL, D_MODEL), ple_w_gate=layer_rows("ple_w_gate", l).reshape(D_MODEL, D_MODEL),
            wdn_h=layer_rows("ffn_w_down", l).reshape(2, FF_HALF, D_MODEL),
            wup_h=jnp.stack([jnp.concatenate([up4[j], up4[2 + j]], axis=1) for j in range(2)]),
            ssd_conv_w=jnp.moveaxis(gathered["ssd_conv_w"][:, l * SSD_CONV:(l + 1) * SSD_CONV, :], 0, 1).reshape(SSD_CONV, XBC),
            ffn_conv_w=jnp.moveaxis(gathered["ffn_conv_w"][:, l * FFN_CONV:(l + 1) * FFN_CONV, :], 0, 1).reshape(FFN_CONV, 2 * D_FF))
        layers.append(lw)

    loss_part, grad_x, grads, d_gf = _device_step(x[0], p[:, 0], loss_target[0], layers, w["final_norm_g"])

    small = _pack([d_gf if n == "final_norm_g" else jnp.stack([grads[l][n] for l in range(DEPTH)]) for n in SMALL_REDUCED],
                  32 * N_CHIPS)
    reduced = []
    for l in reversed(range(DEPTH)):
        g = grads[l]
        up = g["wup_h"]
        big = [jnp.stack([up[0][:, :FF_HALF], up[1][:, :FF_HALF], up[0][:, FF_HALF:], up[1][:, FF_HALF:]]),
               g["wdn_h"].reshape(N_CHIPS, D_FF // N_CHIPS, D_MODEL), _chip_major(g["w_out"], 0), _chip_major(g["ple_w_gate"], 0),
               _chip_major(g["ple_w_proj"], 1), _chip_major(g["w_in"], 1)]
        if l == 0:
            big.append(small.reshape(N_CHIPS, -1, COLS))
        reduced.insert(0, _reduce_scatter(big))
    grad = {n: jnp.stack([reduced[l][i] for l in range(DEPTH)]) for i, n in enumerate(BIG_SHARDED)}
    small_all = _all_gather_chips([reduced[0][-1]], "gather_small_grads")[0].reshape(-1)
    small_shapes = [w[n].shape for n in REPLICATED] + [(DEPTH, SSD_CONV, XBC), (DEPTH, FFN_CONV, 2 * D_FF)]
    grad.update(zip(SMALL_REDUCED, _unpack(small_all, small_shapes)))
    me = 2 * lax.axis_index("x") + lax.axis_index("y")
    for n in F32_SHARDED:
        grad[n] = lax.dynamic_slice_in_dim(grad[n], me * w[n].shape[2], w[n].shape[2], axis=2)

    loss = lax.psum(loss_part[0, 0], ("x", "y", "c"))
    delta, new_m, new_v = {}, {}, {}
    for n in WEIGHT_NAMES:
        delta[n], new_m[n], new_v[n] = _adamw(w[n], grad[n], m[n], v[n], n)
    return (loss, grad_x[None], *[grad[n] for n in WEIGHT_NAMES], *[delta[n] for n in WEIGHT_NAMES],
            *[new_m[n] for n in WEIGHT_NAMES], *[new_v[n] for n in WEIGHT_NAMES])


def kernel(x, p, mix_norm_g, w_in, ssd_conv_w, ssd_conv_b, ssd_dt_bias, ssd_a_log, ssd_d, ssd_norm_g, pool_w, pool_scale, w_out, ffn_norm_g, ffn_w_up, ffn_conv_w, ffn_conv_b, ffn_w_down, ple_norm_g, ple_w_gate, ple_w_proj, final_norm_g, loss_target, m_mix_norm_g, m_w_in, m_ssd_conv_w, m_ssd_conv_b, m_ssd_dt_bias, m_ssd_a_log, m_ssd_d, m_ssd_norm_g, m_pool_w, m_pool_scale, m_w_out, m_ffn_norm_g, m_ffn_w_up, m_ffn_conv_w, m_ffn_conv_b, m_ffn_w_down, m_ple_norm_g, m_ple_w_gate, m_ple_w_proj, m_final_norm_g, v_mix_norm_g, v_w_in, v_ssd_conv_w, v_ssd_conv_b, v_ssd_dt_bias, v_ssd_a_log, v_ssd_d, v_ssd_norm_g, v_pool_w, v_pool_scale, v_w_out, v_ffn_norm_g, v_ffn_w_up, v_ffn_conv_w, v_ffn_conv_b, v_ffn_w_down, v_ple_norm_g, v_ple_w_gate, v_ple_w_proj, v_final_norm_g):
    w = dict(mix_norm_g=mix_norm_g, w_in=w_in, ssd_conv_w=ssd_conv_w, ssd_conv_b=ssd_conv_b, ssd_dt_bias=ssd_dt_bias, ssd_a_log=ssd_a_log, ssd_d=ssd_d, ssd_norm_g=ssd_norm_g, pool_w=pool_w, pool_scale=pool_scale, w_out=w_out, ffn_norm_g=ffn_norm_g, ffn_w_up=ffn_w_up, ffn_conv_w=ffn_conv_w, ffn_conv_b=ffn_conv_b, ffn_w_down=ffn_w_down, ple_norm_g=ple_norm_g, ple_w_gate=ple_w_gate, ple_w_proj=ple_w_proj, final_norm_g=final_norm_g)
    m = dict(mix_norm_g=m_mix_norm_g, w_in=m_w_in, ssd_conv_w=m_ssd_conv_w, ssd_conv_b=m_ssd_conv_b, ssd_dt_bias=m_ssd_dt_bias, ssd_a_log=m_ssd_a_log, ssd_d=m_ssd_d, ssd_norm_g=m_ssd_norm_g, pool_w=m_pool_w, pool_scale=m_pool_scale, w_out=m_w_out, ffn_norm_g=m_ffn_norm_g, ffn_w_up=m_ffn_w_up, ffn_conv_w=m_ffn_conv_w, ffn_conv_b=m_ffn_conv_b, ffn_w_down=m_ffn_w_down, ple_norm_g=m_ple_norm_g, ple_w_gate=m_ple_w_gate, ple_w_proj=m_ple_w_proj, final_norm_g=m_final_norm_g)
    v = dict(mix_norm_g=v_mix_norm_g, w_in=v_w_in, ssd_conv_w=v_ssd_conv_w, ssd_conv_b=v_ssd_conv_b, ssd_dt_bias=v_ssd_dt_bias, ssd_a_log=v_ssd_a_log, ssd_d=v_ssd_d, ssd_norm_g=v_ssd_norm_g, pool_w=v_pool_w, pool_scale=v_pool_scale, w_out=v_w_out, ffn_norm_g=v_ffn_norm_g, ffn_w_up=v_ffn_w_up, ffn_conv_w=v_ffn_conv_w, ffn_conv_b=v_ffn_conv_b, ffn_w_down=v_ffn_w_down, ple_norm_g=v_ple_norm_g, ple_w_gate=v_ple_w_gate, ple_w_proj=v_ple_w_proj, final_norm_g=v_final_norm_g)
    return _train_step(x, p, loss_target, w, m, v)
```

```python
import functools
import math

import jax
import jax.numpy as jnp
from jax import lax
from jax.experimental import pallas as pl
from jax.experimental.pallas import tpu as pltpu

f32, bf16 = jnp.float32, jnp.bfloat16
HI = lax.Precision.HIGHEST

D_MODEL = 1024
D_PLE = 256
DEPTH = 4
SSD_W = 512
HEADS = 8
HEAD_DIM = 64
NSTATE = 128
CHUNK = 128
SSD_CONV = 4
XBC = 1024
POOL_W = 512
POOL_G = 128
WINDOWS = (2, 4, 8, 16)
D_FF = 2816
FF_HALF = D_FF // 2
FFN_CONV = 3
D_IN = 2056
EPS = 1e-6
ADAM_LR, ADAM_B1, ADAM_B2, ADAM_EPS, ADAM_WD, ADAM_STEP = 0.001, 0.9, 0.999, 1e-08, 0.01, 10

LANES = 128
NPROJ = 2048 + LANES
HALO = 16
FHALO = 8
VMEM_LIMIT = 58 * 1024 * 1024
ROW_TILE = 256


def _dot(a, b):
    return jnp.dot(a, b, preferred_element_type=f32)


def _dot_nt(a, b):
    return lax.dot_general(a, b, (((1,), (1,)), ((), ())), preferred_element_type=f32)


def _dot_tn(a, b):
    return lax.dot_general(a, b, (((0,), (0,)), ((), ())), preferred_element_type=f32)


def _dot_hi(a, b):
    return jnp.dot(a, b, precision=HI, preferred_element_type=f32)


def _dot_nt_hi(a, b):
    return lax.dot_general(a, b, (((1,), (1,)), ((), ())), precision=HI, preferred_element_type=f32)


def _rms_fwd(x, g):
    r = lax.rsqrt(jnp.mean(x * x, axis=-1, keepdims=True) + EPS)
    xhat = x * r
    return xhat * g, xhat, r


def _rms_bwd(dy, xhat, r, g):
    dxhat = dy * g
    dx = r * (dxhat - xhat * jnp.mean(dxhat * xhat, axis=-1, keepdims=True))
    return dx, jnp.sum(dy * xhat, axis=0, keepdims=True)


def _sigmoid(x):
    return 1.0 / (1.0 + jnp.exp(-x))


_GELU_C = math.sqrt(2.0 / math.pi)


def _gelu_and_grad(x):
    x2 = x * x
    t = jnp.tanh(_GELU_C * (x + 0.044715 * x * x2))
    g = 0.5 * x * (1.0 + t)
    dg = 0.5 * (1.0 + t) + 0.5 * x * (1.0 - t * t) * _GELU_C * (1.0 + 3.0 * 0.044715 * x2)
    return g, dg


def _gelu(x):
    return 0.5 * x * (1.0 + jnp.tanh(_GELU_C * (x + 0.044715 * x * x * x)))


def _softplus(x):
    return jnp.maximum(x, 0.0) + jnp.log(1.0 + jnp.exp(-jnp.abs(x)))


def _cparams(sem=("arbitrary",)):
    return pltpu.CompilerParams(dimension_semantics=sem, vmem_limit_bytes=VMEM_LIMIT)


def _const_spec(shape):
    nd = len(shape)
    return pl.BlockSpec(shape, lambda *_: (0,) * nd, pipeline_mode=pl.Buffered(1))


WIDE_ROW_TILE = 512


def _row_tile(s, t=ROW_TILE):
    return min(t, s)


def _mix_in_fwd(h, g1, w_r):
    s = h.shape[0]
    t = _row_tile(s, WIDE_ROW_TILE)

    def body(h_ref, g_ref, w_ref, zxu_ref, dtr_ref):
        hn, _, _ = _rms_fwd(h_ref[...], g_ref[...])
        proj = _dot(hn.astype(bf16), w_ref[...])
        zxu_ref[...] = proj[:, :2048].astype(bf16)
        dtr_ref[...] = proj[:, 2048:]

    return pl.pallas_call(
        body, name="mix_in_fwd", grid=(s // t,),
        out_shape=(jax.ShapeDtypeStruct((s, 2048), bf16), jax.ShapeDtypeStruct((s, LANES), f32)),
        in_specs=[pl.BlockSpec((t, D_MODEL), lambda i: (i, 0)), _const_spec((1, D_MODEL)), _const_spec((D_MODEL, NPROJ))],
        out_specs=(pl.BlockSpec((t, 2048), lambda i: (i, 0)), pl.BlockSpec((t, LANES), lambda i: (i, 0))),
        compiler_params=_cparams(),
    )(h, g1, w_r)


def _mix_in_bwd(d_zxu, d_dtr, h, dh1, g1, w_r):
    s = h.shape[0]
    t = _row_tile(s, WIDE_ROW_TILE)
    n = s // t

    def body(dz_ref, dd_ref, h_ref, dh1_ref, g_ref, w_ref, dh_ref, dw_ref, dg_ref, acc):
        i = pl.program_id(0)

        @pl.when(i == 0)
        def _():
            acc[...] = jnp.zeros_like(acc)
            dg_ref[...] = jnp.zeros_like(dg_ref)

        g = g_ref[...]
        hn, xhat, r = _rms_fwd(h_ref[...], g)
        dproj = jnp.concatenate([dz_ref[...], dd_ref[...].astype(bf16)], axis=1)
        d_hn = _dot_nt(dproj, w_ref[...])
        acc[...] += _dot_tn(hn.astype(bf16), dproj)
        dx, dg = _rms_bwd(d_hn, xhat, r, g)
        dg_ref[...] += dg
        dh_ref[...] = dh1_ref[...] + dx

        @pl.when(i == n - 1)
        def _():
            pltpu.sync_copy(acc, dw_ref)

    return pl.pallas_call(
        body, name="mix_in_bwd", grid=(n,),
        out_shape=(jax.ShapeDtypeStruct((s, D_MODEL), f32), jax.ShapeDtypeStruct((D_MODEL, NPROJ), f32),
                   jax.ShapeDtypeStruct((1, D_MODEL), f32)),
        in_specs=[pl.BlockSpec((t, 2048), lambda i: (i, 0)), pl.BlockSpec((t, LANES), lambda i: (i, 0)),
                  pl.BlockSpec((t, D_MODEL), lambda i: (i, 0)), pl.BlockSpec((t, D_MODEL), lambda i: (i, 0)),
                  _const_spec((1, D_MODEL)), _const_spec((D_MODEL, NPROJ))],
        out_specs=(pl.BlockSpec((t, D_MODEL), lambda i: (i, 0)), pl.BlockSpec(memory_space=pl.ANY),
                   pl.BlockSpec((1, D_MODEL), lambda i: (0, 0))),
        scratch_shapes=[pltpu.VMEM((D_MODEL, NPROJ), f32)],
        compiler_params=_cparams(),
    )(d_zxu, d_dtr, h, dh1, g1, w_r)


def _iota2(shape, dim):
    return lax.broadcasted_iota(jnp.int32, shape, dim)


def _lane_bcast(a, h):
    return jnp.broadcast_to(a[:, h:h + 1], (a.shape[0], LANES))


def _to_columns(cols):
    lane = _iota2((cols[0].shape[0], LANES), 1)
    out = jnp.where(lane == 0, cols[0], 0.0)
    for h in range(1, len(cols)):
        out = out + jnp.where(lane == h, cols[h], 0.0)
    return out


def _ssd_pre(zx, dtr, xext, cw, cb, dtb, alog):
    c = cb + cw[0:1] * xext[pl.ds(HALO - 3, CHUNK), :]
    for k in range(1, SSD_CONV):
        c = c + cw[k:k + 1] * xext[pl.ds(HALO - 3 + k, CHUNK), :]
    sig_c = _sigmoid(c)
    xc = c * sig_c
    dt = _softplus(dtr + dtb)
    a_neg = -jnp.exp(alog)
    a = dt * a_neg
    lo = _iota2((CHUNK, LANES), 1) < HEAD_DIM
    dt_w = jnp.concatenate([jnp.where(lo, _lane_bcast(dt, 2 * pp), _lane_bcast(dt, 2 * pp + 1)) for pp in range(4)], axis=1)
    xs = xc[:, :SSD_W]
    xd = xs * dt_w
    tril = (_iota2((CHUNK, CHUNK), 0) >= _iota2((CHUNK, CHUNK), 1))
    acs = _dot_hi(tril.astype(f32), a)
    acs_b = [_lane_bcast(acs, h) for h in range(HEADS)]
    return dict(c=c, sig_c=sig_c, xc=xc, xs=xs, dt=dt, a_neg=a_neg, a=a, dt_w=dt_w, xd=xd, acs_b=acs_b, tril=tril, lo=lo)


def _pair_fwd(q, pp, s_in):
    g = pp // 2
    lo = q["lo"]
    b_g = q["xc"][:, SSD_W + g * NSTATE:SSD_W + (g + 1) * NSTATE]
    c_g = q["xc"][:, SSD_W + 2 * NSTATE + g * NSTATE:SSD_W + 2 * NSTATE + (g + 1) * NSTATE]
    b16, c16 = b_g.astype(bf16), c_g.astype(bf16)
    gmat = _dot_nt(c16, b16)
    xp = q["xd"][:, pp * LANES:(pp + 1) * LANES]
    xp16 = xp.astype(bf16)
    ab0 = q["acs_b"][2 * pp]
    ab1 = q["acs_b"][2 * pp + 1]
    ls, ms, ys = [], [], []
    for ab in (ab0, ab1):
        lmat = jnp.exp(jnp.where(q["tril"], ab - ab.T, -jnp.inf))
        mmat = gmat * lmat
        ls.append(lmat)
        ms.append(mmat)
        ys.append(_dot(mmat.astype(bf16), xp16))
    y_diag = jnp.where(lo, ys[0], ys[1])
    ab_pair = jnp.where(lo, ab0, ab1)
    e_pair = jnp.exp(ab_pair)
    s16 = s_in.astype(bf16)
    y_off = _dot(c16, s16) * e_pair
    alast = ab_pair[CHUNK - 1:CHUNK, :]
    dec_pair = jnp.exp(alast - ab_pair)
    xdec = xp * dec_pair
    st = _dot_tn(b16, xdec.astype(bf16))
    cd_pair = jnp.exp(alast)
    s_out = s_in * cd_pair + st
    return dict(b16=b16, c16=c16, gmat=gmat, xp=xp, xp16=xp16, ls=ls, ms=ms, y=y_diag + y_off, y_off=y_off,
                e_pair=e_pair, dec_pair=dec_pair, xdec=xdec, cd_pair=cd_pair, s_out=s_out, s16=s16, lo=lo)


def _gate_norm_fwd(y_pre, z, ng):
    sz = _sigmoid(z)
    yg = y_pre * (z * sz)
    outs, stats = [], []
    half = SSD_W // 2
    for gi in range(2):
        o, xhat, r = _rms_fwd(yg[:, gi * half:(gi + 1) * half], ng[:, gi * half:(gi + 1) * half])
        outs.append(o)
        stats.append((xhat, r))
    return jnp.concatenate(outs, axis=1), sz, stats


def _pool_fwd(uext, u, row0, pw_ref, ps):
    pos = (row0 + _iota2((CHUNK, 1), 0) + 1).astype(f32)
    pooled, mixed, invs = [], [], []
    for gi, w in enumerate(WINDOWS):
        sl = slice(gi * POOL_G, (gi + 1) * POOL_G)
        acc = uext[pl.ds(HALO, CHUNK), sl]
        for j in range(1, w):
            acc = acc + uext[pl.ds(HALO - j, CHUNK), sl]
        den = jnp.minimum(pos, float(w))
        pg = acc / den - u[:, sl]
        pooled.append(pg)
        invs.append(den)
        mixed.append(_dot(pg.astype(bf16), pw_ref[gi]))
    mixed = jnp.concatenate(mixed, axis=1)
    return mixed * ps, pooled, mixed, invs


def _ssd_specs(s):
    nc = s // CHUNK
    hb = CHUNK // HALO
    return nc, hb


def _ssd_param_specs():
    return [_const_spec((SSD_CONV, XBC)), _const_spec((1, XBC)), _const_spec((1, LANES)), _const_spec((1, LANES)),
            _const_spec((1, SSD_W)), _const_spec((1, SSD_W)), _const_spec((4, POOL_G, POOL_G)), _const_spec((1, POOL_W))]


def _ssd_pool_fwd(zxu, dtr, prm):
    s = zxu.shape[0]
    nc, hb = _ssd_specs(s)

    def body(zx_ref, halo_ref, dtr_ref, cw_ref, cb_ref, dtb_ref, alog_ref, dsk_ref, ng_ref, pw_ref, ps_ref,
             ymix_ref, st_ref, state, xext, uext):
        i = pl.program_id(0)

        @pl.when(i == 0)
        def _():
            state[...] = jnp.zeros_like(state)

        zx = zx_ref[...].astype(f32)
        halo = jnp.where(i > 0, halo_ref[...].astype(f32), 0.0)
        xext[0:HALO, :] = halo[:, SSD_W:SSD_W + XBC]
        xext[HALO:, :] = zx[:, SSD_W:SSD_W + XBC]
        uext[0:HALO, :] = halo[:, SSD_W + XBC:]
        uext[HALO:, :] = zx[:, SSD_W + XBC:]
        q = _ssd_pre(zx, dtr_ref[...], xext, cw_ref[...], cb_ref[...], dtb_ref[...], alog_ref[...])
        ys = []
        for pp in range(4):
            s_in = state[pp]
            st_ref[0, pp] = s_in
            r = _pair_fwd(q, pp, s_in)
            state[pp] = r["s_out"]
            ys.append(r["y"])
        y_pre = jnp.concatenate(ys, axis=1) + q["xs"] * dsk_ref[...]
        y_ssd, _, _ = _gate_norm_fwd(y_pre, zx[:, :SSD_W], ng_ref[...])
        y_pool, _, _, _ = _pool_fwd(uext, zx[:, SSD_W + XBC:], i * CHUNK, pw_ref, ps_ref[...])
        ymix_ref[:, :SSD_W] = y_ssd.astype(bf16)
        ymix_ref[:, SSD_W:] = y_pool.astype(bf16)

    return pl.pallas_call(
        body, name="ssd_pool_fwd", grid=(nc,),
        out_shape=(jax.ShapeDtypeStruct((s, D_MODEL), bf16), jax.ShapeDtypeStruct((nc, 4, NSTATE, LANES), f32)),
        in_specs=[pl.BlockSpec((CHUNK, 2048), lambda i: (i, 0)),
                  pl.BlockSpec((HALO, 2048), lambda i: (jnp.maximum(i * hb - 1, 0), 0)),
                  pl.BlockSpec((CHUNK, LANES), lambda i: (i, 0))] + _ssd_param_specs(),
        out_specs=(pl.BlockSpec((CHUNK, D_MODEL), lambda i: (i, 0)),
                   pl.BlockSpec((1, 4, NSTATE, LANES), lambda i: (i, 0, 0, 0))),
        scratch_shapes=[pltpu.VMEM((4, NSTATE, LANES), f32), pltpu.VMEM((HALO + CHUNK, XBC), f32),
                        pltpu.VMEM((HALO + CHUNK, POOL_W), f32)],
        compiler_params=_cparams(),
    )(zxu, zxu, dtr, *prm)


def _ssd_pool_bwd(d_ymix, zxu, dtr, states, prm):
    s = zxu.shape[0]
    nc, hb = _ssd_specs(s)
    rev = lambda i: nc - 1 - i

    def body(dy_ref, zx_ref, halo_ref, dtr_ref, st_ref, cw_ref, cb_ref, dtb_ref, alog_ref, dsk_ref, ng_ref, pw_ref, ps_ref,
             dzx_ref, ddtr_ref, dcw_ref, dcb_ref, ddtb_ref, dalog_ref, ddsk_ref, dng_ref, dpw_ref, dps_ref,
             dstate, xext, uext, dxext, duext, cx, cu):
        i = pl.program_id(0)
        ci = nc - 1 - i

        @pl.when(i == 0)
        def _():
            dstate[...] = jnp.zeros_like(dstate)
            cx[...] = jnp.zeros_like(cx)
            cu[...] = jnp.zeros_like(cu)
            for r in (dcw_ref, dcb_ref, ddtb_ref, dalog_ref, ddsk_ref, dng_ref, dpw_ref, dps_ref):
                r[...] = jnp.zeros_like(r)

        zx = zx_ref[...].astype(f32)
        halo = jnp.where(ci > 0, halo_ref[...].astype(f32), 0.0)
        xext[0:HALO, :] = halo[:, SSD_W:SSD_W + XBC]
        xext[HALO:, :] = zx[:, SSD_W:SSD_W + XBC]
        uext[0:HALO, :] = halo[:, SSD_W + XBC:]
        uext[HALO:, :] = zx[:, SSD_W + XBC:]
        cw = cw_ref[...]
        q = _ssd_pre(zx, dtr_ref[...], xext, cw, cb_ref[...], dtb_ref[...], alog_ref[...])
        z = zx[:, :SSD_W]
        dy = dy_ref[...].astype(f32)
        d_yssd, d_ypool = dy[:, :SSD_W], dy[:, SSD_W:]

        pairs = [_pair_fwd(q, pp, st_ref[0, pp]) for pp in range(4)]
        dsk = dsk_ref[...]
        ng = ng_ref[...]
        y_pre = jnp.concatenate([r["y"] for r in pairs], axis=1) + q["xs"] * dsk
        _, sz, stats = _gate_norm_fwd(y_pre, z, ng)

        half = SSD_W // 2
        d_yg, d_ng = [], []
        for gi in range(2):
            xhat, r = stats[gi]
            dx, dg = _rms_bwd(d_yssd[:, gi * half:(gi + 1) * half], xhat, r, ng[:, gi * half:(gi + 1) * half])
            d_yg.append(dx)
            d_ng.append(dg)
        d_yg = jnp.concatenate(d_yg, axis=1)
        dng_ref[...] += jnp.concatenate(d_ng, axis=1)
        silu_z = z * sz
        d_ypre = d_yg * silu_z
        d_z = d_yg * y_pre * (sz * (1.0 + z * (1.0 - sz)))
        ddsk_ref[...] += jnp.sum(d_ypre * q["xs"], axis=0, keepdims=True)

        d_xd, acs_cols, dt_cols = [], [], []
        d_b = [None, None]
        d_c = [None, None]
        d_g = [None, None]
        last_row = _iota2((CHUNK, LANES), 0) == CHUNK - 1
        for pp in range(4):
            g = pp // 2
            r = pairs[pp]
            lo = r["lo"]
            dyp = d_ypre[:, pp * LANES:(pp + 1) * LANES]
            dyp16 = dyp.astype(bf16)
            ds_out = dstate[pp]
            ds16 = ds_out.astype(bf16)
            dye16 = (dyp * r["e_pair"]).astype(bf16)
            dstate[pp] = r["cd_pair"] * ds_out + _dot_tn(r["c16"], dye16)
            dc = _dot_nt(dye16, r["s16"])
            dxdec = _dot(r["b16"], ds16)
            db = _dot_nt(r["xdec"].astype(bf16), ds16)
            dxp = dxdec * r["dec_pair"]
            t2 = dxdec * r["xdec"]
            tail = jnp.sum(t2, axis=0, keepdims=True) + jnp.sum(ds_out * st_ref[0, pp] * r["cd_pair"], axis=0, keepdims=True)
            rp = dyp * r["y_off"] - t2 + jnp.where(last_row, tail, 0.0)
            dxs = []
            for hh in range(2):
                msk = lo if hh == 0 else jnp.logical_not(lo)
                m16 = r["ms"][hh].astype(bf16)
                dxs.append(_dot_tn(m16, dyp16))
                dm = _dot_nt(jnp.where(msk, dyp, 0.0).astype(bf16), r["xp16"])
                wmat = dm * r["ms"][hh]
                acs_cols.append(jnp.sum(wmat - wmat.T + jnp.where(msk, rp, 0.0), axis=1, keepdims=True))
                dgh = dm * r["ls"][hh]
                d_g[g] = dgh if d_g[g] is None else d_g[g] + dgh
            dxp = dxp + jnp.where(lo, dxs[0], dxs[1])
            d_xd.append(dxp)
            xprod = dxp * q["xs"][:, pp * LANES:(pp + 1) * LANES]
            dt_cols.append(jnp.sum(jnp.where(lo, xprod, 0.0), axis=1, keepdims=True))
            dt_cols.append(jnp.sum(jnp.where(lo, 0.0, xprod), axis=1, keepdims=True))
            d_b[g] = db if d_b[g] is None else d_b[g] + db
            d_c[g] = dc if d_c[g] is None else d_c[g] + dc
        for g in range(2):
            dg16 = d_g[g].astype(bf16)
            d_c[g] = d_c[g] + _dot(dg16, pairs[2 * g]["b16"])
            d_b[g] = d_b[g] + _dot_tn(dg16, pairs[2 * g]["c16"])
        d_xd = jnp.concatenate(d_xd, axis=1)
        triu = (_iota2((CHUNK, CHUNK), 0) <= _iota2((CHUNK, CHUNK), 1)).astype(f32)
        d_a = _dot_hi(triu, _to_columns(acs_cols))
        d_dt = d_a * q["a_neg"] + _to_columns(dt_cols)
        dalog_ref[...] += jnp.sum(d_a * q["dt"], axis=0, keepdims=True) * q["a_neg"]
        d_dtr = d_dt * _sigmoid(dtr_ref[...] + dtb_ref[...])
        ddtr_ref[...] = d_dtr
        ddtb_ref[...] += jnp.sum(d_dtr, axis=0, keepdims=True)
        d_xs = d_ypre * dsk + d_xd * q["dt_w"]

        d_xc = jnp.concatenate([d_xs, d_b[0], d_b[1], d_c[0], d_c[1]], axis=1)
        sc = q["sig_c"]
        d_conv = d_xc * (sc * (1.0 + q["c"] * (1.0 - sc)))
        dcb_ref[...] += jnp.sum(d_conv, axis=0, keepdims=True)
        dxext[...] = jnp.zeros_like(dxext)
        for k in range(SSD_CONV):
            dcw_ref[k:k + 1, :] += jnp.sum(d_conv * xext[pl.ds(HALO - 3 + k, CHUNK), :], axis=0, keepdims=True)
            dxext[pl.ds(HALO - 3 + k, CHUNK), :] += cw[k:k + 1] * d_conv
        dxext[pl.ds(CHUNK, HALO), :] += cx[...]
        cx[...] = dxext[0:HALO, :]

        ps = ps_ref[...]
        u = zx[:, SSD_W + XBC:]
        _, pooled, mixed, dens = _pool_fwd(uext, u, ci * CHUNK, pw_ref, ps)
        dps_ref[...] += jnp.sum(d_ypool * mixed, axis=0, keepdims=True)
        d_mixed = d_ypool * ps
        duext[...] = jnp.zeros_like(duext)
        for gi, w in enumerate(WINDOWS):
            sl = slice(gi * POOL_G, (gi + 1) * POOL_G)
            dm16 = d_mixed[:, sl].astype(bf16)
            dpw_ref[gi] += _dot_tn(pooled[gi].astype(bf16), dm16)
            d_pg = _dot_nt(dm16, pw_ref[gi])
            d_mean = d_pg / dens[gi]
            duext[pl.ds(HALO, CHUNK), sl] += d_mean - d_pg
            for j in range(1, w):
                duext[pl.ds(HALO - j, CHUNK), sl] += d_mean
        duext[pl.ds(CHUNK, HALO), :] += cu[...]
        cu[...] = duext[0:HALO, :]

        dzx_ref[:, :SSD_W] = d_z.astype(bf16)
        dzx_ref[:, SSD_W:SSD_W + XBC] = dxext[HALO:, :].astype(bf16)
        dzx_ref[:, SSD_W + XBC:] = duext[HALO:, :].astype(bf16)

    small = lambda shape: pl.BlockSpec(shape, lambda i: (0,) * len(shape))
    small_shapes = [(SSD_CONV, XBC), (1, XBC), (1, LANES), (1, LANES), (1, SSD_W), (1, SSD_W), (4, POOL_G, POOL_G), (1, POOL_W)]
    return pl.pallas_call(
        body, name="ssd_pool_bwd", grid=(nc,),
        out_shape=(jax.ShapeDtypeStruct((s, 2048), bf16), jax.ShapeDtypeStruct((s, LANES), f32))
        + tuple(jax.ShapeDtypeStruct(sh, f32) for sh in small_shapes),
        in_specs=[pl.BlockSpec((CHUNK, D_MODEL), lambda i: (rev(i), 0)),
                  pl.BlockSpec((CHUNK, 2048), lambda i: (rev(i), 0)),
                  pl.BlockSpec((HALO, 2048), lambda i: (jnp.maximum(rev(i) * hb - 1, 0), 0)),
                  pl.BlockSpec((CHUNK, LANES), lambda i: (rev(i), 0)),
                  pl.BlockSpec((1, 4, NSTATE, LANES), lambda i: (rev(i), 0, 0, 0))] + _ssd_param_specs(),
        out_specs=(pl.BlockSpec((CHUNK, 2048), lambda i: (rev(i), 0)), pl.BlockSpec((CHUNK, LANES), lambda i: (rev(i), 0)))
        + tuple(small(sh) for sh in small_shapes),
        scratch_shapes=[pltpu.VMEM((4, NSTATE, LANES), f32), pltpu.VMEM((HALO + CHUNK, XBC), f32),
                        pltpu.VMEM((HALO + CHUNK, POOL_W), f32), pltpu.VMEM((HALO + CHUNK, XBC), f32),
                        pltpu.VMEM((HALO + CHUNK, POOL_W), f32), pltpu.VMEM((HALO, XBC), f32), pltpu.VMEM((HALO, POOL_W), f32)],
        compiler_params=_cparams(),
    )(d_ymix, zxu, zxu, dtr, states, *prm)


FFN_BWD_TILE = 256


def _prev_halo_spec(t, width):
    hb = t // HALO
    return pl.BlockSpec((HALO, width), lambda i: (jnp.maximum(i * hb - 1, 0), 0))


def _ffn_half(hn16, j, wup_ref, cw_ref, cb_ref, up_scr, rows):
    up_scr[...] = _dot(hn16, wup_ref[j])
    cw = cw_ref[j]
    cv = cb_ref[j] + cw[0:1] * up_scr[pl.ds(HALO - 2, rows), :]
    for k in range(1, FFN_CONV):
        cv = cv + cw[k:k + 1] * up_scr[pl.ds(HALO - 2 + k, rows), :]
    return cv


def _out_ffn_ple_fwd(h, ymix, p_l, w_out, g2, wup_h, cw_h, cb_h, wdn_h, g3, w_gate, w_proj):
    s = h.shape[0]
    t = _row_tile(s)
    n = s // t

    def body(h_ref, hh_ref, ym_ref, ymh_ref, p_ref, wo_ref, g2_ref, wup_ref, cw_ref, cb_ref, wdn_ref, g3_ref, wg_ref, wp_ref,
             h1_ref, h2_ref, h3_ref, up16_ref, up_scr):
        i = pl.program_id(0)
        hh = jnp.where(i > 0, hh_ref[...], 0.0)
        ymh = jnp.where(i > 0, ymh_ref[...].astype(f32), 0.0)
        h_ext = jnp.concatenate([hh, h_ref[...]], axis=0)
        ym_ext = jnp.concatenate([ymh, ym_ref[...].astype(f32)], axis=0).astype(bf16)
        h1_ext = h_ext + _dot(ym_ext, wo_ref[...])
        hn2, _, _ = _rms_fwd(h1_ext, g2_ref[...])
        hn16 = hn2.astype(bf16)
        h1 = h1_ext[HALO:, :]
        acc = h1
        for j in range(2):
            cv = _ffn_half(hn16, j, wup_ref, cw_ref, cb_ref, up_scr, t)
            up16_ref[j] = up_scr[pl.ds(HALO, t), :].astype(bf16)
            act = _gelu(cv[:, :FF_HALF]) * cv[:, FF_HALF:]
            acc = acc + _dot(act.astype(bf16), wdn_ref[j])
        h2 = acc
        hn3, _, _ = _rms_fwd(h2, g3_ref[...])
        gate = _sigmoid(_dot(hn3.astype(bf16), wg_ref[...]))
        pp = _dot(p_ref[...].astype(bf16), wp_ref[...])
        h1_ref[...] = h1
        h2_ref[...] = h2
        h3_ref[...] = h2 + pp * gate

    row = lambda w: pl.BlockSpec((t, w), lambda i: (i, 0))
    return pl.pallas_call(
        body, name="out_ffn_ple_fwd", grid=(n,),
        out_shape=tuple(jax.ShapeDtypeStruct((s, D_MODEL), f32) for _ in range(3)) + (jax.ShapeDtypeStruct((2, s, D_FF), bf16),),
        in_specs=[row(D_MODEL), _prev_halo_spec(t, D_MODEL), row(D_MODEL), _prev_halo_spec(t, D_MODEL), row(D_PLE),
                  _const_spec((D_MODEL, D_MODEL)), _const_spec((1, D_MODEL)), _const_spec((2, D_MODEL, D_FF)),
                  _const_spec((2, FFN_CONV, D_FF)), _const_spec((2, 1, D_FF)), _const_spec((2, FF_HALF, D_MODEL)),
                  _const_spec((1, D_MODEL)), _const_spec((D_MODEL, D_MODEL)), _const_spec((D_PLE, D_MODEL))],
        out_specs=tuple(row(D_MODEL) for _ in range(3)) + (pl.BlockSpec((2, t, D_FF), lambda i: (0, i, 0)),),
        scratch_shapes=[pltpu.VMEM((HALO + t, D_FF), f32)],
        compiler_params=_cparams(),
    )(h, h, ymix, ymix, p_l, w_out, g2, wup_h, cw_h, cb_h, wdn_h, g3, w_gate, w_proj)


def _ple_bwd(dh3, h2, p_l, g3, w_gate, w_proj):
    s = h2.shape[0]
    t = _row_tile(s, WIDE_ROW_TILE)
    n = s // t

    def body(dh3_ref, h2_ref, p_ref, g3_ref, wg_ref, wp_ref, dh2_ref, dwg_ref, dwp_ref, dg3_ref):
        i = pl.program_id(0)

        @pl.when(i == 0)
        def _():
            dwg_ref[...] = jnp.zeros_like(dwg_ref)
            dwp_ref[...] = jnp.zeros_like(dwp_ref)
            dg3_ref[...] = jnp.zeros_like(dg3_ref)

        g3 = g3_ref[...]
        dh3 = dh3_ref[...]
        hn3, xhat, r = _rms_fwd(h2_ref[...], g3)
        hn16 = hn3.astype(bf16)
        gate = _sigmoid(_dot(hn16, wg_ref[...]))
        p16 = p_ref[...].astype(bf16)
        pp = _dot(p16, wp_ref[...])
        d_pp = (dh3 * gate).astype(bf16)
        d_pre = (dh3 * pp * gate * (1.0 - gate)).astype(bf16)
        dwp_ref[...] += _dot_tn(p16, d_pp)
        dwg_ref[...] += _dot_tn(hn16, d_pre)
        dx, dg = _rms_bwd(_dot_nt(d_pre, wg_ref[...]), xhat, r, g3)
        dg3_ref[...] += dg
        dh2_ref[...] = dh3 + dx

    row = lambda w: pl.BlockSpec((t, w), lambda i: (i, 0))
    fixed = lambda shape: pl.BlockSpec(shape, lambda i: (0,) * len(shape))
    return pl.pallas_call(
        body, name="ple_bwd", grid=(n,),
        out_shape=(jax.ShapeDtypeStruct((s, D_MODEL), f32), jax.ShapeDtypeStruct((D_MODEL, D_MODEL), f32),
                   jax.ShapeDtypeStruct((D_PLE, D_MODEL), f32), jax.ShapeDtypeStruct((1, D_MODEL), f32)),
        in_specs=[row(D_MODEL), row(D_MODEL), row(D_PLE), _const_spec((1, D_MODEL)), _const_spec((D_MODEL, D_MODEL)),
                  _const_spec((D_PLE, D_MODEL))],
        out_specs=(row(D_MODEL), fixed((D_MODEL, D_MODEL)), fixed((D_PLE, D_MODEL)), fixed((1, D_MODEL))),
        compiler_params=_cparams(),
    )(dh3, h2, p_l, g3, w_gate, w_proj)


def _ffn_bwd(dh2, h1, up16, g2, wup_h, cw_h, cb_h, wdn_h):
    s = h1.shape[0]
    t = min(FFN_BWD_TILE, s)
    n = s // t
    hb = t // HALO
    last_hb = s // HALO - 1

    def body(dh2_ref, dh2n_ref, h1_ref, up_ref, upp_ref, upn_ref, g2_ref, wup_ref, cw_ref, cb_ref, wdn_ref,
             part_ref, dwup_ref, dwdn_ref, dcw_ref, dcb_ref, up_scr, dcv_scr, acc_up, acc_dn):
        j = pl.program_id(0)
        i = pl.program_id(1)

        @pl.when(i == 0)
        def _():
            acc_up[...] = jnp.zeros_like(acc_up)
            acc_dn[...] = jnp.zeros_like(acc_dn)
            dcw_ref[...] = jnp.zeros_like(dcw_ref)
            dcb_ref[...] = jnp.zeros_like(dcb_ref)

        hn2, _, _ = _rms_fwd(h1_ref[...], g2_ref[...])
        hn16 = hn2.astype(bf16)
        up_scr[0:HALO, :] = jnp.where(i > 0, upp_ref[0].astype(f32), 0.0)
        up_scr[HALO:HALO + t, :] = up_ref[0].astype(f32)
        up_scr[HALO + t:, :] = upn_ref[0].astype(f32)
        cw = cw_ref[0]
        rows = t + HALO
        cv = cb_ref[0] + cw[0:1] * up_scr[pl.ds(HALO - 2, rows), :]
        for k in range(1, FFN_CONV):
            cv = cv + cw[k:k + 1] * up_scr[pl.ds(HALO - 2 + k, rows), :]
        dh2 = dh2_ref[...]
        dh2n = jnp.where(i < n - 1, dh2n_ref[...], 0.0)
        dh2_ext16 = jnp.concatenate([dh2, dh2n], axis=0).astype(bf16)
        d_act = _dot_nt(dh2_ext16, wdn_ref[0])
        gate, val = cv[:, :FF_HALF], cv[:, FF_HALF:]
        gl, dgl = _gelu_and_grad(gate)
        dcv_scr[:, :FF_HALF] = d_act * val * dgl
        dcv_scr[:, FF_HALF:] = d_act * gl
        act16 = (gl[:t] * val[:t]).astype(bf16)
        acc_dn[...] += _dot_tn(act16, dh2_ext16[:t])
        d_cv = dcv_scr[pl.ds(0, t), :]
        dcb_ref[0] += jnp.sum(d_cv, axis=0, keepdims=True)
        d_up = cw[2:3] * d_cv
        dcw_ref[0, 2:3, :] += jnp.sum(d_cv * up_scr[pl.ds(HALO, t), :], axis=0, keepdims=True)
        for k in range(FFN_CONV - 1):
            dcw_ref[0, k:k + 1, :] += jnp.sum(d_cv * up_scr[pl.ds(HALO - 2 + k, t), :], axis=0, keepdims=True)
            d_up = d_up + cw[k:k + 1] * dcv_scr[pl.ds(2 - k, t), :]
        d_up16 = d_up.astype(bf16)
        part_ref[0] = _dot_nt(d_up16, wup_ref[0])
        acc_up[...] += _dot_tn(hn16, d_up16)

        @pl.when(i == n - 1)
        def _():
            pltpu.sync_copy(acc_up, dwup_ref.at[j])
            pltpu.sync_copy(acc_dn, dwdn_ref.at[j])

    return pl.pallas_call(
        body, name="ffn_bwd", grid=(2, n),
        out_shape=(jax.ShapeDtypeStruct((2, s, D_MODEL), f32), jax.ShapeDtypeStruct((2, D_MODEL, D_FF), f32),
                   jax.ShapeDtypeStruct((2, FF_HALF, D_MODEL), f32), jax.ShapeDtypeStruct((2, FFN_CONV, D_FF), f32),
                   jax.ShapeDtypeStruct((2, 1, D_FF), f32)),
        in_specs=[pl.BlockSpec((t, D_MODEL), lambda j, i: (i, 0)),
                  pl.BlockSpec((HALO, D_MODEL), lambda j, i: (jnp.minimum((i + 1) * hb, last_hb), 0)),
                  pl.BlockSpec((t, D_MODEL), lambda j, i: (i, 0)),
                  pl.BlockSpec((1, t, D_FF), lambda j, i: (j, i, 0)),
                  pl.BlockSpec((1, HALO, D_FF), lambda j, i: (j, jnp.maximum(i * hb - 1, 0), 0)),
                  pl.BlockSpec((1, HALO, D_FF), lambda j, i: (j, jnp.minimum((i + 1) * hb, last_hb), 0)),
                  _const_spec((1, D_MODEL)),
                  pl.BlockSpec((1, D_MODEL, D_FF), lambda j, i: (j, 0, 0), pipeline_mode=pl.Buffered(1)),
                  pl.BlockSpec((1, FFN_CONV, D_FF), lambda j, i: (j, 0, 0)),
                  pl.BlockSpec((1, 1, D_FF), lambda j, i: (j, 0, 0)),
                  pl.BlockSpec((1, FF_HALF, D_MODEL), lambda j, i: (j, 0, 0), pipeline_mode=pl.Buffered(1))],
        out_specs=(pl.BlockSpec((1, t, D_MODEL), lambda j, i: (j, i, 0)), pl.BlockSpec(memory_space=pl.ANY),
                   pl.BlockSpec(memory_space=pl.ANY), pl.BlockSpec((1, FFN_CONV, D_FF), lambda j, i: (j, 0, 0)),
                   pl.BlockSpec((1, 1, D_FF), lambda j, i: (j, 0, 0))),
        scratch_shapes=[pltpu.VMEM((2 * HALO + t, D_FF), f32), pltpu.VMEM((HALO + t, D_FF), f32),
                        pltpu.VMEM((D_MODEL, D_FF), f32), pltpu.VMEM((FF_HALF, D_MODEL), f32)],
        compiler_params=_cparams(("arbitrary", "arbitrary")),
    )(dh2, dh2, h1, up16, up16, up16, g2, wup_h, cw_h, cb_h, wdn_h)


def _out_bwd(dh2, parts, h1, ymix, g2, w_out):
    s = h1.shape[0]
    t = _row_tile(s, WIDE_ROW_TILE)
    n = s // t

    def body(dh2_ref, part_ref, h1_ref, ym_ref, g2_ref, wo_ref, dh1_ref, dym_ref, dwo_ref, dg2_ref):
        i = pl.program_id(0)

        @pl.when(i == 0)
        def _():
            dwo_ref[...] = jnp.zeros_like(dwo_ref)
            dg2_ref[...] = jnp.zeros_like(dg2_ref)

        g2 = g2_ref[...]
        _, xhat, r = _rms_fwd(h1_ref[...], g2)
        dx, dg = _rms_bwd(part_ref[0] + part_ref[1], xhat, r, g2)
        dg2_ref[...] += dg
        dh1 = dh2_ref[...] + dx
        dh1_ref[...] = dh1
        dh16 = dh1.astype(bf16)
        dym_ref[...] = _dot_nt(dh16, wo_ref[...]).astype(bf16)
        dwo_ref[...] += _dot_tn(ym_ref[...], dh16)

    row = lambda w: pl.BlockSpec((t, w), lambda i: (i, 0))
    fixed = lambda shape: pl.BlockSpec(shape, lambda i: (0,) * len(shape))
    return pl.pallas_call(
        body, name="out_bwd", grid=(n,),
        out_shape=(jax.ShapeDtypeStruct((s, D_MODEL), f32), jax.ShapeDtypeStruct((s, D_MODEL), bf16),
                   jax.ShapeDtypeStruct((D_MODEL, D_MODEL), f32), jax.ShapeDtypeStruct((1, D_MODEL), f32)),
        in_specs=[row(D_MODEL), pl.BlockSpec((2, t, D_MODEL), lambda i: (0, i, 0)), row(D_MODEL), row(D_MODEL),
                  _const_spec((1, D_MODEL)), _const_spec((D_MODEL, D_MODEL))],
        out_specs=(row(D_MODEL), row(D_MODEL), fixed((D_MODEL, D_MODEL)), fixed((1, D_MODEL))),
        compiler_params=_cparams(),
    )(dh2, parts, h1, ymix, g2, w_out)


def _loss_head(h, target, gf):
    s = h.shape[0]
    t = _row_tile(s, WIDE_ROW_TILE)

    def body(h_ref, t_ref, g_ref, dh_ref, dg_ref, loss_ref):
        i = pl.program_id(0)

        @pl.when(i == 0)
        def _():
            dg_ref[...] = jnp.zeros_like(dg_ref)
            loss_ref[...] = jnp.zeros_like(loss_ref)

        g = g_ref[...]
        y, xhat, r = _rms_fwd(h_ref[...], g)
        diff = y - t_ref[...]
        per_row = jnp.mean(diff * diff, axis=-1, keepdims=True)
        loss_ref[...] += 0.5 * jnp.sum(per_row, axis=0, keepdims=True)
        dx, dg = _rms_bwd(diff * (1.0 / D_MODEL), xhat, r, g)
        dg_ref[...] += dg
        dh_ref[...] = dx

    row = pl.BlockSpec((t, D_MODEL), lambda i: (i, 0))
    return pl.pallas_call(
        body, name="loss_head", grid=(s // t,),
        out_shape=(jax.ShapeDtypeStruct((s, D_MODEL), f32), jax.ShapeDtypeStruct((1, D_MODEL), f32),
                   jax.ShapeDtypeStruct((1, LANES), f32)),
        in_specs=[row, row, _const_spec((1, D_MODEL))],
        out_specs=(row, pl.BlockSpec((1, D_MODEL), lambda i: (0, 0)), pl.BlockSpec((1, LANES), lambda i: (0, 0))),
        compiler_params=_cparams(),
    )(h, target, gf)


def _prep_layer(w):
    w_in = w["w_in"]
    w_r = jnp.concatenate([w_in[:, :SSD_W + XBC], w_in[:, SSD_W + XBC + HEADS:], w_in[:, SSD_W + XBC:SSD_W + XBC + HEADS],
                           jnp.zeros((D_MODEL, LANES - HEADS), w_in.dtype)], axis=1)
    pad8 = lambda v: jnp.concatenate([v, jnp.zeros((LANES - HEADS,), f32)]).reshape(1, LANES)
    halves = _halves
    ssd_prm = (w["ssd_conv_w"], w["ssd_conv_b"].reshape(1, XBC), pad8(w["ssd_dt_bias"]), pad8(w["ssd_a_log"]),
               jnp.repeat(w["ssd_d"], HEAD_DIM).reshape(1, SSD_W), w["ssd_norm_g"].reshape(1, SSD_W),
               w["pool_w"], w["pool_scale"].reshape(1, POOL_W))
    return dict(
        g1=w["mix_norm_g"].reshape(1, D_MODEL), w_r=w_r, ssd=ssd_prm, w_out=w["w_out"], g2=w["ffn_norm_g"].reshape(1, D_MODEL),
        wup_h=w["wup_h"], cw_h=halves(w["ffn_conv_w"]), cb_h=halves(w["ffn_conv_b"].reshape(1, 2 * D_FF)),
        wdn_h=w["wdn_h"], g3=w["ple_norm_g"].reshape(1, D_MODEL), w_gate=w["ple_w_gate"], w_proj=w["ple_w_proj"])


def _halves(a):
    return jnp.stack([jnp.concatenate([a[..., j * FF_HALF:(j + 1) * FF_HALF],
                                       a[..., D_FF + j * FF_HALF:D_FF + (j + 1) * FF_HALF]], axis=-1) for j in range(2)])


def _unhalve(a):
    return jnp.concatenate([a[0][..., :FF_HALF], a[1][..., :FF_HALF], a[0][..., FF_HALF:], a[1][..., FF_HALF:]], axis=-1)


def _device_step(x, p, target, layers, final_g):
    preps = [_prep_layer(w) for w in layers]
    saved = []
    h = x
    for l, q in enumerate(preps):
        zxu, dtr = _mix_in_fwd(h, q["g1"], q["w_r"])
        ymix, states = _ssd_pool_fwd(zxu, dtr, q["ssd"])
        h1, h2, h3, up16 = _out_ffn_ple_fwd(h, ymix, p[l], q["w_out"], q["g2"], q["wup_h"], q["cw_h"], q["cb_h"], q["wdn_h"],
                                      q["g3"], q["w_gate"], q["w_proj"])
        saved.append((h, zxu, dtr, ymix, states, h1, h2, up16))
        h = h3
    dh, d_gf, loss = _loss_head(h, target, final_g.reshape(1, D_MODEL))
    grads = [None] * len(preps)
    for l in reversed(range(len(preps))):
        q = preps[l]
        h0, zxu, dtr, ymix, states, h1, h2, up16 = saved[l]
        dh2, d_wg, d_wp, d_g3 = _ple_bwd(dh, h2, p[l], q["g3"], q["w_gate"], q["w_proj"])
        parts, d_wup, d_wdn, d_cw, d_cb = _ffn_bwd(dh2, h1, up16, q["g2"], q["wup_h"], q["cw_h"], q["cb_h"], q["wdn_h"])
        dh1, d_ymix, d_wo, d_g2 = _out_bwd(dh2, parts, h1, ymix, q["g2"], q["w_out"])
        (d_zxu, d_dtr, d_scw, d_scb, d_dtb, d_alog, d_dsk, d_ng, d_pw, d_ps) = _ssd_pool_bwd(d_ymix, zxu, dtr, states, q["ssd"])
        dh, d_wr, d_g1 = _mix_in_bwd(d_zxu, d_dtr, h0, dh1, q["g1"], q["w_r"])
        grads[l] = dict(
            mix_norm_g=d_g1.reshape(D_MODEL),
            w_in=jnp.concatenate([d_wr[:, :SSD_W + XBC], d_wr[:, 2048:2048 + HEADS], d_wr[:, SSD_W + XBC:2048]], axis=1),
            ssd_conv_w=d_scw, ssd_conv_b=d_scb.reshape(XBC), ssd_dt_bias=d_dtb[0, :HEADS], ssd_a_log=d_alog[0, :HEADS],
            ssd_d=jnp.sum(d_dsk.reshape(HEADS, HEAD_DIM), axis=1), ssd_norm_g=d_ng.reshape(SSD_W), pool_w=d_pw,
            pool_scale=d_ps.reshape(POOL_W), w_out=d_wo, ffn_norm_g=d_g2.reshape(D_MODEL), wup_h=d_wup,
            ffn_conv_w=_unhalve(d_cw), ffn_conv_b=_unhalve(d_cb).reshape(2 * D_FF), wdn_h=d_wdn,
            ple_norm_g=d_g3.reshape(D_MODEL), ple_w_gate=d_wg, ple_w_proj=d_wp)
    return loss, dh, grads, d_gf.reshape(D_MODEL)


MESH = pl.DeviceIdType.MESH
COLS = 1024
N_CHIPS = 4
WEIGHT_NAMES = ["mix_norm_g", "w_in", "ssd_conv_w", "ssd_conv_b", "ssd_dt_bias", "ssd_a_log", "ssd_d", "ssd_norm_g", "pool_w",
                "pool_scale", "w_out", "ffn_norm_g", "ffn_w_up", "ffn_conv_w", "ffn_conv_b", "ffn_w_down", "ple_norm_g",
                "ple_w_gate", "ple_w_proj", "final_norm_g"]
SHARD_AXIS = {"w_in": 2, "ssd_conv_w": 2, "w_out": 1, "ffn_w_up": 2, "ffn_conv_w": 2, "ffn_w_down": 1, "ple_w_gate": 1,
              "ple_w_proj": 2}
MATMUL_SHARDED = ["w_in", "w_out", "ffn_w_up", "ffn_w_down", "ple_w_gate", "ple_w_proj"]
F32_SHARDED = ["ssd_conv_w", "ffn_conv_w"]
REPLICATED = [n for n in WEIGHT_NAMES if n not in SHARD_AXIS]


def _mesh_pos():
    return lax.axis_index("x"), lax.axis_index("y"), lax.axis_index("c")


def _hbm():
    return pl.BlockSpec(memory_space=pl.ANY)


def _all_gather_chips(arrays, name):
    n = len(arrays)
    half_rows = [a.shape[0] // 2 for a in arrays]

    def body(*refs):
        x_refs, out_refs, (send_sems, recv_sems) = refs[:n], refs[n:2 * n], refs[2 * n:]
        x, y, c = _mesh_pos()
        me = 2 * x + y
        sib = (x, y, 1 - c)
        flips = [(1 - x, y), (x, 1 - y), (1 - x, 1 - y)]

        def rows(a, chip, half):
            return out_refs[a].at[chip, pl.ds(half * half_rows[a], half_rows[a]), :]

        def copy(k, src, dst, to):
            return pltpu.make_async_remote_copy(src_ref=src, dst_ref=dst, send_sem=send_sems.at[k], recv_sem=recv_sems.at[k],
                                                device_id=to, device_id_type=MESH)

        first = [copy(6 * a + j, x_refs[a].at[pl.ds(c * half_rows[a], half_rows[a]), :], rows(a, me, c), (fx, fy, c))
                 for a in range(n) for j, (fx, fy) in enumerate(flips)]
        for cp in first:
            cp.start()
        passed = []
        for j, (fx, fy) in enumerate(flips):
            for a in range(n):
                blk = rows(a, 2 * fx + fy, c)
                copy(6 * a + j, blk, blk, (fx, fy, c)).wait_recv()
                fwd = copy(6 * a + 3 + j, blk, blk, sib)
                fwd.start()
                passed.append(fwd)
        for j, (fx, fy) in enumerate(flips):
            for a in range(n):
                blk = rows(a, 2 * fx + fy, 1 - c)
                copy(6 * a + 3 + j, blk, blk, sib).wait_recv()
        for cp in first + passed:
            cp.wait_send()

    gathered = pl.pallas_call(
        body, name=name, out_shape=[jax.ShapeDtypeStruct((N_CHIPS,) + a.shape, a.dtype) for a in arrays],
        in_specs=[_hbm()] * n, out_specs=[_hbm()] * n,
        scratch_shapes=[pltpu.SemaphoreType.DMA((6 * n,)), pltpu.SemaphoreType.DMA((6 * n,))],
    )(*arrays)
    me = 2 * lax.axis_index("x") + lax.axis_index("y")
    return [lax.dynamic_update_slice(o, a[None], (me, 0, 0)) for o, a in zip(gathered, arrays)]


def _rs_pair_exchange(gs):
    n = len(gs)

    def body(*refs):
        g_refs, land_refs, (send_sems, recv_sems) = refs[:n], refs[n:2 * n], refs[2 * n:]
        x, y, c = _mesh_pos()
        cps = []
        for a in range(n):
            hr = gs[a].shape[1] // 2
            cps.append(pltpu.make_async_remote_copy(
                src_ref=g_refs[a].at[:, pl.ds((1 - c) * hr, hr), :], dst_ref=land_refs[a], send_sem=send_sems.at[a],
                recv_sem=recv_sems.at[a], device_id=(x, y, 1 - c), device_id_type=MESH))
        for cp in cps:
            cp.start()
        for cp in cps:
            cp.wait()

    return pl.pallas_call(
        body, name="rs_pair_exchange",
        out_shape=[jax.ShapeDtypeStruct((N_CHIPS, g.shape[1] // 2, g.shape[2]), g.dtype) for g in gs],
        in_specs=[_hbm()] * n, out_specs=[_hbm()] * n,
        scratch_shapes=[pltpu.SemaphoreType.DMA((n,)), pltpu.SemaphoreType.DMA((n,))],
    )(*gs)


def _rs_pair_add(g, land, c_idx):
    _, r, cols = g.shape
    hr = r // 2

    def body(c_ref, g_ref, l_ref, o_ref):
        o_ref[...] = (g_ref[...] + l_ref[...]).astype(bf16)

    return pl.pallas_call(
        body, name="rs_pair_add", out_shape=jax.ShapeDtypeStruct((N_CHIPS, hr, cols), bf16),
        grid_spec=pltpu.PrefetchScalarGridSpec(
            num_scalar_prefetch=1, grid=(N_CHIPS,),
            in_specs=[pl.BlockSpec((1, hr, cols), lambda k, c_ref: (k, c_ref[0], 0)),
                      pl.BlockSpec((1, hr, cols), lambda k, c_ref: (k, 0, 0))],
            out_specs=pl.BlockSpec((1, hr, cols), lambda k, c_ref: (k, 0, 0))),
        compiler_params=_cparams(),
    )(c_idx, g, land)


def _rs_chip_exchange(parts):
    n = len(parts)

    def body(*refs):
        p_refs, land_refs, (send_sems, recv_sems) = refs[:n], refs[n:2 * n], refs[2 * n:]
        x, y, c = _mesh_pos()
        flips = [(1 - x, y), (x, 1 - y), (1 - x, 1 - y)]
        cps = [pltpu.make_async_remote_copy(src_ref=p_refs[a].at[2 * fx + fy], dst_ref=land_refs[a].at[j],
                                            send_sem=send_sems.at[3 * a + j], recv_sem=recv_sems.at[3 * a + j],
                                            device_id=(fx, fy, c), device_id_type=MESH)
               for a in range(n) for j, (fx, fy) in enumerate(flips)]
        for cp in cps:
            cp.start()
        for cp in cps:
            cp.wait()

    return pl.pallas_call(
        body, name="rs_chip_exchange", out_shape=[jax.ShapeDtypeStruct((3,) + p.shape[1:], p.dtype) for p in parts],
        in_specs=[_hbm()] * n, out_specs=[_hbm()] * n,
        scratch_shapes=[pltpu.SemaphoreType.DMA((3 * n,)), pltpu.SemaphoreType.DMA((3 * n,))],
    )(*parts)


def _rs_chip_add(part, land, me_idx):
    _, hr, cols = part.shape

    def body(me_ref, p_ref, l_ref, o_ref):
        o_ref[...] = ((p_ref[0].astype(f32) + l_ref[0].astype(f32)) + l_ref[1].astype(f32)) + l_ref[2].astype(f32)

    return pl.pallas_call(
        body, name="rs_chip_add", out_shape=jax.ShapeDtypeStruct((hr, cols), f32),
        grid_spec=pltpu.PrefetchScalarGridSpec(
            num_scalar_prefetch=1, grid=(1,),
            in_specs=[pl.BlockSpec((1, hr, cols), lambda i, me_ref: (me_ref[0], 0, 0)),
                      pl.BlockSpec((3, hr, cols), lambda i, me_ref: (0, 0, 0))],
            out_specs=pl.BlockSpec((hr, cols), lambda i, me_ref: (0, 0))),
        compiler_params=_cparams(),
    )(me_idx, part, land)


def _rs_pair_share(reds):
    n = len(reds)

    def body(*refs):
        r_refs, out_refs, (send_sems, recv_sems) = refs[:n], refs[n:2 * n], refs[2 * n:]
        x, y, c = _mesh_pos()

        def copy(a, half):
            hr = reds[a].shape[0]
            return pltpu.make_async_remote_copy(src_ref=r_refs[a], dst_ref=out_refs[a].at[pl.ds(half * hr, hr), :],
                                                send_sem=send_sems.at[a], recv_sem=recv_sems.at[a], device_id=(x, y, 1 - c),
                                                device_id_type=MESH)

        sends = [copy(a, c) for a in range(n)]
        for cp in sends:
            cp.start()
        for a in range(n):
            copy(a, 1 - c).wait_recv()
        for cp in sends:
            cp.wait_send()

    both = pl.pallas_call(
        body, name="rs_pair_share", out_shape=[jax.ShapeDtypeStruct((2 * r.shape[0], r.shape[1]), r.dtype) for r in reds],
        in_specs=[_hbm()] * n, out_specs=[_hbm()] * n,
        scratch_shapes=[pltpu.SemaphoreType.DMA((n,)), pltpu.SemaphoreType.DMA((n,))],
    )(*reds)
    c = lax.axis_index("c")
    return [lax.dynamic_update_slice(o, r, (c * r.shape[0], 0)) for o, r in zip(both, reds)]


def _reduce_scatter(gs):
    x, y, c = _mesh_pos()
    c_idx = jnp.reshape(c, (1,)).astype(jnp.int32)
    me_idx = jnp.reshape(2 * x + y, (1,)).astype(jnp.int32)
    lands = _rs_pair_exchange(gs)
    parts = [_rs_pair_add(g, land, c_idx) for g, land in zip(gs, lands)]
    lands2 = _rs_chip_exchange(parts)
    reds = [_rs_chip_add(part, land, me_idx) for part, land in zip(parts, lands2)]
    return _rs_pair_share(reds)


def _pack(parts, row_multiple):
    flat = jnp.concatenate([a.reshape(-1) for a in parts])
    n = flat.shape[0]
    rows = -(-n // COLS)
    rows = -(-rows // row_multiple) * row_multiple
    return jnp.pad(flat, (0, rows * COLS - n)).reshape(rows, COLS)


def _unpack(flat, shapes):
    out, off = [], 0
    for shp in shapes:
        n = math.prod(shp)
        out.append(flat[off:off + n].reshape(shp))
        off += n
    return out


def _adamw(w, g, m, v, name):
    shape = w.shape
    cols = shape[-1]
    rows = math.prod(shape[:-1]) if len(shape) > 1 else 1
    tr = rows
    if rows > 512:
        tr = next(t for t in (512, 256, 128, 64, 32, 16, 8) if rows % t == 0)
    two_d = lambda a: a.reshape(rows, cols)

    def body(w_ref, g_ref, m_ref, v_ref, d_ref, nm_ref, nv_ref):
        gg = g_ref[...]
        nm = ADAM_B1 * m_ref[...] + (1.0 - ADAM_B1) * gg
        nv = ADAM_B2 * v_ref[...] + (1.0 - ADAM_B2) * (gg * gg)
        m_hat = nm / (1.0 - ADAM_B1 ** ADAM_STEP)
        v_hat = nv / (1.0 - ADAM_B2 ** ADAM_STEP)
        d_ref[...] = -ADAM_LR * (m_hat / (jnp.sqrt(v_hat) + ADAM_EPS) + ADAM_WD * w_ref[...])
        nm_ref[...] = nm
        nv_ref[...] = nv

    spec = pl.BlockSpec((tr, cols), lambda i: (i, 0))
    outs = pl.pallas_call(
        body, name="adamw_" + name, grid=(rows // tr,),
        out_shape=tuple(jax.ShapeDtypeStruct((rows, cols), f32) for _ in range(3)),
        in_specs=[spec] * 4, out_specs=(spec,) * 3, compiler_params=_cparams(),
    )(two_d(w), two_d(g), two_d(m), two_d(v))
    return tuple(o.reshape(shape) for o in outs)


BIG_SHARDED = ["ffn_w_up", "ffn_w_down", "w_out", "ple_w_gate", "ple_w_proj", "w_in"]
SMALL_REDUCED = REPLICATED + F32_SHARDED


def _chip_major(a, axis):
    n = a.shape[axis] // N_CHIPS
    return jnp.moveaxis(a.reshape(a.shape[:axis] + (N_CHIPS, n) + a.shape[axis + 1:]), axis, 0)


def _train_step(x, p, loss_target, w, m, v):
    rows2d = lambda a: a.reshape(-1, a.shape[-1])
    conv_rows = 16
    pad_rows = lambda a: jnp.pad(a, ((0, conv_rows - a.shape[0]), (0, 0)))
    local = [rows2d(w[n].astype(bf16)) for n in MATMUL_SHARDED] + [pad_rows(rows2d(w[n])) for n in F32_SHARDED]
    gathered = dict(zip(MATMUL_SHARDED + F32_SHARDED, _all_gather_chips(local, "gather_weights")))

    def layer_rows(n, l):
        r = w[n].shape[1] if w[n].ndim == 3 else 1
        return gathered[n][:, l * r:(l + 1) * r, :]

    def col_cut(n, l):
        blk = layer_rows(n, l)
        return jnp.moveaxis(blk, 0, 1).reshape(blk.shape[1], N_CHIPS * blk.shape[2])

    layers = []
    for l in range(DEPTH):
        up4 = layer_rows("ffn_w_up", l)
        lw = {n: w[n][l] for n in REPLICATED if n != "final_norm_g"}
        lw["pool_w"] = lw["pool_w"].astype(bf16)
        lw.update(
            w_in=col_cut("w_in", l), ple_w_proj=col_cut("ple_w_proj", l),
            w_out=layer_rows("w_out", l).reshape(D_MODEL, D_MODEL), ple_w_gate=layer_rows("ple_w_gate", l).reshape(D_MODEL, D_MODEL),
            wdn_h=layer_rows("ffn_w_down", l).reshape(2, FF_HALF, D_MODEL),
            wup_h=jnp.stack([jnp.concatenate([up4[j], up4[2 + j]], axis=1) for j in range(2)]),
            ssd_conv_w=jnp.moveaxis(gathered["ssd_conv_w"][:, l * SSD_CONV:(l + 1) * SSD_CONV, :], 0, 1).reshape(SSD_CONV, XBC),
            ffn_conv_w=jnp.moveaxis(gathered["ffn_conv_w"][:, l * FFN_CONV:(l + 1) * FFN_CONV, :], 0, 1).reshape(FFN_CONV, 2 * D_FF))
        layers.append(lw)

    loss_part, grad_x, grads, d_gf = _device_step(x[0], p[:, 0], loss_target[0], layers, w["final_norm_g"])

    small = _pack([d_gf if n == "final_norm_g" else jnp.stack([grads[l][n] for l in range(DEPTH)]) for n in SMALL_REDUCED],
                  32 * N_CHIPS)
    reduced = []
    for l in reversed(range(DEPTH)):
        g = grads[l]
        up = g["wup_h"]
        big = [jnp.stack([up[0][:, :FF_HALF], up[1][:, :FF_HALF], up[0][:, FF_HALF:], up[1][:, FF_HALF:]]),
               g["wdn_h"].reshape(N_CHIPS, D_FF // N_CHIPS, D_MODEL), _chip_major(g["w_out"], 0), _chip_major(g["ple_w_gate"], 0),
               _chip_major(g["ple_w_proj"], 1), _chip_major(g["w_in"], 1)]
        if l == 0:
            big.append(small.reshape(N_CHIPS, -1, COLS))
        reduced.insert(0, _reduce_scatter(big))
    grad = {n: jnp.stack([reduced[l][i] for l in range(DEPTH)]) for i, n in enumerate(BIG_SHARDED)}
    small_all = _all_gather_chips([reduced[0][-1]], "gather_small_grads")[0].reshape(-1)
    small_shapes = [w[n].shape for n in REPLICATED] + [(DEPTH, SSD_CONV, XBC), (DEPTH, FFN_CONV, 2 * D_FF)]
    grad.update(zip(SMALL_REDUCED, _unpack(small_all, small_shapes)))
    me = 2 * lax.axis_index("x") + lax.axis_index("y")
    for n in F32_SHARDED:
        grad[n] = lax.dynamic_slice_in_dim(grad[n], me * w[n].shape[2], w[n].shape[2], axis=2)

    loss = lax.psum(loss_part[0, 0], ("x", "y", "c"))
    delta, new_m, new_v = {}, {}, {}
    for n in WEIGHT_NAMES:
        delta[n], new_m[n], new_v[n] = _adamw(w[n], grad[n], m[n], v[n], n)
    return (loss, grad_x[None], *[grad[n] for n in WEIGHT_NAMES], *[delta[n] for n in WEIGHT_NAMES],
            *[new_m[n] for n in WEIGHT_NAMES], *[new_v[n] for n in WEIGHT_NAMES])


def kernel(x, p, mix_norm_g, w_in, ssd_conv_w, ssd_conv_b, ssd_dt_bias, ssd_a_log, ssd_d, ssd_norm_g, pool_w, pool_scale, w_out, ffn_norm_g, ffn_w_up, ffn_conv_w, ffn_conv_b, ffn_w_down, ple_norm_g, ple_w_gate, ple_w_proj, final_norm_g, loss_target, m_mix_norm_g, m_w_in, m_ssd_conv_w, m_ssd_conv_b, m_ssd_dt_bias, m_ssd_a_log, m_ssd_d, m_ssd_norm_g, m_pool_w, m_pool_scale, m_w_out, m_ffn_norm_g, m_ffn_w_up, m_ffn_conv_w, m_ffn_conv_b, m_ffn_w_down, m_ple_norm_g, m_ple_w_gate, m_ple_w_proj, m_final_norm_g, v_mix_norm_g, v_w_in, v_ssd_conv_w, v_ssd_conv_b, v_ssd_dt_bias, v_ssd_a_log, v_ssd_d, v_ssd_norm_g, v_pool_w, v_pool_scale, v_w_out, v_ffn_norm_g, v_ffn_w_up, v_ffn_conv_w, v_ffn_conv_b, v_ffn_w_down, v_ple_norm_g, v_ple_w_gate, v_ple_w_proj, v_final_norm_g):
    w = dict(mix_norm_g=mix_norm_g, w_in=w_in, ssd_conv_w=ssd_conv_w, ssd_conv_b=ssd_conv_b, ssd_dt_bias=ssd_dt_bias, ssd_a_log=ssd_a_log, ssd_d=ssd_d, ssd_norm_g=ssd_norm_g, pool_w=pool_w, pool_scale=pool_scale, w_out=w_out, ffn_norm_g=ffn_norm_g, ffn_w_up=ffn_w_up, ffn_conv_w=ffn_conv_w, ffn_conv_b=ffn_conv_b, ffn_w_down=ffn_w_down, ple_norm_g=ple_norm_g, ple_w_gate=ple_w_gate, ple_w_proj=ple_w_proj, final_norm_g=final_norm_g)
    m = dict(mix_norm_g=m_mix_norm_g, w_in=m_w_in, ssd_conv_w=m_ssd_conv_w, ssd_conv_b=m_ssd_conv_b, ssd_dt_bias=m_ssd_dt_bias, ssd_a_log=m_ssd_a_log, ssd_d=m_ssd_d, ssd_norm_g=m_ssd_norm_g, pool_w=m_pool_w, pool_scale=m_pool_scale, w_out=m_w_out, ffn_norm_g=m_ffn_norm_g, ffn_w_up=m_ffn_w_up, ffn_conv_w=m_ffn_conv_w, ffn_conv_b=m_ffn_conv_b, ffn_w_down=m_ffn_w_down, ple_norm_g=m_ple_norm_g, ple_w_gate=m_ple_w_gate, ple_w_proj=m_ple_w_proj, final_norm_g=m_final_norm_g)
    v = dict(mix_norm_g=v_mix_norm_g, w_in=v_w_in, ssd_conv_w=v_ssd_conv_w, ssd_conv_b=v_ssd_conv_b, ssd_dt_bias=v_ssd_dt_bias, ssd_a_log=v_ssd_a_log, ssd_d=v_ssd_d, ssd_norm_g=v_ssd_norm_g, pool_w=v_pool_w, pool_scale=v_pool_scale, w_out=v_w_out, ffn_norm_g=v_ffn_norm_g, ffn_w_up=v_ffn_w_up, ffn_conv_w=v_ffn_conv_w, ffn_conv_b=v_ffn_conv_b, ffn_w_down=v_ffn_w_down, ple_norm_g=v_ple_norm_g, ple_w_gate=v_ple_w_gate, ple_w_proj=v_ple_w_proj, final_norm_g=v_final_norm_g)
    return _train_step(x, p, loss_target, w, m, v)
```

```python
import functools
import math

import jax
import jax.numpy as jnp
from jax import lax
from jax.experimental import pallas as pl
from jax.experimental.pallas import tpu as pltpu

f32, bf16 = jnp.float32, jnp.bfloat16
HI = lax.Precision.HIGHEST

D_MODEL = 1024
D_PLE = 256
DEPTH = 4
SSD_W = 512
HEADS = 8
HEAD_DIM = 64
NSTATE = 128
CHUNK = 128
SSD_CONV = 4
XBC = 1024
POOL_W = 512
POOL_G = 128
WINDOWS = (2, 4, 8, 16)
D_FF = 2816
FF_HALF = D_FF // 2
FFN_CONV = 3
D_IN = 2056
EPS = 1e-6
ADAM_LR, ADAM_B1, ADAM_B2, ADAM_EPS, ADAM_WD, ADAM_STEP = 0.001, 0.9, 0.999, 1e-08, 0.01, 10

LANES = 128
NPROJ = 2048 + LANES
HALO = 16
FHALO = 8
VMEM_LIMIT = 58 * 1024 * 1024
ROW_TILE = 256


def _dot(a, b):
    return jnp.dot(a, b, preferred_element_type=f32)


def _dot_nt(a, b):
    return lax.dot_general(a, b, (((1,), (1,)), ((), ())), preferred_element_type=f32)


def _dot_tn(a, b):
    return lax.dot_general(a, b, (((0,), (0,)), ((), ())), preferred_element_type=f32)


def _dot_hi(a, b):
    return jnp.dot(a, b, precision=HI, preferred_element_type=f32)


def _dot_nt_hi(a, b):
    return lax.dot_general(a, b, (((1,), (1,)), ((), ())), precision=HI, preferred_element_type=f32)


def _rms_fwd(x, g):
    r = lax.rsqrt(jnp.mean(x * x, axis=-1, keepdims=True) + EPS)
    xhat = x * r
    return xhat * g, xhat, r


def _rms_bwd(dy, xhat, r, g):
    dxhat = dy * g
    dx = r * (dxhat - xhat * jnp.mean(dxhat * xhat, axis=-1, keepdims=True))
    return dx, jnp.sum(dy * xhat, axis=0, keepdims=True)


def _sigmoid(x):
    return 1.0 / (1.0 + jnp.exp(-x))


_GELU_C = math.sqrt(2.0 / math.pi)


def _gelu_and_grad(x):
    x2 = x * x
    t = jnp.tanh(_GELU_C * (x + 0.044715 * x * x2))
    g = 0.5 * x * (1.0 + t)
    dg = 0.5 * (1.0 + t) + 0.5 * x * (1.0 - t * t) * _GELU_C * (1.0 + 3.0 * 0.044715 * x2)
    return g, dg


def _gelu(x):
    return 0.5 * x * (1.0 + jnp.tanh(_GELU_C * (x + 0.044715 * x * x * x)))


def _softplus(x):
    return jnp.maximum(x, 0.0) + jnp.log(1.0 + jnp.exp(-jnp.abs(x)))


def _cparams(sem=("arbitrary",)):
    return pltpu.CompilerParams(dimension_semantics=sem, vmem_limit_bytes=VMEM_LIMIT)


def _const_spec(shape):
    nd = len(shape)
    return pl.BlockSpec(shape, lambda *_: (0,) * nd, pipeline_mode=pl.Buffered(1))


WIDE_ROW_TILE = 512


def _row_tile(s, t=ROW_TILE):
    return min(t, s)


def _mix_in_fwd(h, g1, w_r):
    s = h.shape[0]
    t = _row_tile(s, WIDE_ROW_TILE)

    def body(h_ref, g_ref, w_ref, zxu_ref, dtr_ref):
        hn, _, _ = _rms_fwd(h_ref[...], g_ref[...])
        proj = _dot(hn.astype(bf16), w_ref[...])
        zxu_ref[...] = proj[:, :2048].astype(bf16)
        dtr_ref[...] = proj[:, 2048:]

    return pl.pallas_call(
        body, name="mix_in_fwd", grid=(s // t,),
        out_shape=(jax.ShapeDtypeStruct((s, 2048), bf16), jax.ShapeDtypeStruct((s, LANES), f32)),
        in_specs=[pl.BlockSpec((t, D_MODEL), lambda i: (i, 0)), _const_spec((1, D_MODEL)), _const_spec((D_MODEL, NPROJ))],
        out_specs=(pl.BlockSpec((t, 2048), lambda i: (i, 0)), pl.BlockSpec((t, LANES), lambda i: (i, 0))),
        compiler_params=_cparams(),
    )(h, g1, w_r)


def _mix_in_bwd(d_zxu, d_dtr, h, dh1, g1, w_r):
    s = h.shape[0]
    t = _row_tile(s, WIDE_ROW_TILE)
    n = s // t

    def body(dz_ref, dd_ref, h_ref, dh1_ref, g_ref, w_ref, dh_ref, dw_ref, dg_ref, acc):
        i = pl.program_id(0)

        @pl.when(i == 0)
        def _():
            acc[...] = jnp.zeros_like(acc)
            dg_ref[...] = jnp.zeros_like(dg_ref)

        g = g_ref[...]
        hn, xhat, r = _rms_fwd(h_ref[...], g)
        dproj = jnp.concatenate([dz_ref[...], dd_ref[...].astype(bf16)], axis=1)
        d_hn = _dot_nt(dproj, w_ref[...])
        acc[...] += _dot_tn(hn.astype(bf16), dproj)
        dx, dg = _rms_bwd(d_hn, xhat, r, g)
        dg_ref[...] += dg
        dh_ref[...] = dh1_ref[...] + dx

        @pl.when(i == n - 1)
        def _():
            pltpu.sync_copy(acc, dw_ref)

    return pl.pallas_call(
        body, name="mix_in_bwd", grid=(n,),
        out_shape=(jax.ShapeDtypeStruct((s, D_MODEL), f32), jax.ShapeDtypeStruct((D_MODEL, NPROJ), f32),
                   jax.ShapeDtypeStruct((1, D_MODEL), f32)),
        in_specs=[pl.BlockSpec((t, 2048), lambda i: (i, 0)), pl.BlockSpec((t, LANES), lambda i: (i, 0)),
                  pl.BlockSpec((t, D_MODEL), lambda i: (i, 0)), pl.BlockSpec((t, D_MODEL), lambda i: (i, 0)),
                  _const_spec((1, D_MODEL)), _const_spec((D_MODEL, NPROJ))],
        out_specs=(pl.BlockSpec((t, D_MODEL), lambda i: (i, 0)), pl.BlockSpec(memory_space=pl.ANY),
                   pl.BlockSpec((1, D_MODEL), lambda i: (0, 0))),
        scratch_shapes=[pltpu.VMEM((D_MODEL, NPROJ), f32)],
        compiler_params=_cparams(),
    )(d_zxu, d_dtr, h, dh1, g1, w_r)


def _iota2(shape, dim):
    return lax.broadcasted_iota(jnp.int32, shape, dim)


def _lane_bcast(a, h):
    return jnp.broadcast_to(a[:, h:h + 1], (a.shape[0], LANES))


def _to_columns(cols):
    lane = _iota2((cols[0].shape[0], LANES), 1)
    out = jnp.where(lane == 0, cols[0], 0.0)
    for h in range(1, len(cols)):
        out = out + jnp.where(lane == h, cols[h], 0.0)
    return out


def _ssd_pre(zx, dtr, xext, cw, cb, dtb, alog):
    c = cb + cw[0:1] * xext[pl.ds(HALO - 3, CHUNK), :]
    for k in range(1, SSD_CONV):
        c = c + cw[k:k + 1] * xext[pl.ds(HALO - 3 + k, CHUNK), :]
    sig_c = _sigmoid(c)
    xc = c * sig_c
    dt = _softplus(dtr + dtb)
    a_neg = -jnp.exp(alog)
    a = dt * a_neg
    lo = _iota2((CHUNK, LANES), 1) < HEAD_DIM
    dt_w = jnp.concatenate([jnp.where(lo, _lane_bcast(dt, 2 * pp), _lane_bcast(dt, 2 * pp + 1)) for pp in range(4)], axis=1)
    xs = xc[:, :SSD_W]
    xd = xs * dt_w
    tril = (_iota2((CHUNK, CHUNK), 0) >= _iota2((CHUNK, CHUNK), 1))
    acs = _dot_hi(tril.astype(f32), a)
    acs_b = [_lane_bcast(acs, h) for h in range(HEADS)]
    return dict(c=c, sig_c=sig_c, xc=xc, xs=xs, dt=dt, a_neg=a_neg, a=a, dt_w=dt_w, xd=xd, acs_b=acs_b, tril=tril, lo=lo)


def _pair_fwd(q, pp, s_in):
    g = pp // 2
    lo = q["lo"]
    b_g = q["xc"][:, SSD_W + g * NSTATE:SSD_W + (g + 1) * NSTATE]
    c_g = q["xc"][:, SSD_W + 2 * NSTATE + g * NSTATE:SSD_W + 2 * NSTATE + (g + 1) * NSTATE]
    b16, c16 = b_g.astype(bf16), c_g.astype(bf16)
    gmat = _dot_nt(c16, b16)
    xp = q["xd"][:, pp * LANES:(pp + 1) * LANES]
    xp16 = xp.astype(bf16)
    ab0 = q["acs_b"][2 * pp]
    ab1 = q["acs_b"][2 * pp + 1]
    ls, ms, ys = [], [], []
    for ab in (ab0, ab1):
        lmat = jnp.exp(jnp.where(q["tril"], ab - ab.T, -jnp.inf))
        mmat = gmat * lmat
        ls.append(lmat)
        ms.append(mmat)
        ys.append(_dot(mmat.astype(bf16), xp16))
    y_diag = jnp.where(lo, ys[0], ys[1])
    ab_pair = jnp.where(lo, ab0, ab1)
    e_pair = jnp.exp(ab_pair)
    s16 = s_in.astype(bf16)
    y_off = _dot(c16, s16) * e_pair
    alast = ab_pair[CHUNK - 1:CHUNK, :]
    dec_pair = jnp.exp(alast - ab_pair)
    xdec = xp * dec_pair
    st = _dot_tn(b16, xdec.astype(bf16))
    cd_pair = jnp.exp(alast)
    s_out = s_in * cd_pair + st
    return dict(b16=b16, c16=c16, gmat=gmat, xp=xp, xp16=xp16, ls=ls, ms=ms, y=y_diag + y_off, y_off=y_off,
                e_pair=e_pair, dec_pair=dec_pair, xdec=xdec, cd_pair=cd_pair, s_out=s_out, s16=s16, lo=lo)


def _gate_norm_fwd(y_pre, z, ng):
    sz = _sigmoid(z)
    yg = y_pre * (z * sz)
    outs, stats = [], []
    half = SSD_W // 2
    for gi in range(2):
        o, xhat, r = _rms_fwd(yg[:, gi * half:(gi + 1) * half], ng[:, gi * half:(gi + 1) * half])
        outs.append(o)
        stats.append((xhat, r))
    return jnp.concatenate(outs, axis=1), sz, stats


def _pool_fwd(uext, u, row0, pw_ref, ps):
    pos = (row0 + _iota2((CHUNK, 1), 0) + 1).astype(f32)
    pooled, mixed, invs = [], [], []
    for gi, w in enumerate(WINDOWS):
        sl = slice(gi * POOL_G, (gi + 1) * POOL_G)
        acc = uext[pl.ds(HALO, CHUNK), sl]
        for j in range(1, w):
            acc = acc + uext[pl.ds(HALO - j, CHUNK), sl]
        den = jnp.minimum(pos, float(w))
        pg = acc / den - u[:, sl]
        pooled.append(pg)
        invs.append(den)
        mixed.append(_dot(pg.astype(bf16), pw_ref[gi]))
    mixed = jnp.concatenate(mixed, axis=1)
    return mixed * ps, pooled, mixed, invs


def _ssd_specs(s):
    nc = s // CHUNK
    hb = CHUNK // HALO
    return nc, hb


def _ssd_param_specs():
    return [_const_spec((SSD_CONV, XBC)), _const_spec((1, XBC)), _const_spec((1, LANES)), _const_spec((1, LANES)),
            _const_spec((1, SSD_W)), _const_spec((1, SSD_W)), _const_spec((4, POOL_G, POOL_G)), _const_spec((1, POOL_W))]


def _ssd_pool_fwd(zxu, dtr, prm):
    s = zxu.shape[0]
    nc, hb = _ssd_specs(s)

    def body(zx_ref, halo_ref, dtr_ref, cw_ref, cb_ref, dtb_ref, alog_ref, dsk_ref, ng_ref, pw_ref, ps_ref,
             ymix_ref, st_ref, state, xext, uext):
        i = pl.program_id(0)

        @pl.when(i == 0)
        def _():
            state[...] = jnp.zeros_like(state)

        zx = zx_ref[...].astype(f32)
        halo = jnp.where(i > 0, halo_ref[...].astype(f32), 0.0)
        xext[0:HALO, :] = halo[:, SSD_W:SSD_W + XBC]
        xext[HALO:, :] = zx[:, SSD_W:SSD_W + XBC]
        uext[0:HALO, :] = halo[:, SSD_W + XBC:]
        uext[HALO:, :] = zx[:, SSD_W + XBC:]
        q = _ssd_pre(zx, dtr_ref[...], xext, cw_ref[...], cb_ref[...], dtb_ref[...], alog_ref[...])
        ys = []
        for pp in range(4):
            s_in = state[pp]
            st_ref[0, pp] = s_in
            r = _pair_fwd(q, pp, s_in)
            state[pp] = r["s_out"]
            ys.append(r["y"])
        y_pre = jnp.concatenate(ys, axis=1) + q["xs"] * dsk_ref[...]
        y_ssd, _, _ = _gate_norm_fwd(y_pre, zx[:, :SSD_W], ng_ref[...])
        y_pool, _, _, _ = _pool_fwd(uext, zx[:, SSD_W + XBC:], i * CHUNK, pw_ref, ps_ref[...])
        ymix_ref[:, :SSD_W] = y_ssd.astype(bf16)
        ymix_ref[:, SSD_W:] = y_pool.astype(bf16)

    return pl.pallas_call(
        body, name="ssd_pool_fwd", grid=(nc,),
        out_shape=(jax.ShapeDtypeStruct((s, D_MODEL), bf16), jax.ShapeDtypeStruct((nc, 4, NSTATE, LANES), f32)),
        in_specs=[pl.BlockSpec((CHUNK, 2048), lambda i: (i, 0)),
                  pl.BlockSpec((HALO, 2048), lambda i: (jnp.maximum(i * hb - 1, 0), 0)),
                  pl.BlockSpec((CHUNK, LANES), lambda i: (i, 0))] + _ssd_param_specs(),
        out_specs=(pl.BlockSpec((CHUNK, D_MODEL), lambda i: (i, 0)),
                   pl.BlockSpec((1, 4, NSTATE, LANES), lambda i: (i, 0, 0, 0))),
        scratch_shapes=[pltpu.VMEM((4, NSTATE, LANES), f32), pltpu.VMEM((HALO + CHUNK, XBC), f32),
                        pltpu.VMEM((HALO + CHUNK, POOL_W), f32)],
        compiler_params=_cparams(),
    )(zxu, zxu, dtr, *prm)


def _ssd_pool_bwd(d_ymix, zxu, dtr, states, prm):
    s = zxu.shape[0]
    nc, hb = _ssd_specs(s)
    rev = lambda i: nc - 1 - i

    def body(dy_ref, zx_ref, halo_ref, dtr_ref, st_ref, cw_ref, cb_ref, dtb_ref, alog_ref, dsk_ref, ng_ref, pw_ref, ps_ref,
             dzx_ref, ddtr_ref, dcw_ref, dcb_ref, ddtb_ref, dalog_ref, ddsk_ref, dng_ref, dpw_ref, dps_ref,
             dstate, xext, uext, dxext, duext, cx, cu):
        i = pl.program_id(0)
        ci = nc - 1 - i

        @pl.when(i == 0)
        def _():
            dstate[...] = jnp.zeros_like(dstate)
            cx[...] = jnp.zeros_like(cx)
            cu[...] = jnp.zeros_like(cu)
            for r in (dcw_ref, dcb_ref, ddtb_ref, dalog_ref, ddsk_ref, dng_ref, dpw_ref, dps_ref):
                r[...] = jnp.zeros_like(r)

        zx = zx_ref[...].astype(f32)
        halo = jnp.where(ci > 0, halo_ref[...].astype(f32), 0.0)
        xext[0:HALO, :] = halo[:, SSD_W:SSD_W + XBC]
        xext[HALO:, :] = zx[:, SSD_W:SSD_W + XBC]
        uext[0:HALO, :] = halo[:, SSD_W + XBC:]
        uext[HALO:, :] = zx[:, SSD_W + XBC:]
        cw = cw_ref[...]
        q = _ssd_pre(zx, dtr_ref[...], xext, cw, cb_ref[...], dtb_ref[...], alog_ref[...])
        z = zx[:, :SSD_W]
        dy = dy_ref[...].astype(f32)
        d_yssd, d_ypool = dy[:, :SSD_W], dy[:, SSD_W:]

        pairs = [_pair_fwd(q, pp, st_ref[0, pp]) for pp in range(4)]
        dsk = dsk_ref[...]
        ng = ng_ref[...]
        y_pre = jnp.concatenate([r["y"] for r in pairs], axis=1) + q["xs"] * dsk
        _, sz, stats = _gate_norm_fwd(y_pre, z, ng)

        half = SSD_W // 2
        d_yg, d_ng = [], []
        for gi in range(2):
            xhat, r = stats[gi]
            dx, dg = _rms_bwd(d_yssd[:, gi * half:(gi + 1) * half], xhat, r, ng[:, gi * half:(gi + 1) * half])
            d_yg.append(dx)
            d_ng.append(dg)
        d_yg = jnp.concatenate(d_yg, axis=1)
        dng_ref[...] += jnp.concatenate(d_ng, axis=1)
        silu_z = z * sz
        d_ypre = d_yg * silu_z
        d_z = d_yg * y_pre * (sz * (1.0 + z * (1.0 - sz)))
        ddsk_ref[...] += jnp.sum(d_ypre * q["xs"], axis=0, keepdims=True)

        d_xd, acs_cols, dt_cols = [], [], []
        d_b = [None, None]
        d_c = [None, None]
        d_g = [None, None]
        last_row = _iota2((CHUNK, LANES), 0) == CHUNK - 1
        for pp in range(4):
            g = pp // 2
            r = pairs[pp]
            lo = r["lo"]
            dyp = d_ypre[:, pp * LANES:(pp + 1) * LANES]
            dyp16 = dyp.astype(bf16)
            ds_out = dstate[pp]
            ds16 = ds_out.astype(bf16)
            dye16 = (dyp * r["e_pair"]).astype(bf16)
            dstate[pp] = r["cd_pair"] * ds_out + _dot_tn(r["c16"], dye16)
            dc = _dot_nt(dye16, r["s16"])
            dxdec = _dot(r["b16"], ds16)
            db = _dot_nt(r["xdec"].astype(bf16), ds16)
            dxp = dxdec * r["dec_pair"]
            t2 = dxdec * r["xdec"]
            tail = jnp.sum(t2, axis=0, keepdims=True) + jnp.sum(ds_out * st_ref[0, pp] * r["cd_pair"], axis=0, keepdims=True)
            rp = dyp * r["y_off"] - t2 + jnp.where(last_row, tail, 0.0)
            dxs = []
            for hh in range(2):
                msk = lo if hh == 0 else jnp.logical_not(lo)
                m16 = r["ms"][hh].astype(bf16)
                dxs.append(_dot_tn(m16, dyp16))
                dm = _dot_nt(jnp.where(msk, dyp, 0.0).astype(bf16), r["xp16"])
                wmat = dm * r["ms"][hh]
                acs_cols.append(jnp.sum(wmat - wmat.T + jnp.where(msk, rp, 0.0), axis=1, keepdims=True))
                dgh = dm * r["ls"][hh]
                d_g[g] = dgh if d_g[g] is None else d_g[g] + dgh
            dxp = dxp + jnp.where(lo, dxs[0], dxs[1])
            d_xd.append(dxp)
            xprod = dxp * q["xs"][:, pp * LANES:(pp + 1) * LANES]
            dt_cols.append(jnp.sum(jnp.where(lo, xprod, 0.0), axis=1, keepdims=True))
            dt_cols.append(jnp.sum(jnp.where(lo, 0.0, xprod), axis=1, keepdims=True))
            d_b[g] = db if d_b[g] is None else d_b[g] + db
            d_c[g] = dc if d_c[g] is None else d_c[g] + dc
        for g in range(2):
            dg16 = d_g[g].astype(bf16)
            d_c[g] = d_c[g] + _dot(dg16, pairs[2 * g]["b16"])
            d_b[g] = d_b[g] + _dot_tn(dg16, pairs[2 * g]["c16"])
        d_xd = jnp.concatenate(d_xd, axis=1)
        triu = (_iota2((CHUNK, CHUNK), 0) <= _iota2((CHUNK, CHUNK), 1)).astype(f32)
        d_a = _dot_hi(triu, _to_columns(acs_cols))
        d_dt = d_a * q["a_neg"] + _to_columns(dt_cols)
        dalog_ref[...] += jnp.sum(d_a * q["dt"], axis=0, keepdims=True) * q["a_neg"]
        d_dtr = d_dt * _sigmoid(dtr_ref[...] + dtb_ref[...])
        ddtr_ref[...] = d_dtr
        ddtb_ref[...] += jnp.sum(d_dtr, axis=0, keepdims=True)
        d_xs = d_ypre * dsk + d_xd * q["dt_w"]

        d_xc = jnp.concatenate([d_xs, d_b[0], d_b[1], d_c[0], d_c[1]], axis=1)
        sc = q["sig_c"]
        d_conv = d_xc * (sc * (1.0 + q["c"] * (1.0 - sc)))
        dcb_ref[...] += jnp.sum(d_conv, axis=0, keepdims=True)
        dxext[...] = jnp.zeros_like(dxext)
        for k in range(SSD_CONV):
            dcw_ref[k:k + 1, :] += jnp.sum(d_conv * xext[pl.ds(HALO - 3 + k, CHUNK), :], axis=0, keepdims=True)
            dxext[pl.ds(HALO - 3 + k, CHUNK), :] += cw[k:k + 1] * d_conv
        dxext[pl.ds(CHUNK, HALO), :] += cx[...]
        cx[...] = dxext[0:HALO, :]

        ps = ps_ref[...]
        u = zx[:, SSD_W + XBC:]
        _, pooled, mixed, dens = _pool_fwd(uext, u, ci * CHUNK, pw_ref, ps)
        dps_ref[...] += jnp.sum(d_ypool * mixed, axis=0, keepdims=True)
        d_mixed = d_ypool * ps
        duext[...] = jnp.zeros_like(duext)
        for gi, w in enumerate(WINDOWS):
            sl = slice(gi * POOL_G, (gi + 1) * POOL_G)
            dm16 = d_mixed[:, sl].astype(bf16)
            dpw_ref[gi] += _dot_tn(pooled[gi].astype(bf16), dm16)
            d_pg = _dot_nt(dm16, pw_ref[gi])
            d_mean = d_pg / dens[gi]
            duext[pl.ds(HALO, CHUNK), sl] += d_mean - d_pg
            for j in range(1, w):
                duext[pl.ds(HALO - j, CHUNK), sl] += d_mean
        duext[pl.ds(CHUNK, HALO), :] += cu[...]
        cu[...] = duext[0:HALO, :]

        dzx_ref[:, :SSD_W] = d_z.astype(bf16)
        dzx_ref[:, SSD_W:SSD_W + XBC] = dxext[HALO:, :].astype(bf16)
        dzx_ref[:, SSD_W + XBC:] = duext[HALO:, :].astype(bf16)

    small = lambda shape: pl.BlockSpec(shape, lambda i: (0,) * len(shape))
    small_shapes = [(SSD_CONV, XBC), (1, XBC), (1, LANES), (1, LANES), (1, SSD_W), (1, SSD_W), (4, POOL_G, POOL_G), (1, POOL_W)]
    return pl.pallas_call(
        body, name="ssd_pool_bwd", grid=(nc,),
        out_shape=(jax.ShapeDtypeStruct((s, 2048), bf16), jax.ShapeDtypeStruct((s, LANES), f32))
        + tuple(jax.ShapeDtypeStruct(sh, f32) for sh in small_shapes),
        in_specs=[pl.BlockSpec((CHUNK, D_MODEL), lambda i: (rev(i), 0)),
                  pl.BlockSpec((CHUNK, 2048), lambda i: (rev(i), 0)),
                  pl.BlockSpec((HALO, 2048), lambda i: (jnp.maximum(rev(i) * hb - 1, 0), 0)),
                  pl.BlockSpec((CHUNK, LANES), lambda i: (rev(i), 0)),
                  pl.BlockSpec((1, 4, NSTATE, LANES), lambda i: (rev(i), 0, 0, 0))] + _ssd_param_specs(),
        out_specs=(pl.BlockSpec((CHUNK, 2048), lambda i: (rev(i), 0)), pl.BlockSpec((CHUNK, LANES), lambda i: (rev(i), 0)))
        + tuple(small(sh) for sh in small_shapes),
        scratch_shapes=[pltpu.VMEM((4, NSTATE, LANES), f32), pltpu.VMEM((HALO + CHUNK, XBC), f32),
                        pltpu.VMEM((HALO + CHUNK, POOL_W), f32), pltpu.VMEM((HALO + CHUNK, XBC), f32),
                        pltpu.VMEM((HALO + CHUNK, POOL_W), f32), pltpu.VMEM((HALO, XBC), f32), pltpu.VMEM((HALO, POOL_W), f32)],
        compiler_params=_cparams(),
    )(d_ymix, zxu, zxu, dtr, states, *prm)


FFN_BWD_TILE = 256


def _prev_halo_spec(t, width):
    hb = t // HALO
    return pl.BlockSpec((HALO, width), lambda i: (jnp.maximum(i * hb - 1, 0), 0))


def _ffn_half(hn16, j, wup_ref, cw_ref, cb_ref, up_scr, rows):
    up_scr[...] = _dot(hn16, wup_ref[j])
    cw = cw_ref[j]
    cv = cb_ref[j] + cw[0:1] * up_scr[pl.ds(HALO - 2, rows), :]
    for k in range(1, FFN_CONV):
        cv = cv + cw[k:k + 1] * up_scr[pl.ds(HALO - 2 + k, rows), :]
    return cv


def _out_ffn_ple_fwd(h, ymix, p_l, w_out, g2, wup_h, cw_h, cb_h, wdn_h, g3, w_gate, w_proj, gather=()):
    s = h.shape[0]
    t = _row_tile(s)
    n = s // t
    ng = len(gather)

    def body(*refs):
        (h_ref, hh_ref, ym_ref, ymh_ref, p_ref, wo_ref, g2_ref, wup_ref, cw_ref, cb_ref, wdn_ref, g3_ref, wg_ref,
         wp_ref) = refs[:14]
        x_refs = refs[14:14 + ng]
        h1_ref, h2_ref, h3_ref, up16_ref = refs[14 + ng:18 + ng]
        land_refs = refs[18 + ng:18 + 2 * ng]
        up_scr = refs[18 + 2 * ng]
        sems = refs[19 + 2 * ng:]
        i = pl.program_id(0)
        if ng:
            @pl.when(i == 0)
            def _():
                _ag_start(x_refs, land_refs, *sems)

            @pl.when(i == n - 1)
            def _():
                _ag_finish(x_refs, land_refs, *sems)

        hh = jnp.where(i > 0, hh_ref[...], 0.0)
        ymh = jnp.where(i > 0, ymh_ref[...].astype(f32), 0.0)
        h_ext = jnp.concatenate([hh, h_ref[...]], axis=0)
        ym_ext = jnp.concatenate([ymh, ym_ref[...].astype(f32)], axis=0).astype(bf16)
        h1_ext = h_ext + _dot(ym_ext, wo_ref[...])
        hn2, _, _ = _rms_fwd(h1_ext, g2_ref[...])
        hn16 = hn2.astype(bf16)
        h1 = h1_ext[HALO:, :]
        acc = h1
        for j in range(2):
            cv = _ffn_half(hn16, j, wup_ref, cw_ref, cb_ref, up_scr, t)
            up16_ref[j] = up_scr[pl.ds(HALO, t), :].astype(bf16)
            act = _gelu(cv[:, :FF_HALF]) * cv[:, FF_HALF:]
            acc = acc + _dot(act.astype(bf16), wdn_ref[j])
        h2 = acc
        hn3, _, _ = _rms_fwd(h2, g3_ref[...])
        gate = _sigmoid(_dot(hn3.astype(bf16), wg_ref[...]))
        pp = _dot(p_ref[...].astype(bf16), wp_ref[...])
        h1_ref[...] = h1
        h2_ref[...] = h2
        h3_ref[...] = h2 + pp * gate

    row = lambda w: pl.BlockSpec((t, w), lambda i: (i, 0))
    outs = pl.pallas_call(
        body, name="out_ffn_ple_fwd_gather" if ng else "out_ffn_ple_fwd", grid=(n,),
        out_shape=tuple(jax.ShapeDtypeStruct((s, D_MODEL), f32) for _ in range(3)) + (jax.ShapeDtypeStruct((2, s, D_FF), bf16),)
        + tuple(jax.ShapeDtypeStruct((N_CHIPS,) + a.shape, a.dtype) for a in gather),
        in_specs=[row(D_MODEL), _prev_halo_spec(t, D_MODEL), row(D_MODEL), _prev_halo_spec(t, D_MODEL), row(D_PLE),
                  _const_spec((D_MODEL, D_MODEL)), _const_spec((1, D_MODEL)), _const_spec((2, D_MODEL, D_FF)),
                  _const_spec((2, FFN_CONV, D_FF)), _const_spec((2, 1, D_FF)), _const_spec((2, FF_HALF, D_MODEL)),
                  _const_spec((1, D_MODEL)), _const_spec((D_MODEL, D_MODEL)), _const_spec((D_PLE, D_MODEL))] + [_hbm()] * ng,
        out_specs=tuple(row(D_MODEL) for _ in range(3)) + (pl.BlockSpec((2, t, D_FF), lambda i: (0, i, 0)),) + (_hbm(),) * ng,
        scratch_shapes=[pltpu.VMEM((HALO + t, D_FF), f32)]
        + ([pltpu.SemaphoreType.DMA((6 * ng,)), pltpu.SemaphoreType.DMA((6 * ng,))] if ng else []),
        compiler_params=_cparams(),
    )(h, h, ymix, ymix, p_l, w_out, g2, wup_h, cw_h, cb_h, wdn_h, g3, w_gate, w_proj, *gather)
    return outs[:4] + (_own_block_in(outs[4:], gather),)


def _ple_bwd(dh3, h2, p_l, g3, w_gate, w_proj):
    s = h2.shape[0]
    t = _row_tile(s, WIDE_ROW_TILE)
    n = s // t

    def body(dh3_ref, h2_ref, p_ref, g3_ref, wg_ref, wp_ref, dh2_ref, dwg_ref, dwp_ref, dg3_ref):
        i = pl.program_id(0)

        @pl.when(i == 0)
        def _():
            dwg_ref[...] = jnp.zeros_like(dwg_ref)
            dwp_ref[...] = jnp.zeros_like(dwp_ref)
            dg3_ref[...] = jnp.zeros_like(dg3_ref)

        g3 = g3_ref[...]
        dh3 = dh3_ref[...]
        hn3, xhat, r = _rms_fwd(h2_ref[...], g3)
        hn16 = hn3.astype(bf16)
        gate = _sigmoid(_dot(hn16, wg_ref[...]))
        p16 = p_ref[...].astype(bf16)
        pp = _dot(p16, wp_ref[...])
        d_pp = (dh3 * gate).astype(bf16)
        d_pre = (dh3 * pp * gate * (1.0 - gate)).astype(bf16)
        dwp_ref[...] += _dot_tn(p16, d_pp)
        dwg_ref[...] += _dot_tn(hn16, d_pre)
        dx, dg = _rms_bwd(_dot_nt(d_pre, wg_ref[...]), xhat, r, g3)
        dg3_ref[...] += dg
        dh2_ref[...] = dh3 + dx

    row = lambda w: pl.BlockSpec((t, w), lambda i: (i, 0))
    fixed = lambda shape: pl.BlockSpec(shape, lambda i: (0,) * len(shape))
    return pl.pallas_call(
        body, name="ple_bwd", grid=(n,),
        out_shape=(jax.ShapeDtypeStruct((s, D_MODEL), f32), jax.ShapeDtypeStruct((D_MODEL, D_MODEL), f32),
                   jax.ShapeDtypeStruct((D_PLE, D_MODEL), f32), jax.ShapeDtypeStruct((1, D_MODEL), f32)),
        in_specs=[row(D_MODEL), row(D_MODEL), row(D_PLE), _const_spec((1, D_MODEL)), _const_spec((D_MODEL, D_MODEL)),
                  _const_spec((D_PLE, D_MODEL))],
        out_specs=(row(D_MODEL), fixed((D_MODEL, D_MODEL)), fixed((D_PLE, D_MODEL)), fixed((1, D_MODEL))),
        compiler_params=_cparams(),
    )(dh3, h2, p_l, g3, w_gate, w_proj)


def _ffn_bwd(dh2, h1, up16, g2, wup_h, cw_h, cb_h, wdn_h, exchange=()):
    s = h1.shape[0]
    t = min(FFN_BWD_TILE, s)
    n = s // t
    hb = t // HALO
    last_hb = s // HALO - 1
    nx = len(exchange)

    def body(*refs):
        dh2_ref, dh2n_ref, h1_ref, up_ref, upp_ref, upn_ref, g2_ref, wup_ref, cw_ref, cb_ref, wdn_ref = refs[:11]
        p_refs = refs[11:11 + nx]
        part_ref, dwup_ref, dwdn_ref, dcw_ref, dcb_ref = refs[11 + nx:16 + nx]
        land_refs = refs[16 + nx:16 + 2 * nx]
        up_scr, dcv_scr, acc_up, acc_dn = refs[16 + 2 * nx:20 + 2 * nx]
        sems = refs[20 + 2 * nx:]
        j = pl.program_id(0)
        i = pl.program_id(1)
        if nx:
            @pl.when(jnp.logical_and(j == 0, i == 0))
            def _():
                _cx_start(p_refs, land_refs, *sems)

            @pl.when(jnp.logical_and(j == 1, i == n - 1))
            def _():
                _cx_finish(p_refs, land_refs, *sems)

        @pl.when(i == 0)
        def _():
            acc_up[...] = jnp.zeros_like(acc_up)
            acc_dn[...] = jnp.zeros_like(acc_dn)
            dcw_ref[...] = jnp.zeros_like(dcw_ref)
            dcb_ref[...] = jnp.zeros_like(dcb_ref)

        hn2, _, _ = _rms_fwd(h1_ref[...], g2_ref[...])
        hn16 = hn2.astype(bf16)
        up_scr[0:HALO, :] = jnp.where(i > 0, upp_ref[0].astype(f32), 0.0)
        up_scr[HALO:HALO + t, :] = up_ref[0].astype(f32)
        up_scr[HALO + t:, :] = upn_ref[0].astype(f32)
        cw = cw_ref[0]
        rows = t + HALO
        cv = cb_ref[0] + cw[0:1] * up_scr[pl.ds(HALO - 2, rows), :]
        for k in range(1, FFN_CONV):
            cv = cv + cw[k:k + 1] * up_scr[pl.ds(HALO - 2 + k, rows), :]
        dh2 = dh2_ref[...]
        dh2n = jnp.where(i < n - 1, dh2n_ref[...], 0.0)
        dh2_ext16 = jnp.concatenate([dh2, dh2n], axis=0).astype(bf16)
        d_act = _dot_nt(dh2_ext16, wdn_ref[0])
        gate, val = cv[:, :FF_HALF], cv[:, FF_HALF:]
        gl, dgl = _gelu_and_grad(gate)
        dcv_scr[:, :FF_HALF] = d_act * val * dgl
        dcv_scr[:, FF_HALF:] = d_act * gl
        act16 = (gl[:t] * val[:t]).astype(bf16)
        acc_dn[...] += _dot_tn(act16, dh2_ext16[:t])
        d_cv = dcv_scr[pl.ds(0, t), :]
        dcb_ref[0] += jnp.sum(d_cv, axis=0, keepdims=True)
        d_up = cw[2:3] * d_cv
        dcw_ref[0, 2:3, :] += jnp.sum(d_cv * up_scr[pl.ds(HALO, t), :], axis=0, keepdims=True)
        for k in range(FFN_CONV - 1):
            dcw_ref[0, k:k + 1, :] += jnp.sum(d_cv * up_scr[pl.ds(HALO - 2 + k, t), :], axis=0, keepdims=True)
            d_up = d_up + cw[k:k + 1] * dcv_scr[pl.ds(2 - k, t), :]
        d_up16 = d_up.astype(bf16)
        part_ref[0] = _dot_nt(d_up16, wup_ref[0])
        acc_up[...] += _dot_tn(hn16, d_up16)

        @pl.when(i == n - 1)
        def _():
            pltpu.sync_copy(acc_up, dwup_ref.at[j])
            pltpu.sync_copy(acc_dn, dwdn_ref.at[j])

    outs = pl.pallas_call(
        body, name="ffn_bwd_exchange" if nx else "ffn_bwd", grid=(2, n),
        out_shape=(jax.ShapeDtypeStruct((2, s, D_MODEL), f32), jax.ShapeDtypeStruct((2, D_MODEL, D_FF), f32),
                   jax.ShapeDtypeStruct((2, FF_HALF, D_MODEL), f32), jax.ShapeDtypeStruct((2, FFN_CONV, D_FF), f32),
                   jax.ShapeDtypeStruct((2, 1, D_FF), f32))
        + tuple(jax.ShapeDtypeStruct((3,) + a.shape[1:], a.dtype) for a in exchange),
        in_specs=[pl.BlockSpec((t, D_MODEL), lambda j, i: (i, 0)),
                  pl.BlockSpec((HALO, D_MODEL), lambda j, i: (jnp.minimum((i + 1) * hb, last_hb), 0)),
                  pl.BlockSpec((t, D_MODEL), lambda j, i: (i, 0)),
                  pl.BlockSpec((1, t, D_FF), lambda j, i: (j, i, 0)),
                  pl.BlockSpec((1, HALO, D_FF), lambda j, i: (j, jnp.maximum(i * hb - 1, 0), 0)),
                  pl.BlockSpec((1, HALO, D_FF), lambda j, i: (j, jnp.minimum((i + 1) * hb, last_hb), 0)),
                  _const_spec((1, D_MODEL)),
                  pl.BlockSpec((1, D_MODEL, D_FF), lambda j, i: (j, 0, 0), pipeline_mode=pl.Buffered(1)),
                  pl.BlockSpec((1, FFN_CONV, D_FF), lambda j, i: (j, 0, 0)),
                  pl.BlockSpec((1, 1, D_FF), lambda j, i: (j, 0, 0)),
                  pl.BlockSpec((1, FF_HALF, D_MODEL), lambda j, i: (j, 0, 0), pipeline_mode=pl.Buffered(1))] + [_hbm()] * nx,
        out_specs=(pl.BlockSpec((1, t, D_MODEL), lambda j, i: (j, i, 0)), pl.BlockSpec(memory_space=pl.ANY),
                   pl.BlockSpec(memory_space=pl.ANY), pl.BlockSpec((1, FFN_CONV, D_FF), lambda j, i: (j, 0, 0)),
                   pl.BlockSpec((1, 1, D_FF), lambda j, i: (j, 0, 0))) + (_hbm(),) * nx,
        scratch_shapes=[pltpu.VMEM((2 * HALO + t, D_FF), f32), pltpu.VMEM((HALO + t, D_FF), f32),
                        pltpu.VMEM((D_MODEL, D_FF), f32), pltpu.VMEM((FF_HALF, D_MODEL), f32)]
        + ([pltpu.SemaphoreType.DMA((3 * nx,)), pltpu.SemaphoreType.DMA((3 * nx,))] if nx else []),
        compiler_params=_cparams(("arbitrary", "arbitrary")),
    )(dh2, dh2, h1, up16, up16, up16, g2, wup_h, cw_h, cb_h, wdn_h, *exchange)
    return outs[:5] + (list(outs[5:]),)


def _out_bwd(dh2, parts, h1, ymix, g2, w_out):
    s = h1.shape[0]
    t = _row_tile(s, WIDE_ROW_TILE)
    n = s // t

    def body(dh2_ref, part_ref, h1_ref, ym_ref, g2_ref, wo_ref, dh1_ref, dym_ref, dwo_ref, dg2_ref):
        i = pl.program_id(0)

        @pl.when(i == 0)
        def _():
            dwo_ref[...] = jnp.zeros_like(dwo_ref)
            dg2_ref[...] = jnp.zeros_like(dg2_ref)

        g2 = g2_ref[...]
        _, xhat, r = _rms_fwd(h1_ref[...], g2)
        dx, dg = _rms_bwd(part_ref[0] + part_ref[1], xhat, r, g2)
        dg2_ref[...] += dg
        dh1 = dh2_ref[...] + dx
        dh1_ref[...] = dh1
        dh16 = dh1.astype(bf16)
        dym_ref[...] = _dot_nt(dh16, wo_ref[...]).astype(bf16)
        dwo_ref[...] += _dot_tn(ym_ref[...], dh16)

    row = lambda w: pl.BlockSpec((t, w), lambda i: (i, 0))
    fixed = lambda shape: pl.BlockSpec(shape, lambda i: (0,) * len(shape))
    return pl.pallas_call(
        body, name="out_bwd", grid=(n,),
        out_shape=(jax.ShapeDtypeStruct((s, D_MODEL), f32), jax.ShapeDtypeStruct((s, D_MODEL), bf16),
                   jax.ShapeDtypeStruct((D_MODEL, D_MODEL), f32), jax.ShapeDtypeStruct((1, D_MODEL), f32)),
        in_specs=[row(D_MODEL), pl.BlockSpec((2, t, D_MODEL), lambda i: (0, i, 0)), row(D_MODEL), row(D_MODEL),
                  _const_spec((1, D_MODEL)), _const_spec((D_MODEL, D_MODEL))],
        out_specs=(row(D_MODEL), row(D_MODEL), fixed((D_MODEL, D_MODEL)), fixed((1, D_MODEL))),
        compiler_params=_cparams(),
    )(dh2, parts, h1, ymix, g2, w_out)


def _loss_head(h, target, gf):
    s = h.shape[0]
    t = _row_tile(s, WIDE_ROW_TILE)

    def body(h_ref, t_ref, g_ref, dh_ref, dg_ref, loss_ref):
        i = pl.program_id(0)

        @pl.when(i == 0)
        def _():
            dg_ref[...] = jnp.zeros_like(dg_ref)
            loss_ref[...] = jnp.zeros_like(loss_ref)

        g = g_ref[...]
        y, xhat, r = _rms_fwd(h_ref[...], g)
        diff = y - t_ref[...]
        per_row = jnp.mean(diff * diff, axis=-1, keepdims=True)
        loss_ref[...] += 0.5 * jnp.sum(per_row, axis=0, keepdims=True)
        dx, dg = _rms_bwd(diff * (1.0 / D_MODEL), xhat, r, g)
        dg_ref[...] += dg
        dh_ref[...] = dx

    row = pl.BlockSpec((t, D_MODEL), lambda i: (i, 0))
    return pl.pallas_call(
        body, name="loss_head", grid=(s // t,),
        out_shape=(jax.ShapeDtypeStruct((s, D_MODEL), f32), jax.ShapeDtypeStruct((1, D_MODEL), f32),
                   jax.ShapeDtypeStruct((1, LANES), f32)),
        in_specs=[row, row, _const_spec((1, D_MODEL))],
        out_specs=(row, pl.BlockSpec((1, D_MODEL), lambda i: (0, 0)), pl.BlockSpec((1, LANES), lambda i: (0, 0))),
        compiler_params=_cparams(),
    )(h, target, gf)


def _prep_layer(w):
    w_in = w["w_in"]
    w_r = jnp.concatenate([w_in[:, :SSD_W + XBC], w_in[:, SSD_W + XBC + HEADS:], w_in[:, SSD_W + XBC:SSD_W + XBC + HEADS],
                           jnp.zeros((D_MODEL, LANES - HEADS), w_in.dtype)], axis=1)
    pad8 = lambda v: jnp.concatenate([v, jnp.zeros((LANES - HEADS,), f32)]).reshape(1, LANES)
    halves = _halves
    ssd_prm = (w["ssd_conv_w"], w["ssd_conv_b"].reshape(1, XBC), pad8(w["ssd_dt_bias"]), pad8(w["ssd_a_log"]),
               jnp.repeat(w["ssd_d"], HEAD_DIM).reshape(1, SSD_W), w["ssd_norm_g"].reshape(1, SSD_W),
               w["pool_w"], w["pool_scale"].reshape(1, POOL_W))
    return dict(
        g1=w["mix_norm_g"].reshape(1, D_MODEL), w_r=w_r, ssd=ssd_prm, w_out=w["w_out"], g2=w["ffn_norm_g"].reshape(1, D_MODEL),
        wup_h=w["wup_h"], cw_h=halves(w["ffn_conv_w"]), cb_h=halves(w["ffn_conv_b"].reshape(1, 2 * D_FF)),
        wdn_h=w["wdn_h"], g3=w["ple_norm_g"].reshape(1, D_MODEL), w_gate=w["ple_w_gate"], w_proj=w["ple_w_proj"])


def _halves(a):
    return jnp.stack([jnp.concatenate([a[..., j * FF_HALF:(j + 1) * FF_HALF],
                                       a[..., D_FF + j * FF_HALF:D_FF + (j + 1) * FF_HALF]], axis=-1) for j in range(2)])


def _unhalve(a):
    return jnp.concatenate([a[0][..., :FF_HALF], a[1][..., :FF_HALF], a[0][..., FF_HALF:], a[1][..., FF_HALF:]], axis=-1)


def _layer_fwd(h, p_l, q, gather=()):
    zxu, dtr = _mix_in_fwd(h, q["g1"], q["w_r"])
    ymix, states = _ssd_pool_fwd(zxu, dtr, q["ssd"])
    h1, h2, h3, up16, gathered = _out_ffn_ple_fwd(h, ymix, p_l, q["w_out"], q["g2"], q["wup_h"], q["cw_h"], q["cb_h"],
                                                   q["wdn_h"], q["g3"], q["w_gate"], q["w_proj"], gather)
    return h3, (h, zxu, dtr, ymix, states, h1, h2, up16), gathered


def _layer_bwd(dh, saved, p_l, q, exchange=()):
    h0, zxu, dtr, ymix, states, h1, h2, up16 = saved
    dh2, d_wg, d_wp, d_g3 = _ple_bwd(dh, h2, p_l, q["g3"], q["w_gate"], q["w_proj"])
    parts, d_wup, d_wdn, d_cw, d_cb, lands = _ffn_bwd(dh2, h1, up16, q["g2"], q["wup_h"], q["cw_h"], q["cb_h"], q["wdn_h"],
                                                      exchange)
    dh1, d_ymix, d_wo, d_g2 = _out_bwd(dh2, parts, h1, ymix, q["g2"], q["w_out"])
    (d_zxu, d_dtr, d_scw, d_scb, d_dtb, d_alog, d_dsk, d_ng, d_pw, d_ps) = _ssd_pool_bwd(d_ymix, zxu, dtr, states, q["ssd"])
    dh, d_wr, d_g1 = _mix_in_bwd(d_zxu, d_dtr, h0, dh1, q["g1"], q["w_r"])
    grads = dict(
        mix_norm_g=d_g1.reshape(D_MODEL),
        w_in=jnp.concatenate([d_wr[:, :SSD_W + XBC], d_wr[:, 2048:2048 + HEADS], d_wr[:, SSD_W + XBC:2048]], axis=1),
        ssd_conv_w=d_scw, ssd_conv_b=d_scb.reshape(XBC), ssd_dt_bias=d_dtb[0, :HEADS], ssd_a_log=d_alog[0, :HEADS],
        ssd_d=jnp.sum(d_dsk.reshape(HEADS, HEAD_DIM), axis=1), ssd_norm_g=d_ng.reshape(SSD_W), pool_w=d_pw,
        pool_scale=d_ps.reshape(POOL_W), w_out=d_wo, ffn_norm_g=d_g2.reshape(D_MODEL), wup_h=d_wup,
        ffn_conv_w=_unhalve(d_cw), ffn_conv_b=_unhalve(d_cb).reshape(2 * D_FF), wdn_h=d_wdn,
        ple_norm_g=d_g3.reshape(D_MODEL), ple_w_gate=d_wg, ple_w_proj=d_wp)
    return dh, grads, lands


def _device_step(x, p, target, layers, final_g):
    preps = [_prep_layer(w) for w in layers]
    saved = []
    h = x
    for l, q in enumerate(preps):
        h, keep, _ = _layer_fwd(h, p[l], q)
        saved.append(keep)
    dh, d_gf, loss = _loss_head(h, target, final_g.reshape(1, D_MODEL))
    grads = [None] * len(preps)
    for l in reversed(range(len(preps))):
        dh, grads[l], _ = _layer_bwd(dh, saved[l], p[l], preps[l])
    return loss, dh, grads, d_gf.reshape(D_MODEL)


MESH = pl.DeviceIdType.MESH
COLS = 1024
N_CHIPS = 4
WEIGHT_NAMES = ["mix_norm_g", "w_in", "ssd_conv_w", "ssd_conv_b", "ssd_dt_bias", "ssd_a_log", "ssd_d", "ssd_norm_g", "pool_w",
                "pool_scale", "w_out", "ffn_norm_g", "ffn_w_up", "ffn_conv_w", "ffn_conv_b", "ffn_w_down", "ple_norm_g",
                "ple_w_gate", "ple_w_proj", "final_norm_g"]
SHARD_AXIS = {"w_in": 2, "ssd_conv_w": 2, "w_out": 1, "ffn_w_up": 2, "ffn_conv_w": 2, "ffn_w_down": 1, "ple_w_gate": 1,
              "ple_w_proj": 2}
MATMUL_SHARDED = ["w_in", "w_out", "ffn_w_up", "ffn_w_down", "ple_w_gate", "ple_w_proj"]
F32_SHARDED = ["ssd_conv_w", "ffn_conv_w"]
REPLICATED = [n for n in WEIGHT_NAMES if n not in SHARD_AXIS]


def _mesh_pos():
    return lax.axis_index("x"), lax.axis_index("y"), lax.axis_index("c")


def _hbm():
    return pl.BlockSpec(memory_space=pl.ANY)


def _all_gather_chips(arrays, name):
    n = len(arrays)

    def body(*refs):
        x_refs, out_refs, sems = refs[:n], refs[n:2 * n], refs[2 * n:]
        _ag_start(x_refs, out_refs, *sems)
        _ag_finish(x_refs, out_refs, *sems)

    gathered = pl.pallas_call(
        body, name=name, out_shape=[jax.ShapeDtypeStruct((N_CHIPS,) + a.shape, a.dtype) for a in arrays],
        in_specs=[_hbm()] * n, out_specs=[_hbm()] * n,
        scratch_shapes=[pltpu.SemaphoreType.DMA((6 * n,)), pltpu.SemaphoreType.DMA((6 * n,))],
    )(*arrays)
    return _own_block_in(gathered, arrays)


def _ag_copies(x_refs, out_refs, send_sems, recv_sems):
    x, y, c = _mesh_pos()
    me = 2 * x + y
    flips = [(1 - x, y), (x, 1 - y), (1 - x, 1 - y)]

    def rows(a, chip, half):
        hr = x_refs[a].shape[0] // 2
        return out_refs[a].at[chip, pl.ds(half * hr, hr), :]

    def copy(k, src, dst, to):
        return pltpu.make_async_remote_copy(src_ref=src, dst_ref=dst, send_sem=send_sems.at[k], recv_sem=recv_sems.at[k],
                                            device_id=to, device_id_type=MESH)

    def first(a, j):
        hr = x_refs[a].shape[0] // 2
        return copy(6 * a + j, x_refs[a].at[pl.ds(c * hr, hr), :], rows(a, me, c), flips[j] + (c,))

    def landed(a, j):
        blk = rows(a, 2 * flips[j][0] + flips[j][1], c)
        return copy(6 * a + j, blk, blk, flips[j] + (c,))

    def passed(a, j, half):
        blk = rows(a, 2 * flips[j][0] + flips[j][1], half)
        return copy(6 * a + 3 + j, blk, blk, (x, y, 1 - c))

    return first, landed, passed, c


def _ag_start(x_refs, out_refs, send_sems, recv_sems):
    first, _, _, _ = _ag_copies(x_refs, out_refs, send_sems, recv_sems)
    for a in range(len(x_refs)):
        for j in range(3):
            first(a, j).start()


def _ag_finish(x_refs, out_refs, send_sems, recv_sems):
    first, landed, passed, c = _ag_copies(x_refs, out_refs, send_sems, recv_sems)
    n = len(x_refs)
    for j in range(3):
        for a in range(n):
            landed(a, j).wait_recv()
            passed(a, j, c).start()
    for j in range(3):
        for a in range(n):
            passed(a, j, 1 - c).wait_recv()
    for a in range(n):
        for j in range(3):
            first(a, j).wait_send()
            passed(a, j, c).wait_send()


def _own_block_in(gathered, arrays):
    if not arrays:
        return []
    me = 2 * lax.axis_index("x") + lax.axis_index("y")
    return [lax.dynamic_update_slice(o, a[None], (me, 0, 0)) for o, a in zip(gathered, arrays)]


def _rs_pair_exchange(gs):
    n = len(gs)

    def body(*refs):
        g_refs, land_refs, (send_sems, recv_sems) = refs[:n], refs[n:2 * n], refs[2 * n:]
        x, y, c = _mesh_pos()
        cps = []
        for a in range(n):
            hr = gs[a].shape[1] // 2
            cps.append(pltpu.make_async_remote_copy(
                src_ref=g_refs[a].at[:, pl.ds((1 - c) * hr, hr), :], dst_ref=land_refs[a], send_sem=send_sems.at[a],
                recv_sem=recv_sems.at[a], device_id=(x, y, 1 - c), device_id_type=MESH))
        for cp in cps:
            cp.start()
        for cp in cps:
            cp.wait()

    return pl.pallas_call(
        body, name="rs_pair_exchange",
        out_shape=[jax.ShapeDtypeStruct((N_CHIPS, g.shape[1] // 2, g.shape[2]), g.dtype) for g in gs],
        in_specs=[_hbm()] * n, out_specs=[_hbm()] * n,
        scratch_shapes=[pltpu.SemaphoreType.DMA((n,)), pltpu.SemaphoreType.DMA((n,))],
    )(*gs)


def _rs_pair_add(g, land, c_idx):
    _, r, cols = g.shape
    hr = r // 2

    def body(c_ref, g_ref, l_ref, o_ref):
        o_ref[...] = (g_ref[...] + l_ref[...]).astype(bf16)

    return pl.pallas_call(
        body, name="rs_pair_add", out_shape=jax.ShapeDtypeStruct((N_CHIPS, hr, cols), bf16),
        grid_spec=pltpu.PrefetchScalarGridSpec(
            num_scalar_prefetch=1, grid=(N_CHIPS,),
            in_specs=[pl.BlockSpec((1, hr, cols), lambda k, c_ref: (k, c_ref[0], 0)),
                      pl.BlockSpec((1, hr, cols), lambda k, c_ref: (k, 0, 0))],
            out_specs=pl.BlockSpec((1, hr, cols), lambda k, c_ref: (k, 0, 0))),
        compiler_params=_cparams(),
    )(c_idx, g, land)


def _rs_chip_exchange(parts):
    n = len(parts)

    def body(*refs):
        p_refs, land_refs, sems = refs[:n], refs[n:2 * n], refs[2 * n:]
        _cx_start(p_refs, land_refs, *sems)
        _cx_finish(p_refs, land_refs, *sems)

    return pl.pallas_call(
        body, name="rs_chip_exchange", out_shape=[jax.ShapeDtypeStruct((3,) + p.shape[1:], p.dtype) for p in parts],
        in_specs=[_hbm()] * n, out_specs=[_hbm()] * n,
        scratch_shapes=[pltpu.SemaphoreType.DMA((3 * n,)), pltpu.SemaphoreType.DMA((3 * n,))],
    )(*parts)


def _cx_copies(p_refs, land_refs, send_sems, recv_sems):
    x, y, c = _mesh_pos()
    flips = [(1 - x, y), (x, 1 - y), (1 - x, 1 - y)]
    return [pltpu.make_async_remote_copy(src_ref=p_refs[a].at[2 * fx + fy], dst_ref=land_refs[a].at[j],
                                         send_sem=send_sems.at[3 * a + j], recv_sem=recv_sems.at[3 * a + j],
                                         device_id=(fx, fy, c), device_id_type=MESH)
            for a in range(len(p_refs)) for j, (fx, fy) in enumerate(flips)]


def _cx_start(p_refs, land_refs, send_sems, recv_sems):
    for cp in _cx_copies(p_refs, land_refs, send_sems, recv_sems):
        cp.start()


def _cx_finish(p_refs, land_refs, send_sems, recv_sems):
    for cp in _cx_copies(p_refs, land_refs, send_sems, recv_sems):
        cp.wait()


def _rs_chip_add(part, land, me_idx):
    _, hr, cols = part.shape

    def body(me_ref, p_ref, l_ref, o_ref):
        o_ref[...] = ((p_ref[0].astype(f32) + l_ref[0].astype(f32)) + l_ref[1].astype(f32)) + l_ref[2].astype(f32)

    return pl.pallas_call(
        body, name="rs_chip_add", out_shape=jax.ShapeDtypeStruct((hr, cols), f32),
        grid_spec=pltpu.PrefetchScalarGridSpec(
            num_scalar_prefetch=1, grid=(1,),
            in_specs=[pl.BlockSpec((1, hr, cols), lambda i, me_ref: (me_ref[0], 0, 0)),
                      pl.BlockSpec((3, hr, cols), lambda i, me_ref: (0, 0, 0))],
            out_specs=pl.BlockSpec((hr, cols), lambda i, me_ref: (0, 0))),
        compiler_params=_cparams(),
    )(me_idx, part, land)


def _rs_pair_share(reds):
    n = len(reds)

    def body(*refs):
        r_refs, out_refs, (send_sems, recv_sems) = refs[:n], refs[n:2 * n], refs[2 * n:]
        x, y, c = _mesh_pos()

        def copy(a, half):
            hr = reds[a].shape[0]
            return pltpu.make_async_remote_copy(src_ref=r_refs[a], dst_ref=out_refs[a].at[pl.ds(half * hr, hr), :],
                                                send_sem=send_sems.at[a], recv_sem=recv_sems.at[a], device_id=(x, y, 1 - c),
                                                device_id_type=MESH)

        sends = [copy(a, c) for a in range(n)]
        for cp in sends:
            cp.start()
        for a in range(n):
            copy(a, 1 - c).wait_recv()
        for cp in sends:
            cp.wait_send()

    both = pl.pallas_call(
        body, name="rs_pair_share", out_shape=[jax.ShapeDtypeStruct((2 * r.shape[0], r.shape[1]), r.dtype) for r in reds],
        in_specs=[_hbm()] * n, out_specs=[_hbm()] * n,
        scratch_shapes=[pltpu.SemaphoreType.DMA((n,)), pltpu.SemaphoreType.DMA((n,))],
    )(*reds)
    c = lax.axis_index("c")
    return [lax.dynamic_update_slice(o, r, (c * r.shape[0], 0)) for o, r in zip(both, reds)]


def _rs_begin(gs):
    c_idx = jnp.reshape(lax.axis_index("c"), (1,)).astype(jnp.int32)
    lands = _rs_pair_exchange(gs)
    return [_rs_pair_add(g, land, c_idx) for g, land in zip(gs, lands)]


def _rs_finish(parts, lands):
    me_idx = jnp.reshape(2 * lax.axis_index("x") + lax.axis_index("y"), (1,)).astype(jnp.int32)
    reds = [_rs_chip_add(part, land, me_idx) for part, land in zip(parts, lands)]
    return _rs_pair_share(reds)


def _pack(parts, row_multiple):
    flat = jnp.concatenate([a.reshape(-1) for a in parts])
    n = flat.shape[0]
    rows = -(-n // COLS)
    rows = -(-rows // row_multiple) * row_multiple
    return jnp.pad(flat, (0, rows * COLS - n)).reshape(rows, COLS)


def _unpack(flat, shapes):
    out, off = [], 0
    for shp in shapes:
        n = math.prod(shp)
        out.append(flat[off:off + n].reshape(shp))
        off += n
    return out


def _adamw(w, g, m, v, name):
    shape = w.shape
    cols = shape[-1]
    rows = math.prod(shape[:-1]) if len(shape) > 1 else 1
    tr = rows
    if rows > 512:
        tr = next(t for t in (512, 256, 128, 64, 32, 16, 8) if rows % t == 0)
    two_d = lambda a: a.reshape(rows, cols)

    def body(w_ref, g_ref, m_ref, v_ref, d_ref, nm_ref, nv_ref):
        gg = g_ref[...]
        nm = ADAM_B1 * m_ref[...] + (1.0 - ADAM_B1) * gg
        nv = ADAM_B2 * v_ref[...] + (1.0 - ADAM_B2) * (gg * gg)
        m_hat = nm / (1.0 - ADAM_B1 ** ADAM_STEP)
        v_hat = nv / (1.0 - ADAM_B2 ** ADAM_STEP)
        d_ref[...] = -ADAM_LR * (m_hat / (jnp.sqrt(v_hat) + ADAM_EPS) + ADAM_WD * w_ref[...])
        nm_ref[...] = nm
        nv_ref[...] = nv

    spec = pl.BlockSpec((tr, cols), lambda i: (i, 0))
    outs = pl.pallas_call(
        body, name="adamw_" + name, grid=(rows // tr,),
        out_shape=tuple(jax.ShapeDtypeStruct((rows, cols), f32) for _ in range(3)),
        in_specs=[spec] * 4, out_specs=(spec,) * 3, compiler_params=_cparams(),
    )(two_d(w), two_d(g), two_d(m), two_d(v))
    return tuple(o.reshape(shape) for o in outs)


BIG_SHARDED = ["ffn_w_up", "ffn_w_down", "w_out", "ple_w_gate", "ple_w_proj", "w_in"]
SMALL_REDUCED = REPLICATED + F32_SHARDED


def _chip_major(a, axis):
    n = a.shape[axis] // N_CHIPS
    return jnp.moveaxis(a.reshape(a.shape[:axis] + (N_CHIPS, n) + a.shape[axis + 1:]), axis, 0)


def _train_step(x, p, loss_target, w, m, v):
    xs, ps, target = x[0], p[:, 0], loss_target[0]
    rows2d = lambda a: a.reshape(-1, a.shape[-1])
    conv_rows = 16
    pad_rows = lambda a: jnp.pad(a, ((0, conv_rows - a.shape[0]), (0, 0)))
    shards = [[w[n][l].astype(bf16) for n in MATMUL_SHARDED] for l in range(DEPTH)]
    first = _all_gather_chips(shards[0] + [pad_rows(rows2d(w[n])) for n in F32_SHARDED], "gather_weights")
    conv4 = dict(zip(F32_SHARDED, first[len(MATMUL_SHARDED):]))

    def col_cut(blk):
        return jnp.moveaxis(blk, 0, 1).reshape(blk.shape[1], N_CHIPS * blk.shape[2])

    def layer_weights(l, gathered):
        g = dict(zip(MATMUL_SHARDED, gathered))
        lw = {n: w[n][l] for n in REPLICATED if n != "final_norm_g"}
        lw["pool_w"] = lw["pool_w"].astype(bf16)
        lw.update(
            w_in=col_cut(g["w_in"]), ple_w_proj=col_cut(g["ple_w_proj"]), w_out=g["w_out"].reshape(D_MODEL, D_MODEL),
            ple_w_gate=g["ple_w_gate"].reshape(D_MODEL, D_MODEL), wdn_h=g["ffn_w_down"].reshape(2, FF_HALF, D_MODEL),
            wup_h=jnp.stack([jnp.concatenate([g["ffn_w_up"][j], g["ffn_w_up"][2 + j]], axis=1) for j in range(2)]),
            ssd_conv_w=col_cut(conv4["ssd_conv_w"][:, l * SSD_CONV:(l + 1) * SSD_CONV, :]),
            ffn_conv_w=col_cut(conv4["ffn_conv_w"][:, l * FFN_CONV:(l + 1) * FFN_CONV, :]))
        return _prep_layer(lw)

    preps, saved = [], []
    h, gathered = xs, first[:len(MATMUL_SHARDED)]
    for l in range(DEPTH):
        preps.append(layer_weights(l, gathered))
        h, keep, gathered = _layer_fwd(h, ps[l], preps[l], gather=shards[l + 1] if l + 1 < DEPTH else ())
        saved.append(keep)
    grad_x, d_gf, loss_part = _loss_head(h, target, w["final_norm_g"].reshape(1, D_MODEL))

    def chip_major_grads(g):
        up = g["wup_h"]
        return [jnp.stack([up[0][:, :FF_HALF], up[1][:, :FF_HALF], up[0][:, FF_HALF:], up[1][:, FF_HALF:]]),
                g["wdn_h"].reshape(N_CHIPS, D_FF // N_CHIPS, D_MODEL), _chip_major(g["w_out"], 0), _chip_major(g["ple_w_gate"], 0),
                _chip_major(g["ple_w_proj"], 1), _chip_major(g["w_in"], 1)]

    grads, reduced, pending = [None] * DEPTH, [None] * DEPTH, ()
    for l in reversed(range(DEPTH)):
        grad_x, grads[l], lands = _layer_bwd(grad_x, saved[l], ps[l], preps[l], exchange=pending)
        if pending:
            reduced[l + 1] = _rs_finish(pending, lands)
        big = chip_major_grads(grads[l])
        if l == 0:
            small = _pack([d_gf.reshape(D_MODEL) if n == "final_norm_g" else jnp.stack([grads[k][n] for k in range(DEPTH)])
                           for n in SMALL_REDUCED], 32 * N_CHIPS)
            big.append(small.reshape(N_CHIPS, -1, COLS))
        pending = _rs_begin(big)
    reduced[0] = _rs_finish(pending, _rs_chip_exchange(pending))
    grad = {n: jnp.stack([reduced[l][i] for l in range(DEPTH)]) for i, n in enumerate(BIG_SHARDED)}
    small_all = _all_gather_chips([reduced[0][-1]], "gather_small_grads")[0].reshape(-1)
    small_shapes = [w[n].shape for n in REPLICATED] + [(DEPTH, SSD_CONV, XBC), (DEPTH, FFN_CONV, 2 * D_FF)]
    grad.update(zip(SMALL_REDUCED, _unpack(small_all, small_shapes)))
    me = 2 * lax.axis_index("x") + lax.axis_index("y")
    for n in F32_SHARDED:
        grad[n] = lax.dynamic_slice_in_dim(grad[n], me * w[n].shape[2], w[n].shape[2], axis=2)

    loss = lax.psum(loss_part[0, 0], ("x", "y", "c"))
    delta, new_m, new_v = {}, {}, {}
    for n in WEIGHT_NAMES:
        delta[n], new_m[n], new_v[n] = _adamw(w[n], grad[n], m[n], v[n], n)
    return (loss, grad_x[None], *[grad[n] for n in WEIGHT_NAMES], *[delta[n] for n in WEIGHT_NAMES],
            *[new_m[n] for n in WEIGHT_NAMES], *[new_v[n] for n in WEIGHT_NAMES])


def kernel(x, p, mix_norm_g, w_in, ssd_conv_w, ssd_conv_b, ssd_dt_bias, ssd_a_log, ssd_d, ssd_norm_g, pool_w, pool_scale, w_out, ffn_norm_g, ffn_w_up, ffn_conv_w, ffn_conv_b, ffn_w_down, ple_norm_g, ple_w_gate, ple_w_proj, final_norm_g, loss_target, m_mix_norm_g, m_w_in, m_ssd_conv_w, m_ssd_conv_b, m_ssd_dt_bias, m_ssd_a_log, m_ssd_d, m_ssd_norm_g, m_pool_w, m_pool_scale, m_w_out, m_ffn_norm_g, m_ffn_w_up, m_ffn_conv_w, m_ffn_conv_b, m_ffn_w_down, m_ple_norm_g, m_ple_w_gate, m_ple_w_proj, m_final_norm_g, v_mix_norm_g, v_w_in, v_ssd_conv_w, v_ssd_conv_b, v_ssd_dt_bias, v_ssd_a_log, v_ssd_d, v_ssd_norm_g, v_pool_w, v_pool_scale, v_w_out, v_ffn_norm_g, v_ffn_w_up, v_ffn_conv_w, v_ffn_conv_b, v_ffn_w_down, v_ple_norm_g, v_ple_w_gate, v_ple_w_proj, v_final_norm_g):
    w = dict(mix_norm_g=mix_norm_g, w_in=w_in, ssd_conv_w=ssd_conv_w, ssd_conv_b=ssd_conv_b, ssd_dt_bias=ssd_dt_bias, ssd_a_log=ssd_a_log, ssd_d=ssd_d, ssd_norm_g=ssd_norm_g, pool_w=pool_w, pool_scale=pool_scale, w_out=w_out, ffn_norm_g=ffn_norm_g, ffn_w_up=ffn_w_up, ffn_conv_w=ffn_conv_w, ffn_conv_b=ffn_conv_b, ffn_w_down=ffn_w_down, ple_norm_g=ple_norm_g, ple_w_gate=ple_w_gate, ple_w_proj=ple_w_proj, final_norm_g=final_norm_g)
    m = dict(mix_norm_g=m_mix_norm_g, w_in=m_w_in, ssd_conv_w=m_ssd_conv_w, ssd_conv_b=m_ssd_conv_b, ssd_dt_bias=m_ssd_dt_bias, ssd_a_log=m_ssd_a_log, ssd_d=m_ssd_d, ssd_norm_g=m_ssd_norm_g, pool_w=m_pool_w, pool_scale=m_pool_scale, w_out=m_w_out, ffn_norm_g=m_ffn_norm_g, ffn_w_up=m_ffn_w_up, ffn_conv_w=m_ffn_conv_w, ffn_conv_b=m_ffn_conv_b, ffn_w_down=m_ffn_w_down, ple_norm_g=m_ple_norm_g, ple_w_gate=m_ple_w_gate, ple_w_proj=m_ple_w_proj, final_norm_g=m_final_norm_g)
    v = dict(mix_norm_g=v_mix_norm_g, w_in=v_w_in, ssd_conv_w=v_ssd_conv_w, ssd_conv_b=v_ssd_conv_b, ssd_dt_bias=v_ssd_dt_bias, ssd_a_log=v_ssd_a_log, ssd_d=v_ssd_d, ssd_norm_g=v_ssd_norm_g, pool_w=v_pool_w, pool_scale=v_pool_scale, w_out=v_w_out, ffn_norm_g=v_ffn_norm_g, ffn_w_up=v_ffn_w_up, ffn_conv_w=v_ffn_conv_w, ffn_conv_b=v_ffn_conv_b, ffn_w_down=v_ffn_w_down, ple_norm_g=v_ple_norm_g, ple_w_gate=v_ple_w_gate, ple_w_proj=v_ple_w_proj, final_norm_g=v_final_norm_g)
    return _train_step(x, p, loss_target, w, m, v)
```

```python
import functools
import math

import jax
import jax.numpy as jnp
from jax import lax
from jax.experimental import pallas as pl
from jax.experimental.pallas import tpu as pltpu

f32, bf16 = jnp.float32, jnp.bfloat16
HI = lax.Precision.HIGHEST

D_MODEL = 1024
D_PLE = 256
DEPTH = 4
SSD_W = 512
HEADS = 8
HEAD_DIM = 64
NSTATE = 128
CHUNK = 128
SSD_CONV = 4
XBC = 1024
POOL_W = 512
POOL_G = 128
WINDOWS = (2, 4, 8, 16)
D_FF = 2816
FF_HALF = D_FF // 2
FFN_CONV = 3
D_IN = 2056
EPS = 1e-6
ADAM_LR, ADAM_B1, ADAM_B2, ADAM_EPS, ADAM_WD, ADAM_STEP = 0.001, 0.9, 0.999, 1e-08, 0.01, 10

LANES = 128
NPROJ = 2048 + LANES
HALO = 16
FHALO = 8
VMEM_LIMIT = 58 * 1024 * 1024
ROW_TILE = 256


def _dot(a, b):
    return jnp.dot(a, b, preferred_element_type=f32)


def _dot_nt(a, b):
    return lax.dot_general(a, b, (((1,), (1,)), ((), ())), preferred_element_type=f32)


def _dot_tn(a, b):
    return lax.dot_general(a, b, (((0,), (0,)), ((), ())), preferred_element_type=f32)


def _dot_hi(a, b):
    return jnp.dot(a, b, precision=HI, preferred_element_type=f32)


def _dot_nt_hi(a, b):
    return lax.dot_general(a, b, (((1,), (1,)), ((), ())), precision=HI, preferred_element_type=f32)


def _rms_fwd(x, g):
    r = lax.rsqrt(jnp.mean(x * x, axis=-1, keepdims=True) + EPS)
    xhat = x * r
    return xhat * g, xhat, r


def _rms_bwd(dy, xhat, r, g):
    dxhat = dy * g
    dx = r * (dxhat - xhat * jnp.mean(dxhat * xhat, axis=-1, keepdims=True))
    return dx, jnp.sum(dy * xhat, axis=0, keepdims=True)


def _sigmoid(x):
    return 1.0 / (1.0 + jnp.exp(-x))


_GELU_C = math.sqrt(2.0 / math.pi)


def _gelu_and_grad(x):
    x2 = x * x
    t = jnp.tanh(x * (_GELU_C + (_GELU_C * 0.044715) * x2))
    u = 0.5 * t + 0.5
    g = x * u
    dg = u + g * (1.0 - t) * (_GELU_C + (3.0 * 0.044715 * _GELU_C) * x2)
    return g, dg


def _gelu(x):
    return x * (0.5 * jnp.tanh(x * (_GELU_C + (_GELU_C * 0.044715) * (x * x))) + 0.5)


def _softplus(x):
    return jnp.maximum(x, 0.0) + jnp.log(1.0 + jnp.exp(-jnp.abs(x)))


def _cparams(sem=("arbitrary",)):
    return pltpu.CompilerParams(dimension_semantics=sem, vmem_limit_bytes=VMEM_LIMIT)


def _const_spec(shape):
    nd = len(shape)
    return pl.BlockSpec(shape, lambda *_: (0,) * nd, pipeline_mode=pl.Buffered(1))


WIDE_ROW_TILE = 512


def _row_tile(s, t=ROW_TILE):
    return min(t, s)


def _mix_in_fwd(h, g1, w_r):
    s = h.shape[0]
    t = _row_tile(s, WIDE_ROW_TILE)

    def body(h_ref, g_ref, w_ref, zxu_ref, dtr_ref):
        hn, _, _ = _rms_fwd(h_ref[...], g_ref[...])
        proj = _dot(hn.astype(bf16), w_ref[...])
        zxu_ref[...] = proj[:, :2048].astype(bf16)
        dtr_ref[...] = proj[:, 2048:]

    return pl.pallas_call(
        body, name="mix_in_fwd", grid=(s // t,),
        out_shape=(jax.ShapeDtypeStruct((s, 2048), bf16), jax.ShapeDtypeStruct((s, LANES), f32)),
        in_specs=[pl.BlockSpec((t, D_MODEL), lambda i: (i, 0)), _const_spec((1, D_MODEL)), _const_spec((D_MODEL, NPROJ))],
        out_specs=(pl.BlockSpec((t, 2048), lambda i: (i, 0)), pl.BlockSpec((t, LANES), lambda i: (i, 0))),
        compiler_params=_cparams(),
    )(h, g1, w_r)


def _mix_in_bwd(d_zxu, d_dtr, h, dh1, g1, w_r):
    s = h.shape[0]
    t = _row_tile(s, WIDE_ROW_TILE)
    n = s // t

    def body(dz_ref, dd_ref, h_ref, dh1_ref, g_ref, w_ref, dh_ref, dw_ref, dg_ref, acc):
        i = pl.program_id(0)

        @pl.when(i == 0)
        def _():
            acc[...] = jnp.zeros_like(acc)
            dg_ref[...] = jnp.zeros_like(dg_ref)

        g = g_ref[...]
        hn, xhat, r = _rms_fwd(h_ref[...], g)
        dproj = jnp.concatenate([dz_ref[...], dd_ref[...].astype(bf16)], axis=1)
        d_hn = _dot_nt(dproj, w_ref[...])
        acc[...] += _dot_tn(hn.astype(bf16), dproj)
        dx, dg = _rms_bwd(d_hn, xhat, r, g)
        dg_ref[...] += dg
        dh_ref[...] = dh1_ref[...] + dx

        @pl.when(i == n - 1)
        def _():
            pltpu.sync_copy(acc, dw_ref)

    return pl.pallas_call(
        body, name="mix_in_bwd", grid=(n,),
        out_shape=(jax.ShapeDtypeStruct((s, D_MODEL), f32), jax.ShapeDtypeStruct((D_MODEL, NPROJ), f32),
                   jax.ShapeDtypeStruct((1, D_MODEL), f32)),
        in_specs=[pl.BlockSpec((t, 2048), lambda i: (i, 0)), pl.BlockSpec((t, LANES), lambda i: (i, 0)),
                  pl.BlockSpec((t, D_MODEL), lambda i: (i, 0)), pl.BlockSpec((t, D_MODEL), lambda i: (i, 0)),
                  _const_spec((1, D_MODEL)), _const_spec((D_MODEL, NPROJ))],
        out_specs=(pl.BlockSpec((t, D_MODEL), lambda i: (i, 0)), pl.BlockSpec(memory_space=pl.ANY),
                   pl.BlockSpec((1, D_MODEL), lambda i: (0, 0))),
        scratch_shapes=[pltpu.VMEM((D_MODEL, NPROJ), f32)],
        compiler_params=_cparams(),
    )(d_zxu, d_dtr, h, dh1, g1, w_r)


def _iota2(shape, dim):
    return lax.broadcasted_iota(jnp.int32, shape, dim)


def _lane_bcast(a, h):
    return jnp.broadcast_to(a[:, h:h + 1], (a.shape[0], LANES))


def _to_columns(cols):
    lane = _iota2((cols[0].shape[0], LANES), 1)
    out = jnp.where(lane == 0, cols[0], 0.0)
    for h in range(1, len(cols)):
        out = out + jnp.where(lane == h, cols[h], 0.0)
    return out


def _ssd_pre(zx, dtr, xext, cw, cb, dtb, alog):
    c = cb + cw[0:1] * xext[pl.ds(HALO - 3, CHUNK), :]
    for k in range(1, SSD_CONV):
        c = c + cw[k:k + 1] * xext[pl.ds(HALO - 3 + k, CHUNK), :]
    sig_c = _sigmoid(c)
    xc = c * sig_c
    dt = _softplus(dtr + dtb)
    a_neg = -jnp.exp(alog)
    a = dt * a_neg
    lo = _iota2((CHUNK, LANES), 1) < HEAD_DIM
    dt_w = jnp.concatenate([jnp.where(lo, _lane_bcast(dt, 2 * pp), _lane_bcast(dt, 2 * pp + 1)) for pp in range(4)], axis=1)
    xs = xc[:, :SSD_W]
    xd = xs * dt_w
    tril = (_iota2((CHUNK, CHUNK), 0) >= _iota2((CHUNK, CHUNK), 1))
    acs = _dot_hi(tril.astype(f32), a)
    acs_b = [_lane_bcast(acs, h) for h in range(HEADS)]
    return dict(c=c, sig_c=sig_c, xc=xc, xs=xs, dt=dt, a_neg=a_neg, a=a, dt_w=dt_w, xd=xd, acs_b=acs_b, tril=tril, lo=lo)


def _pair_fwd(q, pp, s_in):
    g = pp // 2
    lo = q["lo"]
    b_g = q["xc"][:, SSD_W + g * NSTATE:SSD_W + (g + 1) * NSTATE]
    c_g = q["xc"][:, SSD_W + 2 * NSTATE + g * NSTATE:SSD_W + 2 * NSTATE + (g + 1) * NSTATE]
    b16, c16 = b_g.astype(bf16), c_g.astype(bf16)
    gmat = _dot_nt(c16, b16)
    xp = q["xd"][:, pp * LANES:(pp + 1) * LANES]
    xp16 = xp.astype(bf16)
    ab0 = q["acs_b"][2 * pp]
    ab1 = q["acs_b"][2 * pp + 1]
    ls, ms, ys = [], [], []
    for ab in (ab0, ab1):
        lmat = jnp.exp(jnp.where(q["tril"], ab - ab.T, -jnp.inf))
        mmat = gmat * lmat
        ls.append(lmat)
        ms.append(mmat)
        ys.append(_dot(mmat.astype(bf16), xp16))
    y_diag = jnp.where(lo, ys[0], ys[1])
    ab_pair = jnp.where(lo, ab0, ab1)
    e_pair = jnp.exp(ab_pair)
    s16 = s_in.astype(bf16)
    y_off = _dot(c16, s16) * e_pair
    alast = ab_pair[CHUNK - 1:CHUNK, :]
    dec_pair = jnp.exp(alast - ab_pair)
    xdec = xp * dec_pair
    st = _dot_tn(b16, xdec.astype(bf16))
    cd_pair = jnp.exp(alast)
    s_out = s_in * cd_pair + st
    return dict(b16=b16, c16=c16, gmat=gmat, xp=xp, xp16=xp16, ls=ls, ms=ms, y=y_diag + y_off, y_off=y_off,
                e_pair=e_pair, dec_pair=dec_pair, xdec=xdec, cd_pair=cd_pair, s_out=s_out, s16=s16, lo=lo)


def _gate_norm_fwd(y_pre, z, ng):
    sz = _sigmoid(z)
    yg = y_pre * (z * sz)
    outs, stats = [], []
    half = SSD_W // 2
    for gi in range(2):
        o, xhat, r = _rms_fwd(yg[:, gi * half:(gi + 1) * half], ng[:, gi * half:(gi + 1) * half])
        outs.append(o)
        stats.append((xhat, r))
    return jnp.concatenate(outs, axis=1), sz, stats


def _pool_fwd(uext, u, row0, pw_ref, ps):
    pos = (row0 + _iota2((CHUNK, 1), 0) + 1).astype(f32)
    pooled, mixed, invs = [], [], []
    for gi, w in enumerate(WINDOWS):
        sl = slice(gi * POOL_G, (gi + 1) * POOL_G)
        acc = uext[pl.ds(HALO, CHUNK), sl]
        for j in range(1, w):
            acc = acc + uext[pl.ds(HALO - j, CHUNK), sl]
        den = jnp.minimum(pos, float(w))
        pg = acc / den - u[:, sl]
        pooled.append(pg)
        invs.append(den)
        mixed.append(_dot(pg.astype(bf16), pw_ref[gi]))
    mixed = jnp.concatenate(mixed, axis=1)
    return mixed * ps, pooled, mixed, invs


def _ssd_specs(s):
    nc = s // CHUNK
    hb = CHUNK // HALO
    return nc, hb


def _ssd_param_specs():
    return [_const_spec((SSD_CONV, XBC)), _const_spec((1, XBC)), _const_spec((1, LANES)), _const_spec((1, LANES)),
            _const_spec((1, SSD_W)), _const_spec((1, SSD_W)), _const_spec((4, POOL_G, POOL_G)), _const_spec((1, POOL_W))]


def _ssd_pool_fwd(zxu, dtr, prm):
    s = zxu.shape[0]
    nc, hb = _ssd_specs(s)

    def body(zx_ref, halo_ref, dtr_ref, cw_ref, cb_ref, dtb_ref, alog_ref, dsk_ref, ng_ref, pw_ref, ps_ref,
             ymix_ref, st_ref, state, xext, uext):
        i = pl.program_id(0)

        @pl.when(i == 0)
        def _():
            state[...] = jnp.zeros_like(state)

        zx = zx_ref[...].astype(f32)
        halo = jnp.where(i > 0, halo_ref[...].astype(f32), 0.0)
        xext[0:HALO, :] = halo[:, SSD_W:SSD_W + XBC]
        xext[HALO:, :] = zx[:, SSD_W:SSD_W + XBC]
        uext[0:HALO, :] = halo[:, SSD_W + XBC:]
        uext[HALO:, :] = zx[:, SSD_W + XBC:]
        q = _ssd_pre(zx, dtr_ref[...], xext, cw_ref[...], cb_ref[...], dtb_ref[...], alog_ref[...])
        ys = []
        for pp in range(4):
            s_in = state[pp]
            st_ref[0, pp] = s_in
            r = _pair_fwd(q, pp, s_in)
            state[pp] = r["s_out"]
            ys.append(r["y"])
        y_pre = jnp.concatenate(ys, axis=1) + q["xs"] * dsk_ref[...]
        y_ssd, _, _ = _gate_norm_fwd(y_pre, zx[:, :SSD_W], ng_ref[...])
        y_pool, _, _, _ = _pool_fwd(uext, zx[:, SSD_W + XBC:], i * CHUNK, pw_ref, ps_ref[...])
        ymix_ref[:, :SSD_W] = y_ssd.astype(bf16)
        ymix_ref[:, SSD_W:] = y_pool.astype(bf16)

    return pl.pallas_call(
        body, name="ssd_pool_fwd", grid=(nc,),
        out_shape=(jax.ShapeDtypeStruct((s, D_MODEL), bf16), jax.ShapeDtypeStruct((nc, 4, NSTATE, LANES), f32)),
        in_specs=[pl.BlockSpec((CHUNK, 2048), lambda i: (i, 0)),
                  pl.BlockSpec((HALO, 2048), lambda i: (jnp.maximum(i * hb - 1, 0), 0)),
                  pl.BlockSpec((CHUNK, LANES), lambda i: (i, 0))] + _ssd_param_specs(),
        out_specs=(pl.BlockSpec((CHUNK, D_MODEL), lambda i: (i, 0)),
                   pl.BlockSpec((1, 4, NSTATE, LANES), lambda i: (i, 0, 0, 0))),
        scratch_shapes=[pltpu.VMEM((4, NSTATE, LANES), f32), pltpu.VMEM((HALO + CHUNK, XBC), f32),
                        pltpu.VMEM((HALO + CHUNK, POOL_W), f32)],
        compiler_params=_cparams(),
    )(zxu, zxu, dtr, *prm)


def _ssd_pool_bwd(d_ymix, zxu, dtr, states, prm):
    s = zxu.shape[0]
    nc, hb = _ssd_specs(s)
    rev = lambda i: nc - 1 - i

    def body(dy_ref, zx_ref, halo_ref, dtr_ref, st_ref, cw_ref, cb_ref, dtb_ref, alog_ref, dsk_ref, ng_ref, pw_ref, ps_ref,
             dzx_ref, ddtr_ref, dcw_ref, dcb_ref, ddtb_ref, dalog_ref, ddsk_ref, dng_ref, dpw_ref, dps_ref,
             dstate, xext, uext, dxext, duext, cx, cu):
        i = pl.program_id(0)
        ci = nc - 1 - i

        @pl.when(i == 0)
        def _():
            dstate[...] = jnp.zeros_like(dstate)
            cx[...] = jnp.zeros_like(cx)
            cu[...] = jnp.zeros_like(cu)
            for r in (dcw_ref, dcb_ref, ddtb_ref, dalog_ref, ddsk_ref, dng_ref, dpw_ref, dps_ref):
                r[...] = jnp.zeros_like(r)

        zx = zx_ref[...].astype(f32)
        halo = jnp.where(ci > 0, halo_ref[...].astype(f32), 0.0)
        xext[0:HALO, :] = halo[:, SSD_W:SSD_W + XBC]
        xext[HALO:, :] = zx[:, SSD_W:SSD_W + XBC]
        uext[0:HALO, :] = halo[:, SSD_W + XBC:]
        uext[HALO:, :] = zx[:, SSD_W + XBC:]
        cw = cw_ref[...]
        q = _ssd_pre(zx, dtr_ref[...], xext, cw, cb_ref[...], dtb_ref[...], alog_ref[...])
        z = zx[:, :SSD_W]
        dy = dy_ref[...].astype(f32)
        d_yssd, d_ypool = dy[:, :SSD_W], dy[:, SSD_W:]

        pairs = [_pair_fwd(q, pp, st_ref[0, pp]) for pp in range(4)]
        dsk = dsk_ref[...]
        ng = ng_ref[...]
        y_pre = jnp.concatenate([r["y"] for r in pairs], axis=1) + q["xs"] * dsk
        _, sz, stats = _gate_norm_fwd(y_pre, z, ng)

        half = SSD_W // 2
        d_yg, d_ng = [], []
        for gi in range(2):
            xhat, r = stats[gi]
            dx, dg = _rms_bwd(d_yssd[:, gi * half:(gi + 1) * half], xhat, r, ng[:, gi * half:(gi + 1) * half])
            d_yg.append(dx)
            d_ng.append(dg)
        d_yg = jnp.concatenate(d_yg, axis=1)
        dng_ref[...] += jnp.concatenate(d_ng, axis=1)
        silu_z = z * sz
        d_ypre = d_yg * silu_z
        d_z = d_yg * y_pre * (sz * (1.0 + z * (1.0 - sz)))
        ddsk_ref[...] += jnp.sum(d_ypre * q["xs"], axis=0, keepdims=True)

        d_xd, acs_cols, dt_cols = [], [], []
        d_b = [None, None]
        d_c = [None, None]
        d_g = [None, None]
        last_row = _iota2((CHUNK, LANES), 0) == CHUNK - 1
        for pp in range(4):
            g = pp // 2
            r = pairs[pp]
            lo = r["lo"]
            dyp = d_ypre[:, pp * LANES:(pp + 1) * LANES]
            dyp16 = dyp.astype(bf16)
            ds_out = dstate[pp]
            ds16 = ds_out.astype(bf16)
            dye16 = (dyp * r["e_pair"]).astype(bf16)
            dstate[pp] = r["cd_pair"] * ds_out + _dot_tn(r["c16"], dye16)
            dc = _dot_nt(dye16, r["s16"])
            dxdec = _dot(r["b16"], ds16)
            db = _dot_nt(r["xdec"].astype(bf16), ds16)
            dxp = dxdec * r["dec_pair"]
            t2 = dxdec * r["xdec"]
            tail = jnp.sum(t2, axis=0, keepdims=True) + jnp.sum(ds_out * st_ref[0, pp] * r["cd_pair"], axis=0, keepdims=True)
            rp = dyp * r["y_off"] - t2 + jnp.where(last_row, tail, 0.0)
            dxs = []
            for hh in range(2):
                msk = lo if hh == 0 else jnp.logical_not(lo)
                m16 = r["ms"][hh].astype(bf16)
                dxs.append(_dot_tn(m16, dyp16))
                dm = _dot_nt(jnp.where(msk, dyp, 0.0).astype(bf16), r["xp16"])
                wmat = dm * r["ms"][hh]
                acs_cols.append(jnp.sum(wmat - wmat.T + jnp.where(msk, rp, 0.0), axis=1, keepdims=True))
                dgh = dm * r["ls"][hh]
                d_g[g] = dgh if d_g[g] is None else d_g[g] + dgh
            dxp = dxp + jnp.where(lo, dxs[0], dxs[1])
            d_xd.append(dxp)
            xprod = dxp * q["xs"][:, pp * LANES:(pp + 1) * LANES]
            dt_cols.append(jnp.sum(jnp.where(lo, xprod, 0.0), axis=1, keepdims=True))
            dt_cols.append(jnp.sum(jnp.where(lo, 0.0, xprod), axis=1, keepdims=True))
            d_b[g] = db if d_b[g] is None else d_b[g] + db
            d_c[g] = dc if d_c[g] is None else d_c[g] + dc
        for g in range(2):
            dg16 = d_g[g].astype(bf16)
            d_c[g] = d_c[g] + _dot(dg16, pairs[2 * g]["b16"])
            d_b[g] = d_b[g] + _dot_tn(dg16, pairs[2 * g]["c16"])
        d_xd = jnp.concatenate(d_xd, axis=1)
        triu = (_iota2((CHUNK, CHUNK), 0) <= _iota2((CHUNK, CHUNK), 1)).astype(f32)
        d_a = _dot_hi(triu, _to_columns(acs_cols))
        d_dt = d_a * q["a_neg"] + _to_columns(dt_cols)
        dalog_ref[...] += jnp.sum(d_a * q["dt"], axis=0, keepdims=True) * q["a_neg"]
        d_dtr = d_dt * _sigmoid(dtr_ref[...] + dtb_ref[...])
        ddtr_ref[...] = d_dtr
        ddtb_ref[...] += jnp.sum(d_dtr, axis=0, keepdims=True)
        d_xs = d_ypre * dsk + d_xd * q["dt_w"]

        d_xc = jnp.concatenate([d_xs, d_b[0], d_b[1], d_c[0], d_c[1]], axis=1)
        sc = q["sig_c"]
        d_conv = d_xc * (sc * (1.0 + q["c"] * (1.0 - sc)))
        dcb_ref[...] += jnp.sum(d_conv, axis=0, keepdims=True)
        dxext[...] = jnp.zeros_like(dxext)
        for k in range(SSD_CONV):
            dcw_ref[k:k + 1, :] += jnp.sum(d_conv * xext[pl.ds(HALO - 3 + k, CHUNK), :], axis=0, keepdims=True)
            dxext[pl.ds(HALO - 3 + k, CHUNK), :] += cw[k:k + 1] * d_conv
        dxext[pl.ds(CHUNK, HALO), :] += cx[...]
        cx[...] = dxext[0:HALO, :]

        ps = ps_ref[...]
        u = zx[:, SSD_W + XBC:]
        _, pooled, mixed, dens = _pool_fwd(uext, u, ci * CHUNK, pw_ref, ps)
        dps_ref[...] += jnp.sum(d_ypool * mixed, axis=0, keepdims=True)
        d_mixed = d_ypool * ps
        duext[...] = jnp.zeros_like(duext)
        for gi, w in enumerate(WINDOWS):
            sl = slice(gi * POOL_G, (gi + 1) * POOL_G)
            dm16 = d_mixed[:, sl].astype(bf16)
            dpw_ref[gi] += _dot_tn(pooled[gi].astype(bf16), dm16)
            d_pg = _dot_nt(dm16, pw_ref[gi])
            d_mean = d_pg / dens[gi]
            duext[pl.ds(HALO, CHUNK), sl] += d_mean - d_pg
            for j in range(1, w):
                duext[pl.ds(HALO - j, CHUNK), sl] += d_mean
        duext[pl.ds(CHUNK, HALO), :] += cu[...]
        cu[...] = duext[0:HALO, :]

        dzx_ref[:, :SSD_W] = d_z.astype(bf16)
        dzx_ref[:, SSD_W:SSD_W + XBC] = dxext[HALO:, :].astype(bf16)
        dzx_ref[:, SSD_W + XBC:] = duext[HALO:, :].astype(bf16)

    small = lambda shape: pl.BlockSpec(shape, lambda i: (0,) * len(shape))
    small_shapes = [(SSD_CONV, XBC), (1, XBC), (1, LANES), (1, LANES), (1, SSD_W), (1, SSD_W), (4, POOL_G, POOL_G), (1, POOL_W)]
    return pl.pallas_call(
        body, name="ssd_pool_bwd", grid=(nc,),
        out_shape=(jax.ShapeDtypeStruct((s, 2048), bf16), jax.ShapeDtypeStruct((s, LANES), f32))
        + tuple(jax.ShapeDtypeStruct(sh, f32) for sh in small_shapes),
        in_specs=[pl.BlockSpec((CHUNK, D_MODEL), lambda i: (rev(i), 0)),
                  pl.BlockSpec((CHUNK, 2048), lambda i: (rev(i), 0)),
                  pl.BlockSpec((HALO, 2048), lambda i: (jnp.maximum(rev(i) * hb - 1, 0), 0)),
                  pl.BlockSpec((CHUNK, LANES), lambda i: (rev(i), 0)),
                  pl.BlockSpec((1, 4, NSTATE, LANES), lambda i: (rev(i), 0, 0, 0))] + _ssd_param_specs(),
        out_specs=(pl.BlockSpec((CHUNK, 2048), lambda i: (rev(i), 0)), pl.BlockSpec((CHUNK, LANES), lambda i: (rev(i), 0)))
        + tuple(small(sh) for sh in small_shapes),
        scratch_shapes=[pltpu.VMEM((4, NSTATE, LANES), f32), pltpu.VMEM((HALO + CHUNK, XBC), f32),
                        pltpu.VMEM((HALO + CHUNK, POOL_W), f32), pltpu.VMEM((HALO + CHUNK, XBC), f32),
                        pltpu.VMEM((HALO + CHUNK, POOL_W), f32), pltpu.VMEM((HALO, XBC), f32), pltpu.VMEM((HALO, POOL_W), f32)],
        compiler_params=_cparams(),
    )(d_ymix, zxu, zxu, dtr, states, *prm)


FFN_BWD_TILE = 256


def _prev_halo_spec(t, width):
    hb = t // HALO
    return pl.BlockSpec((HALO, width), lambda i: (jnp.maximum(i * hb - 1, 0), 0))


def _ffn_half(hn16, j, wup_ref, cw_ref, cb_ref, up_scr, rows):
    up_scr[...] = _dot(hn16, wup_ref[j])
    cw = cw_ref[j]
    cv = cb_ref[j] + cw[0:1] * up_scr[pl.ds(HALO - 2, rows), :]
    for k in range(1, FFN_CONV):
        cv = cv + cw[k:k + 1] * up_scr[pl.ds(HALO - 2 + k, rows), :]
    return cv


def _out_ffn_ple_fwd(h, ymix, p_l, w_out, g2, wup_h, cw_h, cb_h, wdn_h, g3, w_gate, w_proj, gather=()):
    s = h.shape[0]
    t = _row_tile(s)
    n = s // t
    ng = len(gather)

    def body(*refs):
        (h_ref, hh_ref, ym_ref, ymh_ref, p_ref, wo_ref, g2_ref, wup_ref, cw_ref, cb_ref, wdn_ref, g3_ref, wg_ref,
         wp_ref) = refs[:14]
        x_refs = refs[14:14 + ng]
        h1_ref, h2_ref, h3_ref, up16_ref = refs[14 + ng:18 + ng]
        land_refs = refs[18 + ng:18 + 2 * ng]
        up_scr = refs[18 + 2 * ng]
        sems = refs[19 + 2 * ng:]
        i = pl.program_id(0)
        if ng:
            @pl.when(i == 0)
            def _():
                _ag_start(x_refs, land_refs, *sems)

            @pl.when(i == n - 1)
            def _():
                _ag_finish(x_refs, land_refs, *sems)

        hh = jnp.where(i > 0, hh_ref[...], 0.0)
        ymh = jnp.where(i > 0, ymh_ref[...].astype(f32), 0.0)
        h_ext = jnp.concatenate([hh, h_ref[...]], axis=0)
        ym_ext = jnp.concatenate([ymh, ym_ref[...].astype(f32)], axis=0).astype(bf16)
        h1_ext = h_ext + _dot(ym_ext, wo_ref[...])
        hn2, _, _ = _rms_fwd(h1_ext, g2_ref[...])
        hn16 = hn2.astype(bf16)
        h1 = h1_ext[HALO:, :]
        acc = h1
        for j in range(2):
            cv = _ffn_half(hn16, j, wup_ref, cw_ref, cb_ref, up_scr, t)
            up16_ref[j] = up_scr[pl.ds(HALO, t), :].astype(bf16)
            act = _gelu(cv[:, :FF_HALF]) * cv[:, FF_HALF:]
            acc = acc + _dot(act.astype(bf16), wdn_ref[j])
        h2 = acc
        hn3, _, _ = _rms_fwd(h2, g3_ref[...])
        gate = _sigmoid(_dot(hn3.astype(bf16), wg_ref[...]))
        pp = _dot(p_ref[0].astype(bf16), wp_ref[...])
        h1_ref[...] = h1
        h2_ref[...] = h2
        h3_ref[...] = h2 + pp * gate

    row = lambda w: pl.BlockSpec((t, w), lambda i: (i, 0))
    outs = pl.pallas_call(
        body, name="out_ffn_ple_fwd_gather" if ng else "out_ffn_ple_fwd", grid=(n,),
        out_shape=tuple(jax.ShapeDtypeStruct((s, D_MODEL), f32) for _ in range(3)) + (jax.ShapeDtypeStruct((2, s, D_FF), bf16),)
        + tuple(jax.ShapeDtypeStruct((N_CHIPS,) + a.shape, a.dtype) for a in gather),
        in_specs=[row(D_MODEL), _prev_halo_spec(t, D_MODEL), row(D_MODEL), _prev_halo_spec(t, D_MODEL),
                  pl.BlockSpec((1, t, D_PLE), lambda i: (p_l[1], i, 0)),
                  _const_spec((D_MODEL, D_MODEL)), _const_spec((1, D_MODEL)), _const_spec((2, D_MODEL, D_FF)),
                  _const_spec((2, FFN_CONV, D_FF)), _const_spec((2, 1, D_FF)), _const_spec((2, FF_HALF, D_MODEL)),
                  _const_spec((1, D_MODEL)), _const_spec((D_MODEL, D_MODEL)), _const_spec((D_PLE, D_MODEL))] + [_hbm()] * ng,
        out_specs=tuple(row(D_MODEL) for _ in range(3)) + (pl.BlockSpec((2, t, D_FF), lambda i: (0, i, 0)),) + (_hbm(),) * ng,
        scratch_shapes=[pltpu.VMEM((HALO + t, D_FF), f32)]
        + ([pltpu.SemaphoreType.DMA((6 * ng,)), pltpu.SemaphoreType.DMA((6 * ng,))] if ng else []),
        compiler_params=_cparams(),
    )(h, h, ymix, ymix, p_l[0], w_out, g2, wup_h, cw_h, cb_h, wdn_h, g3, w_gate, w_proj, *gather)
    return outs[:4] + (_own_block_in(outs[4:], gather),)


def _ple_bwd(dh3, h2, p_l, g3, w_gate, w_proj):
    s = h2.shape[0]
    t = _row_tile(s, WIDE_ROW_TILE)
    n = s // t

    def body(dh3_ref, h2_ref, p_ref, g3_ref, wg_ref, wp_ref, dh2_ref, dwg_ref, dwp_ref, dg3_ref):
        i = pl.program_id(0)

        @pl.when(i == 0)
        def _():
            dwg_ref[...] = jnp.zeros_like(dwg_ref)
            dwp_ref[...] = jnp.zeros_like(dwp_ref)
            dg3_ref[...] = jnp.zeros_like(dg3_ref)

        g3 = g3_ref[...]
        dh3 = dh3_ref[...]
        hn3, xhat, r = _rms_fwd(h2_ref[...], g3)
        hn16 = hn3.astype(bf16)
        gate = _sigmoid(_dot(hn16, wg_ref[...]))
        p16 = p_ref[0].astype(bf16)
        pp = _dot(p16, wp_ref[...])
        d_pp = (dh3 * gate).astype(bf16)
        d_pre = (dh3 * pp * gate * (1.0 - gate)).astype(bf16)
        dwp_ref[...] += _dot_tn(p16, d_pp)
        dwg_ref[...] += _dot_tn(hn16, d_pre)
        dx, dg = _rms_bwd(_dot_nt(d_pre, wg_ref[...]), xhat, r, g3)
        dg3_ref[...] += dg
        dh2_ref[...] = dh3 + dx

    row = lambda w: pl.BlockSpec((t, w), lambda i: (i, 0))
    fixed = lambda shape: pl.BlockSpec(shape, lambda i: (0,) * len(shape))
    return pl.pallas_call(
        body, name="ple_bwd", grid=(n,),
        out_shape=(jax.ShapeDtypeStruct((s, D_MODEL), f32), jax.ShapeDtypeStruct((D_MODEL, D_MODEL), f32),
                   jax.ShapeDtypeStruct((D_PLE, D_MODEL), f32), jax.ShapeDtypeStruct((1, D_MODEL), f32)),
        in_specs=[row(D_MODEL), row(D_MODEL), pl.BlockSpec((1, t, D_PLE), lambda i: (p_l[1], i, 0)), _const_spec((1, D_MODEL)),
                  _const_spec((D_MODEL, D_MODEL)), _const_spec((D_PLE, D_MODEL))],
        out_specs=(row(D_MODEL), fixed((D_MODEL, D_MODEL)), fixed((D_PLE, D_MODEL)), fixed((1, D_MODEL))),
        compiler_params=_cparams(),
    )(dh3, h2, p_l[0], g3, w_gate, w_proj)


def _ffn_bwd(dh2, h1, up16, g2, wup_h, cw_h, cb_h, wdn_h, exchange=()):
    s = h1.shape[0]
    t = min(FFN_BWD_TILE, s)
    n = s // t
    hb = t // HALO
    last_hb = s // HALO - 1
    nx = len(exchange)

    def body(*refs):
        dh2_ref, dh2n_ref, h1_ref, up_ref, upp_ref, upn_ref, g2_ref, wup_ref, cw_ref, cb_ref, wdn_ref = refs[:11]
        p_refs = refs[11:11 + nx]
        part_ref, dwup_ref, dwdn_ref, dcw_ref, dcb_ref = refs[11 + nx:16 + nx]
        land_refs = refs[16 + nx:16 + 2 * nx]
        up_scr, dcv_scr, acc_up, acc_dn = refs[16 + 2 * nx:20 + 2 * nx]
        sems = refs[20 + 2 * nx:]
        j = pl.program_id(0)
        i = pl.program_id(1)
        if nx:
            @pl.when(jnp.logical_and(j == 0, i == 0))
            def _():
                _cx_start(p_refs, land_refs, *sems)

            @pl.when(jnp.logical_and(j == 1, i == n - 1))
            def _():
                _cx_finish(p_refs, land_refs, *sems)

        @pl.when(i == 0)
        def _():
            acc_up[...] = jnp.zeros_like(acc_up)
            acc_dn[...] = jnp.zeros_like(acc_dn)
            dcw_ref[...] = jnp.zeros_like(dcw_ref)
            dcb_ref[...] = jnp.zeros_like(dcb_ref)

        hn2, _, _ = _rms_fwd(h1_ref[...], g2_ref[...])
        hn16 = hn2.astype(bf16)
        up_scr[0:HALO, :] = jnp.where(i > 0, upp_ref[0].astype(f32), 0.0)
        up_scr[HALO:HALO + t, :] = up_ref[0].astype(f32)
        up_scr[HALO + t:, :] = upn_ref[0].astype(f32)
        cw = cw_ref[0]
        rows = t + HALO
        cv = cb_ref[0] + cw[0:1] * up_scr[pl.ds(HALO - 2, rows), :]
        for k in range(1, FFN_CONV):
            cv = cv + cw[k:k + 1] * up_scr[pl.ds(HALO - 2 + k, rows), :]
        dh2 = dh2_ref[...]
        dh2n = jnp.where(i < n - 1, dh2n_ref[...], 0.0)
        dh2_ext16 = jnp.concatenate([dh2, dh2n], axis=0).astype(bf16)
        d_act = _dot_nt(dh2_ext16, wdn_ref[0])
        gate, val = cv[:, :FF_HALF], cv[:, FF_HALF:]
        gl, dgl = _gelu_and_grad(gate)
        dcv_scr[:, :FF_HALF] = d_act * val * dgl
        dcv_scr[:, FF_HALF:] = d_act * gl
        act16 = (gl[:t] * val[:t]).astype(bf16)
        acc_dn[...] += _dot_tn(act16, dh2_ext16[:t])
        d_cv = dcv_scr[pl.ds(0, t), :]
        dcb_ref[0] += jnp.sum(d_cv, axis=0, keepdims=True)
        d_up = cw[2:3] * d_cv
        dcw_ref[0, 2:3, :] += jnp.sum(d_cv * up_scr[pl.ds(HALO, t), :], axis=0, keepdims=True)
        for k in range(FFN_CONV - 1):
            dcw_ref[0, k:k + 1, :] += jnp.sum(d_cv * up_scr[pl.ds(HALO - 2 + k, t), :], axis=0, keepdims=True)
            d_up = d_up + cw[k:k + 1] * dcv_scr[pl.ds(2 - k, t), :]
        d_up16 = d_up.astype(bf16)
        part_ref[0] = _dot_nt(d_up16, wup_ref[0])
        acc_up[...] += _dot_tn(hn16, d_up16)

        @pl.when(i == n - 1)
        def _():
            pltpu.sync_copy(acc_up.at[:, pl.ds(0, FF_HALF)], dwup_ref.at[j])
            pltpu.sync_copy(acc_up.at[:, pl.ds(FF_HALF, FF_HALF)], dwup_ref.at[2 + j])
            pltpu.sync_copy(acc_dn, dwdn_ref.at[j])

    outs = pl.pallas_call(
        body, name="ffn_bwd_exchange" if nx else "ffn_bwd", grid=(2, n),
        out_shape=(jax.ShapeDtypeStruct((2, s, D_MODEL), f32), jax.ShapeDtypeStruct((N_CHIPS, D_MODEL, FF_HALF), f32),
                   jax.ShapeDtypeStruct((2, FF_HALF, D_MODEL), f32), jax.ShapeDtypeStruct((2, FFN_CONV, D_FF), f32),
                   jax.ShapeDtypeStruct((2, 1, D_FF), f32))
        + tuple(jax.ShapeDtypeStruct((3,) + a.shape[1:], a.dtype) for a in exchange),
        in_specs=[pl.BlockSpec((t, D_MODEL), lambda j, i: (i, 0)),
                  pl.BlockSpec((HALO, D_MODEL), lambda j, i: (jnp.minimum((i + 1) * hb, last_hb), 0)),
                  pl.BlockSpec((t, D_MODEL), lambda j, i: (i, 0)),
                  pl.BlockSpec((1, t, D_FF), lambda j, i: (j, i, 0)),
                  pl.BlockSpec((1, HALO, D_FF), lambda j, i: (j, jnp.maximum(i * hb - 1, 0), 0)),
                  pl.BlockSpec((1, HALO, D_FF), lambda j, i: (j, jnp.minimum((i + 1) * hb, last_hb), 0)),
                  _const_spec((1, D_MODEL)),
                  pl.BlockSpec((1, D_MODEL, D_FF), lambda j, i: (j, 0, 0), pipeline_mode=pl.Buffered(1)),
                  pl.BlockSpec((1, FFN_CONV, D_FF), lambda j, i: (j, 0, 0)),
                  pl.BlockSpec((1, 1, D_FF), lambda j, i: (j, 0, 0)),
                  pl.BlockSpec((1, FF_HALF, D_MODEL), lambda j, i: (j, 0, 0), pipeline_mode=pl.Buffered(1))] + [_hbm()] * nx,
        out_specs=(pl.BlockSpec((1, t, D_MODEL), lambda j, i: (j, i, 0)), pl.BlockSpec(memory_space=pl.ANY),
                   pl.BlockSpec(memory_space=pl.ANY), pl.BlockSpec((1, FFN_CONV, D_FF), lambda j, i: (j, 0, 0)),
                   pl.BlockSpec((1, 1, D_FF), lambda j, i: (j, 0, 0))) + (_hbm(),) * nx,
        scratch_shapes=[pltpu.VMEM((2 * HALO + t, D_FF), f32), pltpu.VMEM((HALO + t, D_FF), f32),
                        pltpu.VMEM((D_MODEL, D_FF), f32), pltpu.VMEM((FF_HALF, D_MODEL), f32)]
        + ([pltpu.SemaphoreType.DMA((3 * nx,)), pltpu.SemaphoreType.DMA((3 * nx,))] if nx else []),
        compiler_params=_cparams(("arbitrary", "arbitrary")),
    )(dh2, dh2, h1, up16, up16, up16, g2, wup_h, cw_h, cb_h, wdn_h, *exchange)
    return outs[:5] + (list(outs[5:]),)


def _out_bwd(dh2, parts, h1, ymix, g2, w_out):
    s = h1.shape[0]
    t = _row_tile(s, WIDE_ROW_TILE)
    n = s // t

    def body(dh2_ref, part_ref, h1_ref, ym_ref, g2_ref, wo_ref, dh1_ref, dym_ref, dwo_ref, dg2_ref):
        i = pl.program_id(0)

        @pl.when(i == 0)
        def _():
            dwo_ref[...] = jnp.zeros_like(dwo_ref)
            dg2_ref[...] = jnp.zeros_like(dg2_ref)

        g2 = g2_ref[...]
        _, xhat, r = _rms_fwd(h1_ref[...], g2)
        dx, dg = _rms_bwd(part_ref[0] + part_ref[1], xhat, r, g2)
        dg2_ref[...] += dg
        dh1 = dh2_ref[...] + dx
        dh1_ref[...] = dh1
        dh16 = dh1.astype(bf16)
        dym_ref[...] = _dot_nt(dh16, wo_ref[...]).astype(bf16)
        dwo_ref[...] += _dot_tn(ym_ref[...], dh16)

    row = lambda w: pl.BlockSpec((t, w), lambda i: (i, 0))
    fixed = lambda shape: pl.BlockSpec(shape, lambda i: (0,) * len(shape))
    return pl.pallas_call(
        body, name="out_bwd", grid=(n,),
        out_shape=(jax.ShapeDtypeStruct((s, D_MODEL), f32), jax.ShapeDtypeStruct((s, D_MODEL), bf16),
                   jax.ShapeDtypeStruct((D_MODEL, D_MODEL), f32), jax.ShapeDtypeStruct((1, D_MODEL), f32)),
        in_specs=[row(D_MODEL), pl.BlockSpec((2, t, D_MODEL), lambda i: (0, i, 0)), row(D_MODEL), row(D_MODEL),
                  _const_spec((1, D_MODEL)), _const_spec((D_MODEL, D_MODEL))],
        out_specs=(row(D_MODEL), row(D_MODEL), fixed((D_MODEL, D_MODEL)), fixed((1, D_MODEL))),
        compiler_params=_cparams(),
    )(dh2, parts, h1, ymix, g2, w_out)


def _loss_head(h, target, gf):
    s = h.shape[0]
    t = _row_tile(s, WIDE_ROW_TILE)

    def body(h_ref, t_ref, g_ref, dh_ref, dg_ref, loss_ref):
        i = pl.program_id(0)

        @pl.when(i == 0)
        def _():
            dg_ref[...] = jnp.zeros_like(dg_ref)
            loss_ref[...] = jnp.zeros_like(loss_ref)

        g = g_ref[...]
        y, xhat, r = _rms_fwd(h_ref[...], g)
        diff = y - t_ref[...]
        per_row = jnp.mean(diff * diff, axis=-1, keepdims=True)
        loss_ref[...] += 0.5 * jnp.sum(per_row, axis=0, keepdims=True)
        dx, dg = _rms_bwd(diff * (1.0 / D_MODEL), xhat, r, g)
        dg_ref[...] += dg
        dh_ref[...] = dx

    row = pl.BlockSpec((t, D_MODEL), lambda i: (i, 0))
    return pl.pallas_call(
        body, name="loss_head", grid=(s // t,),
        out_shape=(jax.ShapeDtypeStruct((s, D_MODEL), f32), jax.ShapeDtypeStruct((1, D_MODEL), f32),
                   jax.ShapeDtypeStruct((1, LANES), f32)),
        in_specs=[row, row, _const_spec((1, D_MODEL))],
        out_specs=(row, pl.BlockSpec((1, D_MODEL), lambda i: (0, 0)), pl.BlockSpec((1, LANES), lambda i: (0, 0))),
        compiler_params=_cparams(),
    )(h, target, gf)


def _prep_layer(w):
    in4 = w["w_in4"]
    cut = D_IN // N_CHIPS
    tail = 3 * cut - (SSD_W + XBC)
    w_r = jnp.concatenate([in4[0], in4[1], in4[2][:, :cut - tail], in4[3][:, HEADS - tail:], in4[2][:, cut - tail:],
                           in4[3][:, :HEADS - tail], jnp.zeros((D_MODEL, LANES - HEADS), in4.dtype)], axis=1)
    pad8 = lambda v: jnp.concatenate([v, jnp.zeros((LANES - HEADS,), f32)]).reshape(1, LANES)
    halves = _halves
    ssd_prm = (w["ssd_conv_w"], w["ssd_conv_b"].reshape(1, XBC), pad8(w["ssd_dt_bias"]), pad8(w["ssd_a_log"]),
               jnp.repeat(w["ssd_d"], HEAD_DIM).reshape(1, SSD_W), w["ssd_norm_g"].reshape(1, SSD_W),
               w["pool_w"], w["pool_scale"].reshape(1, POOL_W))
    return dict(
        g1=w["mix_norm_g"].reshape(1, D_MODEL), w_r=w_r, ssd=ssd_prm, w_out=w["w_out"], g2=w["ffn_norm_g"].reshape(1, D_MODEL),
        wup_h=w["wup_h"], cw_h=halves(w["ffn_conv_w"]), cb_h=halves(w["ffn_conv_b"].reshape(1, 2 * D_FF)),
        wdn_h=w["wdn_h"], g3=w["ple_norm_g"].reshape(1, D_MODEL), w_gate=w["ple_w_gate"], w_proj=w["ple_w_proj"])


def _halves(a):
    return jnp.stack([jnp.concatenate([a[..., j * FF_HALF:(j + 1) * FF_HALF],
                                       a[..., D_FF + j * FF_HALF:D_FF + (j + 1) * FF_HALF]], axis=-1) for j in range(2)])


def _unhalve(a):
    return jnp.concatenate([a[0][..., :FF_HALF], a[1][..., :FF_HALF], a[0][..., FF_HALF:], a[1][..., FF_HALF:]], axis=-1)


def _layer_fwd(h, p_l, q, gather=()):
    zxu, dtr = _mix_in_fwd(h, q["g1"], q["w_r"])
    ymix, states = _ssd_pool_fwd(zxu, dtr, q["ssd"])
    h1, h2, h3, up16, gathered = _out_ffn_ple_fwd(h, ymix, p_l, q["w_out"], q["g2"], q["wup_h"], q["cw_h"], q["cb_h"],
                                                   q["wdn_h"], q["g3"], q["w_gate"], q["w_proj"], gather)
    return h3, (h, zxu, dtr, ymix, states, h1, h2, up16), gathered


def _layer_bwd(dh, saved, p_l, q, exchange=()):
    h0, zxu, dtr, ymix, states, h1, h2, up16 = saved
    dh2, d_wg, d_wp, d_g3 = _ple_bwd(dh, h2, p_l, q["g3"], q["w_gate"], q["w_proj"])
    parts, d_wup, d_wdn, d_cw, d_cb, lands = _ffn_bwd(dh2, h1, up16, q["g2"], q["wup_h"], q["cw_h"], q["cb_h"], q["wdn_h"],
                                                      exchange)
    dh1, d_ymix, d_wo, d_g2 = _out_bwd(dh2, parts, h1, ymix, q["g2"], q["w_out"])
    (d_zxu, d_dtr, d_scw, d_scb, d_dtb, d_alog, d_dsk, d_ng, d_pw, d_ps) = _ssd_pool_bwd(d_ymix, zxu, dtr, states, q["ssd"])
    dh, d_wr, d_g1 = _mix_in_bwd(d_zxu, d_dtr, h0, dh1, q["g1"], q["w_r"])
    cut = D_IN // N_CHIPS
    tail = 3 * cut - (SSD_W + XBC)
    w_in4 = jnp.stack([d_wr[:, :cut], d_wr[:, cut:2 * cut],
                       jnp.concatenate([d_wr[:, 2 * cut:SSD_W + XBC], d_wr[:, 2048:2048 + tail]], axis=1),
                       jnp.concatenate([d_wr[:, 2048 + tail:2048 + HEADS], d_wr[:, SSD_W + XBC:2048]], axis=1)])
    grads = dict(
        mix_norm_g=d_g1.reshape(D_MODEL), w_in4=w_in4,
        ssd_conv_w=d_scw, ssd_conv_b=d_scb.reshape(XBC), ssd_dt_bias=d_dtb[0, :HEADS], ssd_a_log=d_alog[0, :HEADS],
        ssd_d=jnp.sum(d_dsk.reshape(HEADS, HEAD_DIM), axis=1), ssd_norm_g=d_ng.reshape(SSD_W), pool_w=d_pw,
        pool_scale=d_ps.reshape(POOL_W), w_out=d_wo, ffn_norm_g=d_g2.reshape(D_MODEL), ffn_w_up4=d_wup,
        ffn_conv_w=_unhalve(d_cw), ffn_conv_b=_unhalve(d_cb).reshape(2 * D_FF), wdn_h=d_wdn,
        ple_norm_g=d_g3.reshape(D_MODEL), ple_w_gate=d_wg, ple_w_proj=d_wp)
    return dh, grads, lands


def _device_step(x, p, target, layers, final_g):
    preps = [_prep_layer(w) for w in layers]
    saved = []
    h = x
    for l, q in enumerate(preps):
        h, keep, _ = _layer_fwd(h, (p, l), q)
        saved.append(keep)
    dh, d_gf, loss = _loss_head(h, target, final_g.reshape(1, D_MODEL))
    grads = [None] * len(preps)
    for l in reversed(range(len(preps))):
        dh, grads[l], _ = _layer_bwd(dh, saved[l], (p, l), preps[l])
    return loss, dh, grads, d_gf.reshape(D_MODEL)


MESH = pl.DeviceIdType.MESH
COLS = 1024
N_CHIPS = 4
WEIGHT_NAMES = ["mix_norm_g", "w_in", "ssd_conv_w", "ssd_conv_b", "ssd_dt_bias", "ssd_a_log", "ssd_d", "ssd_norm_g", "pool_w",
                "pool_scale", "w_out", "ffn_norm_g", "ffn_w_up", "ffn_conv_w", "ffn_conv_b", "ffn_w_down", "ple_norm_g",
                "ple_w_gate", "ple_w_proj", "final_norm_g"]
SHARD_AXIS = {"w_in": 2, "ssd_conv_w": 2, "w_out": 1, "ffn_w_up": 2, "ffn_conv_w": 2, "ffn_w_down": 1, "ple_w_gate": 1,
              "ple_w_proj": 2}
MATMUL_SHARDED = ["w_in", "w_out", "ffn_w_up", "ffn_w_down", "ple_w_gate", "ple_w_proj"]
F32_SHARDED = ["ssd_conv_w", "ffn_conv_w"]
REPLICATED = [n for n in WEIGHT_NAMES if n not in SHARD_AXIS]


def _mesh_pos():
    return lax.axis_index("x"), lax.axis_index("y"), lax.axis_index("c")


def _hbm():
    return pl.BlockSpec(memory_space=pl.ANY)


def _all_gather_chips(arrays, name):
    n = len(arrays)

    def body(*refs):
        x_refs, out_refs, sems = refs[:n], refs[n:2 * n], refs[2 * n:]
        _ag_start(x_refs, out_refs, *sems)
        _ag_finish(x_refs, out_refs, *sems)

    gathered = pl.pallas_call(
        body, name=name, out_shape=[jax.ShapeDtypeStruct((N_CHIPS,) + a.shape, a.dtype) for a in arrays],
        in_specs=[_hbm()] * n, out_specs=[_hbm()] * n,
        scratch_shapes=[pltpu.SemaphoreType.DMA((6 * n,)), pltpu.SemaphoreType.DMA((6 * n,))],
    )(*arrays)
    return _own_block_in(gathered, arrays)


def _ag_copies(x_refs, out_refs, send_sems, recv_sems):
    x, y, c = _mesh_pos()
    me = 2 * x + y
    flips = [(1 - x, y), (x, 1 - y), (1 - x, 1 - y)]

    def rows(a, chip, half):
        hr = x_refs[a].shape[0] // 2
        return out_refs[a].at[chip, pl.ds(half * hr, hr), :]

    def copy(k, src, dst, to):
        return pltpu.make_async_remote_copy(src_ref=src, dst_ref=dst, send_sem=send_sems.at[k], recv_sem=recv_sems.at[k],
                                            device_id=to, device_id_type=MESH)

    def first(a, j):
        hr = x_refs[a].shape[0] // 2
        return copy(6 * a + j, x_refs[a].at[pl.ds(c * hr, hr), :], rows(a, me, c), flips[j] + (c,))

    def landed(a, j):
        blk = rows(a, 2 * flips[j][0] + flips[j][1], c)
        return copy(6 * a + j, blk, blk, flips[j] + (c,))

    def passed(a, j, half):
        blk = rows(a, 2 * flips[j][0] + flips[j][1], half)
        return copy(6 * a + 3 + j, blk, blk, (x, y, 1 - c))

    return first, landed, passed, c


def _ag_start(x_refs, out_refs, send_sems, recv_sems):
    first, _, _, _ = _ag_copies(x_refs, out_refs, send_sems, recv_sems)
    for a in range(len(x_refs)):
        for j in range(3):
            first(a, j).start()


def _ag_finish(x_refs, out_refs, send_sems, recv_sems):
    first, landed, passed, c = _ag_copies(x_refs, out_refs, send_sems, recv_sems)
    n = len(x_refs)
    for j in range(3):
        for a in range(n):
            landed(a, j).wait_recv()
            passed(a, j, c).start()
    for j in range(3):
        for a in range(n):
            passed(a, j, 1 - c).wait_recv()
    for a in range(n):
        for j in range(3):
            first(a, j).wait_send()
            passed(a, j, c).wait_send()


def _own_block_in(gathered, arrays):
    if not arrays:
        return []
    me = 2 * lax.axis_index("x") + lax.axis_index("y")
    return [lax.dynamic_update_slice(o, a[None], (me, 0, 0)) for o, a in zip(gathered, arrays)]


def _rs_pair_exchange(gs):
    n = len(gs)

    def body(*refs):
        g_refs, land_refs, (send_sems, recv_sems) = refs[:n], refs[n:2 * n], refs[2 * n:]
        x, y, c = _mesh_pos()
        cps = []
        for a in range(n):
            hr = gs[a].shape[1] // 2
            cps.append(pltpu.make_async_remote_copy(
                src_ref=g_refs[a].at[:, pl.ds((1 - c) * hr, hr), :], dst_ref=land_refs[a], send_sem=send_sems.at[a],
                recv_sem=recv_sems.at[a], device_id=(x, y, 1 - c), device_id_type=MESH))
        for cp in cps:
            cp.start()
        for cp in cps:
            cp.wait()

    return pl.pallas_call(
        body, name="rs_pair_exchange",
        out_shape=[jax.ShapeDtypeStruct((N_CHIPS, g.shape[1] // 2, g.shape[2]), g.dtype) for g in gs],
        in_specs=[_hbm()] * n, out_specs=[_hbm()] * n,
        scratch_shapes=[pltpu.SemaphoreType.DMA((n,)), pltpu.SemaphoreType.DMA((n,))],
    )(*gs)


def _rs_pair_add(g, land, c_idx):
    _, r, cols = g.shape
    hr = r // 2

    def body(c_ref, g_ref, l_ref, o_ref):
        o_ref[...] = (g_ref[...] + l_ref[...]).astype(bf16)

    return pl.pallas_call(
        body, name="rs_pair_add", out_shape=jax.ShapeDtypeStruct((N_CHIPS, hr, cols), bf16),
        grid_spec=pltpu.PrefetchScalarGridSpec(
            num_scalar_prefetch=1, grid=(N_CHIPS,),
            in_specs=[pl.BlockSpec((1, hr, cols), lambda k, c_ref: (k, c_ref[0], 0)),
                      pl.BlockSpec((1, hr, cols), lambda k, c_ref: (k, 0, 0))],
            out_specs=pl.BlockSpec((1, hr, cols), lambda k, c_ref: (k, 0, 0))),
        compiler_params=_cparams(),
    )(c_idx, g, land)


def _rs_chip_exchange(parts):
    n = len(parts)

    def body(*refs):
        p_refs, land_refs, sems = refs[:n], refs[n:2 * n], refs[2 * n:]
        _cx_start(p_refs, land_refs, *sems)
        _cx_finish(p_refs, land_refs, *sems)

    return pl.pallas_call(
        body, name="rs_chip_exchange", out_shape=[jax.ShapeDtypeStruct((3,) + p.shape[1:], p.dtype) for p in parts],
        in_specs=[_hbm()] * n, out_specs=[_hbm()] * n,
        scratch_shapes=[pltpu.SemaphoreType.DMA((3 * n,)), pltpu.SemaphoreType.DMA((3 * n,))],
    )(*parts)


def _cx_copies(p_refs, land_refs, send_sems, recv_sems):
    x, y, c = _mesh_pos()
    flips = [(1 - x, y), (x, 1 - y), (1 - x, 1 - y)]
    return [pltpu.make_async_remote_copy(src_ref=p_refs[a].at[2 * fx + fy], dst_ref=land_refs[a].at[j],
                                         send_sem=send_sems.at[3 * a + j], recv_sem=recv_sems.at[3 * a + j],
                                         device_id=(fx, fy, c), device_id_type=MESH)
            for a in range(len(p_refs)) for j, (fx, fy) in enumerate(flips)]


def _cx_start(p_refs, land_refs, send_sems, recv_sems):
    for cp in _cx_copies(p_refs, land_refs, send_sems, recv_sems):
        cp.start()


def _cx_finish(p_refs, land_refs, send_sems, recv_sems):
    for cp in _cx_copies(p_refs, land_refs, send_sems, recv_sems):
        cp.wait()


def _rs_chip_add(part, land, me_idx):
    _, hr, cols = part.shape

    def body(me_ref, p_ref, l_ref, o_ref):
        o_ref[...] = ((p_ref[0].astype(f32) + l_ref[0].astype(f32)) + l_ref[1].astype(f32)) + l_ref[2].astype(f32)

    return pl.pallas_call(
        body, name="rs_chip_add", out_shape=jax.ShapeDtypeStruct((hr, cols), f32),
        grid_spec=pltpu.PrefetchScalarGridSpec(
            num_scalar_prefetch=1, grid=(1,),
            in_specs=[pl.BlockSpec((1, hr, cols), lambda i, me_ref: (me_ref[0], 0, 0)),
                      pl.BlockSpec((3, hr, cols), lambda i, me_ref: (0, 0, 0))],
            out_specs=pl.BlockSpec((hr, cols), lambda i, me_ref: (0, 0))),
        compiler_params=_cparams(),
    )(me_idx, part, land)


def _rs_pair_share(reds):
    n = len(reds)

    def body(*refs):
        r_refs, out_refs, (send_sems, recv_sems) = refs[:n], refs[n:2 * n], refs[2 * n:]
        x, y, c = _mesh_pos()

        def copy(a, half):
            hr = reds[a].shape[0]
            return pltpu.make_async_remote_copy(src_ref=r_refs[a], dst_ref=out_refs[a].at[pl.ds(half * hr, hr), :],
                                                send_sem=send_sems.at[a], recv_sem=recv_sems.at[a], device_id=(x, y, 1 - c),
                                                device_id_type=MESH)

        sends = [copy(a, c) for a in range(n)]
        for cp in sends:
            cp.start()
        for a in range(n):
            copy(a, 1 - c).wait_recv()
        for cp in sends:
            cp.wait_send()

    both = pl.pallas_call(
        body, name="rs_pair_share", out_shape=[jax.ShapeDtypeStruct((2 * r.shape[0], r.shape[1]), r.dtype) for r in reds],
        in_specs=[_hbm()] * n, out_specs=[_hbm()] * n,
        scratch_shapes=[pltpu.SemaphoreType.DMA((n,)), pltpu.SemaphoreType.DMA((n,))],
    )(*reds)
    c = lax.axis_index("c")
    return [lax.dynamic_update_slice(o, r, (c * r.shape[0], 0)) for o, r in zip(both, reds)]


def _rs_begin(gs):
    c_idx = jnp.reshape(lax.axis_index("c"), (1,)).astype(jnp.int32)
    lands = _rs_pair_exchange(gs)
    return [_rs_pair_add(g, land, c_idx) for g, land in zip(gs, lands)]


def _rs_finish(parts, lands):
    me_idx = jnp.reshape(2 * lax.axis_index("x") + lax.axis_index("y"), (1,)).astype(jnp.int32)
    reds = [_rs_chip_add(part, land, me_idx) for part, land in zip(parts, lands)]
    return _rs_pair_share(reds)


def _pack(parts, row_multiple):
    flat = jnp.concatenate([a.reshape(-1) for a in parts])
    n = flat.shape[0]
    rows = -(-n // COLS)
    rows = -(-rows // row_multiple) * row_multiple
    return jnp.pad(flat, (0, rows * COLS - n)).reshape(rows, COLS)


def _unpack(flat, shapes):
    out, off = [], 0
    for shp in shapes:
        n = math.prod(shp)
        out.append(flat[off:off + n].reshape(shp))
        off += n
    return out


def _adamw(w, g, m, v, name):
    shape = w.shape
    cols = shape[-1]
    rows = math.prod(shape[:-1]) if len(shape) > 1 else 1
    tr = rows
    if rows > 512:
        tr = next(t for t in (512, 256, 128, 64, 32, 16, 8) if rows % t == 0)
    two_d = lambda a: a.reshape(rows, cols)

    def body(w_ref, g_ref, m_ref, v_ref, d_ref, nm_ref, nv_ref):
        gg = g_ref[...]
        nm = ADAM_B1 * m_ref[...] + (1.0 - ADAM_B1) * gg
        nv = ADAM_B2 * v_ref[...] + (1.0 - ADAM_B2) * (gg * gg)
        m_hat = nm / (1.0 - ADAM_B1 ** ADAM_STEP)
        v_hat = nv / (1.0 - ADAM_B2 ** ADAM_STEP)
        d_ref[...] = -ADAM_LR * (m_hat / (jnp.sqrt(v_hat) + ADAM_EPS) + ADAM_WD * w_ref[...])
        nm_ref[...] = nm
        nv_ref[...] = nv

    spec = pl.BlockSpec((tr, cols), lambda i: (i, 0))
    outs = pl.pallas_call(
        body, name="adamw_" + name, grid=(rows // tr,),
        out_shape=tuple(jax.ShapeDtypeStruct((rows, cols), f32) for _ in range(3)),
        in_specs=[spec] * 4, out_specs=(spec,) * 3, compiler_params=_cparams(),
    )(two_d(w), two_d(g), two_d(m), two_d(v))
    return tuple(o.reshape(shape) for o in outs)


BIG_SHARDED = ["ffn_w_up", "ffn_w_down", "w_out", "ple_w_gate", "ple_w_proj", "w_in"]
SMALL_REDUCED = REPLICATED + F32_SHARDED


def _chip_major(a, axis):
    n = a.shape[axis] // N_CHIPS
    return jnp.moveaxis(a.reshape(a.shape[:axis] + (N_CHIPS, n) + a.shape[axis + 1:]), axis, 0)


def _train_step(x, p, loss_target, w, m, v):
    xs, ps, target = x[0], p[:, 0], loss_target[0]
    rows2d = lambda a: a.reshape(-1, a.shape[-1])
    conv_rows = 16
    pad_rows = lambda a: jnp.pad(a, ((0, conv_rows - a.shape[0]), (0, 0)))
    shards = [[w[n][l].astype(bf16) for n in MATMUL_SHARDED] for l in range(DEPTH)]
    first = _all_gather_chips(shards[0] + [pad_rows(rows2d(w[n])) for n in F32_SHARDED], "gather_weights")
    conv4 = dict(zip(F32_SHARDED, first[len(MATMUL_SHARDED):]))

    def col_cut(blk):
        return jnp.moveaxis(blk, 0, 1).reshape(blk.shape[1], N_CHIPS * blk.shape[2])

    def layer_weights(l, gathered):
        g = dict(zip(MATMUL_SHARDED, gathered))
        lw = {n: w[n][l] for n in REPLICATED if n != "final_norm_g"}
        lw["pool_w"] = lw["pool_w"].astype(bf16)
        lw.update(
            w_in4=g["w_in"], ple_w_proj=col_cut(g["ple_w_proj"]), w_out=g["w_out"].reshape(D_MODEL, D_MODEL),
            ple_w_gate=g["ple_w_gate"].reshape(D_MODEL, D_MODEL), wdn_h=g["ffn_w_down"].reshape(2, FF_HALF, D_MODEL),
            wup_h=jnp.stack([jnp.concatenate([g["ffn_w_up"][j], g["ffn_w_up"][2 + j]], axis=1) for j in range(2)]),
            ssd_conv_w=col_cut(conv4["ssd_conv_w"][:, l * SSD_CONV:(l + 1) * SSD_CONV, :]),
            ffn_conv_w=col_cut(conv4["ffn_conv_w"][:, l * FFN_CONV:(l + 1) * FFN_CONV, :]))
        return _prep_layer(lw)

    preps, saved = [], []
    h, gathered = xs, first[:len(MATMUL_SHARDED)]
    for l in range(DEPTH):
        preps.append(layer_weights(l, gathered))
        h, keep, gathered = _layer_fwd(h, (ps, l), preps[l], gather=shards[l + 1] if l + 1 < DEPTH else ())
        saved.append(keep)
    grad_x, d_gf, loss_part = _loss_head(h, target, w["final_norm_g"].reshape(1, D_MODEL))

    def chip_major_grads(g):
        return [g["ffn_w_up4"], g["wdn_h"].reshape(N_CHIPS, D_FF // N_CHIPS, D_MODEL), _chip_major(g["w_out"], 0),
                _chip_major(g["ple_w_gate"], 0), _chip_major(g["ple_w_proj"], 1), g["w_in4"]]

    grads, reduced, pending = [None] * DEPTH, [None] * DEPTH, ()
    for l in reversed(range(DEPTH)):
        grad_x, grads[l], lands = _layer_bwd(grad_x, saved[l], (ps, l), preps[l], exchange=pending)
        if pending:
            reduced[l + 1] = _rs_finish(pending, lands)
        big = chip_major_grads(grads[l])
        if l == 0:
            small = _pack([d_gf.reshape(D_MODEL) if n == "final_norm_g" else jnp.stack([grads[k][n] for k in range(DEPTH)])
                           for n in SMALL_REDUCED], 32 * N_CHIPS)
            big.append(small.reshape(N_CHIPS, -1, COLS))
        pending = _rs_begin(big)
    reduced[0] = _rs_finish(pending, _rs_chip_exchange(pending))
    grad = {n: jnp.stack([reduced[l][i] for l in range(DEPTH)]) for i, n in enumerate(BIG_SHARDED)}
    small_all = _all_gather_chips([reduced[0][-1]], "gather_small_grads")[0].reshape(-1)
    small_shapes = [w[n].shape for n in REPLICATED] + [(DEPTH, SSD_CONV, XBC), (DEPTH, FFN_CONV, 2 * D_FF)]
    grad.update(zip(SMALL_REDUCED, _unpack(small_all, small_shapes)))
    me = 2 * lax.axis_index("x") + lax.axis_index("y")
    for n in F32_SHARDED:
        grad[n] = lax.dynamic_slice_in_dim(grad[n], me * w[n].shape[2], w[n].shape[2], axis=2)

    loss = lax.psum(loss_part[0, 0], ("x", "y", "c"))
    delta, new_m, new_v = {}, {}, {}
    for n in WEIGHT_NAMES:
        delta[n], new_m[n], new_v[n] = _adamw(w[n], grad[n], m[n], v[n], n)
    return (loss, grad_x[None], *[grad[n] for n in WEIGHT_NAMES], *[delta[n] for n in WEIGHT_NAMES],
            *[new_m[n] for n in WEIGHT_NAMES], *[new_v[n] for n in WEIGHT_NAMES])


def kernel(x, p, mix_norm_g, w_in, ssd_conv_w, ssd_conv_b, ssd_dt_bias, ssd_a_log, ssd_d, ssd_norm_g, pool_w, pool_scale, w_out, ffn_norm_g, ffn_w_up, ffn_conv_w, ffn_conv_b, ffn_w_down, ple_norm_g, ple_w_gate, ple_w_proj, final_norm_g, loss_target, m_mix_norm_g, m_w_in, m_ssd_conv_w, m_ssd_conv_b, m_ssd_dt_bias, m_ssd_a_log, m_ssd_d, m_ssd_norm_g, m_pool_w, m_pool_scale, m_w_out, m_ffn_norm_g, m_ffn_w_up, m_ffn_conv_w, m_ffn_conv_b, m_ffn_w_down, m_ple_norm_g, m_ple_w_gate, m_ple_w_proj, m_final_norm_g, v_mix_norm_g, v_w_in, v_ssd_conv_w, v_ssd_conv_b, v_ssd_dt_bias, v_ssd_a_log, v_ssd_d, v_ssd_norm_g, v_pool_w, v_pool_scale, v_w_out, v_ffn_norm_g, v_ffn_w_up, v_ffn_conv_w, v_ffn_conv_b, v_ffn_w_down, v_ple_norm_g, v_ple_w_gate, v_ple_w_proj, v_final_norm_g):
    w = dict(mix_norm_g=mix_norm_g, w_in=w_in, ssd_conv_w=ssd_conv_w, ssd_conv_b=ssd_conv_b, ssd_dt_bias=ssd_dt_bias, ssd_a_log=ssd_a_log, ssd_d=ssd_d, ssd_norm_g=ssd_norm_g, pool_w=pool_w, pool_scale=pool_scale, w_out=w_out, ffn_norm_g=ffn_norm_g, ffn_w_up=ffn_w_up, ffn_conv_w=ffn_conv_w, ffn_conv_b=ffn_conv_b, ffn_w_down=ffn_w_down, ple_norm_g=ple_norm_g, ple_w_gate=ple_w_gate, ple_w_proj=ple_w_proj, final_norm_g=final_norm_g)
    m = dict(mix_norm_g=m_mix_norm_g, w_in=m_w_in, ssd_conv_w=m_ssd_conv_w, ssd_conv_b=m_ssd_conv_b, ssd_dt_bias=m_ssd_dt_bias, ssd_a_log=m_ssd_a_log, ssd_d=m_ssd_d, ssd_norm_g=m_ssd_norm_g, pool_w=m_pool_w, pool_scale=m_pool_scale, w_out=m_w_out, ffn_norm_g=m_ffn_norm_g, ffn_w_up=m_ffn_w_up, ffn_conv_w=m_ffn_conv_w, ffn_conv_b=m_ffn_conv_b, ffn_w_down=m_ffn_w_down, ple_norm_g=m_ple_norm_g, ple_w_gate=m_ple_w_gate, ple_w_proj=m_ple_w_proj, final_norm_g=m_final_norm_g)
    v = dict(mix_norm_g=v_mix_norm_g, w_in=v_w_in, ssd_conv_w=v_ssd_conv_w, ssd_conv_b=v_ssd_conv_b, ssd_dt_bias=v_ssd_dt_bias, ssd_a_log=v_ssd_a_log, ssd_d=v_ssd_d, ssd_norm_g=v_ssd_norm_g, pool_w=v_pool_w, pool_scale=v_pool_scale, w_out=v_w_out, ffn_norm_g=v_ffn_norm_g, ffn_w_up=v_ffn_w_up, ffn_conv_w=v_ffn_conv_w, ffn_conv_b=v_ffn_conv_b, ffn_w_down=v_ffn_w_down, ple_norm_g=v_ple_norm_g, ple_w_gate=v_ple_w_gate, ple_w_proj=v_ple_w_proj, final_norm_g=v_final_norm_g)
    return _train_step(x, p, loss_target, w, m, v)
```

```python
import functools
import math

import jax
import jax.numpy as jnp
from jax import lax
from jax.experimental import pallas as pl
from jax.experimental.pallas import tpu as pltpu

f32, bf16 = jnp.float32, jnp.bfloat16
HI = lax.Precision.HIGHEST

D_MODEL = 1024
D_PLE = 256
DEPTH = 4
SSD_W = 512
HEADS = 8
HEAD_DIM = 64
NSTATE = 128
CHUNK = 128
SSD_CONV = 4
XBC = 1024
POOL_W = 512
POOL_G = 128
WINDOWS = (2, 4, 8, 16)
D_FF = 2816
FF_HALF = D_FF // 2
FFN_CONV = 3
D_IN = 2056
EPS = 1e-6
ADAM_LR, ADAM_B1, ADAM_B2, ADAM_EPS, ADAM_WD, ADAM_STEP = 0.001, 0.9, 0.999, 1e-08, 0.01, 10

LANES = 128
NPROJ = 2048 + LANES
HALO = 16
FHALO = 8
VMEM_LIMIT = 58 * 1024 * 1024
ROW_TILE = 256


def _dot(a, b):
    return jnp.dot(a, b, preferred_element_type=f32)


def _dot_nt(a, b):
    return lax.dot_general(a, b, (((1,), (1,)), ((), ())), preferred_element_type=f32)


def _dot_tn(a, b):
    return lax.dot_general(a, b, (((0,), (0,)), ((), ())), preferred_element_type=f32)


def _dot_hi(a, b):
    return jnp.dot(a, b, precision=HI, preferred_element_type=f32)


def _dot_nt_hi(a, b):
    return lax.dot_general(a, b, (((1,), (1,)), ((), ())), precision=HI, preferred_element_type=f32)


def _rms_fwd(x, g):
    r = lax.rsqrt(jnp.mean(x * x, axis=-1, keepdims=True) + EPS)
    xhat = x * r
    return xhat * g, xhat, r


def _rms_bwd(dy, xhat, r, g):
    dxhat = dy * g
    dx = r * (dxhat - xhat * jnp.mean(dxhat * xhat, axis=-1, keepdims=True))
    return dx, jnp.sum(dy * xhat, axis=0, keepdims=True)


def _sigmoid(x):
    return 1.0 / (1.0 + jnp.exp(-x))


_GELU_C = math.sqrt(2.0 / math.pi)


def _gelu_and_grad(x):
    x2 = x * x
    t = jnp.tanh(x * (_GELU_C + (_GELU_C * 0.044715) * x2))
    u = 0.5 * t + 0.5
    g = x * u
    dg = u + g * (1.0 - t) * (_GELU_C + (3.0 * 0.044715 * _GELU_C) * x2)
    return g, dg


def _gelu(x):
    return x * (0.5 * jnp.tanh(x * (_GELU_C + (_GELU_C * 0.044715) * (x * x))) + 0.5)


def _softplus(x):
    return jnp.maximum(x, 0.0) + jnp.log(1.0 + jnp.exp(-jnp.abs(x)))


def _cparams(sem=("arbitrary",)):
    return pltpu.CompilerParams(dimension_semantics=sem, vmem_limit_bytes=VMEM_LIMIT)


def _const_spec(shape):
    nd = len(shape)
    return pl.BlockSpec(shape, lambda *_: (0,) * nd, pipeline_mode=pl.Buffered(1))


WIDE_ROW_TILE = 512


def _row_tile(s, t=ROW_TILE):
    return min(t, s)


def _mix_in_fwd(h, g1, w_r):
    s = h.shape[0]
    t = _row_tile(s, WIDE_ROW_TILE)

    def body(h_ref, g_ref, w_ref, zxu_ref, dtr_ref):
        hn, _, _ = _rms_fwd(h_ref[...], g_ref[...])
        proj = _dot(hn.astype(bf16), w_ref[...])
        zxu_ref[...] = proj[:, :2048].astype(bf16)
        dtr_ref[...] = proj[:, 2048:]

    return pl.pallas_call(
        body, name="mix_in_fwd", grid=(s // t,),
        out_shape=(jax.ShapeDtypeStruct((s, 2048), bf16), jax.ShapeDtypeStruct((s, LANES), f32)),
        in_specs=[pl.BlockSpec((t, D_MODEL), lambda i: (i, 0)), _const_spec((1, D_MODEL)), _const_spec((D_MODEL, NPROJ))],
        out_specs=(pl.BlockSpec((t, 2048), lambda i: (i, 0)), pl.BlockSpec((t, LANES), lambda i: (i, 0))),
        compiler_params=_cparams(),
    )(h, g1, w_r)


def _mix_in_bwd(d_zxu, d_dtr, h, dh1, g1, w_r):
    s = h.shape[0]
    t = _row_tile(s, WIDE_ROW_TILE)
    n = s // t

    def body(dz_ref, dd_ref, h_ref, dh1_ref, g_ref, w_ref, dh_ref, dw_ref, dg_ref, acc):
        i = pl.program_id(0)

        @pl.when(i == 0)
        def _():
            acc[...] = jnp.zeros_like(acc)
            dg_ref[...] = jnp.zeros_like(dg_ref)

        g = g_ref[...]
        hn, xhat, r = _rms_fwd(h_ref[...], g)
        dproj = jnp.concatenate([dz_ref[...], dd_ref[...].astype(bf16)], axis=1)
        d_hn = _dot_nt(dproj, w_ref[...])
        acc[...] += _dot_tn(hn.astype(bf16), dproj)
        dx, dg = _rms_bwd(d_hn, xhat, r, g)
        dg_ref[...] += dg
        dh_ref[...] = dh1_ref[...] + dx

        @pl.when(i == n - 1)
        def _():
            pltpu.sync_copy(acc, dw_ref)

    return pl.pallas_call(
        body, name="mix_in_bwd", grid=(n,),
        out_shape=(jax.ShapeDtypeStruct((s, D_MODEL), f32), jax.ShapeDtypeStruct((D_MODEL, NPROJ), f32),
                   jax.ShapeDtypeStruct((1, D_MODEL), f32)),
        in_specs=[pl.BlockSpec((t, 2048), lambda i: (i, 0)), pl.BlockSpec((t, LANES), lambda i: (i, 0)),
                  pl.BlockSpec((t, D_MODEL), lambda i: (i, 0)), pl.BlockSpec((t, D_MODEL), lambda i: (i, 0)),
                  _const_spec((1, D_MODEL)), _const_spec((D_MODEL, NPROJ))],
        out_specs=(pl.BlockSpec((t, D_MODEL), lambda i: (i, 0)), pl.BlockSpec(memory_space=pl.ANY),
                   pl.BlockSpec((1, D_MODEL), lambda i: (0, 0))),
        scratch_shapes=[pltpu.VMEM((D_MODEL, NPROJ), f32)],
        compiler_params=_cparams(),
    )(d_zxu, d_dtr, h, dh1, g1, w_r)


def _iota2(shape, dim):
    return lax.broadcasted_iota(jnp.int32, shape, dim)


def _lane_bcast(a, h):
    return jnp.broadcast_to(a[:, h:h + 1], (a.shape[0], LANES))


def _to_columns(cols):
    lane = _iota2((cols[0].shape[0], LANES), 1)
    out = jnp.where(lane == 0, cols[0], 0.0)
    for h in range(1, len(cols)):
        out = out + jnp.where(lane == h, cols[h], 0.0)
    return out


def _ssd_pre(zx, dtr, xext, cw, cb, dtb, alog):
    c = cb + cw[0:1] * xext[pl.ds(HALO - 3, CHUNK), :]
    for k in range(1, SSD_CONV):
        c = c + cw[k:k + 1] * xext[pl.ds(HALO - 3 + k, CHUNK), :]
    sig_c = _sigmoid(c)
    xc = c * sig_c
    dt = _softplus(dtr + dtb)
    a_neg = -jnp.exp(alog)
    a = dt * a_neg
    lo = _iota2((CHUNK, LANES), 1) < HEAD_DIM
    dt_w = jnp.concatenate([jnp.where(lo, _lane_bcast(dt, 2 * pp), _lane_bcast(dt, 2 * pp + 1)) for pp in range(4)], axis=1)
    xs = xc[:, :SSD_W]
    xd = xs * dt_w
    tril = (_iota2((CHUNK, CHUNK), 0) >= _iota2((CHUNK, CHUNK), 1))
    acs = _dot_hi(tril.astype(f32), a)
    acs_b = [_lane_bcast(acs, h) for h in range(HEADS)]
    groups = []
    for g in range(2):
        b16 = xc[:, SSD_W + g * NSTATE:SSD_W + (g + 1) * NSTATE].astype(bf16)
        c16 = xc[:, SSD_W + 2 * NSTATE + g * NSTATE:SSD_W + 2 * NSTATE + (g + 1) * NSTATE].astype(bf16)
        groups.append((b16, c16, _dot_nt(c16, b16)))
    return dict(c=c, sig_c=sig_c, xc=xc, xs=xs, dt=dt, a_neg=a_neg, a=a, dt_w=dt_w, xd=xd, acs_b=acs_b, tril=tril, lo=lo,
                groups=groups)


def _pair_fwd(q, pp, s_in):
    lo = q["lo"]
    b16, c16, gmat = q["groups"][pp // 2]
    xp = q["xd"][:, pp * LANES:(pp + 1) * LANES]
    xp16 = xp.astype(bf16)
    ab0 = q["acs_b"][2 * pp]
    ab1 = q["acs_b"][2 * pp + 1]
    ls, ms, ys = [], [], []
    for ab in (ab0, ab1):
        lmat = jnp.exp(jnp.where(q["tril"], ab - ab.T, -jnp.inf))
        mmat = gmat * lmat
        ls.append(lmat)
        ms.append(mmat)
        ys.append(_dot(mmat.astype(bf16), xp16))
    y_diag = jnp.where(lo, ys[0], ys[1])
    ab_pair = jnp.where(lo, ab0, ab1)
    e_pair = jnp.exp(ab_pair)
    s16 = s_in.astype(bf16)
    y_off = _dot(c16, s16) * e_pair
    alast = ab_pair[CHUNK - 1:CHUNK, :]
    dec_pair = jnp.exp(alast - ab_pair)
    xdec = xp * dec_pair
    st = _dot_tn(b16, xdec.astype(bf16))
    cd_pair = jnp.exp(alast)
    s_out = s_in * cd_pair + st
    return dict(b16=b16, c16=c16, gmat=gmat, xp=xp, xp16=xp16, ls=ls, ms=ms, y=y_diag + y_off, y_off=y_off,
                e_pair=e_pair, dec_pair=dec_pair, xdec=xdec, cd_pair=cd_pair, s_out=s_out, s16=s16, lo=lo)


def _gate_norm_fwd(y_pre, z, ng):
    sz = _sigmoid(z)
    yg = y_pre * (z * sz)
    outs, stats = [], []
    half = SSD_W // 2
    for gi in range(2):
        o, xhat, r = _rms_fwd(yg[:, gi * half:(gi + 1) * half], ng[:, gi * half:(gi + 1) * half])
        outs.append(o)
        stats.append((xhat, r))
    return jnp.concatenate(outs, axis=1), sz, stats


def _pool_fwd(uext, u, row0, pw_ref, ps):
    pos = (row0 + _iota2((CHUNK, 1), 0) + 1).astype(f32)
    pooled, mixed, invs = [], [], []
    for gi, w in enumerate(WINDOWS):
        sl = slice(gi * POOL_G, (gi + 1) * POOL_G)
        acc = uext[pl.ds(HALO, CHUNK), sl]
        for j in range(1, w):
            acc = acc + uext[pl.ds(HALO - j, CHUNK), sl]
        den = jnp.minimum(pos, float(w))
        pg = acc / den - u[:, sl]
        pooled.append(pg)
        invs.append(den)
        mixed.append(_dot(pg.astype(bf16), pw_ref[gi]))
    mixed = jnp.concatenate(mixed, axis=1)
    return mixed * ps, pooled, mixed, invs


def _ssd_specs(s):
    nc = s // CHUNK
    hb = CHUNK // HALO
    return nc, hb


def _ssd_param_specs():
    return [_const_spec((SSD_CONV, XBC)), _const_spec((1, XBC)), _const_spec((1, LANES)), _const_spec((1, LANES)),
            _const_spec((1, SSD_W)), _const_spec((1, SSD_W)), _const_spec((4, POOL_G, POOL_G)), _const_spec((1, POOL_W))]


def _ssd_pool_fwd(zxu, dtr, prm):
    s = zxu.shape[0]
    nc, hb = _ssd_specs(s)

    def body(zx_ref, halo_ref, dtr_ref, cw_ref, cb_ref, dtb_ref, alog_ref, dsk_ref, ng_ref, pw_ref, ps_ref,
             ymix_ref, st_ref, state, xext, uext):
        i = pl.program_id(0)

        @pl.when(i == 0)
        def _():
            state[...] = jnp.zeros_like(state)

        zx = zx_ref[...].astype(f32)
        halo = jnp.where(i > 0, halo_ref[...].astype(f32), 0.0)
        xext[0:HALO, :] = halo[:, SSD_W:SSD_W + XBC]
        xext[HALO:, :] = zx[:, SSD_W:SSD_W + XBC]
        uext[0:HALO, :] = halo[:, SSD_W + XBC:]
        uext[HALO:, :] = zx[:, SSD_W + XBC:]
        q = _ssd_pre(zx, dtr_ref[...], xext, cw_ref[...], cb_ref[...], dtb_ref[...], alog_ref[...])
        ys = []
        for pp in range(4):
            s_in = state[pp]
            st_ref[0, pp] = s_in
            r = _pair_fwd(q, pp, s_in)
            state[pp] = r["s_out"]
            ys.append(r["y"])
        y_pre = jnp.concatenate(ys, axis=1) + q["xs"] * dsk_ref[...]
        y_ssd, _, _ = _gate_norm_fwd(y_pre, zx[:, :SSD_W], ng_ref[...])
        y_pool, _, _, _ = _pool_fwd(uext, zx[:, SSD_W + XBC:], i * CHUNK, pw_ref, ps_ref[...])
        ymix_ref[:, :SSD_W] = y_ssd.astype(bf16)
        ymix_ref[:, SSD_W:] = y_pool.astype(bf16)

    return pl.pallas_call(
        body, name="ssd_pool_fwd", grid=(nc,),
        out_shape=(jax.ShapeDtypeStruct((s, D_MODEL), bf16), jax.ShapeDtypeStruct((nc, 4, NSTATE, LANES), f32)),
        in_specs=[pl.BlockSpec((CHUNK, 2048), lambda i: (i, 0)),
                  pl.BlockSpec((HALO, 2048), lambda i: (jnp.maximum(i * hb - 1, 0), 0)),
                  pl.BlockSpec((CHUNK, LANES), lambda i: (i, 0))] + _ssd_param_specs(),
        out_specs=(pl.BlockSpec((CHUNK, D_MODEL), lambda i: (i, 0)),
                   pl.BlockSpec((1, 4, NSTATE, LANES), lambda i: (i, 0, 0, 0))),
        scratch_shapes=[pltpu.VMEM((4, NSTATE, LANES), f32), pltpu.VMEM((HALO + CHUNK, XBC), f32),
                        pltpu.VMEM((HALO + CHUNK, POOL_W), f32)],
        compiler_params=_cparams(),
    )(zxu, zxu, dtr, *prm)


def _ssd_pool_bwd(d_ymix, zxu, dtr, states, prm):
    s = zxu.shape[0]
    nc, hb = _ssd_specs(s)
    rev = lambda i: nc - 1 - i

    def body(dy_ref, zx_ref, halo_ref, dtr_ref, st_ref, cw_ref, cb_ref, dtb_ref, alog_ref, dsk_ref, ng_ref, pw_ref, ps_ref,
             dzx_ref, ddtr_ref, dcw_ref, dcb_ref, ddtb_ref, dalog_ref, ddsk_ref, dng_ref, dpw_ref, dps_ref,
             dstate, xext, uext, dxext, duext, cx, cu):
        i = pl.program_id(0)
        ci = nc - 1 - i

        @pl.when(i == 0)
        def _():
            dstate[...] = jnp.zeros_like(dstate)
            cx[...] = jnp.zeros_like(cx)
            cu[...] = jnp.zeros_like(cu)
            for r in (dcw_ref, dcb_ref, ddtb_ref, dalog_ref, ddsk_ref, dng_ref, dpw_ref, dps_ref):
                r[...] = jnp.zeros_like(r)

        zx = zx_ref[...].astype(f32)
        halo = jnp.where(ci > 0, halo_ref[...].astype(f32), 0.0)
        xext[0:HALO, :] = halo[:, SSD_W:SSD_W + XBC]
        xext[HALO:, :] = zx[:, SSD_W:SSD_W + XBC]
        uext[0:HALO, :] = halo[:, SSD_W + XBC:]
        uext[HALO:, :] = zx[:, SSD_W + XBC:]
        cw = cw_ref[...]
        q = _ssd_pre(zx, dtr_ref[...], xext, cw, cb_ref[...], dtb_ref[...], alog_ref[...])
        z = zx[:, :SSD_W]
        dy = dy_ref[...].astype(f32)
        d_yssd, d_ypool = dy[:, :SSD_W], dy[:, SSD_W:]

        pairs = [_pair_fwd(q, pp, st_ref[0, pp]) for pp in range(4)]
        dsk = dsk_ref[...]
        ng = ng_ref[...]
        y_pre = jnp.concatenate([r["y"] for r in pairs], axis=1) + q["xs"] * dsk
        _, sz, stats = _gate_norm_fwd(y_pre, z, ng)

        half = SSD_W // 2
        d_yg, d_ng = [], []
        for gi in range(2):
            xhat, r = stats[gi]
            dx, dg = _rms_bwd(d_yssd[:, gi * half:(gi + 1) * half], xhat, r, ng[:, gi * half:(gi + 1) * half])
            d_yg.append(dx)
            d_ng.append(dg)
        d_yg = jnp.concatenate(d_yg, axis=1)
        dng_ref[...] += jnp.concatenate(d_ng, axis=1)
        silu_z = z * sz
        d_ypre = d_yg * silu_z
        d_z = d_yg * y_pre * (sz * (1.0 + z * (1.0 - sz)))
        ddsk_ref[...] += jnp.sum(d_ypre * q["xs"], axis=0, keepdims=True)

        d_xd, acs_cols, dt_cols = [], [], []
        d_b = [None, None]
        d_c = [None, None]
        d_g = [None, None]
        last_row = _iota2((CHUNK, LANES), 0) == CHUNK - 1
        for pp in range(4):
            g = pp // 2
            r = pairs[pp]
            lo = r["lo"]
            dyp = d_ypre[:, pp * LANES:(pp + 1) * LANES]
            dyp16 = dyp.astype(bf16)
            ds_out = dstate[pp]
            ds16 = ds_out.astype(bf16)
            dye16 = (dyp * r["e_pair"]).astype(bf16)
            dstate[pp] = r["cd_pair"] * ds_out + _dot_tn(r["c16"], dye16)
            dc = _dot_nt(dye16, r["s16"])
            dxdec = _dot(r["b16"], ds16)
            db = _dot_nt(r["xdec"].astype(bf16), ds16)
            dxp = dxdec * r["dec_pair"]
            t2 = dxdec * r["xdec"]
            tail = jnp.sum(t2, axis=0, keepdims=True) + jnp.sum(ds_out * st_ref[0, pp] * r["cd_pair"], axis=0, keepdims=True)
            rp = dyp * r["y_off"] - t2 + jnp.where(last_row, tail, 0.0)
            dxs = []
            for hh in range(2):
                msk = lo if hh == 0 else jnp.logical_not(lo)
                m16 = r["ms"][hh].astype(bf16)
                dxs.append(_dot_tn(m16, dyp16))
                dm = _dot_nt(jnp.where(msk, dyp, 0.0).astype(bf16), r["xp16"])
                wmat = dm * r["ms"][hh]
                acs_cols.append(jnp.sum(wmat - wmat.T + jnp.where(msk, rp, 0.0), axis=1, keepdims=True))
                dgh = dm * r["ls"][hh]
                d_g[g] = dgh if d_g[g] is None else d_g[g] + dgh
            dxp = dxp + jnp.where(lo, dxs[0], dxs[1])
            d_xd.append(dxp)
            xprod = dxp * q["xs"][:, pp * LANES:(pp + 1) * LANES]
            dt_cols.append(jnp.sum(jnp.where(lo, xprod, 0.0), axis=1, keepdims=True))
            dt_cols.append(jnp.sum(jnp.where(lo, 0.0, xprod), axis=1, keepdims=True))
            d_b[g] = db if d_b[g] is None else d_b[g] + db
            d_c[g] = dc if d_c[g] is None else d_c[g] + dc
        for g in range(2):
            dg16 = d_g[g].astype(bf16)
            d_c[g] = d_c[g] + _dot(dg16, pairs[2 * g]["b16"])
            d_b[g] = d_b[g] + _dot_tn(dg16, pairs[2 * g]["c16"])
        d_xd = jnp.concatenate(d_xd, axis=1)
        triu = (_iota2((CHUNK, CHUNK), 0) <= _iota2((CHUNK, CHUNK), 1)).astype(f32)
        d_a = _dot_hi(triu, _to_columns(acs_cols))
        d_dt = d_a * q["a_neg"] + _to_columns(dt_cols)
        dalog_ref[...] += jnp.sum(d_a * q["dt"], axis=0, keepdims=True) * q["a_neg"]
        d_dtr = d_dt * _sigmoid(dtr_ref[...] + dtb_ref[...])
        ddtr_ref[...] = d_dtr
        ddtb_ref[...] += jnp.sum(d_dtr, axis=0, keepdims=True)
        d_xs = d_ypre * dsk + d_xd * q["dt_w"]

        d_xc = jnp.concatenate([d_xs, d_b[0], d_b[1], d_c[0], d_c[1]], axis=1)
        sc = q["sig_c"]
        d_conv = d_xc * (sc * (1.0 + q["c"] * (1.0 - sc)))
        dcb_ref[...] += jnp.sum(d_conv, axis=0, keepdims=True)
        dxext[...] = jnp.zeros_like(dxext)
        for k in range(SSD_CONV):
            dcw_ref[k:k + 1, :] += jnp.sum(d_conv * xext[pl.ds(HALO - 3 + k, CHUNK), :], axis=0, keepdims=True)
            dxext[pl.ds(HALO - 3 + k, CHUNK), :] += cw[k:k + 1] * d_conv
        dxext[pl.ds(CHUNK, HALO), :] += cx[...]
        cx[...] = dxext[0:HALO, :]

        ps = ps_ref[...]
        u = zx[:, SSD_W + XBC:]
        _, pooled, mixed, dens = _pool_fwd(uext, u, ci * CHUNK, pw_ref, ps)
        dps_ref[...] += jnp.sum(d_ypool * mixed, axis=0, keepdims=True)
        d_mixed = d_ypool * ps
        duext[...] = jnp.zeros_like(duext)
        for gi, w in enumerate(WINDOWS):
            sl = slice(gi * POOL_G, (gi + 1) * POOL_G)
            dm16 = d_mixed[:, sl].astype(bf16)
            dpw_ref[gi] += _dot_tn(pooled[gi].astype(bf16), dm16)
            d_pg = _dot_nt(dm16, pw_ref[gi])
            d_mean = d_pg / dens[gi]
            duext[pl.ds(HALO, CHUNK), sl] += d_mean - d_pg
            for j in range(1, w):
                duext[pl.ds(HALO - j, CHUNK), sl] += d_mean
        duext[pl.ds(CHUNK, HALO), :] += cu[...]
        cu[...] = duext[0:HALO, :]

        dzx_ref[:, :SSD_W] = d_z.astype(bf16)
        dzx_ref[:, SSD_W:SSD_W + XBC] = dxext[HALO:, :].astype(bf16)
        dzx_ref[:, SSD_W + XBC:] = duext[HALO:, :].astype(bf16)

    small = lambda shape: pl.BlockSpec(shape, lambda i: (0,) * len(shape))
    small_shapes = [(SSD_CONV, XBC), (1, XBC), (1, LANES), (1, LANES), (1, SSD_W), (1, SSD_W), (4, POOL_G, POOL_G), (1, POOL_W)]
    return pl.pallas_call(
        body, name="ssd_pool_bwd", grid=(nc,),
        out_shape=(jax.ShapeDtypeStruct((s, 2048), bf16), jax.ShapeDtypeStruct((s, LANES), f32))
        + tuple(jax.ShapeDtypeStruct(sh, f32) for sh in small_shapes),
        in_specs=[pl.BlockSpec((CHUNK, D_MODEL), lambda i: (rev(i), 0)),
                  pl.BlockSpec((CHUNK, 2048), lambda i: (rev(i), 0)),
                  pl.BlockSpec((HALO, 2048), lambda i: (jnp.maximum(rev(i) * hb - 1, 0), 0)),
                  pl.BlockSpec((CHUNK, LANES), lambda i: (rev(i), 0)),
                  pl.BlockSpec((1, 4, NSTATE, LANES), lambda i: (rev(i), 0, 0, 0))] + _ssd_param_specs(),
        out_specs=(pl.BlockSpec((CHUNK, 2048), lambda i: (rev(i), 0)), pl.BlockSpec((CHUNK, LANES), lambda i: (rev(i), 0)))
        + tuple(small(sh) for sh in small_shapes),
        scratch_shapes=[pltpu.VMEM((4, NSTATE, LANES), f32), pltpu.VMEM((HALO + CHUNK, XBC), f32),
                        pltpu.VMEM((HALO + CHUNK, POOL_W), f32), pltpu.VMEM((HALO + CHUNK, XBC), f32),
                        pltpu.VMEM((HALO + CHUNK, POOL_W), f32), pltpu.VMEM((HALO, XBC), f32), pltpu.VMEM((HALO, POOL_W), f32)],
        compiler_params=_cparams(),
    )(d_ymix, zxu, zxu, dtr, states, *prm)


FFN_BWD_TILE = 256


def _prev_halo_spec(t, width):
    hb = t // HALO
    return pl.BlockSpec((HALO, width), lambda i: (jnp.maximum(i * hb - 1, 0), 0))


def _ffn_half(hn16, j, wup_ref, cw_ref, cb_ref, up_scr, rows):
    up_scr[...] = _dot(hn16, wup_ref[j])
    cw = cw_ref[j]
    cv = cb_ref[j] + cw[0:1] * up_scr[pl.ds(HALO - 2, rows), :]
    for k in range(1, FFN_CONV):
        cv = cv + cw[k:k + 1] * up_scr[pl.ds(HALO - 2 + k, rows), :]
    return cv


def _out_ffn_ple_fwd(h, ymix, p_l, w_out, g2, wup_h, cw_h, cb_h, wdn_h, g3, w_gate, w_proj, gather=()):
    s = h.shape[0]
    t = _row_tile(s)
    n = s // t
    ng = len(gather)

    def body(*refs):
        (h_ref, hh_ref, ym_ref, ymh_ref, p_ref, wo_ref, g2_ref, wup_ref, cw_ref, cb_ref, wdn_ref, g3_ref, wg_ref,
         wp_ref) = refs[:14]
        x_refs = refs[14:14 + ng]
        h1_ref, h2_ref, h3_ref, up16_ref = refs[14 + ng:18 + ng]
        land_refs = refs[18 + ng:18 + 2 * ng]
        up_scr = refs[18 + 2 * ng]
        sems = refs[19 + 2 * ng:]
        i = pl.program_id(0)
        if ng:
            @pl.when(i == 0)
            def _():
                _ag_start(x_refs, land_refs, *sems)

            @pl.when(i == n - 1)
            def _():
                _ag_finish(x_refs, land_refs, *sems)

        hh = jnp.where(i > 0, hh_ref[...], 0.0)
        ymh = jnp.where(i > 0, ymh_ref[...].astype(f32), 0.0)
        h_ext = jnp.concatenate([hh, h_ref[...]], axis=0)
        ym_ext = jnp.concatenate([ymh, ym_ref[...].astype(f32)], axis=0).astype(bf16)
        h1_ext = h_ext + _dot(ym_ext, wo_ref[...])
        hn2, _, _ = _rms_fwd(h1_ext, g2_ref[...])
        hn16 = hn2.astype(bf16)
        h1 = h1_ext[HALO:, :]
        acc = h1
        for j in range(2):
            cv = _ffn_half(hn16, j, wup_ref, cw_ref, cb_ref, up_scr, t)
            up16_ref[j] = up_scr[pl.ds(HALO, t), :].astype(bf16)
            act = _gelu(cv[:, :FF_HALF]) * cv[:, FF_HALF:]
            acc = acc + _dot(act.astype(bf16), wdn_ref[j])
        h2 = acc
        hn3, _, _ = _rms_fwd(h2, g3_ref[...])
        gate = _sigmoid(_dot(hn3.astype(bf16), wg_ref[...]))
        pp = _dot(p_ref[0].astype(bf16), wp_ref[...])
        h1_ref[...] = h1
        h2_ref[...] = h2
        h3_ref[...] = h2 + pp * gate

    row = lambda w: pl.BlockSpec((t, w), lambda i: (i, 0))
    outs = pl.pallas_call(
        body, name="out_ffn_ple_fwd_gather" if ng else "out_ffn_ple_fwd", grid=(n,),
        out_shape=tuple(jax.ShapeDtypeStruct((s, D_MODEL), f32) for _ in range(3)) + (jax.ShapeDtypeStruct((2, s, D_FF), bf16),)
        + tuple(jax.ShapeDtypeStruct((N_CHIPS,) + a.shape, a.dtype) for a in gather),
        in_specs=[row(D_MODEL), _prev_halo_spec(t, D_MODEL), row(D_MODEL), _prev_halo_spec(t, D_MODEL),
                  pl.BlockSpec((1, t, D_PLE), lambda i: (p_l[1], i, 0)),
                  _const_spec((D_MODEL, D_MODEL)), _const_spec((1, D_MODEL)), _const_spec((2, D_MODEL, D_FF)),
                  _const_spec((2, FFN_CONV, D_FF)), _const_spec((2, 1, D_FF)), _const_spec((2, FF_HALF, D_MODEL)),
                  _const_spec((1, D_MODEL)), _const_spec((D_MODEL, D_MODEL)), _const_spec((D_PLE, D_MODEL))] + [_hbm()] * ng,
        out_specs=tuple(row(D_MODEL) for _ in range(3)) + (pl.BlockSpec((2, t, D_FF), lambda i: (0, i, 0)),) + (_hbm(),) * ng,
        scratch_shapes=[pltpu.VMEM((HALO + t, D_FF), f32)]
        + ([pltpu.SemaphoreType.DMA((6 * ng,)), pltpu.SemaphoreType.DMA((6 * ng,))] if ng else []),
        compiler_params=_cparams(),
    )(h, h, ymix, ymix, p_l[0], w_out, g2, wup_h, cw_h, cb_h, wdn_h, g3, w_gate, w_proj, *gather)
    return outs[:4] + (_own_block_in(outs[4:], gather),)


def _ple_bwd(dh3, h2, p_l, g3, w_gate, w_proj, exchange=()):
    s = h2.shape[0]
    t = _row_tile(s, WIDE_ROW_TILE)
    n = s // t
    nx = len(exchange)

    def body(*refs):
        dh3_ref, h2_ref, p_ref, g3_ref, wg_ref, wp_ref = refs[:6]
        g_refs = refs[6:6 + nx]
        dh2_ref, dwg_ref, dwp_ref, dg3_ref = refs[6 + nx:10 + nx]
        land_refs = refs[10 + nx:10 + 2 * nx]
        sems = refs[10 + 2 * nx:]
        i = pl.program_id(0)
        if nx:
            @pl.when(i == 0)
            def _():
                for cp in _px_copies(g_refs, land_refs, *sems):
                    cp.start()

            @pl.when(i == n - 1)
            def _():
                for cp in _px_copies(g_refs, land_refs, *sems):
                    cp.wait()

        @pl.when(i == 0)
        def _():
            dwg_ref[...] = jnp.zeros_like(dwg_ref)
            dwp_ref[...] = jnp.zeros_like(dwp_ref)
            dg3_ref[...] = jnp.zeros_like(dg3_ref)

        g3 = g3_ref[...]
        dh3 = dh3_ref[...]
        hn3, xhat, r = _rms_fwd(h2_ref[...], g3)
        hn16 = hn3.astype(bf16)
        gate = _sigmoid(_dot(hn16, wg_ref[...]))
        p16 = p_ref[0].astype(bf16)
        pp = _dot(p16, wp_ref[...])
        d_pp = (dh3 * gate).astype(bf16)
        d_pre = (dh3 * pp * gate * (1.0 - gate)).astype(bf16)
        dwp_ref[...] += _dot_tn(p16, d_pp)
        dwg_ref[...] += _dot_tn(hn16, d_pre)
        dx, dg = _rms_bwd(_dot_nt(d_pre, wg_ref[...]), xhat, r, g3)
        dg3_ref[...] += dg
        dh2_ref[...] = dh3 + dx

    row = lambda w: pl.BlockSpec((t, w), lambda i: (i, 0))
    fixed = lambda shape: pl.BlockSpec(shape, lambda i: (0,) * len(shape))
    outs = pl.pallas_call(
        body, name="ple_bwd_exchange" if nx else "ple_bwd", grid=(n,),
        out_shape=(jax.ShapeDtypeStruct((s, D_MODEL), f32), jax.ShapeDtypeStruct((D_MODEL, D_MODEL), f32),
                   jax.ShapeDtypeStruct((D_PLE, D_MODEL), f32), jax.ShapeDtypeStruct((1, D_MODEL), f32))
        + tuple(jax.ShapeDtypeStruct((N_CHIPS, g.shape[1] // 2, g.shape[2]), g.dtype) for g in exchange),
        in_specs=[row(D_MODEL), row(D_MODEL), pl.BlockSpec((1, t, D_PLE), lambda i: (p_l[1], i, 0)), _const_spec((1, D_MODEL)),
                  _const_spec((D_MODEL, D_MODEL)), _const_spec((D_PLE, D_MODEL))] + [_hbm()] * nx,
        out_specs=(row(D_MODEL), fixed((D_MODEL, D_MODEL)), fixed((D_PLE, D_MODEL)), fixed((1, D_MODEL))) + (_hbm(),) * nx,
        scratch_shapes=[pltpu.SemaphoreType.DMA((nx,)), pltpu.SemaphoreType.DMA((nx,))] if nx else [],
        compiler_params=_cparams(),
    )(dh3, h2, p_l[0], g3, w_gate, w_proj, *exchange)
    return outs[:4] + (list(outs[4:]),)


def _ffn_bwd(dh2, h1, up16, g2, wup_h, cw_h, cb_h, wdn_h, exchange=()):
    s = h1.shape[0]
    t = min(FFN_BWD_TILE, s)
    n = s // t
    hb = t // HALO
    last_hb = s // HALO - 1
    nx = len(exchange)

    def body(*refs):
        dh2_ref, dh2n_ref, h1_ref, up_ref, upp_ref, upn_ref, g2_ref, wup_ref, cw_ref, cb_ref, wdn_ref = refs[:11]
        p_refs = refs[11:11 + nx]
        part_ref, dwup_ref, dwdn_ref, dcw_ref, dcb_ref = refs[11 + nx:16 + nx]
        land_refs = refs[16 + nx:16 + 2 * nx]
        up_scr, dcv_scr, acc_up, acc_dn = refs[16 + 2 * nx:20 + 2 * nx]
        sems = refs[20 + 2 * nx:]
        j = pl.program_id(0)
        i = pl.program_id(1)
        if nx:
            @pl.when(jnp.logical_and(j == 0, i == 0))
            def _():
                _cx_start(p_refs, land_refs, *sems)

            @pl.when(jnp.logical_and(j == 1, i == n - 1))
            def _():
                _cx_finish(p_refs, land_refs, *sems)

        @pl.when(i == 0)
        def _():
            acc_up[...] = jnp.zeros_like(acc_up)
            acc_dn[...] = jnp.zeros_like(acc_dn)
            dcw_ref[...] = jnp.zeros_like(dcw_ref)
            dcb_ref[...] = jnp.zeros_like(dcb_ref)

        hn2, _, _ = _rms_fwd(h1_ref[...], g2_ref[...])
        hn16 = hn2.astype(bf16)
        up_scr[0:HALO, :] = jnp.where(i > 0, upp_ref[0].astype(f32), 0.0)
        up_scr[HALO:HALO + t, :] = up_ref[0].astype(f32)
        up_scr[HALO + t:, :] = upn_ref[0].astype(f32)
        cw = cw_ref[0]
        rows = t + HALO
        cv = cb_ref[0] + cw[0:1] * up_scr[pl.ds(HALO - 2, rows), :]
        for k in range(1, FFN_CONV):
            cv = cv + cw[k:k + 1] * up_scr[pl.ds(HALO - 2 + k, rows), :]
        dh2 = dh2_ref[...]
        dh2n = jnp.where(i < n - 1, dh2n_ref[...], 0.0)
        dh2_ext16 = jnp.concatenate([dh2, dh2n], axis=0).astype(bf16)
        d_act = _dot_nt(dh2_ext16, wdn_ref[0])
        gate, val = cv[:, :FF_HALF], cv[:, FF_HALF:]
        gl, dgl = _gelu_and_grad(gate)
        dcv_scr[:, :FF_HALF] = d_act * val * dgl
        dcv_scr[:, FF_HALF:] = d_act * gl
        act16 = (gl[:t] * val[:t]).astype(bf16)
        acc_dn[...] += _dot_tn(act16, dh2_ext16[:t])
        d_cv = dcv_scr[pl.ds(0, t), :]
        dcb_ref[0] += jnp.sum(d_cv, axis=0, keepdims=True)
        d_up = cw[2:3] * d_cv
        dcw_ref[0, 2:3, :] += jnp.sum(d_cv * up_scr[pl.ds(HALO, t), :], axis=0, keepdims=True)
        for k in range(FFN_CONV - 1):
            dcw_ref[0, k:k + 1, :] += jnp.sum(d_cv * up_scr[pl.ds(HALO - 2 + k, t), :], axis=0, keepdims=True)
            d_up = d_up + cw[k:k + 1] * dcv_scr[pl.ds(2 - k, t), :]
        d_up16 = d_up.astype(bf16)
        part_ref[0] = _dot_nt(d_up16, wup_ref[0])
        acc_up[...] += _dot_tn(hn16, d_up16)

        @pl.when(i == n - 1)
        def _():
            pltpu.sync_copy(acc_up.at[:, pl.ds(0, FF_HALF)], dwup_ref.at[j])
            pltpu.sync_copy(acc_up.at[:, pl.ds(FF_HALF, FF_HALF)], dwup_ref.at[2 + j])
            pltpu.sync_copy(acc_dn, dwdn_ref.at[j])

    outs = pl.pallas_call(
        body, name="ffn_bwd_exchange" if nx else "ffn_bwd", grid=(2, n),
        out_shape=(jax.ShapeDtypeStruct((2, s, D_MODEL), f32), jax.ShapeDtypeStruct((N_CHIPS, D_MODEL, FF_HALF), f32),
                   jax.ShapeDtypeStruct((2, FF_HALF, D_MODEL), f32), jax.ShapeDtypeStruct((2, FFN_CONV, D_FF), f32),
                   jax.ShapeDtypeStruct((2, 1, D_FF), f32))
        + tuple(jax.ShapeDtypeStruct((3,) + a.shape[1:], a.dtype) for a in exchange),
        in_specs=[pl.BlockSpec((t, D_MODEL), lambda j, i: (i, 0)),
                  pl.BlockSpec((HALO, D_MODEL), lambda j, i: (jnp.minimum((i + 1) * hb, last_hb), 0)),
                  pl.BlockSpec((t, D_MODEL), lambda j, i: (i, 0)),
                  pl.BlockSpec((1, t, D_FF), lambda j, i: (j, i, 0)),
                  pl.BlockSpec((1, HALO, D_FF), lambda j, i: (j, jnp.maximum(i * hb - 1, 0), 0)),
                  pl.BlockSpec((1, HALO, D_FF), lambda j, i: (j, jnp.minimum((i + 1) * hb, last_hb), 0)),
                  _const_spec((1, D_MODEL)),
                  pl.BlockSpec((1, D_MODEL, D_FF), lambda j, i: (j, 0, 0), pipeline_mode=pl.Buffered(1)),
                  pl.BlockSpec((1, FFN_CONV, D_FF), lambda j, i: (j, 0, 0)),
                  pl.BlockSpec((1, 1, D_FF), lambda j, i: (j, 0, 0)),
                  pl.BlockSpec((1, FF_HALF, D_MODEL), lambda j, i: (j, 0, 0), pipeline_mode=pl.Buffered(1))] + [_hbm()] * nx,
        out_specs=(pl.BlockSpec((1, t, D_MODEL), lambda j, i: (j, i, 0)), pl.BlockSpec(memory_space=pl.ANY),
                   pl.BlockSpec(memory_space=pl.ANY), pl.BlockSpec((1, FFN_CONV, D_FF), lambda j, i: (j, 0, 0)),
                   pl.BlockSpec((1, 1, D_FF), lambda j, i: (j, 0, 0))) + (_hbm(),) * nx,
        scratch_shapes=[pltpu.VMEM((2 * HALO + t, D_FF), f32), pltpu.VMEM((HALO + t, D_FF), f32),
                        pltpu.VMEM((D_MODEL, D_FF), f32), pltpu.VMEM((FF_HALF, D_MODEL), f32)]
        + ([pltpu.SemaphoreType.DMA((3 * nx,)), pltpu.SemaphoreType.DMA((3 * nx,))] if nx else []),
        compiler_params=_cparams(("arbitrary", "arbitrary")),
    )(dh2, dh2, h1, up16, up16, up16, g2, wup_h, cw_h, cb_h, wdn_h, *exchange)
    return outs[:5] + (list(outs[5:]),)


def _out_bwd(dh2, parts, h1, ymix, g2, w_out, share=()):
    s = h1.shape[0]
    t = _row_tile(s, WIDE_ROW_TILE)
    n = s // t
    ns = len(share)

    def body(*refs):
        dh2_ref, part_ref, h1_ref, ym_ref, g2_ref, wo_ref = refs[:6]
        r_refs = refs[6:6 + ns]
        dh1_ref, dym_ref, dwo_ref, dg2_ref = refs[6 + ns:10 + ns]
        both_refs = refs[10 + ns:10 + 2 * ns]
        sems = refs[10 + 2 * ns:]
        i = pl.program_id(0)
        if ns:
            @pl.when(i == 0)
            def _():
                _ps_start(r_refs, both_refs, *sems)

            @pl.when(i == n - 1)
            def _():
                _ps_finish(r_refs, both_refs, *sems)

        @pl.when(i == 0)
        def _():
            dwo_ref[...] = jnp.zeros_like(dwo_ref)
            dg2_ref[...] = jnp.zeros_like(dg2_ref)

        g2 = g2_ref[...]
        _, xhat, r = _rms_fwd(h1_ref[...], g2)
        dx, dg = _rms_bwd(part_ref[0] + part_ref[1], xhat, r, g2)
        dg2_ref[...] += dg
        dh1 = dh2_ref[...] + dx
        dh1_ref[...] = dh1
        dh16 = dh1.astype(bf16)
        dym_ref[...] = _dot_nt(dh16, wo_ref[...]).astype(bf16)
        dwo_ref[...] += _dot_tn(ym_ref[...], dh16)

    row = lambda w: pl.BlockSpec((t, w), lambda i: (i, 0))
    fixed = lambda shape: pl.BlockSpec(shape, lambda i: (0,) * len(shape))
    outs = pl.pallas_call(
        body, name="out_bwd_share" if ns else "out_bwd", grid=(n,),
        out_shape=(jax.ShapeDtypeStruct((s, D_MODEL), f32), jax.ShapeDtypeStruct((s, D_MODEL), bf16),
                   jax.ShapeDtypeStruct((D_MODEL, D_MODEL), f32), jax.ShapeDtypeStruct((1, D_MODEL), f32))
        + tuple(jax.ShapeDtypeStruct((2 * r.shape[0], r.shape[1]), r.dtype) for r in share),
        in_specs=[row(D_MODEL), pl.BlockSpec((2, t, D_MODEL), lambda i: (0, i, 0)), row(D_MODEL), row(D_MODEL),
                  _const_spec((1, D_MODEL)), _const_spec((D_MODEL, D_MODEL))] + [_hbm()] * ns,
        out_specs=(row(D_MODEL), row(D_MODEL), fixed((D_MODEL, D_MODEL)), fixed((1, D_MODEL))) + (_hbm(),) * ns,
        scratch_shapes=[pltpu.SemaphoreType.DMA((ns,)), pltpu.SemaphoreType.DMA((ns,))] if ns else [],
        compiler_params=_cparams(),
    )(dh2, parts, h1, ymix, g2, w_out, *share)
    return outs[:4] + (_own_half_in(outs[4:], share),)


def _loss_head(h, target, gf):
    s = h.shape[0]
    t = _row_tile(s, WIDE_ROW_TILE)

    def body(h_ref, t_ref, g_ref, dh_ref, dg_ref, loss_ref):
        i = pl.program_id(0)

        @pl.when(i == 0)
        def _():
            dg_ref[...] = jnp.zeros_like(dg_ref)
            loss_ref[...] = jnp.zeros_like(loss_ref)

        g = g_ref[...]
        y, xhat, r = _rms_fwd(h_ref[...], g)
        diff = y - t_ref[...]
        per_row = jnp.mean(diff * diff, axis=-1, keepdims=True)
        loss_ref[...] += 0.5 * jnp.sum(per_row, axis=0, keepdims=True)
        dx, dg = _rms_bwd(diff * (1.0 / D_MODEL), xhat, r, g)
        dg_ref[...] += dg
        dh_ref[...] = dx

    row = pl.BlockSpec((t, D_MODEL), lambda i: (i, 0))
    return pl.pallas_call(
        body, name="loss_head", grid=(s // t,),
        out_shape=(jax.ShapeDtypeStruct((s, D_MODEL), f32), jax.ShapeDtypeStruct((1, D_MODEL), f32),
                   jax.ShapeDtypeStruct((1, LANES), f32)),
        in_specs=[row, row, _const_spec((1, D_MODEL))],
        out_specs=(row, pl.BlockSpec((1, D_MODEL), lambda i: (0, 0)), pl.BlockSpec((1, LANES), lambda i: (0, 0))),
        compiler_params=_cparams(),
    )(h, target, gf)


def _prep_layer(w):
    in4 = w["w_in4"]
    cut = D_IN // N_CHIPS
    tail = 3 * cut - (SSD_W + XBC)
    w_r = jnp.concatenate([in4[0], in4[1], in4[2][:, :cut - tail], in4[3][:, HEADS - tail:], in4[2][:, cut - tail:],
                           in4[3][:, :HEADS - tail], jnp.zeros((D_MODEL, LANES - HEADS), in4.dtype)], axis=1)
    pad8 = lambda v: jnp.concatenate([v, jnp.zeros((LANES - HEADS,), f32)]).reshape(1, LANES)
    halves = _halves
    ssd_prm = (w["ssd_conv_w"], w["ssd_conv_b"].reshape(1, XBC), pad8(w["ssd_dt_bias"]), pad8(w["ssd_a_log"]),
               jnp.repeat(w["ssd_d"], HEAD_DIM).reshape(1, SSD_W), w["ssd_norm_g"].reshape(1, SSD_W),
               w["pool_w"], w["pool_scale"].reshape(1, POOL_W))
    return dict(
        g1=w["mix_norm_g"].reshape(1, D_MODEL), w_r=w_r, ssd=ssd_prm, w_out=w["w_out"], g2=w["ffn_norm_g"].reshape(1, D_MODEL),
        wup_h=w["wup_h"], cw_h=halves(w["ffn_conv_w"]), cb_h=halves(w["ffn_conv_b"].reshape(1, 2 * D_FF)),
        wdn_h=w["wdn_h"], g3=w["ple_norm_g"].reshape(1, D_MODEL), w_gate=w["ple_w_gate"], w_proj=w["ple_w_proj"])


def _halves(a):
    return jnp.stack([jnp.concatenate([a[..., j * FF_HALF:(j + 1) * FF_HALF],
                                       a[..., D_FF + j * FF_HALF:D_FF + (j + 1) * FF_HALF]], axis=-1) for j in range(2)])


def _unhalve(a):
    return jnp.concatenate([a[0][..., :FF_HALF], a[1][..., :FF_HALF], a[0][..., FF_HALF:], a[1][..., FF_HALF:]], axis=-1)


def _layer_fwd(h, p_l, q, gather=()):
    zxu, dtr = _mix_in_fwd(h, q["g1"], q["w_r"])
    ymix, states = _ssd_pool_fwd(zxu, dtr, q["ssd"])
    h1, h2, h3, up16, gathered = _out_ffn_ple_fwd(h, ymix, p_l, q["w_out"], q["g2"], q["wup_h"], q["cw_h"], q["cb_h"],
                                                   q["wdn_h"], q["g3"], q["w_gate"], q["w_proj"], gather)
    return h3, (h, zxu, dtr, ymix, states, h1, h2, up16), gathered


def _layer_bwd(dh, saved, p_l, q, reduce=()):
    h0, zxu, dtr, ymix, states, h1, h2, up16 = saved
    dh2, d_wg, d_wp, d_g3, lands = _ple_bwd(dh, h2, p_l, q["g3"], q["w_gate"], q["w_proj"], reduce)
    sums = _rs_pair_sums(reduce, lands) if reduce else ()
    parts, d_wup, d_wdn, d_cw, d_cb, lands = _ffn_bwd(dh2, h1, up16, q["g2"], q["wup_h"], q["cw_h"], q["cb_h"], q["wdn_h"],
                                                      sums)
    halves = _rs_chip_sums(sums, lands) if reduce else ()
    dh1, d_ymix, d_wo, d_g2, reduced = _out_bwd(dh2, parts, h1, ymix, q["g2"], q["w_out"], halves)
    (d_zxu, d_dtr, d_scw, d_scb, d_dtb, d_alog, d_dsk, d_ng, d_pw, d_ps) = _ssd_pool_bwd(d_ymix, zxu, dtr, states, q["ssd"])
    dh, d_wr, d_g1 = _mix_in_bwd(d_zxu, d_dtr, h0, dh1, q["g1"], q["w_r"])
    cut = D_IN // N_CHIPS
    tail = 3 * cut - (SSD_W + XBC)
    w_in4 = jnp.stack([d_wr[:, :cut], d_wr[:, cut:2 * cut],
                       jnp.concatenate([d_wr[:, 2 * cut:SSD_W + XBC], d_wr[:, 2048:2048 + tail]], axis=1),
                       jnp.concatenate([d_wr[:, 2048 + tail:2048 + HEADS], d_wr[:, SSD_W + XBC:2048]], axis=1)])
    grads = dict(
        mix_norm_g=d_g1.reshape(D_MODEL), w_in4=w_in4,
        ssd_conv_w=d_scw, ssd_conv_b=d_scb.reshape(XBC), ssd_dt_bias=d_dtb[0, :HEADS], ssd_a_log=d_alog[0, :HEADS],
        ssd_d=jnp.sum(d_dsk.reshape(HEADS, HEAD_DIM), axis=1), ssd_norm_g=d_ng.reshape(SSD_W), pool_w=d_pw,
        pool_scale=d_ps.reshape(POOL_W), w_out=d_wo, ffn_norm_g=d_g2.reshape(D_MODEL), ffn_w_up4=d_wup,
        ffn_conv_w=_unhalve(d_cw), ffn_conv_b=_unhalve(d_cb).reshape(2 * D_FF), wdn_h=d_wdn,
        ple_norm_g=d_g3.reshape(D_MODEL), ple_w_gate=d_wg, ple_w_proj=d_wp)
    return dh, grads, reduced


def _device_step(x, p, target, layers, final_g):
    preps = [_prep_layer(w) for w in layers]
    saved = []
    h = x
    for l, q in enumerate(preps):
        h, keep, _ = _layer_fwd(h, (p, l), q)
        saved.append(keep)
    dh, d_gf, loss = _loss_head(h, target, final_g.reshape(1, D_MODEL))
    grads = [None] * len(preps)
    for l in reversed(range(len(preps))):
        dh, grads[l], _ = _layer_bwd(dh, saved[l], (p, l), preps[l])
    return loss, dh, grads, d_gf.reshape(D_MODEL)


MESH = pl.DeviceIdType.MESH
COLS = 1024
N_CHIPS = 4
WEIGHT_NAMES = ["mix_norm_g", "w_in", "ssd_conv_w", "ssd_conv_b", "ssd_dt_bias", "ssd_a_log", "ssd_d", "ssd_norm_g", "pool_w",
                "pool_scale", "w_out", "ffn_norm_g", "ffn_w_up", "ffn_conv_w", "ffn_conv_b", "ffn_w_down", "ple_norm_g",
                "ple_w_gate", "ple_w_proj", "final_norm_g"]
SHARD_AXIS = {"w_in": 2, "ssd_conv_w": 2, "w_out": 1, "ffn_w_up": 2, "ffn_conv_w": 2, "ffn_w_down": 1, "ple_w_gate": 1,
              "ple_w_proj": 2}
MATMUL_SHARDED = ["w_in", "w_out", "ffn_w_up", "ffn_w_down", "ple_w_gate", "ple_w_proj"]
F32_SHARDED = ["ssd_conv_w", "ffn_conv_w"]
REPLICATED = [n for n in WEIGHT_NAMES if n not in SHARD_AXIS]


def _mesh_pos():
    return lax.axis_index("x"), lax.axis_index("y"), lax.axis_index("c")


def _hbm():
    return pl.BlockSpec(memory_space=pl.ANY)


def _all_gather_chips(arrays, name):
    n = len(arrays)

    def body(*refs):
        x_refs, out_refs, sems = refs[:n], refs[n:2 * n], refs[2 * n:]
        _ag_start(x_refs, out_refs, *sems)
        _ag_finish(x_refs, out_refs, *sems)

    gathered = pl.pallas_call(
        body, name=name, out_shape=[jax.ShapeDtypeStruct((N_CHIPS,) + a.shape, a.dtype) for a in arrays],
        in_specs=[_hbm()] * n, out_specs=[_hbm()] * n,
        scratch_shapes=[pltpu.SemaphoreType.DMA((6 * n,)), pltpu.SemaphoreType.DMA((6 * n,))],
    )(*arrays)
    return _own_block_in(gathered, arrays)


def _ag_copies(x_refs, out_refs, send_sems, recv_sems):
    x, y, c = _mesh_pos()
    me = 2 * x + y
    flips = [(1 - x, y), (x, 1 - y), (1 - x, 1 - y)]

    def rows(a, chip, half):
        hr = x_refs[a].shape[0] // 2
        return out_refs[a].at[chip, pl.ds(half * hr, hr), :]

    def copy(k, src, dst, to):
        return pltpu.make_async_remote_copy(src_ref=src, dst_ref=dst, send_sem=send_sems.at[k], recv_sem=recv_sems.at[k],
                                            device_id=to, device_id_type=MESH)

    def first(a, j):
        hr = x_refs[a].shape[0] // 2
        return copy(6 * a + j, x_refs[a].at[pl.ds(c * hr, hr), :], rows(a, me, c), flips[j] + (c,))

    def landed(a, j):
        blk = rows(a, 2 * flips[j][0] + flips[j][1], c)
        return copy(6 * a + j, blk, blk, flips[j] + (c,))

    def passed(a, j, half):
        blk = rows(a, 2 * flips[j][0] + flips[j][1], half)
        return copy(6 * a + 3 + j, blk, blk, (x, y, 1 - c))

    return first, landed, passed, c


def _ag_start(x_refs, out_refs, send_sems, recv_sems):
    first, _, _, _ = _ag_copies(x_refs, out_refs, send_sems, recv_sems)
    for a in range(len(x_refs)):
        for j in range(3):
            first(a, j).start()


def _ag_finish(x_refs, out_refs, send_sems, recv_sems):
    first, landed, passed, c = _ag_copies(x_refs, out_refs, send_sems, recv_sems)
    n = len(x_refs)
    for j in range(3):
        for a in range(n):
            landed(a, j).wait_recv()
            passed(a, j, c).start()
    for j in range(3):
        for a in range(n):
            passed(a, j, 1 - c).wait_recv()
    for a in range(n):
        for j in range(3):
            first(a, j).wait_send()
            passed(a, j, c).wait_send()


def _own_block_in(gathered, arrays):
    if not arrays:
        return []
    me = 2 * lax.axis_index("x") + lax.axis_index("y")
    return [lax.dynamic_update_slice(o, a[None], (me, 0, 0)) for o, a in zip(gathered, arrays)]


def _rs_pair_exchange(gs):
    n = len(gs)

    def body(*refs):
        g_refs, land_refs, sems = refs[:n], refs[n:2 * n], refs[2 * n:]
        for cp in _px_copies(g_refs, land_refs, *sems):
            cp.start()
        for cp in _px_copies(g_refs, land_refs, *sems):
            cp.wait()

    return pl.pallas_call(
        body, name="rs_pair_exchange",
        out_shape=[jax.ShapeDtypeStruct((N_CHIPS, g.shape[1] // 2, g.shape[2]), g.dtype) for g in gs],
        in_specs=[_hbm()] * n, out_specs=[_hbm()] * n,
        scratch_shapes=[pltpu.SemaphoreType.DMA((n,)), pltpu.SemaphoreType.DMA((n,))],
    )(*gs)


def _px_copies(g_refs, land_refs, send_sems, recv_sems):
    x, y, c = _mesh_pos()
    cps = []
    for a in range(len(g_refs)):
        hr = g_refs[a].shape[1] // 2
        cps.append(pltpu.make_async_remote_copy(
            src_ref=g_refs[a].at[:, pl.ds((1 - c) * hr, hr), :], dst_ref=land_refs[a], send_sem=send_sems.at[a],
            recv_sem=recv_sems.at[a], device_id=(x, y, 1 - c), device_id_type=MESH))
    return cps


def _rs_pair_add(g, land, c_idx):
    _, r, cols = g.shape
    hr = r // 2

    def body(c_ref, g_ref, l_ref, o_ref):
        o_ref[...] = (g_ref[...] + l_ref[...]).astype(bf16)

    return pl.pallas_call(
        body, name="rs_pair_add", out_shape=jax.ShapeDtypeStruct((N_CHIPS, hr, cols), bf16),
        grid_spec=pltpu.PrefetchScalarGridSpec(
            num_scalar_prefetch=1, grid=(N_CHIPS,),
            in_specs=[pl.BlockSpec((1, hr, cols), lambda k, c_ref: (k, c_ref[0], 0)),
                      pl.BlockSpec((1, hr, cols), lambda k, c_ref: (k, 0, 0))],
            out_specs=pl.BlockSpec((1, hr, cols), lambda k, c_ref: (k, 0, 0))),
        compiler_params=_cparams(),
    )(c_idx, g, land)


def _rs_chip_exchange(parts):
    n = len(parts)

    def body(*refs):
        p_refs, land_refs, sems = refs[:n], refs[n:2 * n], refs[2 * n:]
        _cx_start(p_refs, land_refs, *sems)
        _cx_finish(p_refs, land_refs, *sems)

    return pl.pallas_call(
        body, name="rs_chip_exchange", out_shape=[jax.ShapeDtypeStruct((3,) + p.shape[1:], p.dtype) for p in parts],
        in_specs=[_hbm()] * n, out_specs=[_hbm()] * n,
        scratch_shapes=[pltpu.SemaphoreType.DMA((3 * n,)), pltpu.SemaphoreType.DMA((3 * n,))],
    )(*parts)


def _cx_copies(p_refs, land_refs, send_sems, recv_sems):
    x, y, c = _mesh_pos()
    flips = [(1 - x, y), (x, 1 - y), (1 - x, 1 - y)]
    return [pltpu.make_async_remote_copy(src_ref=p_refs[a].at[2 * fx + fy], dst_ref=land_refs[a].at[j],
                                         send_sem=send_sems.at[3 * a + j], recv_sem=recv_sems.at[3 * a + j],
                                         device_id=(fx, fy, c), device_id_type=MESH)
            for a in range(len(p_refs)) for j, (fx, fy) in enumerate(flips)]


def _cx_start(p_refs, land_refs, send_sems, recv_sems):
    for cp in _cx_copies(p_refs, land_refs, send_sems, recv_sems):
        cp.start()


def _cx_finish(p_refs, land_refs, send_sems, recv_sems):
    for cp in _cx_copies(p_refs, land_refs, send_sems, recv_sems):
        cp.wait()


def _rs_chip_add(part, land, me_idx):
    _, hr, cols = part.shape

    def body(me_ref, p_ref, l_ref, o_ref):
        o_ref[...] = ((p_ref[0].astype(f32) + l_ref[0].astype(f32)) + l_ref[1].astype(f32)) + l_ref[2].astype(f32)

    return pl.pallas_call(
        body, name="rs_chip_add", out_shape=jax.ShapeDtypeStruct((hr, cols), f32),
        grid_spec=pltpu.PrefetchScalarGridSpec(
            num_scalar_prefetch=1, grid=(1,),
            in_specs=[pl.BlockSpec((1, hr, cols), lambda i, me_ref: (me_ref[0], 0, 0)),
                      pl.BlockSpec((3, hr, cols), lambda i, me_ref: (0, 0, 0))],
            out_specs=pl.BlockSpec((hr, cols), lambda i, me_ref: (0, 0))),
        compiler_params=_cparams(),
    )(me_idx, part, land)


def _rs_pair_share(reds):
    n = len(reds)

    def body(*refs):
        r_refs, out_refs, sems = refs[:n], refs[n:2 * n], refs[2 * n:]
        _ps_start(r_refs, out_refs, *sems)
        _ps_finish(r_refs, out_refs, *sems)

    both = pl.pallas_call(
        body, name="rs_pair_share", out_shape=[jax.ShapeDtypeStruct((2 * r.shape[0], r.shape[1]), r.dtype) for r in reds],
        in_specs=[_hbm()] * n, out_specs=[_hbm()] * n,
        scratch_shapes=[pltpu.SemaphoreType.DMA((n,)), pltpu.SemaphoreType.DMA((n,))],
    )(*reds)
    return _own_half_in(both, reds)


def _ps_copy(r_refs, out_refs, send_sems, recv_sems, a, half):
    x, y, c = _mesh_pos()
    hr = r_refs[a].shape[0]
    return pltpu.make_async_remote_copy(src_ref=r_refs[a], dst_ref=out_refs[a].at[pl.ds(half * hr, hr), :],
                                        send_sem=send_sems.at[a], recv_sem=recv_sems.at[a], device_id=(x, y, 1 - c),
                                        device_id_type=MESH)


def _ps_start(r_refs, out_refs, send_sems, recv_sems):
    c = lax.axis_index("c")
    for a in range(len(r_refs)):
        _ps_copy(r_refs, out_refs, send_sems, recv_sems, a, c).start()


def _ps_finish(r_refs, out_refs, send_sems, recv_sems):
    c = lax.axis_index("c")
    for a in range(len(r_refs)):
        _ps_copy(r_refs, out_refs, send_sems, recv_sems, a, 1 - c).wait_recv()
    for a in range(len(r_refs)):
        _ps_copy(r_refs, out_refs, send_sems, recv_sems, a, c).wait_send()


def _own_half_in(both, reds):
    if not reds:
        return []
    c = lax.axis_index("c")
    return [lax.dynamic_update_slice(o, r, (c * r.shape[0], 0)) for o, r in zip(both, reds)]


def _rs_pair_sums(gs, lands):
    c_idx = jnp.reshape(lax.axis_index("c"), (1,)).astype(jnp.int32)
    return [_rs_pair_add(g, land, c_idx) for g, land in zip(gs, lands)]


def _rs_chip_sums(parts, lands):
    me_idx = jnp.reshape(2 * lax.axis_index("x") + lax.axis_index("y"), (1,)).astype(jnp.int32)
    return [_rs_chip_add(part, land, me_idx) for part, land in zip(parts, lands)]


def _reduce_scatter(gs):
    sums = _rs_pair_sums(gs, _rs_pair_exchange(gs))
    return _rs_pair_share(_rs_chip_sums(sums, _rs_chip_exchange(sums)))


def _pack(parts, row_multiple):
    flat = jnp.concatenate([a.reshape(-1) for a in parts])
    n = flat.shape[0]
    rows = -(-n // COLS)
    rows = -(-rows // row_multiple) * row_multiple
    return jnp.pad(flat, (0, rows * COLS - n)).reshape(rows, COLS)


def _unpack(flat, shapes):
    out, off = [], 0
    for shp in shapes:
        n = math.prod(shp)
        out.append(flat[off:off + n].reshape(shp))
        off += n
    return out


def _adamw(w, g, m, v, name):
    shape = w.shape
    cols = shape[-1]
    rows = math.prod(shape[:-1]) if len(shape) > 1 else 1
    tr = rows
    if rows > 512:
        tr = next(t for t in (512, 256, 128, 64, 32, 16, 8) if rows % t == 0)
    two_d = lambda a: a.reshape(rows, cols)

    def body(w_ref, g_ref, m_ref, v_ref, d_ref, nm_ref, nv_ref):
        gg = g_ref[...]
        nm = ADAM_B1 * m_ref[...] + (1.0 - ADAM_B1) * gg
        nv = ADAM_B2 * v_ref[...] + (1.0 - ADAM_B2) * (gg * gg)
        m_hat = nm / (1.0 - ADAM_B1 ** ADAM_STEP)
        v_hat = nv / (1.0 - ADAM_B2 ** ADAM_STEP)
        d_ref[...] = -ADAM_LR * (m_hat / (jnp.sqrt(v_hat) + ADAM_EPS) + ADAM_WD * w_ref[...])
        nm_ref[...] = nm
        nv_ref[...] = nv

    spec = pl.BlockSpec((tr, cols), lambda i: (i, 0))
    outs = pl.pallas_call(
        body, name="adamw_" + name, grid=(rows // tr,),
        out_shape=tuple(jax.ShapeDtypeStruct((rows, cols), f32) for _ in range(3)),
        in_specs=[spec] * 4, out_specs=(spec,) * 3, compiler_params=_cparams(),
    )(two_d(w), two_d(g), two_d(m), two_d(v))
    return tuple(o.reshape(shape) for o in outs)


BIG_SHARDED = ["ffn_w_up", "ffn_w_down", "w_out", "ple_w_gate", "ple_w_proj", "w_in"]
SMALL_REDUCED = REPLICATED + F32_SHARDED


def _chip_major(a, axis):
    n = a.shape[axis] // N_CHIPS
    return jnp.moveaxis(a.reshape(a.shape[:axis] + (N_CHIPS, n) + a.shape[axis + 1:]), axis, 0)


def _train_step(x, p, loss_target, w, m, v):
    xs, ps, target = x[0], p[:, 0], loss_target[0]
    rows2d = lambda a: a.reshape(-1, a.shape[-1])
    conv_rows = 16
    pad_rows = lambda a: jnp.pad(a, ((0, conv_rows - a.shape[0]), (0, 0)))
    shards = [[w[n][l].astype(bf16) for n in MATMUL_SHARDED] for l in range(DEPTH)]
    first = _all_gather_chips(shards[0] + [pad_rows(rows2d(w[n])) for n in F32_SHARDED], "gather_weights")
    conv4 = dict(zip(F32_SHARDED, first[len(MATMUL_SHARDED):]))

    def col_cut(blk):
        return jnp.moveaxis(blk, 0, 1).reshape(blk.shape[1], N_CHIPS * blk.shape[2])

    def layer_weights(l, gathered):
        g = dict(zip(MATMUL_SHARDED, gathered))
        lw = {n: w[n][l] for n in REPLICATED if n != "final_norm_g"}
        lw["pool_w"] = lw["pool_w"].astype(bf16)
        lw.update(
            w_in4=g["w_in"], ple_w_proj=col_cut(g["ple_w_proj"]), w_out=g["w_out"].reshape(D_MODEL, D_MODEL),
            ple_w_gate=g["ple_w_gate"].reshape(D_MODEL, D_MODEL), wdn_h=g["ffn_w_down"].reshape(2, FF_HALF, D_MODEL),
            wup_h=jnp.stack([jnp.concatenate([g["ffn_w_up"][j], g["ffn_w_up"][2 + j]], axis=1) for j in range(2)]),
            ssd_conv_w=col_cut(conv4["ssd_conv_w"][:, l * SSD_CONV:(l + 1) * SSD_CONV, :]),
            ffn_conv_w=col_cut(conv4["ffn_conv_w"][:, l * FFN_CONV:(l + 1) * FFN_CONV, :]))
        return _prep_layer(lw)

    preps, saved = [], []
    h, gathered = xs, first[:len(MATMUL_SHARDED)]
    for l in range(DEPTH):
        preps.append(layer_weights(l, gathered))
        h, keep, gathered = _layer_fwd(h, (ps, l), preps[l], gather=shards[l + 1] if l + 1 < DEPTH else ())
        saved.append(keep)
    grad_x, d_gf, loss_part = _loss_head(h, target, w["final_norm_g"].reshape(1, D_MODEL))

    def chip_major_grads(g):
        return [g["ffn_w_up4"], g["wdn_h"].reshape(N_CHIPS, D_FF // N_CHIPS, D_MODEL), _chip_major(g["w_out"], 0),
                _chip_major(g["ple_w_gate"], 0), _chip_major(g["ple_w_proj"], 1), g["w_in4"]]

    grads, reduced, pending = [None] * DEPTH, [None] * DEPTH, ()
    for l in reversed(range(DEPTH)):
        grad_x, grads[l], done = _layer_bwd(grad_x, saved[l], (ps, l), preps[l], reduce=pending)
        if pending:
            reduced[l + 1] = done
        pending = chip_major_grads(grads[l])
    small = _pack([d_gf.reshape(D_MODEL) if n == "final_norm_g" else jnp.stack([grads[k][n] for k in range(DEPTH)])
                   for n in SMALL_REDUCED], 32 * N_CHIPS)
    reduced[0] = _reduce_scatter(pending + [small.reshape(N_CHIPS, -1, COLS)])
    grad = {n: jnp.stack([reduced[l][i] for l in range(DEPTH)]) for i, n in enumerate(BIG_SHARDED)}
    small_all = _all_gather_chips([reduced[0][-1]], "gather_small_grads")[0].reshape(-1)
    small_shapes = [w[n].shape for n in REPLICATED] + [(DEPTH, SSD_CONV, XBC), (DEPTH, FFN_CONV, 2 * D_FF)]
    grad.update(zip(SMALL_REDUCED, _unpack(small_all, small_shapes)))
    me = 2 * lax.axis_index("x") + lax.axis_index("y")
    for n in F32_SHARDED:
        grad[n] = lax.dynamic_slice_in_dim(grad[n], me * w[n].shape[2], w[n].shape[2], axis=2)

    loss = lax.psum(loss_part[0, 0], ("x", "y", "c"))
    delta, new_m, new_v = {}, {}, {}
    for n in WEIGHT_NAMES:
        delta[n], new_m[n], new_v[n] = _adamw(w[n], grad[n], m[n], v[n], n)
    return (loss, grad_x[None], *[grad[n] for n in WEIGHT_NAMES], *[delta[n] for n in WEIGHT_NAMES],
            *[new_m[n] for n in WEIGHT_NAMES], *[new_v[n] for n in WEIGHT_NAMES])


def kernel(x, p, mix_norm_g, w_in, ssd_conv_w, ssd_conv_b, ssd_dt_bias, ssd_a_log, ssd_d, ssd_norm_g, pool_w, pool_scale, w_out, ffn_norm_g, ffn_w_up, ffn_conv_w, ffn_conv_b, ffn_w_down, ple_norm_g, ple_w_gate, ple_w_proj, final_norm_g, loss_target, m_mix_norm_g, m_w_in, m_ssd_conv_w, m_ssd_conv_b, m_ssd_dt_bias, m_ssd_a_log, m_ssd_d, m_ssd_norm_g, m_pool_w, m_pool_scale, m_w_out, m_ffn_norm_g, m_ffn_w_up, m_ffn_conv_w, m_ffn_conv_b, m_ffn_w_down, m_ple_norm_g, m_ple_w_gate, m_ple_w_proj, m_final_norm_g, v_mix_norm_g, v_w_in, v_ssd_conv_w, v_ssd_conv_b, v_ssd_dt_bias, v_ssd_a_log, v_ssd_d, v_ssd_norm_g, v_pool_w, v_pool_scale, v_w_out, v_ffn_norm_g, v_ffn_w_up, v_ffn_conv_w, v_ffn_conv_b, v_ffn_w_down, v_ple_norm_g, v_ple_w_gate, v_ple_w_proj, v_final_norm_g):
    w = dict(mix_norm_g=mix_norm_g, w_in=w_in, ssd_conv_w=ssd_conv_w, ssd_conv_b=ssd_conv_b, ssd_dt_bias=ssd_dt_bias, ssd_a_log=ssd_a_log, ssd_d=ssd_d, ssd_norm_g=ssd_norm_g, pool_w=pool_w, pool_scale=pool_scale, w_out=w_out, ffn_norm_g=ffn_norm_g, ffn_w_up=ffn_w_up, ffn_conv_w=ffn_conv_w, ffn_conv_b=ffn_conv_b, ffn_w_down=ffn_w_down, ple_norm_g=ple_norm_g, ple_w_gate=ple_w_gate, ple_w_proj=ple_w_proj, final_norm_g=final_norm_g)
    m = dict(mix_norm_g=m_mix_norm_g, w_in=m_w_in, ssd_conv_w=m_ssd_conv_w, ssd_conv_b=m_ssd_conv_b, ssd_dt_bias=m_ssd_dt_bias, ssd_a_log=m_ssd_a_log, ssd_d=m_ssd_d, ssd_norm_g=m_ssd_norm_g, pool_w=m_pool_w, pool_scale=m_pool_scale, w_out=m_w_out, ffn_norm_g=m_ffn_norm_g, ffn_w_up=m_ffn_w_up, ffn_conv_w=m_ffn_conv_w, ffn_conv_b=m_ffn_conv_b, ffn_w_down=m_ffn_w_down, ple_norm_g=m_ple_norm_g, ple_w_gate=m_ple_w_gate, ple_w_proj=m_ple_w_proj, final_norm_g=m_final_norm_g)
    v = dict(mix_norm_g=v_mix_norm_g, w_in=v_w_in, ssd_conv_w=v_ssd_conv_w, ssd_conv_b=v_ssd_conv_b, ssd_dt_bias=v_ssd_dt_bias, ssd_a_log=v_ssd_a_log, ssd_d=v_ssd_d, ssd_norm_g=v_ssd_norm_g, pool_w=v_pool_w, pool_scale=v_pool_scale, w_out=v_w_out, ffn_norm_g=v_ffn_norm_g, ffn_w_up=v_ffn_w_up, ffn_conv_w=v_ffn_conv_w, ffn_conv_b=v_ffn_conv_b, ffn_w_down=v_ffn_w_down, ple_norm_g=v_ple_norm_g, ple_w_gate=v_ple_w_gate, ple_w_proj=v_ple_w_proj, final_norm_g=v_final_norm_g)
    return _train_step(x, p, loss_target, w, m, v)
```

```python
import functools
import math

import jax
import jax.numpy as jnp
from jax import lax
from jax.experimental import pallas as pl
from jax.experimental.pallas import tpu as pltpu

f32, bf16 = jnp.float32, jnp.bfloat16
HI = lax.Precision.HIGHEST

D_MODEL = 1024
D_PLE = 256
DEPTH = 4
SSD_W = 512
HEADS = 8
HEAD_DIM = 64
NSTATE = 128
CHUNK = 128
SSD_CONV = 4
XBC = 1024
POOL_W = 512
POOL_G = 128
WINDOWS = (2, 4, 8, 16)
D_FF = 2816
FF_HALF = D_FF // 2
FFN_CONV = 3
D_IN = 2056
EPS = 1e-6
ADAM_LR, ADAM_B1, ADAM_B2, ADAM_EPS, ADAM_WD, ADAM_STEP = 0.001, 0.9, 0.999, 1e-08, 0.01, 10

LANES = 128
NPROJ = 2048 + LANES
HALO = 16
FHALO = 8
VMEM_LIMIT = 58 * 1024 * 1024
ROW_TILE = 256


def _dot(a, b):
    return jnp.dot(a, b, preferred_element_type=f32)


def _dot_nt(a, b):
    return lax.dot_general(a, b, (((1,), (1,)), ((), ())), preferred_element_type=f32)


def _dot_tn(a, b):
    return lax.dot_general(a, b, (((0,), (0,)), ((), ())), preferred_element_type=f32)


def _dot_hi(a, b):
    return jnp.dot(a, b, precision=HI, preferred_element_type=f32)


def _dot_nt_hi(a, b):
    return lax.dot_general(a, b, (((1,), (1,)), ((), ())), precision=HI, preferred_element_type=f32)


def _rms_fwd(x, g):
    r = lax.rsqrt(jnp.mean(x * x, axis=-1, keepdims=True) + EPS)
    xhat = x * r
    return xhat * g, xhat, r


def _rms_bwd(dy, xhat, r, g):
    dxhat = dy * g
    dx = r * (dxhat - xhat * jnp.mean(dxhat * xhat, axis=-1, keepdims=True))
    return dx, jnp.sum(dy * xhat, axis=0, keepdims=True)


def _sigmoid(x):
    return 1.0 / (1.0 + jnp.exp(-x))


_GELU_C = math.sqrt(2.0 / math.pi)


def _gelu_and_grad(x):
    x2 = x * x
    t = jnp.tanh(x * (_GELU_C + (_GELU_C * 0.044715) * x2))
    u = 0.5 * t + 0.5
    g = x * u
    dg = u + g * (1.0 - t) * (_GELU_C + (3.0 * 0.044715 * _GELU_C) * x2)
    return g, dg


def _gelu(x):
    return x * (0.5 * jnp.tanh(x * (_GELU_C + (_GELU_C * 0.044715) * (x * x))) + 0.5)


def _softplus(x):
    return jnp.maximum(x, 0.0) + jnp.log(1.0 + jnp.exp(-jnp.abs(x)))


def _cparams(sem=("arbitrary",)):
    return pltpu.CompilerParams(dimension_semantics=sem, vmem_limit_bytes=VMEM_LIMIT)


def _const_spec(shape):
    nd = len(shape)
    return pl.BlockSpec(shape, lambda *_: (0,) * nd, pipeline_mode=pl.Buffered(1))


WIDE_ROW_TILE = 512


def _row_tile(s, t=ROW_TILE):
    return min(t, s)


def _mix_in_fwd(h, g1, w_r):
    s = h.shape[0]
    t = _row_tile(s, WIDE_ROW_TILE)

    def body(h_ref, g_ref, w_ref, zxu_ref, dtr_ref):
        hn, _, _ = _rms_fwd(h_ref[...], g_ref[...])
        proj = _dot(hn.astype(bf16), w_ref[...])
        zxu_ref[...] = proj[:, :2048].astype(bf16)
        dtr_ref[...] = proj[:, 2048:]

    return pl.pallas_call(
        body, name="mix_in_fwd", grid=(s // t,),
        out_shape=(jax.ShapeDtypeStruct((s, 2048), bf16), jax.ShapeDtypeStruct((s, LANES), f32)),
        in_specs=[pl.BlockSpec((t, D_MODEL), lambda i: (i, 0)), _const_spec((1, D_MODEL)), _const_spec((D_MODEL, NPROJ))],
        out_specs=(pl.BlockSpec((t, 2048), lambda i: (i, 0)), pl.BlockSpec((t, LANES), lambda i: (i, 0))),
        compiler_params=_cparams(),
    )(h, g1, w_r)


def _mix_in_bwd(d_zxu, d_dtr, h, dh1, g1, w_r):
    s = h.shape[0]
    t = _row_tile(s, WIDE_ROW_TILE)
    n = s // t

    def body(dz_ref, dd_ref, h_ref, dh1_ref, g_ref, w_ref, dh_ref, dw_ref, dg_ref, acc):
        i = pl.program_id(0)

        @pl.when(i == 0)
        def _():
            acc[...] = jnp.zeros_like(acc)
            dg_ref[...] = jnp.zeros_like(dg_ref)

        g = g_ref[...]
        hn, xhat, r = _rms_fwd(h_ref[...], g)
        dproj = jnp.concatenate([dz_ref[...], dd_ref[...].astype(bf16)], axis=1)
        d_hn = _dot_nt(dproj, w_ref[...])
        acc[...] += _dot_tn(hn.astype(bf16), dproj)
        dx, dg = _rms_bwd(d_hn, xhat, r, g)
        dg_ref[...] += dg
        dh_ref[...] = dh1_ref[...] + dx

        @pl.when(i == n - 1)
        def _():
            pltpu.sync_copy(acc, dw_ref)

    return pl.pallas_call(
        body, name="mix_in_bwd", grid=(n,),
        out_shape=(jax.ShapeDtypeStruct((s, D_MODEL), f32), jax.ShapeDtypeStruct((D_MODEL, NPROJ), f32),
                   jax.ShapeDtypeStruct((1, D_MODEL), f32)),
        in_specs=[pl.BlockSpec((t, 2048), lambda i: (i, 0)), pl.BlockSpec((t, LANES), lambda i: (i, 0)),
                  pl.BlockSpec((t, D_MODEL), lambda i: (i, 0)), pl.BlockSpec((t, D_MODEL), lambda i: (i, 0)),
                  _const_spec((1, D_MODEL)), _const_spec((D_MODEL, NPROJ))],
        out_specs=(pl.BlockSpec((t, D_MODEL), lambda i: (i, 0)), pl.BlockSpec(memory_space=pl.ANY),
                   pl.BlockSpec((1, D_MODEL), lambda i: (0, 0))),
        scratch_shapes=[pltpu.VMEM((D_MODEL, NPROJ), f32)],
        compiler_params=_cparams(),
    )(d_zxu, d_dtr, h, dh1, g1, w_r)


def _iota2(shape, dim):
    return lax.broadcasted_iota(jnp.int32, shape, dim)


def _lane_bcast(a, h):
    return jnp.broadcast_to(a[:, h:h + 1], (a.shape[0], LANES))


def _to_columns(cols):
    lane = _iota2((cols[0].shape[0], LANES), 1)
    out = jnp.where(lane == 0, cols[0], 0.0)
    for h in range(1, len(cols)):
        out = out + jnp.where(lane == h, cols[h], 0.0)
    return out


def _ssd_pre(zx, dtr, xext, cw, cb, dtb, alog):
    c = cb + cw[0:1] * xext[pl.ds(HALO - 3, CHUNK), :]
    for k in range(1, SSD_CONV):
        c = c + cw[k:k + 1] * xext[pl.ds(HALO - 3 + k, CHUNK), :]
    sig_c = _sigmoid(c)
    xc = c * sig_c
    dt = _softplus(dtr + dtb)
    a_neg = -jnp.exp(alog)
    a = dt * a_neg
    lo = _iota2((CHUNK, LANES), 1) < HEAD_DIM
    dt_w = jnp.concatenate([jnp.where(lo, _lane_bcast(dt, 2 * pp), _lane_bcast(dt, 2 * pp + 1)) for pp in range(4)], axis=1)
    xs = xc[:, :SSD_W]
    xd = xs * dt_w
    tril = (_iota2((CHUNK, CHUNK), 0) >= _iota2((CHUNK, CHUNK), 1))
    acs = _dot_hi(tril.astype(f32), a)
    acs_b = [_lane_bcast(acs, h) for h in range(HEADS)]
    groups = []
    for g in range(2):
        b16 = xc[:, SSD_W + g * NSTATE:SSD_W + (g + 1) * NSTATE].astype(bf16)
        c16 = xc[:, SSD_W + 2 * NSTATE + g * NSTATE:SSD_W + 2 * NSTATE + (g + 1) * NSTATE].astype(bf16)
        groups.append((b16, c16, _dot_nt(c16, b16)))
    return dict(c=c, sig_c=sig_c, xc=xc, xs=xs, dt=dt, a_neg=a_neg, a=a, dt_w=dt_w, xd=xd, acs_b=acs_b, tril=tril, lo=lo,
                groups=groups)


def _pair_fwd(q, pp, s_in):
    lo = q["lo"]
    b16, c16, gmat = q["groups"][pp // 2]
    xp = q["xd"][:, pp * LANES:(pp + 1) * LANES]
    xp16 = xp.astype(bf16)
    ab0 = q["acs_b"][2 * pp]
    ab1 = q["acs_b"][2 * pp + 1]
    ls, ms, ys = [], [], []
    for ab in (ab0, ab1):
        lmat = jnp.exp(jnp.where(q["tril"], ab - ab.T, -jnp.inf))
        mmat = gmat * lmat
        ls.append(lmat)
        ms.append(mmat)
        ys.append(_dot(mmat.astype(bf16), xp16))
    y_diag = jnp.where(lo, ys[0], ys[1])
    ab_pair = jnp.where(lo, ab0, ab1)
    e_pair = jnp.exp(ab_pair)
    s16 = s_in.astype(bf16)
    y_off = _dot(c16, s16) * e_pair
    alast = ab_pair[CHUNK - 1:CHUNK, :]
    dec_pair = jnp.exp(alast - ab_pair)
    xdec = xp * dec_pair
    st = _dot_tn(b16, xdec.astype(bf16))
    cd_pair = jnp.exp(alast)
    s_out = s_in * cd_pair + st
    return dict(b16=b16, c16=c16, gmat=gmat, xp=xp, xp16=xp16, ls=ls, ms=ms, y=y_diag + y_off, y_off=y_off,
                e_pair=e_pair, dec_pair=dec_pair, xdec=xdec, cd_pair=cd_pair, s_out=s_out, s16=s16, lo=lo)


def _gate_norm_fwd(y_pre, z, ng):
    sz = _sigmoid(z)
    yg = y_pre * (z * sz)
    outs, stats = [], []
    half = SSD_W // 2
    for gi in range(2):
        o, xhat, r = _rms_fwd(yg[:, gi * half:(gi + 1) * half], ng[:, gi * half:(gi + 1) * half])
        outs.append(o)
        stats.append((xhat, r))
    return jnp.concatenate(outs, axis=1), sz, stats


def _pool_fwd(uext, u, row0, pw_ref, ps):
    pos = (row0 + _iota2((CHUNK, 1), 0) + 1).astype(f32)
    pooled, mixed, invs = [], [], []
    for gi, w in enumerate(WINDOWS):
        sl = slice(gi * POOL_G, (gi + 1) * POOL_G)
        acc = uext[pl.ds(HALO, CHUNK), sl]
        for j in range(1, w):
            acc = acc + uext[pl.ds(HALO - j, CHUNK), sl]
        den = jnp.minimum(pos, float(w))
        pg = acc / den - u[:, sl]
        pooled.append(pg)
        invs.append(den)
        mixed.append(_dot(pg.astype(bf16), pw_ref[gi]))
    mixed = jnp.concatenate(mixed, axis=1)
    return mixed * ps, pooled, mixed, invs


def _ssd_specs(s):
    nc = s // CHUNK
    hb = CHUNK // HALO
    return nc, hb


def _ssd_param_specs():
    return [_const_spec((SSD_CONV, XBC)), _const_spec((1, XBC)), _const_spec((1, LANES)), _const_spec((1, LANES)),
            _const_spec((1, SSD_W)), _const_spec((1, SSD_W)), _const_spec((4, POOL_G, POOL_G)), _const_spec((1, POOL_W))]


def _ssd_pool_fwd(zxu, dtr, prm):
    s = zxu.shape[0]
    nc, hb = _ssd_specs(s)

    def body(zx_ref, halo_ref, dtr_ref, cw_ref, cb_ref, dtb_ref, alog_ref, dsk_ref, ng_ref, pw_ref, ps_ref,
             ymix_ref, st_ref, state, xext, uext):
        i = pl.program_id(0)

        @pl.when(i == 0)
        def _():
            state[...] = jnp.zeros_like(state)

        zx = zx_ref[...].astype(f32)
        halo = jnp.where(i > 0, halo_ref[...].astype(f32), 0.0)
        xext[0:HALO, :] = halo[:, SSD_W:SSD_W + XBC]
        xext[HALO:, :] = zx[:, SSD_W:SSD_W + XBC]
        uext[0:HALO, :] = halo[:, SSD_W + XBC:]
        uext[HALO:, :] = zx[:, SSD_W + XBC:]
        q = _ssd_pre(zx, dtr_ref[...], xext, cw_ref[...], cb_ref[...], dtb_ref[...], alog_ref[...])
        ys = []
        for pp in range(4):
            s_in = state[pp]
            st_ref[0, pp] = s_in
            r = _pair_fwd(q, pp, s_in)
            state[pp] = r["s_out"]
            ys.append(r["y"])
        y_pre = jnp.concatenate(ys, axis=1) + q["xs"] * dsk_ref[...]
        y_ssd, _, _ = _gate_norm_fwd(y_pre, zx[:, :SSD_W], ng_ref[...])
        y_pool, _, _, _ = _pool_fwd(uext, zx[:, SSD_W + XBC:], i * CHUNK, pw_ref, ps_ref[...])
        ymix_ref[:, :SSD_W] = y_ssd.astype(bf16)
        ymix_ref[:, SSD_W:] = y_pool.astype(bf16)

    return pl.pallas_call(
        body, name="ssd_pool_fwd", grid=(nc,),
        out_shape=(jax.ShapeDtypeStruct((s, D_MODEL), bf16), jax.ShapeDtypeStruct((nc, 4, NSTATE, LANES), f32)),
        in_specs=[pl.BlockSpec((CHUNK, 2048), lambda i: (i, 0)),
                  pl.BlockSpec((HALO, 2048), lambda i: (jnp.maximum(i * hb - 1, 0), 0)),
                  pl.BlockSpec((CHUNK, LANES), lambda i: (i, 0))] + _ssd_param_specs(),
        out_specs=(pl.BlockSpec((CHUNK, D_MODEL), lambda i: (i, 0)),
                   pl.BlockSpec((1, 4, NSTATE, LANES), lambda i: (i, 0, 0, 0))),
        scratch_shapes=[pltpu.VMEM((4, NSTATE, LANES), f32), pltpu.VMEM((HALO + CHUNK, XBC), f32),
                        pltpu.VMEM((HALO + CHUNK, POOL_W), f32)],
        compiler_params=_cparams(),
    )(zxu, zxu, dtr, *prm)


def _ssd_pool_bwd(d_ymix, zxu, dtr, states, prm, exchange=()):
    s = zxu.shape[0]
    nc, hb = _ssd_specs(s)
    rev = lambda i: nc - 1 - i
    nx = len(exchange)

    def body(*refs):
        (dy_ref, zx_ref, halo_ref, dtr_ref, st_ref, cw_ref, cb_ref, dtb_ref, alog_ref, dsk_ref, ng_ref, pw_ref,
         ps_ref) = refs[:13]
        p_refs = refs[13:13 + nx]
        (dzx_ref, ddtr_ref, dcw_ref, dcb_ref, ddtb_ref, dalog_ref, ddsk_ref, dng_ref, dpw_ref,
         dps_ref) = refs[13 + nx:23 + nx]
        land_refs = refs[23 + nx:23 + 2 * nx]
        dstate, xext, uext, dxext, duext, cx, cu = refs[23 + 2 * nx:30 + 2 * nx]
        sems = refs[30 + 2 * nx:]
        i = pl.program_id(0)
        ci = nc - 1 - i
        if nx:
            @pl.when(i == 0)
            def _():
                _cx_start(p_refs, land_refs, *sems)

            @pl.when(i == nc - 1)
            def _():
                _cx_finish(p_refs, land_refs, *sems)

        @pl.when(i == 0)
        def _():
            dstate[...] = jnp.zeros_like(dstate)
            cx[...] = jnp.zeros_like(cx)
            cu[...] = jnp.zeros_like(cu)
            for r in (dcw_ref, dcb_ref, ddtb_ref, dalog_ref, ddsk_ref, dng_ref, dpw_ref, dps_ref):
                r[...] = jnp.zeros_like(r)

        zx = zx_ref[...].astype(f32)
        halo = jnp.where(ci > 0, halo_ref[...].astype(f32), 0.0)
        xext[0:HALO, :] = halo[:, SSD_W:SSD_W + XBC]
        xext[HALO:, :] = zx[:, SSD_W:SSD_W + XBC]
        uext[0:HALO, :] = halo[:, SSD_W + XBC:]
        uext[HALO:, :] = zx[:, SSD_W + XBC:]
        cw = cw_ref[...]
        q = _ssd_pre(zx, dtr_ref[...], xext, cw, cb_ref[...], dtb_ref[...], alog_ref[...])
        z = zx[:, :SSD_W]
        dy = dy_ref[...].astype(f32)
        d_yssd, d_ypool = dy[:, :SSD_W], dy[:, SSD_W:]

        pairs = [_pair_fwd(q, pp, st_ref[0, pp]) for pp in range(4)]
        dsk = dsk_ref[...]
        ng = ng_ref[...]
        y_pre = jnp.concatenate([r["y"] for r in pairs], axis=1) + q["xs"] * dsk
        _, sz, stats = _gate_norm_fwd(y_pre, z, ng)

        half = SSD_W // 2
        d_yg, d_ng = [], []
        for gi in range(2):
            xhat, r = stats[gi]
            dx, dg = _rms_bwd(d_yssd[:, gi * half:(gi + 1) * half], xhat, r, ng[:, gi * half:(gi + 1) * half])
            d_yg.append(dx)
            d_ng.append(dg)
        d_yg = jnp.concatenate(d_yg, axis=1)
        dng_ref[...] += jnp.concatenate(d_ng, axis=1)
        silu_z = z * sz
        d_ypre = d_yg * silu_z
        d_z = d_yg * y_pre * (sz * (1.0 + z * (1.0 - sz)))
        ddsk_ref[...] += jnp.sum(d_ypre * q["xs"], axis=0, keepdims=True)

        d_xd, acs_cols, dt_cols = [], [], []
        d_b = [None, None]
        d_c = [None, None]
        d_g = [None, None]
        last_row = _iota2((CHUNK, LANES), 0) == CHUNK - 1
        for pp in range(4):
            g = pp // 2
            r = pairs[pp]
            lo = r["lo"]
            dyp = d_ypre[:, pp * LANES:(pp + 1) * LANES]
            dyp16 = dyp.astype(bf16)
            ds_out = dstate[pp]
            ds16 = ds_out.astype(bf16)
            dye16 = (dyp * r["e_pair"]).astype(bf16)
            dstate[pp] = r["cd_pair"] * ds_out + _dot_tn(r["c16"], dye16)
            dc = _dot_nt(dye16, r["s16"])
            dxdec = _dot(r["b16"], ds16)
            db = _dot_nt(r["xdec"].astype(bf16), ds16)
            dxp = dxdec * r["dec_pair"]
            t2 = dxdec * r["xdec"]
            tail = jnp.sum(t2, axis=0, keepdims=True) + jnp.sum(ds_out * st_ref[0, pp] * r["cd_pair"], axis=0, keepdims=True)
            rp = dyp * r["y_off"] - t2 + jnp.where(last_row, tail, 0.0)
            dxs = []
            for hh in range(2):
                msk = lo if hh == 0 else jnp.logical_not(lo)
                m16 = r["ms"][hh].astype(bf16)
                dxs.append(_dot_tn(m16, dyp16))
                dm = _dot_nt(jnp.where(msk, dyp, 0.0).astype(bf16), r["xp16"])
                wmat = dm * r["ms"][hh]
                acs_cols.append(jnp.sum(wmat - wmat.T + jnp.where(msk, rp, 0.0), axis=1, keepdims=True))
                dgh = dm * r["ls"][hh]
                d_g[g] = dgh if d_g[g] is None else d_g[g] + dgh
            dxp = dxp + jnp.where(lo, dxs[0], dxs[1])
            d_xd.append(dxp)
            xprod = dxp * q["xs"][:, pp * LANES:(pp + 1) * LANES]
            dt_cols.append(jnp.sum(jnp.where(lo, xprod, 0.0), axis=1, keepdims=True))
            dt_cols.append(jnp.sum(jnp.where(lo, 0.0, xprod), axis=1, keepdims=True))
            d_b[g] = db if d_b[g] is None else d_b[g] + db
            d_c[g] = dc if d_c[g] is None else d_c[g] + dc
        for g in range(2):
            dg16 = d_g[g].astype(bf16)
            d_c[g] = d_c[g] + _dot(dg16, pairs[2 * g]["b16"])
            d_b[g] = d_b[g] + _dot_tn(dg16, pairs[2 * g]["c16"])
        d_xd = jnp.concatenate(d_xd, axis=1)
        triu = (_iota2((CHUNK, CHUNK), 0) <= _iota2((CHUNK, CHUNK), 1)).astype(f32)
        d_a = _dot_hi(triu, _to_columns(acs_cols))
        d_dt = d_a * q["a_neg"] + _to_columns(dt_cols)
        dalog_ref[...] += jnp.sum(d_a * q["dt"], axis=0, keepdims=True) * q["a_neg"]
        d_dtr = d_dt * _sigmoid(dtr_ref[...] + dtb_ref[...])
        ddtr_ref[...] = d_dtr
        ddtb_ref[...] += jnp.sum(d_dtr, axis=0, keepdims=True)
        d_xs = d_ypre * dsk + d_xd * q["dt_w"]

        d_xc = jnp.concatenate([d_xs, d_b[0], d_b[1], d_c[0], d_c[1]], axis=1)
        sc = q["sig_c"]
        d_conv = d_xc * (sc * (1.0 + q["c"] * (1.0 - sc)))
        dcb_ref[...] += jnp.sum(d_conv, axis=0, keepdims=True)
        dxext[...] = jnp.zeros_like(dxext)
        for k in range(SSD_CONV):
            dcw_ref[k:k + 1, :] += jnp.sum(d_conv * xext[pl.ds(HALO - 3 + k, CHUNK), :], axis=0, keepdims=True)
            dxext[pl.ds(HALO - 3 + k, CHUNK), :] += cw[k:k + 1] * d_conv
        dxext[pl.ds(CHUNK, HALO), :] += cx[...]
        cx[...] = dxext[0:HALO, :]

        ps = ps_ref[...]
        u = zx[:, SSD_W + XBC:]
        _, pooled, mixed, dens = _pool_fwd(uext, u, ci * CHUNK, pw_ref, ps)
        dps_ref[...] += jnp.sum(d_ypool * mixed, axis=0, keepdims=True)
        d_mixed = d_ypool * ps
        duext[...] = jnp.zeros_like(duext)
        for gi, w in enumerate(WINDOWS):
            sl = slice(gi * POOL_G, (gi + 1) * POOL_G)
            dm16 = d_mixed[:, sl].astype(bf16)
            dpw_ref[gi] += _dot_tn(pooled[gi].astype(bf16), dm16)
            d_pg = _dot_nt(dm16, pw_ref[gi])
            d_mean = d_pg / dens[gi]
            duext[pl.ds(HALO, CHUNK), sl] += d_mean - d_pg
            for j in range(1, w):
                duext[pl.ds(HALO - j, CHUNK), sl] += d_mean
        duext[pl.ds(CHUNK, HALO), :] += cu[...]
        cu[...] = duext[0:HALO, :]

        dzx_ref[:, :SSD_W] = d_z.astype(bf16)
        dzx_ref[:, SSD_W:SSD_W + XBC] = dxext[HALO:, :].astype(bf16)
        dzx_ref[:, SSD_W + XBC:] = duext[HALO:, :].astype(bf16)

    small = lambda shape: pl.BlockSpec(shape, lambda i: (0,) * len(shape))
    small_shapes = [(SSD_CONV, XBC), (1, XBC), (1, LANES), (1, LANES), (1, SSD_W), (1, SSD_W), (4, POOL_G, POOL_G), (1, POOL_W)]
    outs = pl.pallas_call(
        body, name="ssd_pool_bwd_exchange" if nx else "ssd_pool_bwd", grid=(nc,),
        out_shape=(jax.ShapeDtypeStruct((s, 2048), bf16), jax.ShapeDtypeStruct((s, LANES), f32))
        + tuple(jax.ShapeDtypeStruct(sh, f32) for sh in small_shapes)
        + tuple(jax.ShapeDtypeStruct((3,) + a.shape[1:], a.dtype) for a in exchange),
        in_specs=[pl.BlockSpec((CHUNK, D_MODEL), lambda i: (rev(i), 0)),
                  pl.BlockSpec((CHUNK, 2048), lambda i: (rev(i), 0)),
                  pl.BlockSpec((HALO, 2048), lambda i: (jnp.maximum(rev(i) * hb - 1, 0), 0)),
                  pl.BlockSpec((CHUNK, LANES), lambda i: (rev(i), 0)),
                  pl.BlockSpec((1, 4, NSTATE, LANES), lambda i: (rev(i), 0, 0, 0))] + _ssd_param_specs() + [_hbm()] * nx,
        out_specs=(pl.BlockSpec((CHUNK, 2048), lambda i: (rev(i), 0)), pl.BlockSpec((CHUNK, LANES), lambda i: (rev(i), 0)))
        + tuple(small(sh) for sh in small_shapes) + (_hbm(),) * nx,
        scratch_shapes=[pltpu.VMEM((4, NSTATE, LANES), f32), pltpu.VMEM((HALO + CHUNK, XBC), f32),
                        pltpu.VMEM((HALO + CHUNK, POOL_W), f32), pltpu.VMEM((HALO + CHUNK, XBC), f32),
                        pltpu.VMEM((HALO + CHUNK, POOL_W), f32), pltpu.VMEM((HALO, XBC), f32), pltpu.VMEM((HALO, POOL_W), f32)]
        + ([pltpu.SemaphoreType.DMA((3 * nx,)), pltpu.SemaphoreType.DMA((3 * nx,))] if nx else []),
        compiler_params=_cparams(),
    )(d_ymix, zxu, zxu, dtr, states, *prm, *exchange)
    return outs[:10] + (list(outs[10:]),)


FFN_BWD_TILE = 256


def _prev_halo_spec(t, width):
    hb = t // HALO
    return pl.BlockSpec((HALO, width), lambda i: (jnp.maximum(i * hb - 1, 0), 0))


def _ffn_half(hn16, j, wup_ref, cw_ref, cb_ref, up_scr, rows):
    up_scr[...] = _dot(hn16, wup_ref[j])
    cw = cw_ref[j]
    cv = cb_ref[j] + cw[0:1] * up_scr[pl.ds(HALO - 2, rows), :]
    for k in range(1, FFN_CONV):
        cv = cv + cw[k:k + 1] * up_scr[pl.ds(HALO - 2 + k, rows), :]
    return cv


def _out_ffn_ple_fwd(h, ymix, p_l, w_out, g2, wup_h, cw_h, cb_h, wdn_h, g3, w_gate, w_proj, gather=()):
    s = h.shape[0]
    t = _row_tile(s)
    n = s // t
    ng = len(gather)

    def body(*refs):
        (h_ref, hh_ref, ym_ref, ymh_ref, p_ref, wo_ref, g2_ref, wup_ref, cw_ref, cb_ref, wdn_ref, g3_ref, wg_ref,
         wp_ref) = refs[:14]
        x_refs = refs[14:14 + ng]
        h1_ref, h2_ref, h3_ref, up16_ref = refs[14 + ng:18 + ng]
        land_refs = refs[18 + ng:18 + 2 * ng]
        up_scr = refs[18 + 2 * ng]
        sems = refs[19 + 2 * ng:]
        i = pl.program_id(0)
        if ng:
            @pl.when(i == 0)
            def _():
                _ag_start(x_refs, land_refs, *sems)

            @pl.when(i == n - 1)
            def _():
                _ag_finish(x_refs, land_refs, *sems)

        hh = jnp.where(i > 0, hh_ref[...], 0.0)
        ymh = jnp.where(i > 0, ymh_ref[...].astype(f32), 0.0)
        h_ext = jnp.concatenate([hh, h_ref[...]], axis=0)
        ym_ext = jnp.concatenate([ymh, ym_ref[...].astype(f32)], axis=0).astype(bf16)
        h1_ext = h_ext + _dot(ym_ext, wo_ref[...])
        hn2, _, _ = _rms_fwd(h1_ext, g2_ref[...])
        hn16 = hn2.astype(bf16)
        h1 = h1_ext[HALO:, :]
        acc = h1
        for j in range(2):
            cv = _ffn_half(hn16, j, wup_ref, cw_ref, cb_ref, up_scr, t)
            up16_ref[j] = up_scr[pl.ds(HALO, t), :].astype(bf16)
            act = _gelu(cv[:, :FF_HALF]) * cv[:, FF_HALF:]
            acc = acc + _dot(act.astype(bf16), wdn_ref[j])
        h2 = acc
        hn3, _, _ = _rms_fwd(h2, g3_ref[...])
        gate = _sigmoid(_dot(hn3.astype(bf16), wg_ref[...]))
        pp = _dot(p_ref[0].astype(bf16), wp_ref[...])
        h1_ref[...] = h1
        h2_ref[...] = h2
        h3_ref[...] = h2 + pp * gate

    row = lambda w: pl.BlockSpec((t, w), lambda i: (i, 0))
    outs = pl.pallas_call(
        body, name="out_ffn_ple_fwd_gather" if ng else "out_ffn_ple_fwd", grid=(n,),
        out_shape=tuple(jax.ShapeDtypeStruct((s, D_MODEL), f32) for _ in range(3)) + (jax.ShapeDtypeStruct((2, s, D_FF), bf16),)
        + tuple(jax.ShapeDtypeStruct((N_CHIPS,) + a.shape, a.dtype) for a in gather),
        in_specs=[row(D_MODEL), _prev_halo_spec(t, D_MODEL), row(D_MODEL), _prev_halo_spec(t, D_MODEL),
                  pl.BlockSpec((1, t, D_PLE), lambda i: (p_l[1], i, 0)),
                  _const_spec((D_MODEL, D_MODEL)), _const_spec((1, D_MODEL)), _const_spec((2, D_MODEL, D_FF)),
                  _const_spec((2, FFN_CONV, D_FF)), _const_spec((2, 1, D_FF)), _const_spec((2, FF_HALF, D_MODEL)),
                  _const_spec((1, D_MODEL)), _const_spec((D_MODEL, D_MODEL)), _const_spec((D_PLE, D_MODEL))] + [_hbm()] * ng,
        out_specs=tuple(row(D_MODEL) for _ in range(3)) + (pl.BlockSpec((2, t, D_FF), lambda i: (0, i, 0)),) + (_hbm(),) * ng,
        scratch_shapes=[pltpu.VMEM((HALO + t, D_FF), f32)]
        + ([pltpu.SemaphoreType.DMA((6 * ng,)), pltpu.SemaphoreType.DMA((6 * ng,))] if ng else []),
        compiler_params=_cparams(),
    )(h, h, ymix, ymix, p_l[0], w_out, g2, wup_h, cw_h, cb_h, wdn_h, g3, w_gate, w_proj, *gather)
    return outs[:4] + (_own_block_in(outs[4:], gather),)


def _ple_bwd(dh3, h2, p_l, g3, w_gate, w_proj, exchange=()):
    s = h2.shape[0]
    t = _row_tile(s, WIDE_ROW_TILE)
    n = s // t
    nx = len(exchange)

    def body(*refs):
        dh3_ref, h2_ref, p_ref, g3_ref, wg_ref, wp_ref = refs[:6]
        g_refs = refs[6:6 + nx]
        dh2_ref, dwg_ref, dwp_ref, dg3_ref = refs[6 + nx:10 + nx]
        land_refs = refs[10 + nx:10 + 2 * nx]
        sems = refs[10 + 2 * nx:]
        i = pl.program_id(0)
        if nx:
            @pl.when(i == 0)
            def _():
                for cp in _px_copies(g_refs, land_refs, *sems):
                    cp.start()

            @pl.when(i == n - 1)
            def _():
                for cp in _px_copies(g_refs, land_refs, *sems):
                    cp.wait()

        @pl.when(i == 0)
        def _():
            dwg_ref[...] = jnp.zeros_like(dwg_ref)
            dwp_ref[...] = jnp.zeros_like(dwp_ref)
            dg3_ref[...] = jnp.zeros_like(dg3_ref)

        g3 = g3_ref[...]
        dh3 = dh3_ref[...]
        hn3, xhat, r = _rms_fwd(h2_ref[...], g3)
        hn16 = hn3.astype(bf16)
        gate = _sigmoid(_dot(hn16, wg_ref[...]))
        p16 = p_ref[0].astype(bf16)
        pp = _dot(p16, wp_ref[...])
        d_pp = (dh3 * gate).astype(bf16)
        d_pre = (dh3 * pp * gate * (1.0 - gate)).astype(bf16)
        dwp_ref[...] += _dot_tn(p16, d_pp)
        dwg_ref[...] += _dot_tn(hn16, d_pre)
        dx, dg = _rms_bwd(_dot_nt(d_pre, wg_ref[...]), xhat, r, g3)
        dg3_ref[...] += dg
        dh2_ref[...] = dh3 + dx

    row = lambda w: pl.BlockSpec((t, w), lambda i: (i, 0))
    fixed = lambda shape: pl.BlockSpec(shape, lambda i: (0,) * len(shape))
    outs = pl.pallas_call(
        body, name="ple_bwd_exchange" if nx else "ple_bwd", grid=(n,),
        out_shape=(jax.ShapeDtypeStruct((s, D_MODEL), f32), jax.ShapeDtypeStruct((D_MODEL, D_MODEL), f32),
                   jax.ShapeDtypeStruct((D_PLE, D_MODEL), f32), jax.ShapeDtypeStruct((1, D_MODEL), f32))
        + tuple(jax.ShapeDtypeStruct((N_CHIPS, g.shape[1] // 2, g.shape[2]), g.dtype) for g in exchange),
        in_specs=[row(D_MODEL), row(D_MODEL), pl.BlockSpec((1, t, D_PLE), lambda i: (p_l[1], i, 0)), _const_spec((1, D_MODEL)),
                  _const_spec((D_MODEL, D_MODEL)), _const_spec((D_PLE, D_MODEL))] + [_hbm()] * nx,
        out_specs=(row(D_MODEL), fixed((D_MODEL, D_MODEL)), fixed((D_PLE, D_MODEL)), fixed((1, D_MODEL))) + (_hbm(),) * nx,
        scratch_shapes=[pltpu.SemaphoreType.DMA((nx,)), pltpu.SemaphoreType.DMA((nx,))] if nx else [],
        compiler_params=_cparams(),
    )(dh3, h2, p_l[0], g3, w_gate, w_proj, *exchange)
    return outs[:4] + (list(outs[4:]),)


def _ffn_bwd(dh2, h1, up16, g2, wup_h, cw_h, cb_h, wdn_h, exchange=()):
    s = h1.shape[0]
    t = min(FFN_BWD_TILE, s)
    n = s // t
    hb = t // HALO
    last_hb = s // HALO - 1
    nx = len(exchange)

    def body(*refs):
        dh2_ref, dh2n_ref, h1_ref, up_ref, upp_ref, upn_ref, g2_ref, wup_ref, cw_ref, cb_ref, wdn_ref = refs[:11]
        p_refs = refs[11:11 + nx]
        part_ref, dwup_ref, dwdn_ref, dcw_ref, dcb_ref = refs[11 + nx:16 + nx]
        land_refs = refs[16 + nx:16 + 2 * nx]
        up_scr, dcv_scr, acc_up, acc_dn = refs[16 + 2 * nx:20 + 2 * nx]
        sems = refs[20 + 2 * nx:]
        j = pl.program_id(0)
        i = pl.program_id(1)
        if nx:
            @pl.when(jnp.logical_and(j == 0, i == 0))
            def _():
                _cx_start(p_refs, land_refs, *sems)

            @pl.when(jnp.logical_and(j == 1, i == n - 1))
            def _():
                _cx_finish(p_refs, land_refs, *sems)

        @pl.when(i == 0)
        def _():
            acc_up[...] = jnp.zeros_like(acc_up)
            acc_dn[...] = jnp.zeros_like(acc_dn)
            dcw_ref[...] = jnp.zeros_like(dcw_ref)
            dcb_ref[...] = jnp.zeros_like(dcb_ref)

        hn2, _, _ = _rms_fwd(h1_ref[...], g2_ref[...])
        hn16 = hn2.astype(bf16)
        up_scr[0:HALO, :] = jnp.where(i > 0, upp_ref[0].astype(f32), 0.0)
        up_scr[HALO:HALO + t, :] = up_ref[0].astype(f32)
        up_scr[HALO + t:, :] = upn_ref[0].astype(f32)
        cw = cw_ref[0]
        rows = t + HALO
        cv = cb_ref[0] + cw[0:1] * up_scr[pl.ds(HALO - 2, rows), :]
        for k in range(1, FFN_CONV):
            cv = cv + cw[k:k + 1] * up_scr[pl.ds(HALO - 2 + k, rows), :]
        dh2 = dh2_ref[...]
        dh2n = jnp.where(i < n - 1, dh2n_ref[...], 0.0)
        dh2_ext16 = jnp.concatenate([dh2, dh2n], axis=0).astype(bf16)
        d_act = _dot_nt(dh2_ext16, wdn_ref[0])
        gate, val = cv[:, :FF_HALF], cv[:, FF_HALF:]
        gl, dgl = _gelu_and_grad(gate)
        dcv_scr[:, :FF_HALF] = d_act * val * dgl
        dcv_scr[:, FF_HALF:] = d_act * gl
        act16 = (gl[:t] * val[:t]).astype(bf16)
        acc_dn[...] += _dot_tn(act16, dh2_ext16[:t])
        d_cv = dcv_scr[pl.ds(0, t), :]
        dcb_ref[0] += jnp.sum(d_cv, axis=0, keepdims=True)
        d_up = cw[2:3] * d_cv
        dcw_ref[0, 2:3, :] += jnp.sum(d_cv * up_scr[pl.ds(HALO, t), :], axis=0, keepdims=True)
        for k in range(FFN_CONV - 1):
            dcw_ref[0, k:k + 1, :] += jnp.sum(d_cv * up_scr[pl.ds(HALO - 2 + k, t), :], axis=0, keepdims=True)
            d_up = d_up + cw[k:k + 1] * dcv_scr[pl.ds(2 - k, t), :]
        d_up16 = d_up.astype(bf16)
        part_ref[0] = _dot_nt(d_up16, wup_ref[0])
        acc_up[...] += _dot_tn(hn16, d_up16)

        @pl.when(i == n - 1)
        def _():
            pltpu.sync_copy(acc_up.at[:, pl.ds(0, FF_HALF)], dwup_ref.at[j])
            pltpu.sync_copy(acc_up.at[:, pl.ds(FF_HALF, FF_HALF)], dwup_ref.at[2 + j])
            pltpu.sync_copy(acc_dn, dwdn_ref.at[j])

    outs = pl.pallas_call(
        body, name="ffn_bwd_exchange" if nx else "ffn_bwd", grid=(2, n),
        out_shape=(jax.ShapeDtypeStruct((2, s, D_MODEL), f32), jax.ShapeDtypeStruct((N_CHIPS, D_MODEL, FF_HALF), f32),
                   jax.ShapeDtypeStruct((2, FF_HALF, D_MODEL), f32), jax.ShapeDtypeStruct((2, FFN_CONV, D_FF), f32),
                   jax.ShapeDtypeStruct((2, 1, D_FF), f32))
        + tuple(jax.ShapeDtypeStruct((3,) + a.shape[1:], a.dtype) for a in exchange),
        in_specs=[pl.BlockSpec((t, D_MODEL), lambda j, i: (i, 0)),
                  pl.BlockSpec((HALO, D_MODEL), lambda j, i: (jnp.minimum((i + 1) * hb, last_hb), 0)),
                  pl.BlockSpec((t, D_MODEL), lambda j, i: (i, 0)),
                  pl.BlockSpec((1, t, D_FF), lambda j, i: (j, i, 0)),
                  pl.BlockSpec((1, HALO, D_FF), lambda j, i: (j, jnp.maximum(i * hb - 1, 0), 0)),
                  pl.BlockSpec((1, HALO, D_FF), lambda j, i: (j, jnp.minimum((i + 1) * hb, last_hb), 0)),
                  _const_spec((1, D_MODEL)),
                  pl.BlockSpec((1, D_MODEL, D_FF), lambda j, i: (j, 0, 0), pipeline_mode=pl.Buffered(1)),
                  pl.BlockSpec((1, FFN_CONV, D_FF), lambda j, i: (j, 0, 0)),
                  pl.BlockSpec((1, 1, D_FF), lambda j, i: (j, 0, 0)),
                  pl.BlockSpec((1, FF_HALF, D_MODEL), lambda j, i: (j, 0, 0), pipeline_mode=pl.Buffered(1))] + [_hbm()] * nx,
        out_specs=(pl.BlockSpec((1, t, D_MODEL), lambda j, i: (j, i, 0)), pl.BlockSpec(memory_space=pl.ANY),
                   pl.BlockSpec(memory_space=pl.ANY), pl.BlockSpec((1, FFN_CONV, D_FF), lambda j, i: (j, 0, 0)),
                   pl.BlockSpec((1, 1, D_FF), lambda j, i: (j, 0, 0))) + (_hbm(),) * nx,
        scratch_shapes=[pltpu.VMEM((2 * HALO + t, D_FF), f32), pltpu.VMEM((HALO + t, D_FF), f32),
                        pltpu.VMEM((D_MODEL, D_FF), f32), pltpu.VMEM((FF_HALF, D_MODEL), f32)]
        + ([pltpu.SemaphoreType.DMA((3 * nx,)), pltpu.SemaphoreType.DMA((3 * nx,))] if nx else []),
        compiler_params=_cparams(("arbitrary", "arbitrary")),
    )(dh2, dh2, h1, up16, up16, up16, g2, wup_h, cw_h, cb_h, wdn_h, *exchange)
    return outs[:5] + (list(outs[5:]),)


def _out_bwd(dh2, parts, h1, ymix, g2, w_out, share=()):
    s = h1.shape[0]
    t = _row_tile(s, WIDE_ROW_TILE)
    n = s // t
    ns = len(share)

    def body(*refs):
        dh2_ref, part_ref, h1_ref, ym_ref, g2_ref, wo_ref = refs[:6]
        r_refs = refs[6:6 + ns]
        dh1_ref, dym_ref, dwo_ref, dg2_ref = refs[6 + ns:10 + ns]
        both_refs = refs[10 + ns:10 + 2 * ns]
        sems = refs[10 + 2 * ns:]
        i = pl.program_id(0)
        if ns:
            @pl.when(i == 0)
            def _():
                _ps_start(r_refs, both_refs, *sems)

            @pl.when(i == n - 1)
            def _():
                _ps_finish(r_refs, both_refs, *sems)

        @pl.when(i == 0)
        def _():
            dwo_ref[...] = jnp.zeros_like(dwo_ref)
            dg2_ref[...] = jnp.zeros_like(dg2_ref)

        g2 = g2_ref[...]
        _, xhat, r = _rms_fwd(h1_ref[...], g2)
        dx, dg = _rms_bwd(part_ref[0] + part_ref[1], xhat, r, g2)
        dg2_ref[...] += dg
        dh1 = dh2_ref[...] + dx
        dh1_ref[...] = dh1
        dh16 = dh1.astype(bf16)
        dym_ref[...] = _dot_nt(dh16, wo_ref[...]).astype(bf16)
        dwo_ref[...] += _dot_tn(ym_ref[...], dh16)

    row = lambda w: pl.BlockSpec((t, w), lambda i: (i, 0))
    fixed = lambda shape: pl.BlockSpec(shape, lambda i: (0,) * len(shape))
    outs = pl.pallas_call(
        body, name="out_bwd_share" if ns else "out_bwd", grid=(n,),
        out_shape=(jax.ShapeDtypeStruct((s, D_MODEL), f32), jax.ShapeDtypeStruct((s, D_MODEL), bf16),
                   jax.ShapeDtypeStruct((D_MODEL, D_MODEL), f32), jax.ShapeDtypeStruct((1, D_MODEL), f32))
        + tuple(jax.ShapeDtypeStruct((2 * r.shape[0], r.shape[1]), r.dtype) for r in share),
        in_specs=[row(D_MODEL), pl.BlockSpec((2, t, D_MODEL), lambda i: (0, i, 0)), row(D_MODEL), row(D_MODEL),
                  _const_spec((1, D_MODEL)), _const_spec((D_MODEL, D_MODEL))] + [_hbm()] * ns,
        out_specs=(row(D_MODEL), row(D_MODEL), fixed((D_MODEL, D_MODEL)), fixed((1, D_MODEL))) + (_hbm(),) * ns,
        scratch_shapes=[pltpu.SemaphoreType.DMA((ns,)), pltpu.SemaphoreType.DMA((ns,))] if ns else [],
        compiler_params=_cparams(),
    )(dh2, parts, h1, ymix, g2, w_out, *share)
    return outs[:4] + (_own_half_in(outs[4:], share),)


def _loss_head(h, target, gf):
    s = h.shape[0]
    t = _row_tile(s, WIDE_ROW_TILE)

    def body(h_ref, t_ref, g_ref, dh_ref, dg_ref, loss_ref):
        i = pl.program_id(0)

        @pl.when(i == 0)
        def _():
            dg_ref[...] = jnp.zeros_like(dg_ref)
            loss_ref[...] = jnp.zeros_like(loss_ref)

        g = g_ref[...]
        y, xhat, r = _rms_fwd(h_ref[...], g)
        diff = y - t_ref[...]
        per_row = jnp.mean(diff * diff, axis=-1, keepdims=True)
        loss_ref[...] += 0.5 * jnp.sum(per_row, axis=0, keepdims=True)
        dx, dg = _rms_bwd(diff * (1.0 / D_MODEL), xhat, r, g)
        dg_ref[...] += dg
        dh_ref[...] = dx

    row = pl.BlockSpec((t, D_MODEL), lambda i: (i, 0))
    return pl.pallas_call(
        body, name="loss_head", grid=(s // t,),
        out_shape=(jax.ShapeDtypeStruct((s, D_MODEL), f32), jax.ShapeDtypeStruct((1, D_MODEL), f32),
                   jax.ShapeDtypeStruct((1, LANES), f32)),
        in_specs=[row, row, _const_spec((1, D_MODEL))],
        out_specs=(row, pl.BlockSpec((1, D_MODEL), lambda i: (0, 0)), pl.BlockSpec((1, LANES), lambda i: (0, 0))),
        compiler_params=_cparams(),
    )(h, target, gf)


def _prep_layer(w):
    in4 = w["w_in4"]
    cut = D_IN // N_CHIPS
    tail = 3 * cut - (SSD_W + XBC)
    w_r = jnp.concatenate([in4[0], in4[1], in4[2][:, :cut - tail], in4[3][:, HEADS - tail:], in4[2][:, cut - tail:],
                           in4[3][:, :HEADS - tail], jnp.zeros((D_MODEL, LANES - HEADS), in4.dtype)], axis=1)
    pad8 = lambda v: jnp.concatenate([v, jnp.zeros((LANES - HEADS,), f32)]).reshape(1, LANES)
    halves = _halves
    ssd_prm = (w["ssd_conv_w"], w["ssd_conv_b"].reshape(1, XBC), pad8(w["ssd_dt_bias"]), pad8(w["ssd_a_log"]),
               jnp.repeat(w["ssd_d"], HEAD_DIM).reshape(1, SSD_W), w["ssd_norm_g"].reshape(1, SSD_W),
               w["pool_w"], w["pool_scale"].reshape(1, POOL_W))
    return dict(
        g1=w["mix_norm_g"].reshape(1, D_MODEL), w_r=w_r, ssd=ssd_prm, w_out=w["w_out"], g2=w["ffn_norm_g"].reshape(1, D_MODEL),
        wup_h=w["wup_h"], cw_h=halves(w["ffn_conv_w"]), cb_h=halves(w["ffn_conv_b"].reshape(1, 2 * D_FF)),
        wdn_h=w["wdn_h"], g3=w["ple_norm_g"].reshape(1, D_MODEL), w_gate=w["ple_w_gate"], w_proj=w["ple_w_proj"])


def _halves(a):
    return jnp.stack([jnp.concatenate([a[..., j * FF_HALF:(j + 1) * FF_HALF],
                                       a[..., D_FF + j * FF_HALF:D_FF + (j + 1) * FF_HALF]], axis=-1) for j in range(2)])


def _unhalve(a):
    return jnp.concatenate([a[0][..., :FF_HALF], a[1][..., :FF_HALF], a[0][..., FF_HALF:], a[1][..., FF_HALF:]], axis=-1)


def _layer_fwd(h, p_l, q, gather=()):
    zxu, dtr = _mix_in_fwd(h, q["g1"], q["w_r"])
    ymix, states = _ssd_pool_fwd(zxu, dtr, q["ssd"])
    h1, h2, h3, up16, gathered = _out_ffn_ple_fwd(h, ymix, p_l, q["w_out"], q["g2"], q["wup_h"], q["cw_h"], q["cb_h"],
                                                   q["wdn_h"], q["g3"], q["w_gate"], q["w_proj"], gather)
    return h3, (h, zxu, dtr, ymix, states, h1, h2, up16), gathered


def _layer_bwd(dh, saved, p_l, q, reduce=(), early=False):
    h0, zxu, dtr, ymix, states, h1, h2, up16 = saved
    dh2, d_wg, d_wp, d_g3, lands = _ple_bwd(dh, h2, p_l, q["g3"], q["w_gate"], q["w_proj"], reduce)
    sums = _rs_pair_sums(reduce, lands) if reduce else ()
    parts, d_wup, d_wdn, d_cw, d_cb, lands = _ffn_bwd(dh2, h1, up16, q["g2"], q["wup_h"], q["cw_h"], q["cb_h"], q["wdn_h"],
                                                      sums)
    halves = _rs_chip_sums(sums, lands) if reduce else ()
    dh1, d_ymix, d_wo, d_g2, reduced = _out_bwd(dh2, parts, h1, ymix, q["g2"], q["w_out"], halves)
    early_sums = ()
    if early:
        own = [d_wup, d_wdn.reshape(N_CHIPS, D_FF // N_CHIPS, D_MODEL), _chip_major(d_wg, 0), _chip_major(d_wp, 1)]
        early_sums = _rs_pair_sums(own, _rs_pair_exchange(own))
    (d_zxu, d_dtr, d_scw, d_scb, d_dtb, d_alog, d_dsk, d_ng, d_pw, d_ps, early_lands) = _ssd_pool_bwd(
        d_ymix, zxu, dtr, states, q["ssd"], early_sums)
    dh, d_wr, d_g1 = _mix_in_bwd(d_zxu, d_dtr, h0, dh1, q["g1"], q["w_r"])
    cut = D_IN // N_CHIPS
    tail = 3 * cut - (SSD_W + XBC)
    w_in4 = jnp.stack([d_wr[:, :cut], d_wr[:, cut:2 * cut],
                       jnp.concatenate([d_wr[:, 2 * cut:SSD_W + XBC], d_wr[:, 2048:2048 + tail]], axis=1),
                       jnp.concatenate([d_wr[:, 2048 + tail:2048 + HEADS], d_wr[:, SSD_W + XBC:2048]], axis=1)])
    grads = dict(
        mix_norm_g=d_g1.reshape(D_MODEL), w_in4=w_in4,
        ssd_conv_w=d_scw, ssd_conv_b=d_scb.reshape(XBC), ssd_dt_bias=d_dtb[0, :HEADS], ssd_a_log=d_alog[0, :HEADS],
        ssd_d=jnp.sum(d_dsk.reshape(HEADS, HEAD_DIM), axis=1), ssd_norm_g=d_ng.reshape(SSD_W), pool_w=d_pw,
        pool_scale=d_ps.reshape(POOL_W), w_out=d_wo, ffn_norm_g=d_g2.reshape(D_MODEL), ffn_w_up4=d_wup,
        ffn_conv_w=_unhalve(d_cw), ffn_conv_b=_unhalve(d_cb).reshape(2 * D_FF), wdn_h=d_wdn,
        ple_norm_g=d_g3.reshape(D_MODEL), ple_w_gate=d_wg, ple_w_proj=d_wp)
    if early:
        grads["early"] = (early_sums, early_lands)
    return dh, grads, reduced


def _device_step(x, p, target, layers, final_g):
    preps = [_prep_layer(w) for w in layers]
    saved = []
    h = x
    for l, q in enumerate(preps):
        h, keep, _ = _layer_fwd(h, (p, l), q)
        saved.append(keep)
    dh, d_gf, loss = _loss_head(h, target, final_g.reshape(1, D_MODEL))
    grads = [None] * len(preps)
    for l in reversed(range(len(preps))):
        dh, grads[l], _ = _layer_bwd(dh, saved[l], (p, l), preps[l])
    return loss, dh, grads, d_gf.reshape(D_MODEL)


MESH = pl.DeviceIdType.MESH
COLS = 1024
N_CHIPS = 4
WEIGHT_NAMES = ["mix_norm_g", "w_in", "ssd_conv_w", "ssd_conv_b", "ssd_dt_bias", "ssd_a_log", "ssd_d", "ssd_norm_g", "pool_w",
                "pool_scale", "w_out", "ffn_norm_g", "ffn_w_up", "ffn_conv_w", "ffn_conv_b", "ffn_w_down", "ple_norm_g",
                "ple_w_gate", "ple_w_proj", "final_norm_g"]
SHARD_AXIS = {"w_in": 2, "ssd_conv_w": 2, "w_out": 1, "ffn_w_up": 2, "ffn_conv_w": 2, "ffn_w_down": 1, "ple_w_gate": 1,
              "ple_w_proj": 2}
MATMUL_SHARDED = ["w_in", "w_out", "ffn_w_up", "ffn_w_down", "ple_w_gate", "ple_w_proj"]
F32_SHARDED = ["ssd_conv_w", "ffn_conv_w"]
REPLICATED = [n for n in WEIGHT_NAMES if n not in SHARD_AXIS]


def _mesh_pos():
    return lax.axis_index("x"), lax.axis_index("y"), lax.axis_index("c")


def _hbm():
    return pl.BlockSpec(memory_space=pl.ANY)


def _all_gather_chips(arrays, name):
    n = len(arrays)

    def body(*refs):
        x_refs, out_refs, sems = refs[:n], refs[n:2 * n], refs[2 * n:]
        _ag_start(x_refs, out_refs, *sems)
        _ag_finish(x_refs, out_refs, *sems)

    gathered = pl.pallas_call(
        body, name=name, out_shape=[jax.ShapeDtypeStruct((N_CHIPS,) + a.shape, a.dtype) for a in arrays],
        in_specs=[_hbm()] * n, out_specs=[_hbm()] * n,
        scratch_shapes=[pltpu.SemaphoreType.DMA((6 * n,)), pltpu.SemaphoreType.DMA((6 * n,))],
    )(*arrays)
    return _own_block_in(gathered, arrays)


def _ag_copies(x_refs, out_refs, send_sems, recv_sems):
    x, y, c = _mesh_pos()
    me = 2 * x + y
    flips = [(1 - x, y), (x, 1 - y), (1 - x, 1 - y)]

    def rows(a, chip, half):
        hr = x_refs[a].shape[0] // 2
        return out_refs[a].at[chip, pl.ds(half * hr, hr), :]

    def copy(k, src, dst, to):
        return pltpu.make_async_remote_copy(src_ref=src, dst_ref=dst, send_sem=send_sems.at[k], recv_sem=recv_sems.at[k],
                                            device_id=to, device_id_type=MESH)

    def first(a, j):
        hr = x_refs[a].shape[0] // 2
        return copy(6 * a + j, x_refs[a].at[pl.ds(c * hr, hr), :], rows(a, me, c), flips[j] + (c,))

    def landed(a, j):
        blk = rows(a, 2 * flips[j][0] + flips[j][1], c)
        return copy(6 * a + j, blk, blk, flips[j] + (c,))

    def passed(a, j, half):
        blk = rows(a, 2 * flips[j][0] + flips[j][1], half)
        return copy(6 * a + 3 + j, blk, blk, (x, y, 1 - c))

    return first, landed, passed, c


def _ag_start(x_refs, out_refs, send_sems, recv_sems):
    first, _, _, _ = _ag_copies(x_refs, out_refs, send_sems, recv_sems)
    for a in range(len(x_refs)):
        for j in range(3):
            first(a, j).start()


def _ag_finish(x_refs, out_refs, send_sems, recv_sems):
    first, landed, passed, c = _ag_copies(x_refs, out_refs, send_sems, recv_sems)
    n = len(x_refs)
    for j in range(3):
        for a in range(n):
            landed(a, j).wait_recv()
            passed(a, j, c).start()
    for j in range(3):
        for a in range(n):
            passed(a, j, 1 - c).wait_recv()
    for a in range(n):
        for j in range(3):
            first(a, j).wait_send()
            passed(a, j, c).wait_send()


def _own_block_in(gathered, arrays):
    if not arrays:
        return []
    me = 2 * lax.axis_index("x") + lax.axis_index("y")
    return [lax.dynamic_update_slice(o, a[None], (me, 0, 0)) for o, a in zip(gathered, arrays)]


def _rs_pair_exchange(gs):
    n = len(gs)

    def body(*refs):
        g_refs, land_refs, sems = refs[:n], refs[n:2 * n], refs[2 * n:]
        for cp in _px_copies(g_refs, land_refs, *sems):
            cp.start()
        for cp in _px_copies(g_refs, land_refs, *sems):
            cp.wait()

    return pl.pallas_call(
        body, name="rs_pair_exchange",
        out_shape=[jax.ShapeDtypeStruct((N_CHIPS, g.shape[1] // 2, g.shape[2]), g.dtype) for g in gs],
        in_specs=[_hbm()] * n, out_specs=[_hbm()] * n,
        scratch_shapes=[pltpu.SemaphoreType.DMA((n,)), pltpu.SemaphoreType.DMA((n,))],
    )(*gs)


def _px_copies(g_refs, land_refs, send_sems, recv_sems):
    x, y, c = _mesh_pos()
    cps = []
    for a in range(len(g_refs)):
        hr = g_refs[a].shape[1] // 2
        cps.append(pltpu.make_async_remote_copy(
            src_ref=g_refs[a].at[:, pl.ds((1 - c) * hr, hr), :], dst_ref=land_refs[a], send_sem=send_sems.at[a],
            recv_sem=recv_sems.at[a], device_id=(x, y, 1 - c), device_id_type=MESH))
    return cps


def _rs_pair_add(g, land, c_idx):
    _, r, cols = g.shape
    hr = r // 2

    def body(c_ref, g_ref, l_ref, o_ref):
        o_ref[...] = (g_ref[...] + l_ref[...]).astype(bf16)

    return pl.pallas_call(
        body, name="rs_pair_add", out_shape=jax.ShapeDtypeStruct((N_CHIPS, hr, cols), bf16),
        grid_spec=pltpu.PrefetchScalarGridSpec(
            num_scalar_prefetch=1, grid=(N_CHIPS,),
            in_specs=[pl.BlockSpec((1, hr, cols), lambda k, c_ref: (k, c_ref[0], 0)),
                      pl.BlockSpec((1, hr, cols), lambda k, c_ref: (k, 0, 0))],
            out_specs=pl.BlockSpec((1, hr, cols), lambda k, c_ref: (k, 0, 0))),
        compiler_params=_cparams(),
    )(c_idx, g, land)


def _rs_chip_exchange(parts):
    n = len(parts)

    def body(*refs):
        p_refs, land_refs, sems = refs[:n], refs[n:2 * n], refs[2 * n:]
        _cx_start(p_refs, land_refs, *sems)
        _cx_finish(p_refs, land_refs, *sems)

    return pl.pallas_call(
        body, name="rs_chip_exchange", out_shape=[jax.ShapeDtypeStruct((3,) + p.shape[1:], p.dtype) for p in parts],
        in_specs=[_hbm()] * n, out_specs=[_hbm()] * n,
        scratch_shapes=[pltpu.SemaphoreType.DMA((3 * n,)), pltpu.SemaphoreType.DMA((3 * n,))],
    )(*parts)


def _cx_copies(p_refs, land_refs, send_sems, recv_sems):
    x, y, c = _mesh_pos()
    flips = [(1 - x, y), (x, 1 - y), (1 - x, 1 - y)]
    return [pltpu.make_async_remote_copy(src_ref=p_refs[a].at[2 * fx + fy], dst_ref=land_refs[a].at[j],
                                         send_sem=send_sems.at[3 * a + j], recv_sem=recv_sems.at[3 * a + j],
                                         device_id=(fx, fy, c), device_id_type=MESH)
            for a in range(len(p_refs)) for j, (fx, fy) in enumerate(flips)]


def _cx_start(p_refs, land_refs, send_sems, recv_sems):
    for cp in _cx_copies(p_refs, land_refs, send_sems, recv_sems):
        cp.start()


def _cx_finish(p_refs, land_refs, send_sems, recv_sems):
    for cp in _cx_copies(p_refs, land_refs, send_sems, recv_sems):
        cp.wait()


def _rs_chip_add(part, land, me_idx):
    _, hr, cols = part.shape

    def body(me_ref, p_ref, l_ref, o_ref):
        o_ref[...] = ((p_ref[0].astype(f32) + l_ref[0].astype(f32)) + l_ref[1].astype(f32)) + l_ref[2].astype(f32)

    return pl.pallas_call(
        body, name="rs_chip_add", out_shape=jax.ShapeDtypeStruct((hr, cols), f32),
        grid_spec=pltpu.PrefetchScalarGridSpec(
            num_scalar_prefetch=1, grid=(1,),
            in_specs=[pl.BlockSpec((1, hr, cols), lambda i, me_ref: (me_ref[0], 0, 0)),
                      pl.BlockSpec((3, hr, cols), lambda i, me_ref: (0, 0, 0))],
            out_specs=pl.BlockSpec((hr, cols), lambda i, me_ref: (0, 0))),
        compiler_params=_cparams(),
    )(me_idx, part, land)


def _rs_pair_share(reds):
    n = len(reds)

    def body(*refs):
        r_refs, out_refs, sems = refs[:n], refs[n:2 * n], refs[2 * n:]
        _ps_start(r_refs, out_refs, *sems)
        _ps_finish(r_refs, out_refs, *sems)

    both = pl.pallas_call(
        body, name="rs_pair_share", out_shape=[jax.ShapeDtypeStruct((2 * r.shape[0], r.shape[1]), r.dtype) for r in reds],
        in_specs=[_hbm()] * n, out_specs=[_hbm()] * n,
        scratch_shapes=[pltpu.SemaphoreType.DMA((n,)), pltpu.SemaphoreType.DMA((n,))],
    )(*reds)
    return _own_half_in(both, reds)


def _ps_copy(r_refs, out_refs, send_sems, recv_sems, a, half):
    x, y, c = _mesh_pos()
    hr = r_refs[a].shape[0]
    return pltpu.make_async_remote_copy(src_ref=r_refs[a], dst_ref=out_refs[a].at[pl.ds(half * hr, hr), :],
                                        send_sem=send_sems.at[a], recv_sem=recv_sems.at[a], device_id=(x, y, 1 - c),
                                        device_id_type=MESH)


def _ps_start(r_refs, out_refs, send_sems, recv_sems):
    c = lax.axis_index("c")
    for a in range(len(r_refs)):
        _ps_copy(r_refs, out_refs, send_sems, recv_sems, a, c).start()


def _ps_finish(r_refs, out_refs, send_sems, recv_sems):
    c = lax.axis_index("c")
    for a in range(len(r_refs)):
        _ps_copy(r_refs, out_refs, send_sems, recv_sems, a, 1 - c).wait_recv()
    for a in range(len(r_refs)):
        _ps_copy(r_refs, out_refs, send_sems, recv_sems, a, c).wait_send()


def _own_half_in(both, reds):
    if not reds:
        return []
    c = lax.axis_index("c")
    return [lax.dynamic_update_slice(o, r, (c * r.shape[0], 0)) for o, r in zip(both, reds)]


def _rs_pair_sums(gs, lands):
    c_idx = jnp.reshape(lax.axis_index("c"), (1,)).astype(jnp.int32)
    return [_rs_pair_add(g, land, c_idx) for g, land in zip(gs, lands)]


def _rs_chip_sums(parts, lands):
    me_idx = jnp.reshape(2 * lax.axis_index("x") + lax.axis_index("y"), (1,)).astype(jnp.int32)
    return [_rs_chip_add(part, land, me_idx) for part, land in zip(parts, lands)]


def _reduce_scatter(gs):
    sums = _rs_pair_sums(gs, _rs_pair_exchange(gs))
    return _rs_pair_share(_rs_chip_sums(sums, _rs_chip_exchange(sums)))


def _pack(parts, row_multiple):
    flat = jnp.concatenate([a.reshape(-1) for a in parts])
    n = flat.shape[0]
    rows = -(-n // COLS)
    rows = -(-rows // row_multiple) * row_multiple
    return jnp.pad(flat, (0, rows * COLS - n)).reshape(rows, COLS)


def _unpack(flat, shapes):
    out, off = [], 0
    for shp in shapes:
        n = math.prod(shp)
        out.append(flat[off:off + n].reshape(shp))
        off += n
    return out


def _adamw(w, g, m, v, name):
    shape = w.shape
    cols = shape[-1]
    rows = math.prod(shape[:-1]) if len(shape) > 1 else 1
    tr = rows
    if rows > 512:
        tr = next(t for t in (512, 256, 128, 64, 32, 16, 8) if rows % t == 0)
    two_d = lambda a: a.reshape(rows, cols)

    def body(w_ref, g_ref, m_ref, v_ref, d_ref, nm_ref, nv_ref):
        gg = g_ref[...]
        nm = ADAM_B1 * m_ref[...] + (1.0 - ADAM_B1) * gg
        nv = ADAM_B2 * v_ref[...] + (1.0 - ADAM_B2) * (gg * gg)
        m_hat = nm / (1.0 - ADAM_B1 ** ADAM_STEP)
        v_hat = nv / (1.0 - ADAM_B2 ** ADAM_STEP)
        d_ref[...] = -ADAM_LR * (m_hat / (jnp.sqrt(v_hat) + ADAM_EPS) + ADAM_WD * w_ref[...])
        nm_ref[...] = nm
        nv_ref[...] = nv

    spec = pl.BlockSpec((tr, cols), lambda i: (i, 0))
    outs = pl.pallas_call(
        body, name="adamw_" + name, grid=(rows // tr,),
        out_shape=tuple(jax.ShapeDtypeStruct((rows, cols), f32) for _ in range(3)),
        in_specs=[spec] * 4, out_specs=(spec,) * 3, compiler_params=_cparams(),
    )(two_d(w), two_d(g), two_d(m), two_d(v))
    return tuple(o.reshape(shape) for o in outs)


BIG_SHARDED = ["ffn_w_up", "ffn_w_down", "w_out", "ple_w_gate", "ple_w_proj", "w_in"]
SMALL_REDUCED = REPLICATED + F32_SHARDED


def _chip_major(a, axis):
    n = a.shape[axis] // N_CHIPS
    return jnp.moveaxis(a.reshape(a.shape[:axis] + (N_CHIPS, n) + a.shape[axis + 1:]), axis, 0)


def _train_step(x, p, loss_target, w, m, v):
    xs, ps, target = x[0], p[:, 0], loss_target[0]
    rows2d = lambda a: a.reshape(-1, a.shape[-1])
    conv_rows = 16
    pad_rows = lambda a: jnp.pad(a, ((0, conv_rows - a.shape[0]), (0, 0)))
    shards = [[w[n][l].astype(bf16) for n in MATMUL_SHARDED] for l in range(DEPTH)]
    first = _all_gather_chips(shards[0] + [pad_rows(rows2d(w[n])) for n in F32_SHARDED], "gather_weights")
    conv4 = dict(zip(F32_SHARDED, first[len(MATMUL_SHARDED):]))

    def col_cut(blk):
        return jnp.moveaxis(blk, 0, 1).reshape(blk.shape[1], N_CHIPS * blk.shape[2])

    def layer_weights(l, gathered):
        g = dict(zip(MATMUL_SHARDED, gathered))
        lw = {n: w[n][l] for n in REPLICATED if n != "final_norm_g"}
        lw["pool_w"] = lw["pool_w"].astype(bf16)
        lw.update(
            w_in4=g["w_in"], ple_w_proj=col_cut(g["ple_w_proj"]), w_out=g["w_out"].reshape(D_MODEL, D_MODEL),
            ple_w_gate=g["ple_w_gate"].reshape(D_MODEL, D_MODEL), wdn_h=g["ffn_w_down"].reshape(2, FF_HALF, D_MODEL),
            wup_h=jnp.stack([jnp.concatenate([g["ffn_w_up"][j], g["ffn_w_up"][2 + j]], axis=1) for j in range(2)]),
            ssd_conv_w=col_cut(conv4["ssd_conv_w"][:, l * SSD_CONV:(l + 1) * SSD_CONV, :]),
            ffn_conv_w=col_cut(conv4["ffn_conv_w"][:, l * FFN_CONV:(l + 1) * FFN_CONV, :]))
        return _prep_layer(lw)

    preps, saved = [], []
    h, gathered = xs, first[:len(MATMUL_SHARDED)]
    for l in range(DEPTH):
        preps.append(layer_weights(l, gathered))
        h, keep, gathered = _layer_fwd(h, (ps, l), preps[l], gather=shards[l + 1] if l + 1 < DEPTH else ())
        saved.append(keep)
    grad_x, d_gf, loss_part = _loss_head(h, target, w["final_norm_g"].reshape(1, D_MODEL))

    def chip_major_grads(g):
        return [g["ffn_w_up4"], g["wdn_h"].reshape(N_CHIPS, D_FF // N_CHIPS, D_MODEL), _chip_major(g["w_out"], 0),
                _chip_major(g["ple_w_gate"], 0), _chip_major(g["ple_w_proj"], 1), g["w_in4"]]

    grads, reduced, pending = [None] * DEPTH, [None] * DEPTH, ()
    for l in reversed(range(DEPTH)):
        grad_x, grads[l], done = _layer_bwd(grad_x, saved[l], (ps, l), preps[l], reduce=pending, early=(l == 0))
        if pending:
            reduced[l + 1] = done
        pending = chip_major_grads(grads[l])
    small = _pack([d_gf.reshape(D_MODEL) if n == "final_norm_g" else jnp.stack([grads[k][n] for k in range(DEPTH)])
                   for n in SMALL_REDUCED], 32 * N_CHIPS)
    early_sums, early_lands = grads[0]["early"]
    late = [pending[2], pending[5], small.reshape(N_CHIPS, -1, COLS)]
    late_sums = _rs_pair_sums(late, _rs_pair_exchange(late))
    up_r, dn_r, gate_r, proj_r, out_r, in_r, small_r = _rs_pair_share(
        _rs_chip_sums(list(early_sums) + late_sums, list(early_lands) + list(_rs_chip_exchange(late_sums))))
    reduced[0] = [up_r, dn_r, out_r, gate_r, proj_r, in_r, small_r]
    grad = {n: jnp.stack([reduced[l][i] for l in range(DEPTH)]) for i, n in enumerate(BIG_SHARDED)}
    small_all = _all_gather_chips([reduced[0][-1]], "gather_small_grads")[0].reshape(-1)
    small_shapes = [w[n].shape for n in REPLICATED] + [(DEPTH, SSD_CONV, XBC), (DEPTH, FFN_CONV, 2 * D_FF)]
    grad.update(zip(SMALL_REDUCED, _unpack(small_all, small_shapes)))
    me = 2 * lax.axis_index("x") + lax.axis_index("y")
    for n in F32_SHARDED:
        grad[n] = lax.dynamic_slice_in_dim(grad[n], me * w[n].shape[2], w[n].shape[2], axis=2)

    loss = lax.psum(loss_part[0, 0], ("x", "y", "c"))
    delta, new_m, new_v = {}, {}, {}
    for n in WEIGHT_NAMES:
        delta[n], new_m[n], new_v[n] = _adamw(w[n], grad[n], m[n], v[n], n)
    return (loss, grad_x[None], *[grad[n] for n in WEIGHT_NAMES], *[delta[n] for n in WEIGHT_NAMES],
            *[new_m[n] for n in WEIGHT_NAMES], *[new_v[n] for n in WEIGHT_NAMES])


def kernel(x, p, mix_norm_g, w_in, ssd_conv_w, ssd_conv_b, ssd_dt_bias, ssd_a_log, ssd_d, ssd_norm_g, pool_w, pool_scale, w_out, ffn_norm_g, ffn_w_up, ffn_conv_w, ffn_conv_b, ffn_w_down, ple_norm_g, ple_w_gate, ple_w_proj, final_norm_g, loss_target, m_mix_norm_g, m_w_in, m_ssd_conv_w, m_ssd_conv_b, m_ssd_dt_bias, m_ssd_a_log, m_ssd_d, m_ssd_norm_g, m_pool_w, m_pool_scale, m_w_out, m_ffn_norm_g, m_ffn_w_up, m_ffn_conv_w, m_ffn_conv_b, m_ffn_w_down, m_ple_norm_g, m_ple_w_gate, m_ple_w_proj, m_final_norm_g, v_mix_norm_g, v_w_in, v_ssd_conv_w, v_ssd_conv_b, v_ssd_dt_bias, v_ssd_a_log, v_ssd_d, v_ssd_norm_g, v_pool_w, v_pool_scale, v_w_out, v_ffn_norm_g, v_ffn_w_up, v_ffn_conv_w, v_ffn_conv_b, v_ffn_w_down, v_ple_norm_g, v_ple_w_gate, v_ple_w_proj, v_final_norm_g):
    w = dict(mix_norm_g=mix_norm_g, w_in=w_in, ssd_conv_w=ssd_conv_w, ssd_conv_b=ssd_conv_b, ssd_dt_bias=ssd_dt_bias, ssd_a_log=ssd_a_log, ssd_d=ssd_d, ssd_norm_g=ssd_norm_g, pool_w=pool_w, pool_scale=pool_scale, w_out=w_out, ffn_norm_g=ffn_norm_g, ffn_w_up=ffn_w_up, ffn_conv_w=ffn_conv_w, ffn_conv_b=ffn_conv_b, ffn_w_down=ffn_w_down, ple_norm_g=ple_norm_g, ple_w_gate=ple_w_gate, ple_w_proj=ple_w_proj, final_norm_g=final_norm_g)
    m = dict(mix_norm_g=m_mix_norm_g, w_in=m_w_in, ssd_conv_w=m_ssd_conv_w, ssd_conv_b=m_ssd_conv_b, ssd_dt_bias=m_ssd_dt_bias, ssd_a_log=m_ssd_a_log, ssd_d=m_ssd_d, ssd_norm_g=m_ssd_norm_g, pool_w=m_pool_w, pool_scale=m_pool_scale, w_out=m_w_out, ffn_norm_g=m_ffn_norm_g, ffn_w_up=m_ffn_w_up, ffn_conv_w=m_ffn_conv_w, ffn_conv_b=m_ffn_conv_b, ffn_w_down=m_ffn_w_down, ple_norm_g=m_ple_norm_g, ple_w_gate=m_ple_w_gate, ple_w_proj=m_ple_w_proj, final_norm_g=m_final_norm_g)
    v = dict(mix_norm_g=v_mix_norm_g, w_in=v_w_in, ssd_conv_w=v_ssd_conv_w, ssd_conv_b=v_ssd_conv_b, ssd_dt_bias=v_ssd_dt_bias, ssd_a_log=v_ssd_a_log, ssd_d=v_ssd_d, ssd_norm_g=v_ssd_norm_g, pool_w=v_pool_w, pool_scale=v_pool_scale, w_out=v_w_out, ffn_norm_g=v_ffn_norm_g, ffn_w_up=v_ffn_w_up, ffn_conv_w=v_ffn_conv_w, ffn_conv_b=v_ffn_conv_b, ffn_w_down=v_ffn_w_down, ple_norm_g=v_ple_norm_g, ple_w_gate=v_ple_w_gate, ple_w_proj=v_ple_w_proj, final_norm_g=v_final_norm_g)
    return _train_step(x, p, loss_target, w, m, v)
```

```python
import functools
import math

import jax
import jax.numpy as jnp
from jax import lax
from jax.experimental import pallas as pl
from jax.experimental.pallas import tpu as pltpu

f32, bf16 = jnp.float32, jnp.bfloat16
HI = lax.Precision.HIGHEST

D_MODEL = 1024
D_PLE = 256
DEPTH = 4
SSD_W = 512
HEADS = 8
HEAD_DIM = 64
NSTATE = 128
CHUNK = 128
SSD_CONV = 4
XBC = 1024
POOL_W = 512
POOL_G = 128
WINDOWS = (2, 4, 8, 16)
D_FF = 2816
FF_HALF = D_FF // 2
FFN_CONV = 3
D_IN = 2056
EPS = 1e-6
ADAM_LR, ADAM_B1, ADAM_B2, ADAM_EPS, ADAM_WD, ADAM_STEP = 0.001, 0.9, 0.999, 1e-08, 0.01, 10

LANES = 128
NPROJ = 2048 + LANES
HALO = 16
VMEM_LIMIT = 58 * 1024 * 1024
ROW_TILE = 256


def _dot(a, b):
    return jnp.dot(a, b, preferred_element_type=f32)


def _dot_nt(a, b):
    return lax.dot_general(a, b, (((1,), (1,)), ((), ())), preferred_element_type=f32)


def _dot_tn(a, b):
    return lax.dot_general(a, b, (((0,), (0,)), ((), ())), preferred_element_type=f32)


def _dot_hi(a, b):
    return jnp.dot(a, b, precision=HI, preferred_element_type=f32)


def _rms_fwd(x, g):
    r = lax.rsqrt(jnp.mean(x * x, axis=-1, keepdims=True) + EPS)
    xhat = x * r
    return xhat * g, xhat, r


def _rms_bwd(dy, xhat, r, g):
    dxhat = dy * g
    dx = r * (dxhat - xhat * jnp.mean(dxhat * xhat, axis=-1, keepdims=True))
    return dx, jnp.sum(dy * xhat, axis=0, keepdims=True)


def _sigmoid(x):
    return 1.0 / (1.0 + jnp.exp(-x))


_GELU_C = math.sqrt(2.0 / math.pi)


def _gelu_and_grad(x):
    x2 = x * x
    t = jnp.tanh(x * (_GELU_C + (_GELU_C * 0.044715) * x2))
    u = 0.5 * t + 0.5
    g = x * u
    dg = u + g * (1.0 - t) * (_GELU_C + (3.0 * 0.044715 * _GELU_C) * x2)
    return g, dg


def _gelu(x):
    return x * (0.5 * jnp.tanh(x * (_GELU_C + (_GELU_C * 0.044715) * (x * x))) + 0.5)


def _softplus(x):
    return jnp.maximum(x, 0.0) + jnp.log(1.0 + jnp.exp(-jnp.abs(x)))


def _cparams(sem=("arbitrary",)):
    return pltpu.CompilerParams(dimension_semantics=sem, vmem_limit_bytes=VMEM_LIMIT)


def _const_spec(shape):
    nd = len(shape)
    return pl.BlockSpec(shape, lambda *_: (0,) * nd, pipeline_mode=pl.Buffered(1))


WIDE_ROW_TILE = 512


def _row_tile(s, t=ROW_TILE):
    return min(t, s)


def _mix_in_fwd(h, g1, w_r):
    s = h.shape[0]
    t = _row_tile(s, WIDE_ROW_TILE)

    def body(h_ref, g_ref, w_ref, zxu_ref, dtr_ref):
        hn, _, _ = _rms_fwd(h_ref[...], g_ref[...])
        proj = _dot(hn.astype(bf16), w_ref[...])
        zxu_ref[...] = proj[:, :2048].astype(bf16)
        dtr_ref[...] = proj[:, 2048:]

    return pl.pallas_call(
        body, name="mix_in_fwd", grid=(s // t,),
        out_shape=(jax.ShapeDtypeStruct((s, 2048), bf16), jax.ShapeDtypeStruct((s, LANES), f32)),
        in_specs=[pl.BlockSpec((t, D_MODEL), lambda i: (i, 0)), _const_spec((1, D_MODEL)), _const_spec((D_MODEL, NPROJ))],
        out_specs=(pl.BlockSpec((t, 2048), lambda i: (i, 0)), pl.BlockSpec((t, LANES), lambda i: (i, 0))),
        compiler_params=_cparams(),
    )(h, g1, w_r)


def _mix_in_bwd(d_zxu, d_dtr, h, dh1, g1, w_r):
    s = h.shape[0]
    t = _row_tile(s, WIDE_ROW_TILE)
    n = s // t

    def body(dz_ref, dd_ref, h_ref, dh1_ref, g_ref, w_ref, dh_ref, dw_ref, dg_ref, acc):
        i = pl.program_id(0)

        @pl.when(i == 0)
        def _():
            acc[...] = jnp.zeros_like(acc)
            dg_ref[...] = jnp.zeros_like(dg_ref)

        g = g_ref[...]
        hn, xhat, r = _rms_fwd(h_ref[...], g)
        dproj = jnp.concatenate([dz_ref[...], dd_ref[...].astype(bf16)], axis=1)
        d_hn = _dot_nt(dproj, w_ref[...])
        acc[...] += _dot_tn(hn.astype(bf16), dproj)
        dx, dg = _rms_bwd(d_hn, xhat, r, g)
        dg_ref[...] += dg
        dh_ref[...] = dh1_ref[...] + dx

        @pl.when(i == n - 1)
        def _():
            pltpu.sync_copy(acc, dw_ref)

    return pl.pallas_call(
        body, name="mix_in_bwd", grid=(n,),
        out_shape=(jax.ShapeDtypeStruct((s, D_MODEL), f32), jax.ShapeDtypeStruct((D_MODEL, NPROJ), f32),
                   jax.ShapeDtypeStruct((1, D_MODEL), f32)),
        in_specs=[pl.BlockSpec((t, 2048), lambda i: (i, 0)), pl.BlockSpec((t, LANES), lambda i: (i, 0)),
                  pl.BlockSpec((t, D_MODEL), lambda i: (i, 0)), pl.BlockSpec((t, D_MODEL), lambda i: (i, 0)),
                  _const_spec((1, D_MODEL)), _const_spec((D_MODEL, NPROJ))],
        out_specs=(pl.BlockSpec((t, D_MODEL), lambda i: (i, 0)), pl.BlockSpec(memory_space=pl.ANY),
                   pl.BlockSpec((1, D_MODEL), lambda i: (0, 0))),
        scratch_shapes=[pltpu.VMEM((D_MODEL, NPROJ), f32)],
        compiler_params=_cparams(),
    )(d_zxu, d_dtr, h, dh1, g1, w_r)


def _iota2(shape, dim):
    return lax.broadcasted_iota(jnp.int32, shape, dim)


def _lane_bcast(a, h):
    return jnp.broadcast_to(a[:, h:h + 1], (a.shape[0], LANES))


def _to_columns(cols):
    lane = _iota2((cols[0].shape[0], LANES), 1)
    out = jnp.where(lane == 0, cols[0], 0.0)
    for h in range(1, len(cols)):
        out = out + jnp.where(lane == h, cols[h], 0.0)
    return out


def _ssd_pre(zx, dtr, xext, cw, cb, dtb, alog):
    c = cb + cw[0:1] * xext[pl.ds(HALO - 3, CHUNK), :]
    for k in range(1, SSD_CONV):
        c = c + cw[k:k + 1] * xext[pl.ds(HALO - 3 + k, CHUNK), :]
    sig_c = _sigmoid(c)
    xc = c * sig_c
    dt = _softplus(dtr + dtb)
    a_neg = -jnp.exp(alog)
    a = dt * a_neg
    lo = _iota2((CHUNK, LANES), 1) < HEAD_DIM
    dt_w = jnp.concatenate([jnp.where(lo, _lane_bcast(dt, 2 * pp), _lane_bcast(dt, 2 * pp + 1)) for pp in range(4)], axis=1)
    xs = xc[:, :SSD_W]
    xd = xs * dt_w
    tril = (_iota2((CHUNK, CHUNK), 0) >= _iota2((CHUNK, CHUNK), 1))
    acs = _dot_hi(tril.astype(f32), a)
    acs_b = [_lane_bcast(acs, h) for h in range(HEADS)]
    groups = []
    for g in range(2):
        b16 = xc[:, SSD_W + g * NSTATE:SSD_W + (g + 1) * NSTATE].astype(bf16)
        c16 = xc[:, SSD_W + 2 * NSTATE + g * NSTATE:SSD_W + 2 * NSTATE + (g + 1) * NSTATE].astype(bf16)
        groups.append((b16, c16, _dot_nt(c16, b16)))
    return dict(c=c, sig_c=sig_c, xc=xc, xs=xs, dt=dt, a_neg=a_neg, a=a, dt_w=dt_w, xd=xd, acs_b=acs_b, tril=tril, lo=lo,
                groups=groups)


def _pair_fwd(q, pp, s_in):
    lo = q["lo"]
    b16, c16, gmat = q["groups"][pp // 2]
    xp = q["xd"][:, pp * LANES:(pp + 1) * LANES]
    xp16 = xp.astype(bf16)
    ab0 = q["acs_b"][2 * pp]
    ab1 = q["acs_b"][2 * pp + 1]
    ls, ms, ys = [], [], []
    for ab in (ab0, ab1):
        lmat = jnp.exp(jnp.where(q["tril"], ab - ab.T, -jnp.inf))
        mmat = gmat * lmat
        ls.append(lmat)
        ms.append(mmat)
        ys.append(_dot(mmat.astype(bf16), xp16))
    y_diag = jnp.where(lo, ys[0], ys[1])
    ab_pair = jnp.where(lo, ab0, ab1)
    e_pair = jnp.exp(ab_pair)
    s16 = s_in.astype(bf16)
    y_off = _dot(c16, s16) * e_pair
    alast = ab_pair[CHUNK - 1:CHUNK, :]
    dec_pair = jnp.exp(alast - ab_pair)
    xdec = xp * dec_pair
    st = _dot_tn(b16, xdec.astype(bf16))
    cd_pair = jnp.exp(alast)
    s_out = s_in * cd_pair + st
    return dict(b16=b16, c16=c16, gmat=gmat, xp=xp, xp16=xp16, ls=ls, ms=ms, y=y_diag + y_off, y_off=y_off,
                e_pair=e_pair, dec_pair=dec_pair, xdec=xdec, cd_pair=cd_pair, s_out=s_out, s16=s16, lo=lo)


def _gate_norm_fwd(y_pre, z, ng):
    sz = _sigmoid(z)
    yg = y_pre * (z * sz)
    outs, stats = [], []
    half = SSD_W // 2
    for gi in range(2):
        o, xhat, r = _rms_fwd(yg[:, gi * half:(gi + 1) * half], ng[:, gi * half:(gi + 1) * half])
        outs.append(o)
        stats.append((xhat, r))
    return jnp.concatenate(outs, axis=1), sz, stats


def _pool_fwd(uext, u, row0, pw_ref, ps):
    pos = (row0 + _iota2((CHUNK, 1), 0) + 1).astype(f32)
    pooled, mixed, invs = [], [], []
    for gi, w in enumerate(WINDOWS):
        sl = slice(gi * POOL_G, (gi + 1) * POOL_G)
        acc = uext[pl.ds(HALO, CHUNK), sl]
        for j in range(1, w):
            acc = acc + uext[pl.ds(HALO - j, CHUNK), sl]
        den = jnp.minimum(pos, float(w))
        pg = acc / den - u[:, sl]
        pooled.append(pg)
        invs.append(den)
        mixed.append(_dot(pg.astype(bf16), pw_ref[gi]))
    mixed = jnp.concatenate(mixed, axis=1)
    return mixed * ps, pooled, mixed, invs


def _ssd_specs(s):
    nc = s // CHUNK
    hb = CHUNK // HALO
    return nc, hb


def _ssd_param_specs():
    return [_const_spec((SSD_CONV, XBC)), _const_spec((1, XBC)), _const_spec((1, LANES)), _const_spec((1, LANES)),
            _const_spec((1, SSD_W)), _const_spec((1, SSD_W)), _const_spec((4, POOL_G, POOL_G)), _const_spec((1, POOL_W))]


def _ssd_pool_fwd(zxu, dtr, prm):
    s = zxu.shape[0]
    nc, hb = _ssd_specs(s)

    def body(zx_ref, halo_ref, dtr_ref, cw_ref, cb_ref, dtb_ref, alog_ref, dsk_ref, ng_ref, pw_ref, ps_ref,
             ymix_ref, st_ref, state, xext, uext):
        i = pl.program_id(0)

        @pl.when(i == 0)
        def _():
            state[...] = jnp.zeros_like(state)

        zx = zx_ref[...].astype(f32)
        halo = jnp.where(i > 0, halo_ref[...].astype(f32), 0.0)
        xext[0:HALO, :] = halo[:, SSD_W:SSD_W + XBC]
        xext[HALO:, :] = zx[:, SSD_W:SSD_W + XBC]
        uext[0:HALO, :] = halo[:, SSD_W + XBC:]
        uext[HALO:, :] = zx[:, SSD_W + XBC:]
        q = _ssd_pre(zx, dtr_ref[...], xext, cw_ref[...], cb_ref[...], dtb_ref[...], alog_ref[...])
        ys = []
        for pp in range(4):
            s_in = state[pp]
            st_ref[0, pp] = s_in
            r = _pair_fwd(q, pp, s_in)
            state[pp] = r["s_out"]
            ys.append(r["y"])
        y_pre = jnp.concatenate(ys, axis=1) + q["xs"] * dsk_ref[...]
        y_ssd, _, _ = _gate_norm_fwd(y_pre, zx[:, :SSD_W], ng_ref[...])
        y_pool, _, _, _ = _pool_fwd(uext, zx[:, SSD_W + XBC:], i * CHUNK, pw_ref, ps_ref[...])
        ymix_ref[:, :SSD_W] = y_ssd.astype(bf16)
        ymix_ref[:, SSD_W:] = y_pool.astype(bf16)

    return pl.pallas_call(
        body, name="ssd_pool_fwd", grid=(nc,),
        out_shape=(jax.ShapeDtypeStruct((s, D_MODEL), bf16), jax.ShapeDtypeStruct((nc, 4, NSTATE, LANES), f32)),
        in_specs=[pl.BlockSpec((CHUNK, 2048), lambda i: (i, 0)),
                  pl.BlockSpec((HALO, 2048), lambda i: (jnp.maximum(i * hb - 1, 0), 0)),
                  pl.BlockSpec((CHUNK, LANES), lambda i: (i, 0))] + _ssd_param_specs(),
        out_specs=(pl.BlockSpec((CHUNK, D_MODEL), lambda i: (i, 0)),
                   pl.BlockSpec((1, 4, NSTATE, LANES), lambda i: (i, 0, 0, 0))),
        scratch_shapes=[pltpu.VMEM((4, NSTATE, LANES), f32), pltpu.VMEM((HALO + CHUNK, XBC), f32),
                        pltpu.VMEM((HALO + CHUNK, POOL_W), f32)],
        compiler_params=_cparams(),
    )(zxu, zxu, dtr, *prm)


def _ssd_pool_bwd(d_ymix, zxu, dtr, states, prm, exchange=()):
    s = zxu.shape[0]
    nc, hb = _ssd_specs(s)
    rev = lambda i: nc - 1 - i
    nx = len(exchange)

    def body(*refs):
        (dy_ref, zx_ref, halo_ref, dtr_ref, st_ref, cw_ref, cb_ref, dtb_ref, alog_ref, dsk_ref, ng_ref, pw_ref,
         ps_ref) = refs[:13]
        p_refs = refs[13:13 + nx]
        (dzx_ref, ddtr_ref, dcw_ref, dcb_ref, ddtb_ref, dalog_ref, ddsk_ref, dng_ref, dpw_ref,
         dps_ref) = refs[13 + nx:23 + nx]
        land_refs = refs[23 + nx:23 + 2 * nx]
        dstate, xext, uext, dxext, duext, cx, cu = refs[23 + 2 * nx:30 + 2 * nx]
        sems = refs[30 + 2 * nx:]
        i = pl.program_id(0)
        ci = nc - 1 - i
        if nx:
            @pl.when(i == 0)
            def _():
                _cx_start(p_refs, land_refs, *sems)

            @pl.when(i == nc - 1)
            def _():
                _cx_finish(p_refs, land_refs, *sems)

        @pl.when(i == 0)
        def _():
            dstate[...] = jnp.zeros_like(dstate)
            cx[...] = jnp.zeros_like(cx)
            cu[...] = jnp.zeros_like(cu)
            for r in (dcw_ref, dcb_ref, ddtb_ref, dalog_ref, ddsk_ref, dng_ref, dpw_ref, dps_ref):
                r[...] = jnp.zeros_like(r)

        zx = zx_ref[...].astype(f32)
        halo = jnp.where(ci > 0, halo_ref[...].astype(f32), 0.0)
        xext[0:HALO, :] = halo[:, SSD_W:SSD_W + XBC]
        xext[HALO:, :] = zx[:, SSD_W:SSD_W + XBC]
        uext[0:HALO, :] = halo[:, SSD_W + XBC:]
        uext[HALO:, :] = zx[:, SSD_W + XBC:]
        cw = cw_ref[...]
        q = _ssd_pre(zx, dtr_ref[...], xext, cw, cb_ref[...], dtb_ref[...], alog_ref[...])
        z = zx[:, :SSD_W]
        dy = dy_ref[...].astype(f32)
        d_yssd, d_ypool = dy[:, :SSD_W], dy[:, SSD_W:]

        pairs = [_pair_fwd(q, pp, st_ref[0, pp]) for pp in range(4)]
        dsk = dsk_ref[...]
        ng = ng_ref[...]
        y_pre = jnp.concatenate([r["y"] for r in pairs], axis=1) + q["xs"] * dsk
        _, sz, stats = _gate_norm_fwd(y_pre, z, ng)

        half = SSD_W // 2
        d_yg, d_ng = [], []
        for gi in range(2):
            xhat, r = stats[gi]
            dx, dg = _rms_bwd(d_yssd[:, gi * half:(gi + 1) * half], xhat, r, ng[:, gi * half:(gi + 1) * half])
            d_yg.append(dx)
            d_ng.append(dg)
        d_yg = jnp.concatenate(d_yg, axis=1)
        dng_ref[...] += jnp.concatenate(d_ng, axis=1)
        silu_z = z * sz
        d_ypre = d_yg * silu_z
        d_z = d_yg * y_pre * (sz * (1.0 + z * (1.0 - sz)))
        ddsk_ref[...] += jnp.sum(d_ypre * q["xs"], axis=0, keepdims=True)

        d_xd, acs_cols, dt_cols = [], [], []
        d_b = [None, None]
        d_c = [None, None]
        d_g = [None, None]
        last_row = _iota2((CHUNK, LANES), 0) == CHUNK - 1
        for pp in range(4):
            g = pp // 2
            r = pairs[pp]
            lo = r["lo"]
            dyp = d_ypre[:, pp * LANES:(pp + 1) * LANES]
            dyp16 = dyp.astype(bf16)
            ds_out = dstate[pp]
            ds16 = ds_out.astype(bf16)
            dye16 = (dyp * r["e_pair"]).astype(bf16)
            dstate[pp] = r["cd_pair"] * ds_out + _dot_tn(r["c16"], dye16)
            dc = _dot_nt(dye16, r["s16"])
            dxdec = _dot(r["b16"], ds16)
            db = _dot_nt(r["xdec"].astype(bf16), ds16)
            dxp = dxdec * r["dec_pair"]
            t2 = dxdec * r["xdec"]
            tail = jnp.sum(t2, axis=0, keepdims=True) + jnp.sum(ds_out * st_ref[0, pp] * r["cd_pair"], axis=0, keepdims=True)
            rp = dyp * r["y_off"] - t2 + jnp.where(last_row, tail, 0.0)
            dxs = []
            for hh in range(2):
                msk = lo if hh == 0 else jnp.logical_not(lo)
                m16 = r["ms"][hh].astype(bf16)
                dxs.append(_dot_tn(m16, dyp16))
                dm = _dot_nt(jnp.where(msk, dyp, 0.0).astype(bf16), r["xp16"])
                wmat = dm * r["ms"][hh]
                acs_cols.append(jnp.sum(wmat - wmat.T + jnp.where(msk, rp, 0.0), axis=1, keepdims=True))
                dgh = dm * r["ls"][hh]
                d_g[g] = dgh if d_g[g] is None else d_g[g] + dgh
            dxp = dxp + jnp.where(lo, dxs[0], dxs[1])
            d_xd.append(dxp)
            xprod = dxp * q["xs"][:, pp * LANES:(pp + 1) * LANES]
            dt_cols.append(jnp.sum(jnp.where(lo, xprod, 0.0), axis=1, keepdims=True))
            dt_cols.append(jnp.sum(jnp.where(lo, 0.0, xprod), axis=1, keepdims=True))
            d_b[g] = db if d_b[g] is None else d_b[g] + db
            d_c[g] = dc if d_c[g] is None else d_c[g] + dc
        for g in range(2):
            dg16 = d_g[g].astype(bf16)
            d_c[g] = d_c[g] + _dot(dg16, pairs[2 * g]["b16"])
            d_b[g] = d_b[g] + _dot_tn(dg16, pairs[2 * g]["c16"])
        d_xd = jnp.concatenate(d_xd, axis=1)
        triu = (_iota2((CHUNK, CHUNK), 0) <= _iota2((CHUNK, CHUNK), 1)).astype(f32)
        d_a = _dot_hi(triu, _to_columns(acs_cols))
        d_dt = d_a * q["a_neg"] + _to_columns(dt_cols)
        dalog_ref[...] += jnp.sum(d_a * q["dt"], axis=0, keepdims=True) * q["a_neg"]
        d_dtr = d_dt * _sigmoid(dtr_ref[...] + dtb_ref[...])
        ddtr_ref[...] = d_dtr
        ddtb_ref[...] += jnp.sum(d_dtr, axis=0, keepdims=True)
        d_xs = d_ypre * dsk + d_xd * q["dt_w"]

        d_xc = jnp.concatenate([d_xs, d_b[0], d_b[1], d_c[0], d_c[1]], axis=1)
        sc = q["sig_c"]
        d_conv = d_xc * (sc * (1.0 + q["c"] * (1.0 - sc)))
        dcb_ref[...] += jnp.sum(d_conv, axis=0, keepdims=True)
        dxext[...] = jnp.zeros_like(dxext)
        for k in range(SSD_CONV):
            dcw_ref[k:k + 1, :] += jnp.sum(d_conv * xext[pl.ds(HALO - 3 + k, CHUNK), :], axis=0, keepdims=True)
            dxext[pl.ds(HALO - 3 + k, CHUNK), :] += cw[k:k + 1] * d_conv
        dxext[pl.ds(CHUNK, HALO), :] += cx[...]
        cx[...] = dxext[0:HALO, :]

        ps = ps_ref[...]
        u = zx[:, SSD_W + XBC:]
        _, pooled, mixed, dens = _pool_fwd(uext, u, ci * CHUNK, pw_ref, ps)
        dps_ref[...] += jnp.sum(d_ypool * mixed, axis=0, keepdims=True)
        d_mixed = d_ypool * ps
        duext[...] = jnp.zeros_like(duext)
        for gi, w in enumerate(WINDOWS):
            sl = slice(gi * POOL_G, (gi + 1) * POOL_G)
            dm16 = d_mixed[:, sl].astype(bf16)
            dpw_ref[gi] += _dot_tn(pooled[gi].astype(bf16), dm16)
            d_pg = _dot_nt(dm16, pw_ref[gi])
            d_mean = d_pg / dens[gi]
            duext[pl.ds(HALO, CHUNK), sl] += d_mean - d_pg
            for j in range(1, w):
                duext[pl.ds(HALO - j, CHUNK), sl] += d_mean
        duext[pl.ds(CHUNK, HALO), :] += cu[...]
        cu[...] = duext[0:HALO, :]

        dzx_ref[:, :SSD_W] = d_z.astype(bf16)
        dzx_ref[:, SSD_W:SSD_W + XBC] = dxext[HALO:, :].astype(bf16)
        dzx_ref[:, SSD_W + XBC:] = duext[HALO:, :].astype(bf16)

    small = lambda shape: pl.BlockSpec(shape, lambda i: (0,) * len(shape))
    small_shapes = [(SSD_CONV, XBC), (1, XBC), (1, LANES), (1, LANES), (1, SSD_W), (1, SSD_W), (4, POOL_G, POOL_G), (1, POOL_W)]
    outs = pl.pallas_call(
        body, name="ssd_pool_bwd_exchange" if nx else "ssd_pool_bwd", grid=(nc,),
        out_shape=(jax.ShapeDtypeStruct((s, 2048), bf16), jax.ShapeDtypeStruct((s, LANES), f32))
        + tuple(jax.ShapeDtypeStruct(sh, f32) for sh in small_shapes)
        + tuple(jax.ShapeDtypeStruct((3,) + a.shape[1:], a.dtype) for a in exchange),
        in_specs=[pl.BlockSpec((CHUNK, D_MODEL), lambda i: (rev(i), 0)),
                  pl.BlockSpec((CHUNK, 2048), lambda i: (rev(i), 0)),
                  pl.BlockSpec((HALO, 2048), lambda i: (jnp.maximum(rev(i) * hb - 1, 0), 0)),
                  pl.BlockSpec((CHUNK, LANES), lambda i: (rev(i), 0)),
                  pl.BlockSpec((1, 4, NSTATE, LANES), lambda i: (rev(i), 0, 0, 0))] + _ssd_param_specs() + [_hbm()] * nx,
        out_specs=(pl.BlockSpec((CHUNK, 2048), lambda i: (rev(i), 0)), pl.BlockSpec((CHUNK, LANES), lambda i: (rev(i), 0)))
        + tuple(small(sh) for sh in small_shapes) + (_hbm(),) * nx,
        scratch_shapes=[pltpu.VMEM((4, NSTATE, LANES), f32), pltpu.VMEM((HALO + CHUNK, XBC), f32),
                        pltpu.VMEM((HALO + CHUNK, POOL_W), f32), pltpu.VMEM((HALO + CHUNK, XBC), f32),
                        pltpu.VMEM((HALO + CHUNK, POOL_W), f32), pltpu.VMEM((HALO, XBC), f32), pltpu.VMEM((HALO, POOL_W), f32)]
        + ([pltpu.SemaphoreType.DMA((3 * nx,)), pltpu.SemaphoreType.DMA((3 * nx,))] if nx else []),
        compiler_params=_cparams(),
    )(d_ymix, zxu, zxu, dtr, states, *prm, *exchange)
    return outs[:10] + (list(outs[10:]),)


FFN_BWD_TILE = 256


def _prev_halo_spec(t, width):
    hb = t // HALO
    return pl.BlockSpec((HALO, width), lambda i: (jnp.maximum(i * hb - 1, 0), 0))


def _ffn_half(hn16, j, wup_ref, cw_ref, cb_ref, up_scr, rows):
    up_scr[...] = _dot(hn16, wup_ref[j])
    cw = cw_ref[j]
    cv = cb_ref[j] + cw[0:1] * up_scr[pl.ds(HALO - 2, rows), :]
    for k in range(1, FFN_CONV):
        cv = cv + cw[k:k + 1] * up_scr[pl.ds(HALO - 2 + k, rows), :]
    return cv


def _out_ffn_ple_fwd(h, ymix, p_l, w_out, g2, wup_h, cw_h, cb_h, wdn_h, g3, w_gate, w_proj, gather=()):
    s = h.shape[0]
    t = _row_tile(s)
    n = s // t
    ng = len(gather)

    def body(*refs):
        (h_ref, hh_ref, ym_ref, ymh_ref, p_ref, wo_ref, g2_ref, wup_ref, cw_ref, cb_ref, wdn_ref, g3_ref, wg_ref,
         wp_ref) = refs[:14]
        x_refs = refs[14:14 + ng]
        h1_ref, h2_ref, h3_ref, up16_ref = refs[14 + ng:18 + ng]
        land_refs = refs[18 + ng:18 + 2 * ng]
        up_scr = refs[18 + 2 * ng]
        sems = refs[19 + 2 * ng:]
        i = pl.program_id(0)
        if ng:
            @pl.when(i == 0)
            def _():
                _ag_start(x_refs, land_refs, *sems)

            @pl.when(i == n - 1)
            def _():
                _ag_finish(x_refs, land_refs, *sems)

        hh = jnp.where(i > 0, hh_ref[...], 0.0)
        ymh = jnp.where(i > 0, ymh_ref[...].astype(f32), 0.0)
        h_ext = jnp.concatenate([hh, h_ref[...]], axis=0)
        ym_ext = jnp.concatenate([ymh, ym_ref[...].astype(f32)], axis=0).astype(bf16)
        h1_ext = h_ext + _dot(ym_ext, wo_ref[...])
        hn2, _, _ = _rms_fwd(h1_ext, g2_ref[...])
        hn16 = hn2.astype(bf16)
        h1 = h1_ext[HALO:, :]
        acc = h1
        for j in range(2):
            cv = _ffn_half(hn16, j, wup_ref, cw_ref, cb_ref, up_scr, t)
            up16_ref[j] = up_scr[pl.ds(HALO, t), :].astype(bf16)
            act = _gelu(cv[:, :FF_HALF]) * cv[:, FF_HALF:]
            acc = acc + _dot(act.astype(bf16), wdn_ref[j])
        h2 = acc
        hn3, _, _ = _rms_fwd(h2, g3_ref[...])
        gate = _sigmoid(_dot(hn3.astype(bf16), wg_ref[...]))
        pp = _dot(p_ref[0].astype(bf16), wp_ref[...])
        h1_ref[...] = h1
        h2_ref[...] = h2
        h3_ref[...] = h2 + pp * gate

    row = lambda w: pl.BlockSpec((t, w), lambda i: (i, 0))
    outs = pl.pallas_call(
        body, name="out_ffn_ple_fwd_gather" if ng else "out_ffn_ple_fwd", grid=(n,),
        out_shape=tuple(jax.ShapeDtypeStruct((s, D_MODEL), f32) for _ in range(3)) + (jax.ShapeDtypeStruct((2, s, D_FF), bf16),)
        + tuple(jax.ShapeDtypeStruct((N_CHIPS,) + a.shape, a.dtype) for a in gather),
        in_specs=[row(D_MODEL), _prev_halo_spec(t, D_MODEL), row(D_MODEL), _prev_halo_spec(t, D_MODEL),
                  pl.BlockSpec((1, t, D_PLE), lambda i: (p_l[1], i, 0)),
                  _const_spec((D_MODEL, D_MODEL)), _const_spec((1, D_MODEL)), _const_spec((2, D_MODEL, D_FF)),
                  _const_spec((2, FFN_CONV, D_FF)), _const_spec((2, 1, D_FF)), _const_spec((2, FF_HALF, D_MODEL)),
                  _const_spec((1, D_MODEL)), _const_spec((D_MODEL, D_MODEL)), _const_spec((D_PLE, D_MODEL))] + [_hbm()] * ng,
        out_specs=tuple(row(D_MODEL) for _ in range(3)) + (pl.BlockSpec((2, t, D_FF), lambda i: (0, i, 0)),) + (_hbm(),) * ng,
        scratch_shapes=[pltpu.VMEM((HALO + t, D_FF), f32)]
        + ([pltpu.SemaphoreType.DMA((6 * ng,)), pltpu.SemaphoreType.DMA((6 * ng,))] if ng else []),
        compiler_params=_cparams(),
    )(h, h, ymix, ymix, p_l[0], w_out, g2, wup_h, cw_h, cb_h, wdn_h, g3, w_gate, w_proj, *gather)
    return outs[:4] + (_own_block_in(outs[4:], gather),)


def _ple_bwd(dh3, h2, p_l, g3, w_gate, w_proj, exchange=()):
    s = h2.shape[0]
    t = _row_tile(s, WIDE_ROW_TILE)
    n = s // t
    nx = len(exchange)

    def body(*refs):
        dh3_ref, h2_ref, p_ref, g3_ref, wg_ref, wp_ref = refs[:6]
        g_refs = refs[6:6 + nx]
        dh2_ref, dwg_ref, dwp_ref, dg3_ref = refs[6 + nx:10 + nx]
        land_refs = refs[10 + nx:10 + 2 * nx]
        sems = refs[10 + 2 * nx:]
        i = pl.program_id(0)
        if nx:
            @pl.when(i == 0)
            def _():
                for cp in _px_copies(g_refs, land_refs, *sems):
                    cp.start()

            @pl.when(i == n - 1)
            def _():
                for cp in _px_copies(g_refs, land_refs, *sems):
                    cp.wait()

        @pl.when(i == 0)
        def _():
            dwg_ref[...] = jnp.zeros_like(dwg_ref)
            dwp_ref[...] = jnp.zeros_like(dwp_ref)
            dg3_ref[...] = jnp.zeros_like(dg3_ref)

        g3 = g3_ref[...]
        dh3 = dh3_ref[...]
        hn3, xhat, r = _rms_fwd(h2_ref[...], g3)
        hn16 = hn3.astype(bf16)
        gate = _sigmoid(_dot(hn16, wg_ref[...]))
        p16 = p_ref[0].astype(bf16)
        pp = _dot(p16, wp_ref[...])
        d_pp = (dh3 * gate).astype(bf16)
        d_pre = (dh3 * pp * gate * (1.0 - gate)).astype(bf16)
        dwp_ref[...] += _dot_tn(p16, d_pp)
        dwg_ref[...] += _dot_tn(hn16, d_pre)
        dx, dg = _rms_bwd(_dot_nt(d_pre, wg_ref[...]), xhat, r, g3)
        dg3_ref[...] += dg
        dh2_ref[...] = dh3 + dx

    row = lambda w: pl.BlockSpec((t, w), lambda i: (i, 0))
    fixed = lambda shape: pl.BlockSpec(shape, lambda i: (0,) * len(shape))
    outs = pl.pallas_call(
        body, name="ple_bwd_exchange" if nx else "ple_bwd", grid=(n,),
        out_shape=(jax.ShapeDtypeStruct((s, D_MODEL), f32), jax.ShapeDtypeStruct((D_MODEL, D_MODEL), f32),
                   jax.ShapeDtypeStruct((D_PLE, D_MODEL), f32), jax.ShapeDtypeStruct((1, D_MODEL), f32))
        + tuple(jax.ShapeDtypeStruct((N_CHIPS, g.shape[1] // 2, g.shape[2]), g.dtype) for g in exchange),
        in_specs=[row(D_MODEL), row(D_MODEL), pl.BlockSpec((1, t, D_PLE), lambda i: (p_l[1], i, 0)), _const_spec((1, D_MODEL)),
                  _const_spec((D_MODEL, D_MODEL)), _const_spec((D_PLE, D_MODEL))] + [_hbm()] * nx,
        out_specs=(row(D_MODEL), fixed((D_MODEL, D_MODEL)), fixed((D_PLE, D_MODEL)), fixed((1, D_MODEL))) + (_hbm(),) * nx,
        scratch_shapes=[pltpu.SemaphoreType.DMA((nx,)), pltpu.SemaphoreType.DMA((nx,))] if nx else [],
        compiler_params=_cparams(),
    )(dh3, h2, p_l[0], g3, w_gate, w_proj, *exchange)
    return outs[:4] + (list(outs[4:]),)


def _ffn_bwd(dh2, h1, up16, g2, wup_h, cw_h, cb_h, wdn_h, exchange=()):
    s = h1.shape[0]
    t = min(FFN_BWD_TILE, s)
    n = s // t
    hb = t // HALO
    last_hb = s // HALO - 1
    nx = len(exchange)

    def body(*refs):
        dh2_ref, dh2n_ref, h1_ref, up_ref, upp_ref, upn_ref, g2_ref, wup_ref, cw_ref, cb_ref, wdn_ref = refs[:11]
        p_refs = refs[11:11 + nx]
        part_ref, dwup_ref, dwdn_ref, dcw_ref, dcb_ref = refs[11 + nx:16 + nx]
        land_refs = refs[16 + nx:16 + 2 * nx]
        up_scr, dcv_scr, acc_up, acc_dn = refs[16 + 2 * nx:20 + 2 * nx]
        sems = refs[20 + 2 * nx:]
        j = pl.program_id(0)
        i = pl.program_id(1)
        if nx:
            @pl.when(jnp.logical_and(j == 0, i == 0))
            def _():
                _cx_start(p_refs, land_refs, *sems)

            @pl.when(jnp.logical_and(j == 1, i == n - 1))
            def _():
                _cx_finish(p_refs, land_refs, *sems)

        @pl.when(i == 0)
        def _():
            acc_up[...] = jnp.zeros_like(acc_up)
            acc_dn[...] = jnp.zeros_like(acc_dn)
            dcw_ref[...] = jnp.zeros_like(dcw_ref)
            dcb_ref[...] = jnp.zeros_like(dcb_ref)

        hn2, _, _ = _rms_fwd(h1_ref[...], g2_ref[...])
        hn16 = hn2.astype(bf16)
        up_scr[0:HALO, :] = jnp.where(i > 0, upp_ref[0].astype(f32), 0.0)
        up_scr[HALO:HALO + t, :] = up_ref[0].astype(f32)
        up_scr[HALO + t:, :] = upn_ref[0].astype(f32)
        cw = cw_ref[0]
        rows = t + HALO
        cv = cb_ref[0] + cw[0:1] * up_scr[pl.ds(HALO - 2, rows), :]
        for k in range(1, FFN_CONV):
            cv = cv + cw[k:k + 1] * up_scr[pl.ds(HALO - 2 + k, rows), :]
        dh2 = dh2_ref[...]
        dh2n = jnp.where(i < n - 1, dh2n_ref[...], 0.0)
        dh2_ext16 = jnp.concatenate([dh2, dh2n], axis=0).astype(bf16)
        d_act = _dot_nt(dh2_ext16, wdn_ref[0])
        gate, val = cv[:, :FF_HALF], cv[:, FF_HALF:]
        gl, dgl = _gelu_and_grad(gate)
        dcv_scr[:, :FF_HALF] = d_act * val * dgl
        dcv_scr[:, FF_HALF:] = d_act * gl
        act16 = (gl[:t] * val[:t]).astype(bf16)
        acc_dn[...] += _dot_tn(act16, dh2_ext16[:t])
        d_cv = dcv_scr[pl.ds(0, t), :]
        dcb_ref[0] += jnp.sum(d_cv, axis=0, keepdims=True)
        d_up = cw[2:3] * d_cv
        dcw_ref[0, 2:3, :] += jnp.sum(d_cv * up_scr[pl.ds(HALO, t), :], axis=0, keepdims=True)
        for k in range(FFN_CONV - 1):
            dcw_ref[0, k:k + 1, :] += jnp.sum(d_cv * up_scr[pl.ds(HALO - 2 + k, t), :], axis=0, keepdims=True)
            d_up = d_up + cw[k:k + 1] * dcv_scr[pl.ds(2 - k, t), :]
        d_up16 = d_up.astype(bf16)
        part_ref[0] = _dot_nt(d_up16, wup_ref[0])
        acc_up[...] += _dot_tn(hn16, d_up16)

        @pl.when(i == n - 1)
        def _():
            pltpu.sync_copy(acc_up.at[:, pl.ds(0, FF_HALF)], dwup_ref.at[j])
            pltpu.sync_copy(acc_up.at[:, pl.ds(FF_HALF, FF_HALF)], dwup_ref.at[2 + j])
            pltpu.sync_copy(acc_dn, dwdn_ref.at[j])

    outs = pl.pallas_call(
        body, name="ffn_bwd_exchange" if nx else "ffn_bwd", grid=(2, n),
        out_shape=(jax.ShapeDtypeStruct((2, s, D_MODEL), f32), jax.ShapeDtypeStruct((N_CHIPS, D_MODEL, FF_HALF), f32),
                   jax.ShapeDtypeStruct((2, FF_HALF, D_MODEL), f32), jax.ShapeDtypeStruct((2, FFN_CONV, D_FF), f32),
                   jax.ShapeDtypeStruct((2, 1, D_FF), f32))
        + tuple(jax.ShapeDtypeStruct((3,) + a.shape[1:], a.dtype) for a in exchange),
        in_specs=[pl.BlockSpec((t, D_MODEL), lambda j, i: (i, 0)),
                  pl.BlockSpec((HALO, D_MODEL), lambda j, i: (jnp.minimum((i + 1) * hb, last_hb), 0)),
                  pl.BlockSpec((t, D_MODEL), lambda j, i: (i, 0)),
                  pl.BlockSpec((1, t, D_FF), lambda j, i: (j, i, 0)),
                  pl.BlockSpec((1, HALO, D_FF), lambda j, i: (j, jnp.maximum(i * hb - 1, 0), 0)),
                  pl.BlockSpec((1, HALO, D_FF), lambda j, i: (j, jnp.minimum((i + 1) * hb, last_hb), 0)),
                  _const_spec((1, D_MODEL)),
                  pl.BlockSpec((1, D_MODEL, D_FF), lambda j, i: (j, 0, 0), pipeline_mode=pl.Buffered(1)),
                  pl.BlockSpec((1, FFN_CONV, D_FF), lambda j, i: (j, 0, 0)),
                  pl.BlockSpec((1, 1, D_FF), lambda j, i: (j, 0, 0)),
                  pl.BlockSpec((1, FF_HALF, D_MODEL), lambda j, i: (j, 0, 0), pipeline_mode=pl.Buffered(1))] + [_hbm()] * nx,
        out_specs=(pl.BlockSpec((1, t, D_MODEL), lambda j, i: (j, i, 0)), pl.BlockSpec(memory_space=pl.ANY),
                   pl.BlockSpec(memory_space=pl.ANY), pl.BlockSpec((1, FFN_CONV, D_FF), lambda j, i: (j, 0, 0)),
                   pl.BlockSpec((1, 1, D_FF), lambda j, i: (j, 0, 0))) + (_hbm(),) * nx,
        scratch_shapes=[pltpu.VMEM((2 * HALO + t, D_FF), f32), pltpu.VMEM((HALO + t, D_FF), f32),
                        pltpu.VMEM((D_MODEL, D_FF), f32), pltpu.VMEM((FF_HALF, D_MODEL), f32)]
        + ([pltpu.SemaphoreType.DMA((3 * nx,)), pltpu.SemaphoreType.DMA((3 * nx,))] if nx else []),
        compiler_params=_cparams(("arbitrary", "arbitrary")),
    )(dh2, dh2, h1, up16, up16, up16, g2, wup_h, cw_h, cb_h, wdn_h, *exchange)
    return outs[:5] + (list(outs[5:]),)


def _out_bwd(dh2, parts, h1, ymix, g2, w_out, share=()):
    s = h1.shape[0]
    t = _row_tile(s, WIDE_ROW_TILE)
    n = s // t
    ns = len(share)

    def body(*refs):
        dh2_ref, part_ref, h1_ref, ym_ref, g2_ref, wo_ref = refs[:6]
        r_refs = refs[6:6 + ns]
        dh1_ref, dym_ref, dwo_ref, dg2_ref = refs[6 + ns:10 + ns]
        both_refs = refs[10 + ns:10 + 2 * ns]
        sems = refs[10 + 2 * ns:]
        i = pl.program_id(0)
        if ns:
            @pl.when(i == 0)
            def _():
                _ps_start(r_refs, both_refs, *sems)

            @pl.when(i == n - 1)
            def _():
                _ps_finish(r_refs, both_refs, *sems)

        @pl.when(i == 0)
        def _():
            dwo_ref[...] = jnp.zeros_like(dwo_ref)
            dg2_ref[...] = jnp.zeros_like(dg2_ref)

        g2 = g2_ref[...]
        _, xhat, r = _rms_fwd(h1_ref[...], g2)
        dx, dg = _rms_bwd(part_ref[0] + part_ref[1], xhat, r, g2)
        dg2_ref[...] += dg
        dh1 = dh2_ref[...] + dx
        dh1_ref[...] = dh1
        dh16 = dh1.astype(bf16)
        dym_ref[...] = _dot_nt(dh16, wo_ref[...]).astype(bf16)
        dwo_ref[...] += _dot_tn(ym_ref[...], dh16)

    row = lambda w: pl.BlockSpec((t, w), lambda i: (i, 0))
    fixed = lambda shape: pl.BlockSpec(shape, lambda i: (0,) * len(shape))
    outs = pl.pallas_call(
        body, name="out_bwd_share" if ns else "out_bwd", grid=(n,),
        out_shape=(jax.ShapeDtypeStruct((s, D_MODEL), f32), jax.ShapeDtypeStruct((s, D_MODEL), bf16),
                   jax.ShapeDtypeStruct((D_MODEL, D_MODEL), f32), jax.ShapeDtypeStruct((1, D_MODEL), f32))
        + tuple(jax.ShapeDtypeStruct((2 * r.shape[0], r.shape[1]), r.dtype) for r in share),
        in_specs=[row(D_MODEL), pl.BlockSpec((2, t, D_MODEL), lambda i: (0, i, 0)), row(D_MODEL), row(D_MODEL),
                  _const_spec((1, D_MODEL)), _const_spec((D_MODEL, D_MODEL))] + [_hbm()] * ns,
        out_specs=(row(D_MODEL), row(D_MODEL), fixed((D_MODEL, D_MODEL)), fixed((1, D_MODEL))) + (_hbm(),) * ns,
        scratch_shapes=[pltpu.SemaphoreType.DMA((ns,)), pltpu.SemaphoreType.DMA((ns,))] if ns else [],
        compiler_params=_cparams(),
    )(dh2, parts, h1, ymix, g2, w_out, *share)
    return outs[:4] + (_own_half_in(outs[4:], share),)


def _loss_head(h, target, gf):
    s = h.shape[0]
    t = _row_tile(s, WIDE_ROW_TILE)

    def body(h_ref, t_ref, g_ref, dh_ref, dg_ref, loss_ref):
        i = pl.program_id(0)

        @pl.when(i == 0)
        def _():
            dg_ref[...] = jnp.zeros_like(dg_ref)
            loss_ref[...] = jnp.zeros_like(loss_ref)

        g = g_ref[...]
        y, xhat, r = _rms_fwd(h_ref[...], g)
        diff = y - t_ref[...]
        per_row = jnp.mean(diff * diff, axis=-1, keepdims=True)
        loss_ref[...] += 0.5 * jnp.sum(per_row, axis=0, keepdims=True)
        dx, dg = _rms_bwd(diff * (1.0 / D_MODEL), xhat, r, g)
        dg_ref[...] += dg
        dh_ref[...] = dx

    row = pl.BlockSpec((t, D_MODEL), lambda i: (i, 0))
    return pl.pallas_call(
        body, name="loss_head", grid=(s // t,),
        out_shape=(jax.ShapeDtypeStruct((s, D_MODEL), f32), jax.ShapeDtypeStruct((1, D_MODEL), f32),
                   jax.ShapeDtypeStruct((1, LANES), f32)),
        in_specs=[row, row, _const_spec((1, D_MODEL))],
        out_specs=(row, pl.BlockSpec((1, D_MODEL), lambda i: (0, 0)), pl.BlockSpec((1, LANES), lambda i: (0, 0))),
        compiler_params=_cparams(),
    )(h, target, gf)


def _prep_layer(w):
    in4 = w["w_in4"]
    cut = D_IN // N_CHIPS
    tail = 3 * cut - (SSD_W + XBC)
    w_r = jnp.concatenate([in4[0], in4[1], in4[2][:, :cut - tail], in4[3][:, HEADS - tail:], in4[2][:, cut - tail:],
                           in4[3][:, :HEADS - tail], jnp.zeros((D_MODEL, LANES - HEADS), in4.dtype)], axis=1)
    pad8 = lambda v: jnp.concatenate([v, jnp.zeros((LANES - HEADS,), f32)]).reshape(1, LANES)
    halves = _halves
    ssd_prm = (w["ssd_conv_w"], w["ssd_conv_b"].reshape(1, XBC), pad8(w["ssd_dt_bias"]), pad8(w["ssd_a_log"]),
               jnp.repeat(w["ssd_d"], HEAD_DIM).reshape(1, SSD_W), w["ssd_norm_g"].reshape(1, SSD_W),
               w["pool_w"], w["pool_scale"].reshape(1, POOL_W))
    return dict(
        g1=w["mix_norm_g"].reshape(1, D_MODEL), w_r=w_r, ssd=ssd_prm, w_out=w["w_out"], g2=w["ffn_norm_g"].reshape(1, D_MODEL),
        wup_h=w["wup_h"], cw_h=halves(w["ffn_conv_w"]), cb_h=halves(w["ffn_conv_b"].reshape(1, 2 * D_FF)),
        wdn_h=w["wdn_h"], g3=w["ple_norm_g"].reshape(1, D_MODEL), w_gate=w["ple_w_gate"], w_proj=w["ple_w_proj"])


def _halves(a):
    return jnp.stack([jnp.concatenate([a[..., j * FF_HALF:(j + 1) * FF_HALF],
                                       a[..., D_FF + j * FF_HALF:D_FF + (j + 1) * FF_HALF]], axis=-1) for j in range(2)])


def _unhalve(a):
    return jnp.concatenate([a[0][..., :FF_HALF], a[1][..., :FF_HALF], a[0][..., FF_HALF:], a[1][..., FF_HALF:]], axis=-1)


def _layer_fwd(h, p_l, q, gather=()):
    zxu, dtr = _mix_in_fwd(h, q["g1"], q["w_r"])
    ymix, states = _ssd_pool_fwd(zxu, dtr, q["ssd"])
    h1, h2, h3, up16, gathered = _out_ffn_ple_fwd(h, ymix, p_l, q["w_out"], q["g2"], q["wup_h"], q["cw_h"], q["cb_h"],
                                                   q["wdn_h"], q["g3"], q["w_gate"], q["w_proj"], gather)
    return h3, (h, zxu, dtr, ymix, states, h1, h2, up16), gathered


def _layer_bwd(dh, saved, p_l, q, reduce=(), early=False):
    h0, zxu, dtr, ymix, states, h1, h2, up16 = saved
    dh2, d_wg, d_wp, d_g3, lands = _ple_bwd(dh, h2, p_l, q["g3"], q["w_gate"], q["w_proj"], reduce)
    sums = _rs_pair_sums(reduce, lands) if reduce else ()
    parts, d_wup, d_wdn, d_cw, d_cb, lands = _ffn_bwd(dh2, h1, up16, q["g2"], q["wup_h"], q["cw_h"], q["cb_h"], q["wdn_h"],
                                                      sums)
    halves = _rs_chip_sums(sums, lands) if reduce else ()
    dh1, d_ymix, d_wo, d_g2, reduced = _out_bwd(dh2, parts, h1, ymix, q["g2"], q["w_out"], halves)
    early_sums = ()
    if early:
        own = [d_wup, d_wdn.reshape(N_CHIPS, D_FF // N_CHIPS, D_MODEL), _chip_major(d_wg, 0), _chip_major(d_wp, 1)]
        early_sums = _rs_pair_sums(own, _rs_pair_exchange(own))
    (d_zxu, d_dtr, d_scw, d_scb, d_dtb, d_alog, d_dsk, d_ng, d_pw, d_ps, early_lands) = _ssd_pool_bwd(
        d_ymix, zxu, dtr, states, q["ssd"], early_sums)
    dh, d_wr, d_g1 = _mix_in_bwd(d_zxu, d_dtr, h0, dh1, q["g1"], q["w_r"])
    cut = D_IN // N_CHIPS
    tail = 3 * cut - (SSD_W + XBC)
    w_in4 = jnp.stack([d_wr[:, :cut], d_wr[:, cut:2 * cut],
                       jnp.concatenate([d_wr[:, 2 * cut:SSD_W + XBC], d_wr[:, 2048:2048 + tail]], axis=1),
                       jnp.concatenate([d_wr[:, 2048 + tail:2048 + HEADS], d_wr[:, SSD_W + XBC:2048]], axis=1)])
    grads = dict(
        mix_norm_g=d_g1.reshape(D_MODEL), w_in4=w_in4,
        ssd_conv_w=d_scw, ssd_conv_b=d_scb.reshape(XBC), ssd_dt_bias=d_dtb[0, :HEADS], ssd_a_log=d_alog[0, :HEADS],
        ssd_d=jnp.sum(d_dsk.reshape(HEADS, HEAD_DIM), axis=1), ssd_norm_g=d_ng.reshape(SSD_W), pool_w=d_pw,
        pool_scale=d_ps.reshape(POOL_W), w_out=d_wo, ffn_norm_g=d_g2.reshape(D_MODEL), ffn_w_up4=d_wup,
        ffn_conv_w=_unhalve(d_cw), ffn_conv_b=_unhalve(d_cb).reshape(2 * D_FF), wdn_h=d_wdn,
        ple_norm_g=d_g3.reshape(D_MODEL), ple_w_gate=d_wg, ple_w_proj=d_wp)
    if early:
        grads["early"] = (early_sums, early_lands)
    return dh, grads, reduced


MESH = pl.DeviceIdType.MESH
COLS = 1024
N_CHIPS = 4
WEIGHT_NAMES = ["mix_norm_g", "w_in", "ssd_conv_w", "ssd_conv_b", "ssd_dt_bias", "ssd_a_log", "ssd_d", "ssd_norm_g", "pool_w",
                "pool_scale", "w_out", "ffn_norm_g", "ffn_w_up", "ffn_conv_w", "ffn_conv_b", "ffn_w_down", "ple_norm_g",
                "ple_w_gate", "ple_w_proj", "final_norm_g"]
SHARD_AXIS = {"w_in": 2, "ssd_conv_w": 2, "w_out": 1, "ffn_w_up": 2, "ffn_conv_w": 2, "ffn_w_down": 1, "ple_w_gate": 1,
              "ple_w_proj": 2}
MATMUL_SHARDED = ["w_in", "w_out", "ffn_w_up", "ffn_w_down", "ple_w_gate", "ple_w_proj"]
F32_SHARDED = ["ssd_conv_w", "ffn_conv_w"]
REPLICATED = [n for n in WEIGHT_NAMES if n not in SHARD_AXIS]


def _mesh_pos():
    return lax.axis_index("x"), lax.axis_index("y"), lax.axis_index("c")


def _hbm():
    return pl.BlockSpec(memory_space=pl.ANY)


def _all_gather_chips(arrays, name):
    n = len(arrays)

    def body(*refs):
        x_refs, out_refs, sems = refs[:n], refs[n:2 * n], refs[2 * n:]
        _ag_start(x_refs, out_refs, *sems)
        _ag_finish(x_refs, out_refs, *sems)

    gathered = pl.pallas_call(
        body, name=name, out_shape=[jax.ShapeDtypeStruct((N_CHIPS,) + a.shape, a.dtype) for a in arrays],
        in_specs=[_hbm()] * n, out_specs=[_hbm()] * n,
        scratch_shapes=[pltpu.SemaphoreType.DMA((6 * n,)), pltpu.SemaphoreType.DMA((6 * n,))],
    )(*arrays)
    return _own_block_in(gathered, arrays)


def _ag_copies(x_refs, out_refs, send_sems, recv_sems):
    x, y, c = _mesh_pos()
    me = 2 * x + y
    flips = [(1 - x, y), (x, 1 - y), (1 - x, 1 - y)]

    def rows(a, chip, half):
        hr = x_refs[a].shape[0] // 2
        return out_refs[a].at[chip, pl.ds(half * hr, hr), :]

    def copy(k, src, dst, to):
        return pltpu.make_async_remote_copy(src_ref=src, dst_ref=dst, send_sem=send_sems.at[k], recv_sem=recv_sems.at[k],
                                            device_id=to, device_id_type=MESH)

    def first(a, j):
        hr = x_refs[a].shape[0] // 2
        return copy(6 * a + j, x_refs[a].at[pl.ds(c * hr, hr), :], rows(a, me, c), flips[j] + (c,))

    def landed(a, j):
        blk = rows(a, 2 * flips[j][0] + flips[j][1], c)
        return copy(6 * a + j, blk, blk, flips[j] + (c,))

    def passed(a, j, half):
        blk = rows(a, 2 * flips[j][0] + flips[j][1], half)
        return copy(6 * a + 3 + j, blk, blk, (x, y, 1 - c))

    return first, landed, passed, c


def _ag_start(x_refs, out_refs, send_sems, recv_sems):
    first, _, _, _ = _ag_copies(x_refs, out_refs, send_sems, recv_sems)
    for a in range(len(x_refs)):
        for j in range(3):
            first(a, j).start()


def _ag_finish(x_refs, out_refs, send_sems, recv_sems):
    first, landed, passed, c = _ag_copies(x_refs, out_refs, send_sems, recv_sems)
    n = len(x_refs)
    for j in range(3):
        for a in range(n):
            landed(a, j).wait_recv()
            passed(a, j, c).start()
    for j in range(3):
        for a in range(n):
            passed(a, j, 1 - c).wait_recv()
    for a in range(n):
        for j in range(3):
            first(a, j).wait_send()
            passed(a, j, c).wait_send()


def _own_block_in(gathered, arrays):
    if not arrays:
        return []
    me = 2 * lax.axis_index("x") + lax.axis_index("y")
    return [lax.dynamic_update_slice(o, a[None], (me, 0, 0)) for o, a in zip(gathered, arrays)]


def _rs_pair_exchange(gs):
    n = len(gs)

    def body(*refs):
        g_refs, land_refs, sems = refs[:n], refs[n:2 * n], refs[2 * n:]
        for cp in _px_copies(g_refs, land_refs, *sems):
            cp.start()
        for cp in _px_copies(g_refs, land_refs, *sems):
            cp.wait()

    return pl.pallas_call(
        body, name="rs_pair_exchange",
        out_shape=[jax.ShapeDtypeStruct((N_CHIPS, g.shape[1] // 2, g.shape[2]), g.dtype) for g in gs],
        in_specs=[_hbm()] * n, out_specs=[_hbm()] * n,
        scratch_shapes=[pltpu.SemaphoreType.DMA((n,)), pltpu.SemaphoreType.DMA((n,))],
    )(*gs)


def _px_copies(g_refs, land_refs, send_sems, recv_sems):
    x, y, c = _mesh_pos()
    cps = []
    for a in range(len(g_refs)):
        hr = g_refs[a].shape[1] // 2
        cps.append(pltpu.make_async_remote_copy(
            src_ref=g_refs[a].at[:, pl.ds((1 - c) * hr, hr), :], dst_ref=land_refs[a], send_sem=send_sems.at[a],
            recv_sem=recv_sems.at[a], device_id=(x, y, 1 - c), device_id_type=MESH))
    return cps


def _rs_pair_add(gs, lands, c_idx):
    n = len(gs)
    shapes = [(g.shape[1] // 2, g.shape[2]) for g in gs]

    def body(c_ref, *refs):
        for g_ref, l_ref, o_ref in zip(refs[:n], refs[n:2 * n], refs[2 * n:]):
            o_ref[...] = (g_ref[...] + l_ref[...]).astype(bf16)

    return pl.pallas_call(
        body, name="rs_pair_add", out_shape=[jax.ShapeDtypeStruct((N_CHIPS, hr, cols), bf16) for hr, cols in shapes],
        grid_spec=pltpu.PrefetchScalarGridSpec(
            num_scalar_prefetch=1, grid=(N_CHIPS,),
            in_specs=[pl.BlockSpec((1, hr, cols), lambda k, c_ref: (k, c_ref[0], 0)) for hr, cols in shapes]
            + [pl.BlockSpec((1, hr, cols), lambda k, c_ref: (k, 0, 0)) for hr, cols in shapes],
            out_specs=[pl.BlockSpec((1, hr, cols), lambda k, c_ref: (k, 0, 0)) for hr, cols in shapes]),
        compiler_params=_cparams(),
    )(c_idx, *gs, *lands)


def _rs_chip_exchange(parts):
    n = len(parts)

    def body(*refs):
        p_refs, land_refs, sems = refs[:n], refs[n:2 * n], refs[2 * n:]
        _cx_start(p_refs, land_refs, *sems)
        _cx_finish(p_refs, land_refs, *sems)

    return pl.pallas_call(
        body, name="rs_chip_exchange", out_shape=[jax.ShapeDtypeStruct((3,) + p.shape[1:], p.dtype) for p in parts],
        in_specs=[_hbm()] * n, out_specs=[_hbm()] * n,
        scratch_shapes=[pltpu.SemaphoreType.DMA((3 * n,)), pltpu.SemaphoreType.DMA((3 * n,))],
    )(*parts)


def _cx_copies(p_refs, land_refs, send_sems, recv_sems):
    x, y, c = _mesh_pos()
    flips = [(1 - x, y), (x, 1 - y), (1 - x, 1 - y)]
    return [pltpu.make_async_remote_copy(src_ref=p_refs[a].at[2 * fx + fy], dst_ref=land_refs[a].at[j],
                                         send_sem=send_sems.at[3 * a + j], recv_sem=recv_sems.at[3 * a + j],
                                         device_id=(fx, fy, c), device_id_type=MESH)
            for a in range(len(p_refs)) for j, (fx, fy) in enumerate(flips)]


def _cx_start(p_refs, land_refs, send_sems, recv_sems):
    for cp in _cx_copies(p_refs, land_refs, send_sems, recv_sems):
        cp.start()


def _cx_finish(p_refs, land_refs, send_sems, recv_sems):
    for cp in _cx_copies(p_refs, land_refs, send_sems, recv_sems):
        cp.wait()


def _rs_chip_add(parts, lands, me_idx):
    n = len(parts)
    shapes = [p.shape[1:] for p in parts]

    def body(me_ref, *refs):
        for p_ref, l_ref, o_ref in zip(refs[:n], refs[n:2 * n], refs[2 * n:]):
            o_ref[...] = ((p_ref[0].astype(f32) + l_ref[0].astype(f32)) + l_ref[1].astype(f32)) + l_ref[2].astype(f32)

    return pl.pallas_call(
        body, name="rs_chip_add", out_shape=[jax.ShapeDtypeStruct((hr, cols), f32) for hr, cols in shapes],
        grid_spec=pltpu.PrefetchScalarGridSpec(
            num_scalar_prefetch=1, grid=(1,),
            in_specs=[pl.BlockSpec((1, hr, cols), lambda i, me_ref: (me_ref[0], 0, 0)) for hr, cols in shapes]
            + [pl.BlockSpec((3, hr, cols), lambda i, me_ref: (0, 0, 0)) for hr, cols in shapes],
            out_specs=[pl.BlockSpec((hr, cols), lambda i, me_ref: (0, 0)) for hr, cols in shapes]),
        compiler_params=_cparams(),
    )(me_idx, *parts, *lands)


def _rs_pair_share(reds):
    n = len(reds)

    def body(*refs):
        r_refs, out_refs, sems = refs[:n], refs[n:2 * n], refs[2 * n:]
        _ps_start(r_refs, out_refs, *sems)
        _ps_finish(r_refs, out_refs, *sems)

    both = pl.pallas_call(
        body, name="rs_pair_share", out_shape=[jax.ShapeDtypeStruct((2 * r.shape[0], r.shape[1]), r.dtype) for r in reds],
        in_specs=[_hbm()] * n, out_specs=[_hbm()] * n,
        scratch_shapes=[pltpu.SemaphoreType.DMA((n,)), pltpu.SemaphoreType.DMA((n,))],
    )(*reds)
    return _own_half_in(both, reds)


def _ps_copy(r_refs, out_refs, send_sems, recv_sems, a, half):
    x, y, c = _mesh_pos()
    hr = r_refs[a].shape[0]
    return pltpu.make_async_remote_copy(src_ref=r_refs[a], dst_ref=out_refs[a].at[pl.ds(half * hr, hr), :],
                                        send_sem=send_sems.at[a], recv_sem=recv_sems.at[a], device_id=(x, y, 1 - c),
                                        device_id_type=MESH)


def _ps_start(r_refs, out_refs, send_sems, recv_sems):
    c = lax.axis_index("c")
    for a in range(len(r_refs)):
        _ps_copy(r_refs, out_refs, send_sems, recv_sems, a, c).start()


def _ps_finish(r_refs, out_refs, send_sems, recv_sems):
    c = lax.axis_index("c")
    for a in range(len(r_refs)):
        _ps_copy(r_refs, out_refs, send_sems, recv_sems, a, 1 - c).wait_recv()
    for a in range(len(r_refs)):
        _ps_copy(r_refs, out_refs, send_sems, recv_sems, a, c).wait_send()


def _own_half_in(both, reds):
    if not reds:
        return []
    c = lax.axis_index("c")
    return [lax.dynamic_update_slice(o, r, (c * r.shape[0], 0)) for o, r in zip(both, reds)]


def _rs_pair_sums(gs, lands):
    c_idx = jnp.reshape(lax.axis_index("c"), (1,)).astype(jnp.int32)
    return list(_rs_pair_add(list(gs), list(lands), c_idx))


def _rs_chip_sums(parts, lands):
    me_idx = jnp.reshape(2 * lax.axis_index("x") + lax.axis_index("y"), (1,)).astype(jnp.int32)
    return list(_rs_chip_add(list(parts), list(lands), me_idx))


def _pack(parts, row_multiple):
    flat = jnp.concatenate([a.reshape(-1) for a in parts])
    n = flat.shape[0]
    rows = -(-n // COLS)
    rows = -(-rows // row_multiple) * row_multiple
    return jnp.pad(flat, (0, rows * COLS - n)).reshape(rows, COLS)


def _unpack(flat, shapes):
    out, off = [], 0
    for shp in shapes:
        n = math.prod(shp)
        out.append(flat[off:off + n].reshape(shp))
        off += n
    return out


def _adamw(w, g, m, v, name):
    shape = w.shape
    cols = shape[-1]
    rows = math.prod(shape[:-1]) if len(shape) > 1 else 1
    tr = rows
    if rows > 512:
        tr = next(t for t in (512, 256, 128, 64, 32, 16, 8) if rows % t == 0)
    two_d = lambda a: a.reshape(rows, cols)

    def body(w_ref, g_ref, m_ref, v_ref, d_ref, nm_ref, nv_ref):
        gg = g_ref[...]
        nm = ADAM_B1 * m_ref[...] + (1.0 - ADAM_B1) * gg
        nv = ADAM_B2 * v_ref[...] + (1.0 - ADAM_B2) * (gg * gg)
        m_hat = nm / (1.0 - ADAM_B1 ** ADAM_STEP)
        v_hat = nv / (1.0 - ADAM_B2 ** ADAM_STEP)
        d_ref[...] = -ADAM_LR * (m_hat / (jnp.sqrt(v_hat) + ADAM_EPS) + ADAM_WD * w_ref[...])
        nm_ref[...] = nm
        nv_ref[...] = nv

    spec = pl.BlockSpec((tr, cols), lambda i: (i, 0))
    outs = pl.pallas_call(
        body, name="adamw_" + name, grid=(rows // tr,),
        out_shape=tuple(jax.ShapeDtypeStruct((rows, cols), f32) for _ in range(3)),
        in_specs=[spec] * 4, out_specs=(spec,) * 3, compiler_params=_cparams(),
    )(two_d(w), two_d(g), two_d(m), two_d(v))
    return tuple(o.reshape(shape) for o in outs)


BIG_SHARDED = ["ffn_w_up", "ffn_w_down", "w_out", "ple_w_gate", "ple_w_proj", "w_in"]
SMALL_REDUCED = REPLICATED + F32_SHARDED


def _chip_major(a, axis):
    n = a.shape[axis] // N_CHIPS
    return jnp.moveaxis(a.reshape(a.shape[:axis] + (N_CHIPS, n) + a.shape[axis + 1:]), axis, 0)


def _train_step(x, p, loss_target, w, m, v):
    xs, ps, target = x[0], p[:, 0], loss_target[0]
    rows2d = lambda a: a.reshape(-1, a.shape[-1])
    conv_rows = 16
    pad_rows = lambda a: jnp.pad(a, ((0, conv_rows - a.shape[0]), (0, 0)))
    shards = [[w[n][l].astype(bf16) for n in MATMUL_SHARDED] for l in range(DEPTH)]
    first = _all_gather_chips(shards[0] + [pad_rows(rows2d(w[n])) for n in F32_SHARDED], "gather_weights")
    conv4 = dict(zip(F32_SHARDED, first[len(MATMUL_SHARDED):]))

    def col_cut(blk):
        return jnp.moveaxis(blk, 0, 1).reshape(blk.shape[1], N_CHIPS * blk.shape[2])

    def layer_weights(l, gathered):
        g = dict(zip(MATMUL_SHARDED, gathered))
        lw = {n: w[n][l] for n in REPLICATED if n != "final_norm_g"}
        lw["pool_w"] = lw["pool_w"].astype(bf16)
        lw.update(
            w_in4=g["w_in"], ple_w_proj=col_cut(g["ple_w_proj"]), w_out=g["w_out"].reshape(D_MODEL, D_MODEL),
            ple_w_gate=g["ple_w_gate"].reshape(D_MODEL, D_MODEL), wdn_h=g["ffn_w_down"].reshape(2, FF_HALF, D_MODEL),
            wup_h=jnp.stack([jnp.concatenate([g["ffn_w_up"][j], g["ffn_w_up"][2 + j]], axis=1) for j in range(2)]),
            ssd_conv_w=col_cut(conv4["ssd_conv_w"][:, l * SSD_CONV:(l + 1) * SSD_CONV, :]),
            ffn_conv_w=col_cut(conv4["ffn_conv_w"][:, l * FFN_CONV:(l + 1) * FFN_CONV, :]))
        return _prep_layer(lw)

    preps, saved = [], []
    h, gathered = xs, first[:len(MATMUL_SHARDED)]
    for l in range(DEPTH):
        preps.append(layer_weights(l, gathered))
        h, keep, gathered = _layer_fwd(h, (ps, l), preps[l], gather=shards[l + 1] if l + 1 < DEPTH else ())
        saved.append(keep)
    grad_x, d_gf, loss_part = _loss_head(h, target, w["final_norm_g"].reshape(1, D_MODEL))

    def chip_major_grads(g):
        return [g["ffn_w_up4"], g["wdn_h"].reshape(N_CHIPS, D_FF // N_CHIPS, D_MODEL), _chip_major(g["w_out"], 0),
                _chip_major(g["ple_w_gate"], 0), _chip_major(g["ple_w_proj"], 1), g["w_in4"]]

    grads, reduced, pending = [None] * DEPTH, [None] * DEPTH, ()
    for l in reversed(range(DEPTH)):
        grad_x, grads[l], done = _layer_bwd(grad_x, saved[l], (ps, l), preps[l], reduce=pending, early=(l == 0))
        if pending:
            reduced[l + 1] = done
        pending = chip_major_grads(grads[l])
    small = _pack([d_gf.reshape(D_MODEL) if n == "final_norm_g" else jnp.stack([grads[k][n] for k in range(DEPTH)])
                   for n in SMALL_REDUCED], 32 * N_CHIPS)
    early_sums, early_lands = grads[0]["early"]
    late = [pending[2], pending[5], small.reshape(N_CHIPS, -1, COLS)]
    late_sums = _rs_pair_sums(late, _rs_pair_exchange(late))
    up_r, dn_r, gate_r, proj_r, out_r, in_r, small_r = _rs_pair_share(
        _rs_chip_sums(list(early_sums) + late_sums, list(early_lands) + list(_rs_chip_exchange(late_sums))))
    reduced[0] = [up_r, dn_r, out_r, gate_r, proj_r, in_r, small_r]
    grad = {n: jnp.stack([reduced[l][i] for l in range(DEPTH)]) for i, n in enumerate(BIG_SHARDED)}
    small_all = _all_gather_chips([reduced[0][-1]], "gather_small_grads")[0].reshape(-1)
    small_shapes = [w[n].shape for n in REPLICATED] + [(DEPTH, SSD_CONV, XBC), (DEPTH, FFN_CONV, 2 * D_FF)]
    grad.update(zip(SMALL_REDUCED, _unpack(small_all, small_shapes)))
    me = 2 * lax.axis_index("x") + lax.axis_index("y")
    for n in F32_SHARDED:
        grad[n] = lax.dynamic_slice_in_dim(grad[n], me * w[n].shape[2], w[n].shape[2], axis=2)

    loss = lax.psum(loss_part[0, 0], ("x", "y", "c"))
    delta, new_m, new_v = {}, {}, {}
    for n in WEIGHT_NAMES:
        delta[n], new_m[n], new_v[n] = _adamw(w[n], grad[n], m[n], v[n], n)
    return (loss, grad_x[None], *[grad[n] for n in WEIGHT_NAMES], *[delta[n] for n in WEIGHT_NAMES],
            *[new_m[n] for n in WEIGHT_NAMES], *[new_v[n] for n in WEIGHT_NAMES])


def kernel(x, p, mix_norm_g, w_in, ssd_conv_w, ssd_conv_b, ssd_dt_bias, ssd_a_log, ssd_d, ssd_norm_g, pool_w, pool_scale, w_out, ffn_norm_g, ffn_w_up, ffn_conv_w, ffn_conv_b, ffn_w_down, ple_norm_g, ple_w_gate, ple_w_proj, final_norm_g, loss_target, m_mix_norm_g, m_w_in, m_ssd_conv_w, m_ssd_conv_b, m_ssd_dt_bias, m_ssd_a_log, m_ssd_d, m_ssd_norm_g, m_pool_w, m_pool_scale, m_w_out, m_ffn_norm_g, m_ffn_w_up, m_ffn_conv_w, m_ffn_conv_b, m_ffn_w_down, m_ple_norm_g, m_ple_w_gate, m_ple_w_proj, m_final_norm_g, v_mix_norm_g, v_w_in, v_ssd_conv_w, v_ssd_conv_b, v_ssd_dt_bias, v_ssd_a_log, v_ssd_d, v_ssd_norm_g, v_pool_w, v_pool_scale, v_w_out, v_ffn_norm_g, v_ffn_w_up, v_ffn_conv_w, v_ffn_conv_b, v_ffn_w_down, v_ple_norm_g, v_ple_w_gate, v_ple_w_proj, v_final_norm_g):
    w = dict(mix_norm_g=mix_norm_g, w_in=w_in, ssd_conv_w=ssd_conv_w, ssd_conv_b=ssd_conv_b, ssd_dt_bias=ssd_dt_bias, ssd_a_log=ssd_a_log, ssd_d=ssd_d, ssd_norm_g=ssd_norm_g, pool_w=pool_w, pool_scale=pool_scale, w_out=w_out, ffn_norm_g=ffn_norm_g, ffn_w_up=ffn_w_up, ffn_conv_w=ffn_conv_w, ffn_conv_b=ffn_conv_b, ffn_w_down=ffn_w_down, ple_norm_g=ple_norm_g, ple_w_gate=ple_w_gate, ple_w_proj=ple_w_proj, final_norm_g=final_norm_g)
    m = dict(mix_norm_g=m_mix_norm_g, w_in=m_w_in, ssd_conv_w=m_ssd_conv_w, ssd_conv_b=m_ssd_conv_b, ssd_dt_bias=m_ssd_dt_bias, ssd_a_log=m_ssd_a_log, ssd_d=m_ssd_d, ssd_norm_g=m_ssd_norm_g, pool_w=m_pool_w, pool_scale=m_pool_scale, w_out=m_w_out, ffn_norm_g=m_ffn_norm_g, ffn_w_up=m_ffn_w_up, ffn_conv_w=m_ffn_conv_w, ffn_conv_b=m_ffn_conv_b, ffn_w_down=m_ffn_w_down, ple_norm_g=m_ple_norm_g, ple_w_gate=m_ple_w_gate, ple_w_proj=m_ple_w_proj, final_norm_g=m_final_norm_g)
    v = dict(mix_norm_g=v_mix_norm_g, w_in=v_w_in, ssd_conv_w=v_ssd_conv_w, ssd_conv_b=v_ssd_conv_b, ssd_dt_bias=v_ssd_dt_bias, ssd_a_log=v_ssd_a_log, ssd_d=v_ssd_d, ssd_norm_g=v_ssd_norm_g, pool_w=v_pool_w, pool_scale=v_pool_scale, w_out=v_w_out, ffn_norm_g=v_ffn_norm_g, ffn_w_up=v_ffn_w_up, ffn_conv_w=v_ffn_conv_w, ffn_conv_b=v_ffn_conv_b, ffn_w_down=v_ffn_w_down, ple_norm_g=v_ple_norm_g, ple_w_gate=v_ple_w_gate, ple_w_proj=v_ple_w_proj, final_norm_g=v_final_norm_g)
    return _train_step(x, p, loss_target, w, m, v)
```

```python
import functools
import math

import jax
import jax.numpy as jnp
from jax import lax
from jax.experimental import pallas as pl
from jax.experimental.pallas import tpu as pltpu

f32, bf16 = jnp.float32, jnp.bfloat16
HI = lax.Precision.HIGHEST

D_MODEL = 1024
D_PLE = 256
DEPTH = 4
SSD_W = 512
HEADS = 8
HEAD_DIM = 64
NSTATE = 128
CHUNK = 128
SSD_CONV = 4
XBC = 1024
POOL_W = 512
POOL_G = 128
WINDOWS = (2, 4, 8, 16)
D_FF = 2816
FF_HALF = D_FF // 2
FFN_CONV = 3
D_IN = 2056
EPS = 1e-6
ADAM_LR, ADAM_B1, ADAM_B2, ADAM_EPS, ADAM_WD, ADAM_STEP = 0.001, 0.9, 0.999, 1e-08, 0.01, 10

LANES = 128
NPROJ = 2048 + LANES
HALO = 16
VMEM_LIMIT = 58 * 1024 * 1024
ROW_TILE = 256


def _dot(a, b):
    return jnp.dot(a, b, preferred_element_type=f32)


def _dot_nt(a, b):
    return lax.dot_general(a, b, (((1,), (1,)), ((), ())), preferred_element_type=f32)


def _dot_tn(a, b):
    return lax.dot_general(a, b, (((0,), (0,)), ((), ())), preferred_element_type=f32)


def _dot_hi(a, b):
    return jnp.dot(a, b, precision=HI, preferred_element_type=f32)


def _rms_fwd(x, g):
    r = lax.rsqrt(jnp.mean(x * x, axis=-1, keepdims=True) + EPS)
    xhat = x * r
    return xhat * g, xhat, r


def _rms_bwd(dy, xhat, r, g):
    dxhat = dy * g
    dx = r * (dxhat - xhat * jnp.mean(dxhat * xhat, axis=-1, keepdims=True))
    return dx, jnp.sum(dy * xhat, axis=0, keepdims=True)


def _sigmoid(x):
    return 1.0 / (1.0 + jnp.exp(-x))


_GELU_C = math.sqrt(2.0 / math.pi)


def _gelu_and_grad(x):
    x2 = x * x
    t = jnp.tanh(x * (_GELU_C + (_GELU_C * 0.044715) * x2))
    u = 0.5 * t + 0.5
    g = x * u
    dg = u + g * (1.0 - t) * (_GELU_C + (3.0 * 0.044715 * _GELU_C) * x2)
    return g, dg


def _gelu(x):
    return x * (0.5 * jnp.tanh(x * (_GELU_C + (_GELU_C * 0.044715) * (x * x))) + 0.5)


def _softplus(x):
    return jnp.maximum(x, 0.0) + jnp.log(1.0 + jnp.exp(-jnp.abs(x)))


def _cparams(sem=("arbitrary",)):
    return pltpu.CompilerParams(dimension_semantics=sem, vmem_limit_bytes=VMEM_LIMIT)


def _const_spec(shape):
    nd = len(shape)
    return pl.BlockSpec(shape, lambda *_: (0,) * nd, pipeline_mode=pl.Buffered(1))


WIDE_ROW_TILE = 512


def _row_tile(s, t=ROW_TILE):
    return min(t, s)


def _mix_in_fwd(h, g1, w_r):
    s = h.shape[0]
    t = _row_tile(s, WIDE_ROW_TILE)

    def body(h_ref, g_ref, w_ref, zxu_ref, dtr_ref):
        hn, _, _ = _rms_fwd(h_ref[...], g_ref[...])
        proj = _dot(hn.astype(bf16), w_ref[...])
        zxu_ref[...] = proj[:, :2048].astype(bf16)
        dtr_ref[...] = proj[:, 2048:]

    return pl.pallas_call(
        body, name="mix_in_fwd", grid=(s // t,),
        out_shape=(jax.ShapeDtypeStruct((s, 2048), bf16), jax.ShapeDtypeStruct((s, LANES), f32)),
        in_specs=[pl.BlockSpec((t, D_MODEL), lambda i: (i, 0)), _const_spec((1, D_MODEL)), _const_spec((D_MODEL, NPROJ))],
        out_specs=(pl.BlockSpec((t, 2048), lambda i: (i, 0)), pl.BlockSpec((t, LANES), lambda i: (i, 0))),
        compiler_params=_cparams(),
    )(h, g1, w_r)


def _mix_in_bwd(d_zxu, d_dtr, h, dh1, g1, w_r):
    s = h.shape[0]
    t = _row_tile(s, WIDE_ROW_TILE)
    n = s // t

    def body(dz_ref, dd_ref, h_ref, dh1_ref, g_ref, w_ref, dh_ref, dw_ref, dg_ref, acc):
        i = pl.program_id(0)

        @pl.when(i == 0)
        def _():
            acc[...] = jnp.zeros_like(acc)
            dg_ref[...] = jnp.zeros_like(dg_ref)

        g = g_ref[...]
        hn, xhat, r = _rms_fwd(h_ref[...], g)
        dproj = jnp.concatenate([dz_ref[...], dd_ref[...].astype(bf16)], axis=1)
        d_hn = _dot_nt(dproj, w_ref[...])
        acc[...] += _dot_tn(hn.astype(bf16), dproj)
        dx, dg = _rms_bwd(d_hn, xhat, r, g)
        dg_ref[...] += dg
        dh_ref[...] = dh1_ref[...] + dx

        @pl.when(i == n - 1)
        def _():
            pltpu.sync_copy(acc, dw_ref)

    return pl.pallas_call(
        body, name="mix_in_bwd", grid=(n,),
        out_shape=(jax.ShapeDtypeStruct((s, D_MODEL), f32), jax.ShapeDtypeStruct((D_MODEL, NPROJ), f32),
                   jax.ShapeDtypeStruct((1, D_MODEL), f32)),
        in_specs=[pl.BlockSpec((t, 2048), lambda i: (i, 0)), pl.BlockSpec((t, LANES), lambda i: (i, 0)),
                  pl.BlockSpec((t, D_MODEL), lambda i: (i, 0)), pl.BlockSpec((t, D_MODEL), lambda i: (i, 0)),
                  _const_spec((1, D_MODEL)), _const_spec((D_MODEL, NPROJ))],
        out_specs=(pl.BlockSpec((t, D_MODEL), lambda i: (i, 0)), pl.BlockSpec(memory_space=pl.ANY),
                   pl.BlockSpec((1, D_MODEL), lambda i: (0, 0))),
        scratch_shapes=[pltpu.VMEM((D_MODEL, NPROJ), f32)],
        compiler_params=_cparams(),
    )(d_zxu, d_dtr, h, dh1, g1, w_r)


def _iota2(shape, dim):
    return lax.broadcasted_iota(jnp.int32, shape, dim)


def _lane_bcast(a, h):
    return jnp.broadcast_to(a[:, h:h + 1], (a.shape[0], LANES))


def _to_columns(cols):
    lane = _iota2((cols[0].shape[0], LANES), 1)
    out = jnp.where(lane == 0, cols[0], 0.0)
    for h in range(1, len(cols)):
        out = out + jnp.where(lane == h, cols[h], 0.0)
    return out


def _ssd_pre(zx, dtr, xext, cw, cb, dtb, alog):
    c = cb + cw[0:1] * xext[pl.ds(HALO - 3, CHUNK), :]
    for k in range(1, SSD_CONV):
        c = c + cw[k:k + 1] * xext[pl.ds(HALO - 3 + k, CHUNK), :]
    sig_c = _sigmoid(c)
    xc = c * sig_c
    dt = _softplus(dtr + dtb)
    a_neg = -jnp.exp(alog)
    a = dt * a_neg
    lo = _iota2((CHUNK, LANES), 1) < HEAD_DIM
    dt_w = jnp.concatenate([jnp.where(lo, _lane_bcast(dt, 2 * pp), _lane_bcast(dt, 2 * pp + 1)) for pp in range(4)], axis=1)
    xs = xc[:, :SSD_W]
    xd = xs * dt_w
    tril = (_iota2((CHUNK, CHUNK), 0) >= _iota2((CHUNK, CHUNK), 1))
    acs = _dot_hi(tril.astype(f32), a)
    acs_b = [_lane_bcast(acs, h) for h in range(HEADS)]
    groups = []
    for g in range(2):
        b16 = xc[:, SSD_W + g * NSTATE:SSD_W + (g + 1) * NSTATE].astype(bf16)
        c16 = xc[:, SSD_W + 2 * NSTATE + g * NSTATE:SSD_W + 2 * NSTATE + (g + 1) * NSTATE].astype(bf16)
        groups.append((b16, c16, _dot_nt(c16, b16)))
    return dict(c=c, sig_c=sig_c, xc=xc, xs=xs, dt=dt, a_neg=a_neg, a=a, dt_w=dt_w, xd=xd, acs_b=acs_b, tril=tril, lo=lo,
                groups=groups)


def _pair_fwd(q, pp, s_in):
    lo = q["lo"]
    b16, c16, gmat = q["groups"][pp // 2]
    xp = q["xd"][:, pp * LANES:(pp + 1) * LANES]
    xp16 = xp.astype(bf16)
    ab0 = q["acs_b"][2 * pp]
    ab1 = q["acs_b"][2 * pp + 1]
    ls, ms, ys = [], [], []
    for ab in (ab0, ab1):
        lmat = jnp.exp(jnp.where(q["tril"], ab - ab.T, -jnp.inf))
        mmat = gmat * lmat
        ls.append(lmat)
        ms.append(mmat)
        ys.append(_dot(mmat.astype(bf16), xp16))
    y_diag = jnp.where(lo, ys[0], ys[1])
    ab_pair = jnp.where(lo, ab0, ab1)
    e_pair = jnp.exp(ab_pair)
    s16 = s_in.astype(bf16)
    y_off = _dot(c16, s16) * e_pair
    alast = ab_pair[CHUNK - 1:CHUNK, :]
    dec_pair = jnp.exp(alast - ab_pair)
    xdec = xp * dec_pair
    st = _dot_tn(b16, xdec.astype(bf16))
    cd_pair = jnp.exp(alast)
    s_out = s_in * cd_pair + st
    return dict(b16=b16, c16=c16, gmat=gmat, xp=xp, xp16=xp16, ls=ls, ms=ms, y=y_diag + y_off, y_off=y_off,
                e_pair=e_pair, dec_pair=dec_pair, xdec=xdec, cd_pair=cd_pair, s_out=s_out, s16=s16, lo=lo)


def _gate_norm_fwd(y_pre, z, ng):
    sz = _sigmoid(z)
    yg = y_pre * (z * sz)
    outs, stats = [], []
    half = SSD_W // 2
    for gi in range(2):
        o, xhat, r = _rms_fwd(yg[:, gi * half:(gi + 1) * half], ng[:, gi * half:(gi + 1) * half])
        outs.append(o)
        stats.append((xhat, r))
    return jnp.concatenate(outs, axis=1), sz, stats


def _pool_fwd(uext, u, row0, pw_ref, ps):
    pos = (row0 + _iota2((CHUNK, 1), 0) + 1).astype(f32)
    pooled, mixed, invs = [], [], []
    for gi, w in enumerate(WINDOWS):
        sl = slice(gi * POOL_G, (gi + 1) * POOL_G)
        acc = uext[pl.ds(HALO, CHUNK), sl]
        for j in range(1, w):
            acc = acc + uext[pl.ds(HALO - j, CHUNK), sl]
        den = jnp.minimum(pos, float(w))
        pg = acc / den - u[:, sl]
        pooled.append(pg)
        invs.append(den)
        mixed.append(_dot(pg.astype(bf16), pw_ref[gi]))
    mixed = jnp.concatenate(mixed, axis=1)
    return mixed * ps, pooled, mixed, invs


def _ssd_specs(s):
    nc = s // CHUNK
    hb = CHUNK // HALO
    return nc, hb


def _ssd_param_specs():
    return [_const_spec((SSD_CONV, XBC)), _const_spec((1, XBC)), _const_spec((1, LANES)), _const_spec((1, LANES)),
            _const_spec((1, SSD_W)), _const_spec((1, SSD_W)), _const_spec((4, POOL_G, POOL_G)), _const_spec((1, POOL_W))]


def _ssd_pool_fwd(zxu, dtr, prm):
    s = zxu.shape[0]
    nc, hb = _ssd_specs(s)

    def body(zx_ref, halo_ref, dtr_ref, cw_ref, cb_ref, dtb_ref, alog_ref, dsk_ref, ng_ref, pw_ref, ps_ref,
             ymix_ref, st_ref, state, xext, uext):
        i = pl.program_id(0)

        @pl.when(i == 0)
        def _():
            state[...] = jnp.zeros_like(state)

        zx = zx_ref[...].astype(f32)
        halo = jnp.where(i > 0, halo_ref[...].astype(f32), 0.0)
        xext[0:HALO, :] = halo[:, SSD_W:SSD_W + XBC]
        xext[HALO:, :] = zx[:, SSD_W:SSD_W + XBC]
        uext[0:HALO, :] = halo[:, SSD_W + XBC:]
        uext[HALO:, :] = zx[:, SSD_W + XBC:]
        q = _ssd_pre(zx, dtr_ref[...], xext, cw_ref[...], cb_ref[...], dtb_ref[...], alog_ref[...])
        ys = []
        for pp in range(4):
            s_in = state[pp]
            st_ref[0, pp] = s_in
            r = _pair_fwd(q, pp, s_in)
            state[pp] = r["s_out"]
            ys.append(r["y"])
        y_pre = jnp.concatenate(ys, axis=1) + q["xs"] * dsk_ref[...]
        y_ssd, _, _ = _gate_norm_fwd(y_pre, zx[:, :SSD_W], ng_ref[...])
        y_pool, _, _, _ = _pool_fwd(uext, zx[:, SSD_W + XBC:], i * CHUNK, pw_ref, ps_ref[...])
        ymix_ref[:, :SSD_W] = y_ssd.astype(bf16)
        ymix_ref[:, SSD_W:] = y_pool.astype(bf16)

    return pl.pallas_call(
        body, name="ssd_pool_fwd", grid=(nc,),
        out_shape=(jax.ShapeDtypeStruct((s, D_MODEL), bf16), jax.ShapeDtypeStruct((nc, 4, NSTATE, LANES), f32)),
        in_specs=[pl.BlockSpec((CHUNK, 2048), lambda i: (i, 0)),
                  pl.BlockSpec((HALO, 2048), lambda i: (jnp.maximum(i * hb - 1, 0), 0)),
                  pl.BlockSpec((CHUNK, LANES), lambda i: (i, 0))] + _ssd_param_specs(),
        out_specs=(pl.BlockSpec((CHUNK, D_MODEL), lambda i: (i, 0)),
                   pl.BlockSpec((1, 4, NSTATE, LANES), lambda i: (i, 0, 0, 0))),
        scratch_shapes=[pltpu.VMEM((4, NSTATE, LANES), f32), pltpu.VMEM((HALO + CHUNK, XBC), f32),
                        pltpu.VMEM((HALO + CHUNK, POOL_W), f32)],
        compiler_params=_cparams(),
    )(zxu, zxu, dtr, *prm)


def _ssd_pool_bwd(d_ymix, zxu, dtr, states, prm, exchange=()):
    s = zxu.shape[0]
    nc, hb = _ssd_specs(s)
    rev = lambda i: nc - 1 - i
    nx = len(exchange)

    def body(*refs):
        (dy_ref, zx_ref, halo_ref, dtr_ref, st_ref, cw_ref, cb_ref, dtb_ref, alog_ref, dsk_ref, ng_ref, pw_ref,
         ps_ref) = refs[:13]
        p_refs = refs[13:13 + nx]
        (dzx_ref, ddtr_ref, dcw_ref, dcb_ref, ddtb_ref, dalog_ref, ddsk_ref, dng_ref, dpw_ref,
         dps_ref) = refs[13 + nx:23 + nx]
        land_refs = refs[23 + nx:23 + 2 * nx]
        dstate, xext, uext, dxext, duext, cx, cu = refs[23 + 2 * nx:30 + 2 * nx]
        sems = refs[30 + 2 * nx:]
        i = pl.program_id(0)
        ci = nc - 1 - i
        if nx:
            @pl.when(i == 0)
            def _():
                _cx_start(p_refs, land_refs, *sems)

            @pl.when(i == nc - 1)
            def _():
                _cx_finish(p_refs, land_refs, *sems)

        @pl.when(i == 0)
        def _():
            dstate[...] = jnp.zeros_like(dstate)
            cx[...] = jnp.zeros_like(cx)
            cu[...] = jnp.zeros_like(cu)
            for r in (dcw_ref, dcb_ref, ddtb_ref, dalog_ref, ddsk_ref, dng_ref, dpw_ref, dps_ref):
                r[...] = jnp.zeros_like(r)

        zx = zx_ref[...].astype(f32)
        halo = jnp.where(ci > 0, halo_ref[...].astype(f32), 0.0)
        xext[0:HALO, :] = halo[:, SSD_W:SSD_W + XBC]
        xext[HALO:, :] = zx[:, SSD_W:SSD_W + XBC]
        uext[0:HALO, :] = halo[:, SSD_W + XBC:]
        uext[HALO:, :] = zx[:, SSD_W + XBC:]
        cw = cw_ref[...]
        q = _ssd_pre(zx, dtr_ref[...], xext, cw, cb_ref[...], dtb_ref[...], alog_ref[...])
        z = zx[:, :SSD_W]
        dy = dy_ref[...].astype(f32)
        d_yssd, d_ypool = dy[:, :SSD_W], dy[:, SSD_W:]

        pairs = [_pair_fwd(q, pp, st_ref[0, pp]) for pp in range(4)]
        dsk = dsk_ref[...]
        ng = ng_ref[...]
        y_pre = jnp.concatenate([r["y"] for r in pairs], axis=1) + q["xs"] * dsk
        _, sz, stats = _gate_norm_fwd(y_pre, z, ng)

        half = SSD_W // 2
        d_yg, d_ng = [], []
        for gi in range(2):
            xhat, r = stats[gi]
            dx, dg = _rms_bwd(d_yssd[:, gi * half:(gi + 1) * half], xhat, r, ng[:, gi * half:(gi + 1) * half])
            d_yg.append(dx)
            d_ng.append(dg)
        d_yg = jnp.concatenate(d_yg, axis=1)
        dng_ref[...] += jnp.concatenate(d_ng, axis=1)
        silu_z = z * sz
        d_ypre = d_yg * silu_z
        d_z = d_yg * y_pre * (sz * (1.0 + z * (1.0 - sz)))
        ddsk_ref[...] += jnp.sum(d_ypre * q["xs"], axis=0, keepdims=True)

        d_xd, acs_cols, dt_cols = [], [], []
        d_b = [None, None]
        d_c = [None, None]
        d_g = [None, None]
        last_row = _iota2((CHUNK, LANES), 0) == CHUNK - 1
        for pp in range(4):
            g = pp // 2
            r = pairs[pp]
            lo = r["lo"]
            dyp = d_ypre[:, pp * LANES:(pp + 1) * LANES]
            dyp16 = dyp.astype(bf16)
            ds_out = dstate[pp]
            ds16 = ds_out.astype(bf16)
            dye16 = (dyp * r["e_pair"]).astype(bf16)
            dstate[pp] = r["cd_pair"] * ds_out + _dot_tn(r["c16"], dye16)
            dc = _dot_nt(dye16, r["s16"])
            dxdec = _dot(r["b16"], ds16)
            db = _dot_nt(r["xdec"].astype(bf16), ds16)
            dxp = dxdec * r["dec_pair"]
            t2 = dxdec * r["xdec"]
            tail = jnp.sum(t2, axis=0, keepdims=True) + jnp.sum(ds_out * st_ref[0, pp] * r["cd_pair"], axis=0, keepdims=True)
            rp = dyp * r["y_off"] - t2 + jnp.where(last_row, tail, 0.0)
            dxs = []
            for hh in range(2):
                msk = lo if hh == 0 else jnp.logical_not(lo)
                m16 = r["ms"][hh].astype(bf16)
                dxs.append(_dot_tn(m16, dyp16))
                dm = _dot_nt(jnp.where(msk, dyp, 0.0).astype(bf16), r["xp16"])
                wmat = dm * r["ms"][hh]
                acs_cols.append(jnp.sum(wmat - wmat.T + jnp.where(msk, rp, 0.0), axis=1, keepdims=True))
                dgh = dm * r["ls"][hh]
                d_g[g] = dgh if d_g[g] is None else d_g[g] + dgh
            dxp = dxp + jnp.where(lo, dxs[0], dxs[1])
            d_xd.append(dxp)
            xprod = dxp * q["xs"][:, pp * LANES:(pp + 1) * LANES]
            dt_cols.append(jnp.sum(jnp.where(lo, xprod, 0.0), axis=1, keepdims=True))
            dt_cols.append(jnp.sum(jnp.where(lo, 0.0, xprod), axis=1, keepdims=True))
            d_b[g] = db if d_b[g] is None else d_b[g] + db
            d_c[g] = dc if d_c[g] is None else d_c[g] + dc
        for g in range(2):
            dg16 = d_g[g].astype(bf16)
            d_c[g] = d_c[g] + _dot(dg16, pairs[2 * g]["b16"])
            d_b[g] = d_b[g] + _dot_tn(dg16, pairs[2 * g]["c16"])
        d_xd = jnp.concatenate(d_xd, axis=1)
        triu = (_iota2((CHUNK, CHUNK), 0) <= _iota2((CHUNK, CHUNK), 1)).astype(f32)
        d_a = _dot_hi(triu, _to_columns(acs_cols))
        d_dt = d_a * q["a_neg"] + _to_columns(dt_cols)
        dalog_ref[...] += jnp.sum(d_a * q["dt"], axis=0, keepdims=True) * q["a_neg"]
        d_dtr = d_dt * _sigmoid(dtr_ref[...] + dtb_ref[...])
        ddtr_ref[...] = d_dtr
        ddtb_ref[...] += jnp.sum(d_dtr, axis=0, keepdims=True)
        d_xs = d_ypre * dsk + d_xd * q["dt_w"]

        d_xc = jnp.concatenate([d_xs, d_b[0], d_b[1], d_c[0], d_c[1]], axis=1)
        sc = q["sig_c"]
        d_conv = d_xc * (sc * (1.0 + q["c"] * (1.0 - sc)))
        dcb_ref[...] += jnp.sum(d_conv, axis=0, keepdims=True)
        dxext[...] = jnp.zeros_like(dxext)
        for k in range(SSD_CONV):
            dcw_ref[k:k + 1, :] += jnp.sum(d_conv * xext[pl.ds(HALO - 3 + k, CHUNK), :], axis=0, keepdims=True)
            dxext[pl.ds(HALO - 3 + k, CHUNK), :] += cw[k:k + 1] * d_conv
        dxext[pl.ds(CHUNK, HALO), :] += cx[...]
        cx[...] = dxext[0:HALO, :]

        ps = ps_ref[...]
        u = zx[:, SSD_W + XBC:]
        _, pooled, mixed, dens = _pool_fwd(uext, u, ci * CHUNK, pw_ref, ps)
        dps_ref[...] += jnp.sum(d_ypool * mixed, axis=0, keepdims=True)
        d_mixed = d_ypool * ps
        duext[...] = jnp.zeros_like(duext)
        for gi, w in enumerate(WINDOWS):
            sl = slice(gi * POOL_G, (gi + 1) * POOL_G)
            dm16 = d_mixed[:, sl].astype(bf16)
            dpw_ref[gi] += _dot_tn(pooled[gi].astype(bf16), dm16)
            d_pg = _dot_nt(dm16, pw_ref[gi])
            d_mean = d_pg / dens[gi]
            duext[pl.ds(HALO, CHUNK), sl] += d_mean - d_pg
            for j in range(1, w):
                duext[pl.ds(HALO - j, CHUNK), sl] += d_mean
        duext[pl.ds(CHUNK, HALO), :] += cu[...]
        cu[...] = duext[0:HALO, :]

        dzx_ref[:, :SSD_W] = d_z.astype(bf16)
        dzx_ref[:, SSD_W:SSD_W + XBC] = dxext[HALO:, :].astype(bf16)
        dzx_ref[:, SSD_W + XBC:] = duext[HALO:, :].astype(bf16)

    small = lambda shape: pl.BlockSpec(shape, lambda i: (0,) * len(shape))
    small_shapes = [(SSD_CONV, XBC), (1, XBC), (1, LANES), (1, LANES), (1, SSD_W), (1, SSD_W), (4, POOL_G, POOL_G), (1, POOL_W)]
    outs = pl.pallas_call(
        body, name="ssd_pool_bwd_exchange" if nx else "ssd_pool_bwd", grid=(nc,),
        out_shape=(jax.ShapeDtypeStruct((s, 2048), bf16), jax.ShapeDtypeStruct((s, LANES), f32))
        + tuple(jax.ShapeDtypeStruct(sh, f32) for sh in small_shapes)
        + tuple(jax.ShapeDtypeStruct((3,) + a.shape[1:], a.dtype) for a in exchange),
        in_specs=[pl.BlockSpec((CHUNK, D_MODEL), lambda i: (rev(i), 0)),
                  pl.BlockSpec((CHUNK, 2048), lambda i: (rev(i), 0)),
                  pl.BlockSpec((HALO, 2048), lambda i: (jnp.maximum(rev(i) * hb - 1, 0), 0)),
                  pl.BlockSpec((CHUNK, LANES), lambda i: (rev(i), 0)),
                  pl.BlockSpec((1, 4, NSTATE, LANES), lambda i: (rev(i), 0, 0, 0))] + _ssd_param_specs() + [_hbm()] * nx,
        out_specs=(pl.BlockSpec((CHUNK, 2048), lambda i: (rev(i), 0)), pl.BlockSpec((CHUNK, LANES), lambda i: (rev(i), 0)))
        + tuple(small(sh) for sh in small_shapes) + (_hbm(),) * nx,
        scratch_shapes=[pltpu.VMEM((4, NSTATE, LANES), f32), pltpu.VMEM((HALO + CHUNK, XBC), f32),
                        pltpu.VMEM((HALO + CHUNK, POOL_W), f32), pltpu.VMEM((HALO + CHUNK, XBC), f32),
                        pltpu.VMEM((HALO + CHUNK, POOL_W), f32), pltpu.VMEM((HALO, XBC), f32), pltpu.VMEM((HALO, POOL_W), f32)]
        + ([pltpu.SemaphoreType.DMA((3 * nx,)), pltpu.SemaphoreType.DMA((3 * nx,))] if nx else []),
        compiler_params=_cparams(),
    )(d_ymix, zxu, zxu, dtr, states, *prm, *exchange)
    return outs[:10] + (list(outs[10:]),)


FFN_BWD_TILE = 256


def _prev_halo_spec(t, width):
    hb = t // HALO
    return pl.BlockSpec((HALO, width), lambda i: (jnp.maximum(i * hb - 1, 0), 0))


def _ffn_half(hn16, j, wup_ref, cw_ref, cb_ref, up_scr, rows):
    up_scr[...] = _dot(hn16, wup_ref[j])
    cw = cw_ref[j]
    cv = cb_ref[j] + cw[0:1] * up_scr[pl.ds(HALO - 2, rows), :]
    for k in range(1, FFN_CONV):
        cv = cv + cw[k:k + 1] * up_scr[pl.ds(HALO - 2 + k, rows), :]
    return cv


def _out_ffn_ple_fwd(h, ymix, p_l, w_out, g2, wup_h, cw_h, cb_h, wdn_h, g3, w_gate, w_proj, gather=()):
    s = h.shape[0]
    t = _row_tile(s)
    n = s // t
    ng = len(gather)

    def body(*refs):
        (h_ref, hh_ref, ym_ref, ymh_ref, p_ref, wo_ref, g2_ref, wup_ref, cw_ref, cb_ref, wdn_ref, g3_ref, wg_ref,
         wp_ref) = refs[:14]
        x_refs = refs[14:14 + ng]
        h1_ref, h2_ref, h3_ref, up16_ref = refs[14 + ng:18 + ng]
        land_refs = refs[18 + ng:18 + 2 * ng]
        up_scr = refs[18 + 2 * ng]
        sems = refs[19 + 2 * ng:]
        i = pl.program_id(0)
        if ng:
            @pl.when(i == 0)
            def _():
                _ag_start(x_refs, land_refs, *sems)

            @pl.when(i == (2 * n) // 3)
            def _():
                _ag_forward(x_refs, land_refs, *sems)

            @pl.when(i == n - 1)
            def _():
                _ag_drain(x_refs, land_refs, *sems)

        hh = jnp.where(i > 0, hh_ref[...], 0.0)
        ymh = jnp.where(i > 0, ymh_ref[...].astype(f32), 0.0)
        h_ext = jnp.concatenate([hh, h_ref[...]], axis=0)
        ym_ext = jnp.concatenate([ymh, ym_ref[...].astype(f32)], axis=0).astype(bf16)
        h1_ext = h_ext + _dot(ym_ext, wo_ref[...])
        hn2, _, _ = _rms_fwd(h1_ext, g2_ref[...])
        hn16 = hn2.astype(bf16)
        h1 = h1_ext[HALO:, :]
        acc = h1
        for j in range(2):
            cv = _ffn_half(hn16, j, wup_ref, cw_ref, cb_ref, up_scr, t)
            up16_ref[j] = up_scr[pl.ds(HALO, t), :].astype(bf16)
            act = _gelu(cv[:, :FF_HALF]) * cv[:, FF_HALF:]
            acc = acc + _dot(act.astype(bf16), wdn_ref[j])
        h2 = acc
        hn3, _, _ = _rms_fwd(h2, g3_ref[...])
        gate = _sigmoid(_dot(hn3.astype(bf16), wg_ref[...]))
        pp = _dot(p_ref[0].astype(bf16), wp_ref[...])
        h1_ref[...] = h1
        h2_ref[...] = h2
        h3_ref[...] = h2 + pp * gate

    row = lambda w: pl.BlockSpec((t, w), lambda i: (i, 0))
    outs = pl.pallas_call(
        body, name="out_ffn_ple_fwd_gather" if ng else "out_ffn_ple_fwd", grid=(n,),
        out_shape=tuple(jax.ShapeDtypeStruct((s, D_MODEL), f32) for _ in range(3)) + (jax.ShapeDtypeStruct((2, s, D_FF), bf16),)
        + tuple(jax.ShapeDtypeStruct((N_CHIPS,) + a.shape, a.dtype) for a in gather),
        in_specs=[row(D_MODEL), _prev_halo_spec(t, D_MODEL), row(D_MODEL), _prev_halo_spec(t, D_MODEL),
                  pl.BlockSpec((1, t, D_PLE), lambda i: (p_l[1], i, 0)),
                  _const_spec((D_MODEL, D_MODEL)), _const_spec((1, D_MODEL)), _const_spec((2, D_MODEL, D_FF)),
                  _const_spec((2, FFN_CONV, D_FF)), _const_spec((2, 1, D_FF)), _const_spec((2, FF_HALF, D_MODEL)),
                  _const_spec((1, D_MODEL)), _const_spec((D_MODEL, D_MODEL)), _const_spec((D_PLE, D_MODEL))] + [_hbm()] * ng,
        out_specs=tuple(row(D_MODEL) for _ in range(3)) + (pl.BlockSpec((2, t, D_FF), lambda i: (0, i, 0)),) + (_hbm(),) * ng,
        scratch_shapes=[pltpu.VMEM((HALO + t, D_FF), f32)]
        + ([pltpu.SemaphoreType.DMA((6 * ng,)), pltpu.SemaphoreType.DMA((6 * ng,))] if ng else []),
        compiler_params=_cparams(),
    )(h, h, ymix, ymix, p_l[0], w_out, g2, wup_h, cw_h, cb_h, wdn_h, g3, w_gate, w_proj, *gather)
    return outs[:4] + (_own_block_in(outs[4:], gather),)


def _ple_bwd(dh3, h2, p_l, g3, w_gate, w_proj, exchange=()):
    s = h2.shape[0]
    t = _row_tile(s, WIDE_ROW_TILE)
    n = s // t
    nx = len(exchange)

    def body(*refs):
        dh3_ref, h2_ref, p_ref, g3_ref, wg_ref, wp_ref = refs[:6]
        g_refs = refs[6:6 + nx]
        dh2_ref, dwg_ref, dwp_ref, dg3_ref = refs[6 + nx:10 + nx]
        land_refs = refs[10 + nx:10 + 2 * nx]
        sems = refs[10 + 2 * nx:]
        i = pl.program_id(0)
        if nx:
            @pl.when(i == 0)
            def _():
                for cp in _px_copies(g_refs, land_refs, *sems):
                    cp.start()

            @pl.when(i == n - 1)
            def _():
                for cp in _px_copies(g_refs, land_refs, *sems):
                    cp.wait()

        @pl.when(i == 0)
        def _():
            dwg_ref[...] = jnp.zeros_like(dwg_ref)
            dwp_ref[...] = jnp.zeros_like(dwp_ref)
            dg3_ref[...] = jnp.zeros_like(dg3_ref)

        g3 = g3_ref[...]
        dh3 = dh3_ref[...]
        hn3, xhat, r = _rms_fwd(h2_ref[...], g3)
        hn16 = hn3.astype(bf16)
        gate = _sigmoid(_dot(hn16, wg_ref[...]))
        p16 = p_ref[0].astype(bf16)
        pp = _dot(p16, wp_ref[...])
        d_pp = (dh3 * gate).astype(bf16)
        d_pre = (dh3 * pp * gate * (1.0 - gate)).astype(bf16)
        dwp_ref[...] += _dot_tn(p16, d_pp)
        dwg_ref[...] += _dot_tn(hn16, d_pre)
        dx, dg = _rms_bwd(_dot_nt(d_pre, wg_ref[...]), xhat, r, g3)
        dg3_ref[...] += dg
        dh2_ref[...] = dh3 + dx

    row = lambda w: pl.BlockSpec((t, w), lambda i: (i, 0))
    fixed = lambda shape: pl.BlockSpec(shape, lambda i: (0,) * len(shape))
    outs = pl.pallas_call(
        body, name="ple_bwd_exchange" if nx else "ple_bwd", grid=(n,),
        out_shape=(jax.ShapeDtypeStruct((s, D_MODEL), f32), jax.ShapeDtypeStruct((D_MODEL, D_MODEL), f32),
                   jax.ShapeDtypeStruct((D_PLE, D_MODEL), f32), jax.ShapeDtypeStruct((1, D_MODEL), f32))
        + tuple(jax.ShapeDtypeStruct((N_CHIPS, g.shape[1] // 2, g.shape[2]), g.dtype) for g in exchange),
        in_specs=[row(D_MODEL), row(D_MODEL), pl.BlockSpec((1, t, D_PLE), lambda i: (p_l[1], i, 0)), _const_spec((1, D_MODEL)),
                  _const_spec((D_MODEL, D_MODEL)), _const_spec((D_PLE, D_MODEL))] + [_hbm()] * nx,
        out_specs=(row(D_MODEL), fixed((D_MODEL, D_MODEL)), fixed((D_PLE, D_MODEL)), fixed((1, D_MODEL))) + (_hbm(),) * nx,
        scratch_shapes=[pltpu.SemaphoreType.DMA((nx,)), pltpu.SemaphoreType.DMA((nx,))] if nx else [],
        compiler_params=_cparams(),
    )(dh3, h2, p_l[0], g3, w_gate, w_proj, *exchange)
    return outs[:4] + (list(outs[4:]),)


def _ffn_bwd(dh2, h1, up16, g2, wup_h, cw_h, cb_h, wdn_h, exchange=()):
    s = h1.shape[0]
    t = min(FFN_BWD_TILE, s)
    n = s // t
    hb = t // HALO
    last_hb = s // HALO - 1
    nx = len(exchange)

    def body(*refs):
        dh2_ref, dh2n_ref, h1_ref, up_ref, upp_ref, upn_ref, g2_ref, wup_ref, cw_ref, cb_ref, wdn_ref = refs[:11]
        p_refs = refs[11:11 + nx]
        part_ref, dwup_ref, dwdn_ref, dcw_ref, dcb_ref = refs[11 + nx:16 + nx]
        land_refs = refs[16 + nx:16 + 2 * nx]
        up_scr, dcv_scr, acc_up, acc_dn = refs[16 + 2 * nx:20 + 2 * nx]
        sems = refs[20 + 2 * nx:]
        j = pl.program_id(0)
        i = pl.program_id(1)
        if nx:
            @pl.when(jnp.logical_and(j == 0, i == 0))
            def _():
                _cx_start(p_refs, land_refs, *sems)

            @pl.when(jnp.logical_and(j == 1, i == n - 1))
            def _():
                _cx_finish(p_refs, land_refs, *sems)

        @pl.when(i == 0)
        def _():
            acc_up[...] = jnp.zeros_like(acc_up)
            acc_dn[...] = jnp.zeros_like(acc_dn)
            dcw_ref[...] = jnp.zeros_like(dcw_ref)
            dcb_ref[...] = jnp.zeros_like(dcb_ref)

        hn2, _, _ = _rms_fwd(h1_ref[...], g2_ref[...])
        hn16 = hn2.astype(bf16)
        up_scr[0:HALO, :] = jnp.where(i > 0, upp_ref[0].astype(f32), 0.0)
        up_scr[HALO:HALO + t, :] = up_ref[0].astype(f32)
        up_scr[HALO + t:, :] = upn_ref[0].astype(f32)
        cw = cw_ref[0]
        rows = t + HALO
        cv = cb_ref[0] + cw[0:1] * up_scr[pl.ds(HALO - 2, rows), :]
        for k in range(1, FFN_CONV):
            cv = cv + cw[k:k + 1] * up_scr[pl.ds(HALO - 2 + k, rows), :]
        dh2 = dh2_ref[...]
        dh2n = jnp.where(i < n - 1, dh2n_ref[...], 0.0)
        dh2_ext16 = jnp.concatenate([dh2, dh2n], axis=0).astype(bf16)
        d_act = _dot_nt(dh2_ext16, wdn_ref[0])
        gate, val = cv[:, :FF_HALF], cv[:, FF_HALF:]
        gl, dgl = _gelu_and_grad(gate)
        dcv_scr[:, :FF_HALF] = d_act * val * dgl
        dcv_scr[:, FF_HALF:] = d_act * gl
        act16 = (gl[:t] * val[:t]).astype(bf16)
        acc_dn[...] += _dot_tn(act16, dh2_ext16[:t])
        d_cv = dcv_scr[pl.ds(0, t), :]
        dcb_ref[0] += jnp.sum(d_cv, axis=0, keepdims=True)
        d_up = cw[2:3] * d_cv
        dcw_ref[0, 2:3, :] += jnp.sum(d_cv * up_scr[pl.ds(HALO, t), :], axis=0, keepdims=True)
        for k in range(FFN_CONV - 1):
            dcw_ref[0, k:k + 1, :] += jnp.sum(d_cv * up_scr[pl.ds(HALO - 2 + k, t), :], axis=0, keepdims=True)
            d_up = d_up + cw[k:k + 1] * dcv_scr[pl.ds(2 - k, t), :]
        d_up16 = d_up.astype(bf16)
        part_ref[0] = _dot_nt(d_up16, wup_ref[0])
        acc_up[...] += _dot_tn(hn16, d_up16)

        @pl.when(i == n - 1)
        def _():
            pltpu.sync_copy(acc_up.at[:, pl.ds(0, FF_HALF)], dwup_ref.at[j])
            pltpu.sync_copy(acc_up.at[:, pl.ds(FF_HALF, FF_HALF)], dwup_ref.at[2 + j])
            pltpu.sync_copy(acc_dn, dwdn_ref.at[j])

    outs = pl.pallas_call(
        body, name="ffn_bwd_exchange" if nx else "ffn_bwd", grid=(2, n),
        out_shape=(jax.ShapeDtypeStruct((2, s, D_MODEL), f32), jax.ShapeDtypeStruct((N_CHIPS, D_MODEL, FF_HALF), f32),
                   jax.ShapeDtypeStruct((2, FF_HALF, D_MODEL), f32), jax.ShapeDtypeStruct((2, FFN_CONV, D_FF), f32),
                   jax.ShapeDtypeStruct((2, 1, D_FF), f32))
        + tuple(jax.ShapeDtypeStruct((3,) + a.shape[1:], a.dtype) for a in exchange),
        in_specs=[pl.BlockSpec((t, D_MODEL), lambda j, i: (i, 0)),
                  pl.BlockSpec((HALO, D_MODEL), lambda j, i: (jnp.minimum((i + 1) * hb, last_hb), 0)),
                  pl.BlockSpec((t, D_MODEL), lambda j, i: (i, 0)),
                  pl.BlockSpec((1, t, D_FF), lambda j, i: (j, i, 0)),
                  pl.BlockSpec((1, HALO, D_FF), lambda j, i: (j, jnp.maximum(i * hb - 1, 0), 0)),
                  pl.BlockSpec((1, HALO, D_FF), lambda j, i: (j, jnp.minimum((i + 1) * hb, last_hb), 0)),
                  _const_spec((1, D_MODEL)),
                  pl.BlockSpec((1, D_MODEL, D_FF), lambda j, i: (j, 0, 0), pipeline_mode=pl.Buffered(1)),
                  pl.BlockSpec((1, FFN_CONV, D_FF), lambda j, i: (j, 0, 0)),
                  pl.BlockSpec((1, 1, D_FF), lambda j, i: (j, 0, 0)),
                  pl.BlockSpec((1, FF_HALF, D_MODEL), lambda j, i: (j, 0, 0), pipeline_mode=pl.Buffered(1))] + [_hbm()] * nx,
        out_specs=(pl.BlockSpec((1, t, D_MODEL), lambda j, i: (j, i, 0)), pl.BlockSpec(memory_space=pl.ANY),
                   pl.BlockSpec(memory_space=pl.ANY), pl.BlockSpec((1, FFN_CONV, D_FF), lambda j, i: (j, 0, 0)),
                   pl.BlockSpec((1, 1, D_FF), lambda j, i: (j, 0, 0))) + (_hbm(),) * nx,
        scratch_shapes=[pltpu.VMEM((2 * HALO + t, D_FF), f32), pltpu.VMEM((HALO + t, D_FF), f32),
                        pltpu.VMEM((D_MODEL, D_FF), f32), pltpu.VMEM((FF_HALF, D_MODEL), f32)]
        + ([pltpu.SemaphoreType.DMA((3 * nx,)), pltpu.SemaphoreType.DMA((3 * nx,))] if nx else []),
        compiler_params=_cparams(("arbitrary", "arbitrary")),
    )(dh2, dh2, h1, up16, up16, up16, g2, wup_h, cw_h, cb_h, wdn_h, *exchange)
    return outs[:5] + (list(outs[5:]),)


def _out_bwd(dh2, parts, h1, ymix, g2, w_out, share=()):
    s = h1.shape[0]
    t = _row_tile(s, WIDE_ROW_TILE)
    n = s // t
    ns = len(share)

    def body(*refs):
        dh2_ref, part_ref, h1_ref, ym_ref, g2_ref, wo_ref = refs[:6]
        r_refs = refs[6:6 + ns]
        dh1_ref, dym_ref, dwo_ref, dg2_ref = refs[6 + ns:10 + ns]
        both_refs = refs[10 + ns:10 + 2 * ns]
        sems = refs[10 + 2 * ns:]
        i = pl.program_id(0)
        if ns:
            @pl.when(i == 0)
            def _():
                _ps_start(r_refs, both_refs, *sems)

            @pl.when(i == n - 1)
            def _():
                _ps_finish(r_refs, both_refs, *sems)

        @pl.when(i == 0)
        def _():
            dwo_ref[...] = jnp.zeros_like(dwo_ref)
            dg2_ref[...] = jnp.zeros_like(dg2_ref)

        g2 = g2_ref[...]
        _, xhat, r = _rms_fwd(h1_ref[...], g2)
        dx, dg = _rms_bwd(part_ref[0] + part_ref[1], xhat, r, g2)
        dg2_ref[...] += dg
        dh1 = dh2_ref[...] + dx
        dh1_ref[...] = dh1
        dh16 = dh1.astype(bf16)
        dym_ref[...] = _dot_nt(dh16, wo_ref[...]).astype(bf16)
        dwo_ref[...] += _dot_tn(ym_ref[...], dh16)

    row = lambda w: pl.BlockSpec((t, w), lambda i: (i, 0))
    fixed = lambda shape: pl.BlockSpec(shape, lambda i: (0,) * len(shape))
    outs = pl.pallas_call(
        body, name="out_bwd_share" if ns else "out_bwd", grid=(n,),
        out_shape=(jax.ShapeDtypeStruct((s, D_MODEL), f32), jax.ShapeDtypeStruct((s, D_MODEL), bf16),
                   jax.ShapeDtypeStruct((D_MODEL, D_MODEL), f32), jax.ShapeDtypeStruct((1, D_MODEL), f32))
        + tuple(jax.ShapeDtypeStruct((2 * r.shape[0], r.shape[1]), r.dtype) for r in share),
        in_specs=[row(D_MODEL), pl.BlockSpec((2, t, D_MODEL), lambda i: (0, i, 0)), row(D_MODEL), row(D_MODEL),
                  _const_spec((1, D_MODEL)), _const_spec((D_MODEL, D_MODEL))] + [_hbm()] * ns,
        out_specs=(row(D_MODEL), row(D_MODEL), fixed((D_MODEL, D_MODEL)), fixed((1, D_MODEL))) + (_hbm(),) * ns,
        scratch_shapes=[pltpu.SemaphoreType.DMA((ns,)), pltpu.SemaphoreType.DMA((ns,))] if ns else [],
        compiler_params=_cparams(),
    )(dh2, parts, h1, ymix, g2, w_out, *share)
    return outs[:4] + (_own_half_in(outs[4:], share),)


def _loss_head(h, target, gf):
    s = h.shape[0]
    t = _row_tile(s, WIDE_ROW_TILE)

    def body(h_ref, t_ref, g_ref, dh_ref, dg_ref, loss_ref):
        i = pl.program_id(0)

        @pl.when(i == 0)
        def _():
            dg_ref[...] = jnp.zeros_like(dg_ref)
            loss_ref[...] = jnp.zeros_like(loss_ref)

        g = g_ref[...]
        y, xhat, r = _rms_fwd(h_ref[...], g)
        diff = y - t_ref[...]
        per_row = jnp.mean(diff * diff, axis=-1, keepdims=True)
        loss_ref[...] += 0.5 * jnp.sum(per_row, axis=0, keepdims=True)
        dx, dg = _rms_bwd(diff * (1.0 / D_MODEL), xhat, r, g)
        dg_ref[...] += dg
        dh_ref[...] = dx

    row = pl.BlockSpec((t, D_MODEL), lambda i: (i, 0))
    return pl.pallas_call(
        body, name="loss_head", grid=(s // t,),
        out_shape=(jax.ShapeDtypeStruct((s, D_MODEL), f32), jax.ShapeDtypeStruct((1, D_MODEL), f32),
                   jax.ShapeDtypeStruct((1, LANES), f32)),
        in_specs=[row, row, _const_spec((1, D_MODEL))],
        out_specs=(row, pl.BlockSpec((1, D_MODEL), lambda i: (0, 0)), pl.BlockSpec((1, LANES), lambda i: (0, 0))),
        compiler_params=_cparams(),
    )(h, target, gf)


def _prep_layer(w):
    in4 = w["w_in4"]
    cut = D_IN // N_CHIPS
    tail = 3 * cut - (SSD_W + XBC)
    w_r = jnp.concatenate([in4[0], in4[1], in4[2][:, :cut - tail], in4[3][:, HEADS - tail:], in4[2][:, cut - tail:],
                           in4[3][:, :HEADS - tail], jnp.zeros((D_MODEL, LANES - HEADS), in4.dtype)], axis=1)
    pad8 = lambda v: jnp.concatenate([v, jnp.zeros((LANES - HEADS,), f32)]).reshape(1, LANES)
    halves = _halves
    ssd_prm = (w["ssd_conv_w"], w["ssd_conv_b"].reshape(1, XBC), pad8(w["ssd_dt_bias"]), pad8(w["ssd_a_log"]),
               jnp.repeat(w["ssd_d"], HEAD_DIM).reshape(1, SSD_W), w["ssd_norm_g"].reshape(1, SSD_W),
               w["pool_w"], w["pool_scale"].reshape(1, POOL_W))
    return dict(
        g1=w["mix_norm_g"].reshape(1, D_MODEL), w_r=w_r, ssd=ssd_prm, w_out=w["w_out"], g2=w["ffn_norm_g"].reshape(1, D_MODEL),
        wup_h=w["wup_h"], cw_h=halves(w["ffn_conv_w"]), cb_h=halves(w["ffn_conv_b"].reshape(1, 2 * D_FF)),
        wdn_h=w["wdn_h"], g3=w["ple_norm_g"].reshape(1, D_MODEL), w_gate=w["ple_w_gate"], w_proj=w["ple_w_proj"])


def _halves(a):
    return jnp.stack([jnp.concatenate([a[..., j * FF_HALF:(j + 1) * FF_HALF],
                                       a[..., D_FF + j * FF_HALF:D_FF + (j + 1) * FF_HALF]], axis=-1) for j in range(2)])


def _unhalve(a):
    return jnp.concatenate([a[0][..., :FF_HALF], a[1][..., :FF_HALF], a[0][..., FF_HALF:], a[1][..., FF_HALF:]], axis=-1)


def _layer_fwd(h, p_l, q, gather=()):
    zxu, dtr = _mix_in_fwd(h, q["g1"], q["w_r"])
    ymix, states = _ssd_pool_fwd(zxu, dtr, q["ssd"])
    h1, h2, h3, up16, gathered = _out_ffn_ple_fwd(h, ymix, p_l, q["w_out"], q["g2"], q["wup_h"], q["cw_h"], q["cb_h"],
                                                   q["wdn_h"], q["g3"], q["w_gate"], q["w_proj"], gather)
    return h3, (h, zxu, dtr, ymix, states, h1, h2, up16), gathered


def _layer_bwd(dh, saved, p_l, q, reduce=(), early=False):
    h0, zxu, dtr, ymix, states, h1, h2, up16 = saved
    dh2, d_wg, d_wp, d_g3, lands = _ple_bwd(dh, h2, p_l, q["g3"], q["w_gate"], q["w_proj"], reduce)
    sums = _rs_pair_sums(reduce, lands) if reduce else ()
    parts, d_wup, d_wdn, d_cw, d_cb, lands = _ffn_bwd(dh2, h1, up16, q["g2"], q["wup_h"], q["cw_h"], q["cb_h"], q["wdn_h"],
                                                      sums)
    halves = _rs_chip_sums(sums, lands) if reduce else ()
    dh1, d_ymix, d_wo, d_g2, reduced = _out_bwd(dh2, parts, h1, ymix, q["g2"], q["w_out"], halves)
    early_sums = ()
    if early:
        own = [d_wup, d_wdn.reshape(N_CHIPS, D_FF // N_CHIPS, D_MODEL), _chip_major(d_wg, 0), _chip_major(d_wp, 1)]
        early_sums = _rs_pair_sums(own, _rs_pair_exchange(own))
    (d_zxu, d_dtr, d_scw, d_scb, d_dtb, d_alog, d_dsk, d_ng, d_pw, d_ps, early_lands) = _ssd_pool_bwd(
        d_ymix, zxu, dtr, states, q["ssd"], early_sums)
    dh, d_wr, d_g1 = _mix_in_bwd(d_zxu, d_dtr, h0, dh1, q["g1"], q["w_r"])
    cut = D_IN // N_CHIPS
    tail = 3 * cut - (SSD_W + XBC)
    w_in4 = jnp.stack([d_wr[:, :cut], d_wr[:, cut:2 * cut],
                       jnp.concatenate([d_wr[:, 2 * cut:SSD_W + XBC], d_wr[:, 2048:2048 + tail]], axis=1),
                       jnp.concatenate([d_wr[:, 2048 + tail:2048 + HEADS], d_wr[:, SSD_W + XBC:2048]], axis=1)])
    grads = dict(
        mix_norm_g=d_g1.reshape(D_MODEL), w_in4=w_in4,
        ssd_conv_w=d_scw, ssd_conv_b=d_scb.reshape(XBC), ssd_dt_bias=d_dtb[0, :HEADS], ssd_a_log=d_alog[0, :HEADS],
        ssd_d=jnp.sum(d_dsk.reshape(HEADS, HEAD_DIM), axis=1), ssd_norm_g=d_ng.reshape(SSD_W), pool_w=d_pw,
        pool_scale=d_ps.reshape(POOL_W), w_out=d_wo, ffn_norm_g=d_g2.reshape(D_MODEL), ffn_w_up4=d_wup,
        ffn_conv_w=_unhalve(d_cw), ffn_conv_b=_unhalve(d_cb).reshape(2 * D_FF), wdn_h=d_wdn,
        ple_norm_g=d_g3.reshape(D_MODEL), ple_w_gate=d_wg, ple_w_proj=d_wp)
    if early:
        grads["early"] = (early_sums, early_lands)
    return dh, grads, reduced


MESH = pl.DeviceIdType.MESH
COLS = 1024
N_CHIPS = 4
WEIGHT_NAMES = ["mix_norm_g", "w_in", "ssd_conv_w", "ssd_conv_b", "ssd_dt_bias", "ssd_a_log", "ssd_d", "ssd_norm_g", "pool_w",
                "pool_scale", "w_out", "ffn_norm_g", "ffn_w_up", "ffn_conv_w", "ffn_conv_b", "ffn_w_down", "ple_norm_g",
                "ple_w_gate", "ple_w_proj", "final_norm_g"]
SHARD_AXIS = {"w_in": 2, "ssd_conv_w": 2, "w_out": 1, "ffn_w_up": 2, "ffn_conv_w": 2, "ffn_w_down": 1, "ple_w_gate": 1,
              "ple_w_proj": 2}
MATMUL_SHARDED = ["w_in", "w_out", "ffn_w_up", "ffn_w_down", "ple_w_gate", "ple_w_proj"]
F32_SHARDED = ["ssd_conv_w", "ffn_conv_w"]
REPLICATED = [n for n in WEIGHT_NAMES if n not in SHARD_AXIS]


def _mesh_pos():
    return lax.axis_index("x"), lax.axis_index("y"), lax.axis_index("c")


def _hbm():
    return pl.BlockSpec(memory_space=pl.ANY)


def _all_gather_chips(arrays, name):
    n = len(arrays)

    def body(*refs):
        x_refs, out_refs, sems = refs[:n], refs[n:2 * n], refs[2 * n:]
        _ag_start(x_refs, out_refs, *sems)
        _ag_finish(x_refs, out_refs, *sems)

    gathered = pl.pallas_call(
        body, name=name, out_shape=[jax.ShapeDtypeStruct((N_CHIPS,) + a.shape, a.dtype) for a in arrays],
        in_specs=[_hbm()] * n, out_specs=[_hbm()] * n,
        scratch_shapes=[pltpu.SemaphoreType.DMA((6 * n,)), pltpu.SemaphoreType.DMA((6 * n,))],
    )(*arrays)
    return _own_block_in(gathered, arrays)


def _ag_copies(x_refs, out_refs, send_sems, recv_sems):
    x, y, c = _mesh_pos()
    me = 2 * x + y
    flips = [(1 - x, y), (x, 1 - y), (1 - x, 1 - y)]

    def rows(a, chip, half):
        hr = x_refs[a].shape[0] // 2
        return out_refs[a].at[chip, pl.ds(half * hr, hr), :]

    def copy(k, src, dst, to):
        return pltpu.make_async_remote_copy(src_ref=src, dst_ref=dst, send_sem=send_sems.at[k], recv_sem=recv_sems.at[k],
                                            device_id=to, device_id_type=MESH)

    def first(a, j):
        hr = x_refs[a].shape[0] // 2
        return copy(6 * a + j, x_refs[a].at[pl.ds(c * hr, hr), :], rows(a, me, c), flips[j] + (c,))

    def landed(a, j):
        blk = rows(a, 2 * flips[j][0] + flips[j][1], c)
        return copy(6 * a + j, blk, blk, flips[j] + (c,))

    def passed(a, j, half):
        blk = rows(a, 2 * flips[j][0] + flips[j][1], half)
        return copy(6 * a + 3 + j, blk, blk, (x, y, 1 - c))

    return first, landed, passed, c


def _ag_start(x_refs, out_refs, send_sems, recv_sems):
    first, _, _, _ = _ag_copies(x_refs, out_refs, send_sems, recv_sems)
    for a in range(len(x_refs)):
        for j in range(3):
            first(a, j).start()


def _ag_finish(x_refs, out_refs, send_sems, recv_sems):
    _ag_forward(x_refs, out_refs, send_sems, recv_sems)
    _ag_drain(x_refs, out_refs, send_sems, recv_sems)


def _ag_forward(x_refs, out_refs, send_sems, recv_sems):
    _, landed, passed, c = _ag_copies(x_refs, out_refs, send_sems, recv_sems)
    for j in range(3):
        for a in range(len(x_refs)):
            landed(a, j).wait_recv()
            passed(a, j, c).start()


def _ag_drain(x_refs, out_refs, send_sems, recv_sems):
    first, _, passed, c = _ag_copies(x_refs, out_refs, send_sems, recv_sems)
    n = len(x_refs)
    for j in range(3):
        for a in range(n):
            passed(a, j, 1 - c).wait_recv()
    for a in range(n):
        for j in range(3):
            first(a, j).wait_send()
            passed(a, j, c).wait_send()


def _own_block_in(gathered, arrays):
    if not arrays:
        return []
    me = 2 * lax.axis_index("x") + lax.axis_index("y")
    return [lax.dynamic_update_slice(o, a[None], (me, 0, 0)) for o, a in zip(gathered, arrays)]


def _rs_pair_exchange(gs):
    n = len(gs)

    def body(*refs):
        g_refs, land_refs, sems = refs[:n], refs[n:2 * n], refs[2 * n:]
        for cp in _px_copies(g_refs, land_refs, *sems):
            cp.start()
        for cp in _px_copies(g_refs, land_refs, *sems):
            cp.wait()

    return pl.pallas_call(
        body, name="rs_pair_exchange",
        out_shape=[jax.ShapeDtypeStruct((N_CHIPS, g.shape[1] // 2, g.shape[2]), g.dtype) for g in gs],
        in_specs=[_hbm()] * n, out_specs=[_hbm()] * n,
        scratch_shapes=[pltpu.SemaphoreType.DMA((n,)), pltpu.SemaphoreType.DMA((n,))],
    )(*gs)


def _px_copies(g_refs, land_refs, send_sems, recv_sems):
    x, y, c = _mesh_pos()
    cps = []
    for a in range(len(g_refs)):
        hr = g_refs[a].shape[1] // 2
        cps.append(pltpu.make_async_remote_copy(
            src_ref=g_refs[a].at[:, pl.ds((1 - c) * hr, hr), :], dst_ref=land_refs[a], send_sem=send_sems.at[a],
            recv_sem=recv_sems.at[a], device_id=(x, y, 1 - c), device_id_type=MESH))
    return cps


def _rs_pair_add(gs, lands, c_idx):
    n = len(gs)
    shapes = [(g.shape[1] // 2, g.shape[2]) for g in gs]

    def body(c_ref, *refs):
        for g_ref, l_ref, o_ref in zip(refs[:n], refs[n:2 * n], refs[2 * n:]):
            o_ref[...] = (g_ref[...] + l_ref[...]).astype(bf16)

    return pl.pallas_call(
        body, name="rs_pair_add", out_shape=[jax.ShapeDtypeStruct((N_CHIPS, hr, cols), bf16) for hr, cols in shapes],
        grid_spec=pltpu.PrefetchScalarGridSpec(
            num_scalar_prefetch=1, grid=(N_CHIPS,),
            in_specs=[pl.BlockSpec((1, hr, cols), lambda k, c_ref: (k, c_ref[0], 0)) for hr, cols in shapes]
            + [pl.BlockSpec((1, hr, cols), lambda k, c_ref: (k, 0, 0)) for hr, cols in shapes],
            out_specs=[pl.BlockSpec((1, hr, cols), lambda k, c_ref: (k, 0, 0)) for hr, cols in shapes]),
        compiler_params=_cparams(),
    )(c_idx, *gs, *lands)


def _rs_chip_exchange(parts):
    n = len(parts)

    def body(*refs):
        p_refs, land_refs, sems = refs[:n], refs[n:2 * n], refs[2 * n:]
        _cx_start(p_refs, land_refs, *sems)
        _cx_finish(p_refs, land_refs, *sems)

    return pl.pallas_call(
        body, name="rs_chip_exchange", out_shape=[jax.ShapeDtypeStruct((3,) + p.shape[1:], p.dtype) for p in parts],
        in_specs=[_hbm()] * n, out_specs=[_hbm()] * n,
        scratch_shapes=[pltpu.SemaphoreType.DMA((3 * n,)), pltpu.SemaphoreType.DMA((3 * n,))],
    )(*parts)


def _cx_copies(p_refs, land_refs, send_sems, recv_sems):
    x, y, c = _mesh_pos()
    flips = [(1 - x, y), (x, 1 - y), (1 - x, 1 - y)]
    return [pltpu.make_async_remote_copy(src_ref=p_refs[a].at[2 * fx + fy], dst_ref=land_refs[a].at[j],
                                         send_sem=send_sems.at[3 * a + j], recv_sem=recv_sems.at[3 * a + j],
                                         device_id=(fx, fy, c), device_id_type=MESH)
            for a in range(len(p_refs)) for j, (fx, fy) in enumerate(flips)]


def _cx_start(p_refs, land_refs, send_sems, recv_sems):
    for cp in _cx_copies(p_refs, land_refs, send_sems, recv_sems):
        cp.start()


def _cx_finish(p_refs, land_refs, send_sems, recv_sems):
    for cp in _cx_copies(p_refs, land_refs, send_sems, recv_sems):
        cp.wait()


def _rs_chip_add(parts, lands, me_idx):
    n = len(parts)
    shapes = [p.shape[1:] for p in parts]

    def body(me_ref, *refs):
        for p_ref, l_ref, o_ref in zip(refs[:n], refs[n:2 * n], refs[2 * n:]):
            o_ref[...] = ((p_ref[0].astype(f32) + l_ref[0].astype(f32)) + l_ref[1].astype(f32)) + l_ref[2].astype(f32)

    return pl.pallas_call(
        body, name="rs_chip_add", out_shape=[jax.ShapeDtypeStruct((hr, cols), f32) for hr, cols in shapes],
        grid_spec=pltpu.PrefetchScalarGridSpec(
            num_scalar_prefetch=1, grid=(1,),
            in_specs=[pl.BlockSpec((1, hr, cols), lambda i, me_ref: (me_ref[0], 0, 0)) for hr, cols in shapes]
            + [pl.BlockSpec((3, hr, cols), lambda i, me_ref: (0, 0, 0)) for hr, cols in shapes],
            out_specs=[pl.BlockSpec((hr, cols), lambda i, me_ref: (0, 0)) for hr, cols in shapes]),
        compiler_params=_cparams(),
    )(me_idx, *parts, *lands)


def _rs_pair_share(reds):
    n = len(reds)

    def body(*refs):
        r_refs, out_refs, sems = refs[:n], refs[n:2 * n], refs[2 * n:]
        _ps_start(r_refs, out_refs, *sems)
        _ps_finish(r_refs, out_refs, *sems)

    both = pl.pallas_call(
        body, name="rs_pair_share", out_shape=[jax.ShapeDtypeStruct((2 * r.shape[0], r.shape[1]), r.dtype) for r in reds],
        in_specs=[_hbm()] * n, out_specs=[_hbm()] * n,
        scratch_shapes=[pltpu.SemaphoreType.DMA((n,)), pltpu.SemaphoreType.DMA((n,))],
    )(*reds)
    return _own_half_in(both, reds)


def _ps_copy(r_refs, out_refs, send_sems, recv_sems, a, half):
    x, y, c = _mesh_pos()
    hr = r_refs[a].shape[0]
    return pltpu.make_async_remote_copy(src_ref=r_refs[a], dst_ref=out_refs[a].at[pl.ds(half * hr, hr), :],
                                        send_sem=send_sems.at[a], recv_sem=recv_sems.at[a], device_id=(x, y, 1 - c),
                                        device_id_type=MESH)


def _ps_start(r_refs, out_refs, send_sems, recv_sems):
    c = lax.axis_index("c")
    for a in range(len(r_refs)):
        _ps_copy(r_refs, out_refs, send_sems, recv_sems, a, c).start()


def _ps_finish(r_refs, out_refs, send_sems, recv_sems):
    c = lax.axis_index("c")
    for a in range(len(r_refs)):
        _ps_copy(r_refs, out_refs, send_sems, recv_sems, a, 1 - c).wait_recv()
    for a in range(len(r_refs)):
        _ps_copy(r_refs, out_refs, send_sems, recv_sems, a, c).wait_send()


def _own_half_in(both, reds):
    if not reds:
        return []
    c = lax.axis_index("c")
    return [lax.dynamic_update_slice(o, r, (c * r.shape[0], 0)) for o, r in zip(both, reds)]


def _rs_pair_sums(gs, lands):
    c_idx = jnp.reshape(lax.axis_index("c"), (1,)).astype(jnp.int32)
    return list(_rs_pair_add(list(gs), list(lands), c_idx))


def _rs_chip_sums(parts, lands):
    me_idx = jnp.reshape(2 * lax.axis_index("x") + lax.axis_index("y"), (1,)).astype(jnp.int32)
    return list(_rs_chip_add(list(parts), list(lands), me_idx))


def _pack(parts, row_multiple):
    flat = jnp.concatenate([a.reshape(-1) for a in parts])
    n = flat.shape[0]
    rows = -(-n // COLS)
    rows = -(-rows // row_multiple) * row_multiple
    return jnp.pad(flat, (0, rows * COLS - n)).reshape(rows, COLS)


def _unpack(flat, shapes):
    out, off = [], 0
    for shp in shapes:
        n = math.prod(shp)
        out.append(flat[off:off + n].reshape(shp))
        off += n
    return out


def _adamw(w, g, m, v, name):
    shape = w.shape
    cols = shape[-1]
    rows = math.prod(shape[:-1]) if len(shape) > 1 else 1
    tr = rows
    if rows > 512:
        tr = next(t for t in (512, 256, 128, 64, 32, 16, 8) if rows % t == 0)
    two_d = lambda a: a.reshape(rows, cols)

    def body(w_ref, g_ref, m_ref, v_ref, d_ref, nm_ref, nv_ref):
        gg = g_ref[...]
        nm = ADAM_B1 * m_ref[...] + (1.0 - ADAM_B1) * gg
        nv = ADAM_B2 * v_ref[...] + (1.0 - ADAM_B2) * (gg * gg)
        m_hat = nm / (1.0 - ADAM_B1 ** ADAM_STEP)
        v_hat = nv / (1.0 - ADAM_B2 ** ADAM_STEP)
        d_ref[...] = -ADAM_LR * (m_hat / (jnp.sqrt(v_hat) + ADAM_EPS) + ADAM_WD * w_ref[...])
        nm_ref[...] = nm
        nv_ref[...] = nv

    spec = pl.BlockSpec((tr, cols), lambda i: (i, 0))
    outs = pl.pallas_call(
        body, name="adamw_" + name, grid=(rows // tr,),
        out_shape=tuple(jax.ShapeDtypeStruct((rows, cols), f32) for _ in range(3)),
        in_specs=[spec] * 4, out_specs=(spec,) * 3, compiler_params=_cparams(),
    )(two_d(w), two_d(g), two_d(m), two_d(v))
    return tuple(o.reshape(shape) for o in outs)


BIG_SHARDED = ["ffn_w_up", "ffn_w_down", "w_out", "ple_w_gate", "ple_w_proj", "w_in"]
SMALL_REDUCED = REPLICATED + F32_SHARDED


def _chip_major(a, axis):
    n = a.shape[axis] // N_CHIPS
    return jnp.moveaxis(a.reshape(a.shape[:axis] + (N_CHIPS, n) + a.shape[axis + 1:]), axis, 0)


def _train_step(x, p, loss_target, w, m, v):
    xs, ps, target = x[0], p[:, 0], loss_target[0]
    rows2d = lambda a: a.reshape(-1, a.shape[-1])
    conv_rows = 16
    pad_rows = lambda a: jnp.pad(a, ((0, conv_rows - a.shape[0]), (0, 0)))
    shards = [[w[n][l].astype(bf16) for n in MATMUL_SHARDED] for l in range(DEPTH)]
    first = _all_gather_chips(shards[0] + [pad_rows(rows2d(w[n])) for n in F32_SHARDED], "gather_weights")
    conv4 = dict(zip(F32_SHARDED, first[len(MATMUL_SHARDED):]))

    def col_cut(blk):
        return jnp.moveaxis(blk, 0, 1).reshape(blk.shape[1], N_CHIPS * blk.shape[2])

    def layer_weights(l, gathered):
        g = dict(zip(MATMUL_SHARDED, gathered))
        lw = {n: w[n][l] for n in REPLICATED if n != "final_norm_g"}
        lw["pool_w"] = lw["pool_w"].astype(bf16)
        lw.update(
            w_in4=g["w_in"], ple_w_proj=col_cut(g["ple_w_proj"]), w_out=g["w_out"].reshape(D_MODEL, D_MODEL),
            ple_w_gate=g["ple_w_gate"].reshape(D_MODEL, D_MODEL), wdn_h=g["ffn_w_down"].reshape(2, FF_HALF, D_MODEL),
            wup_h=jnp.stack([jnp.concatenate([g["ffn_w_up"][j], g["ffn_w_up"][2 + j]], axis=1) for j in range(2)]),
            ssd_conv_w=col_cut(conv4["ssd_conv_w"][:, l * SSD_CONV:(l + 1) * SSD_CONV, :]),
            ffn_conv_w=col_cut(conv4["ffn_conv_w"][:, l * FFN_CONV:(l + 1) * FFN_CONV, :]))
        return _prep_layer(lw)

    preps, saved = [], []
    h, gathered = xs, first[:len(MATMUL_SHARDED)]
    for l in range(DEPTH):
        preps.append(layer_weights(l, gathered))
        h, keep, gathered = _layer_fwd(h, (ps, l), preps[l], gather=shards[l + 1] if l + 1 < DEPTH else ())
        saved.append(keep)
    grad_x, d_gf, loss_part = _loss_head(h, target, w["final_norm_g"].reshape(1, D_MODEL))

    def chip_major_grads(g):
        return [g["ffn_w_up4"], g["wdn_h"].reshape(N_CHIPS, D_FF // N_CHIPS, D_MODEL), _chip_major(g["w_out"], 0),
                _chip_major(g["ple_w_gate"], 0), _chip_major(g["ple_w_proj"], 1), g["w_in4"]]

    grads, reduced, pending = [None] * DEPTH, [None] * DEPTH, ()
    for l in reversed(range(DEPTH)):
        grad_x, grads[l], done = _layer_bwd(grad_x, saved[l], (ps, l), preps[l], reduce=pending, early=(l == 0))
        if pending:
            reduced[l + 1] = done
        pending = chip_major_grads(grads[l])
    small = _pack([d_gf.reshape(D_MODEL) if n == "final_norm_g" else jnp.stack([grads[k][n] for k in range(DEPTH)])
                   for n in SMALL_REDUCED], 32 * N_CHIPS)
    early_sums, early_lands = grads[0]["early"]
    late = [pending[2], pending[5], small.reshape(N_CHIPS, -1, COLS)]
    late_sums = _rs_pair_sums(late, _rs_pair_exchange(late))
    up_r, dn_r, gate_r, proj_r, out_r, in_r, small_r = _rs_pair_share(
        _rs_chip_sums(list(early_sums) + late_sums, list(early_lands) + list(_rs_chip_exchange(late_sums))))
    reduced[0] = [up_r, dn_r, out_r, gate_r, proj_r, in_r, small_r]
    grad = {n: jnp.stack([reduced[l][i] for l in range(DEPTH)]) for i, n in enumerate(BIG_SHARDED)}
    small_all = _all_gather_chips([reduced[0][-1]], "gather_small_grads")[0].reshape(-1)
    small_shapes = [w[n].shape for n in REPLICATED] + [(DEPTH, SSD_CONV, XBC), (DEPTH, FFN_CONV, 2 * D_FF)]
    grad.update(zip(SMALL_REDUCED, _unpack(small_all, small_shapes)))
    me = 2 * lax.axis_index("x") + lax.axis_index("y")
    for n in F32_SHARDED:
        grad[n] = lax.dynamic_slice_in_dim(grad[n], me * w[n].shape[2], w[n].shape[2], axis=2)

    loss = lax.psum(loss_part[0, 0], ("x", "y", "c"))
    delta, new_m, new_v = {}, {}, {}
    for n in WEIGHT_NAMES:
        delta[n], new_m[n], new_v[n] = _adamw(w[n], grad[n], m[n], v[n], n)
    return (loss, grad_x[None], *[grad[n] for n in WEIGHT_NAMES], *[delta[n] for n in WEIGHT_NAMES],
            *[new_m[n] for n in WEIGHT_NAMES], *[new_v[n] for n in WEIGHT_NAMES])


def kernel(x, p, mix_norm_g, w_in, ssd_conv_w, ssd_conv_b, ssd_dt_bias, ssd_a_log, ssd_d, ssd_norm_g, pool_w, pool_scale, w_out, ffn_norm_g, ffn_w_up, ffn_conv_w, ffn_conv_b, ffn_w_down, ple_norm_g, ple_w_gate, ple_w_proj, final_norm_g, loss_target, m_mix_norm_g, m_w_in, m_ssd_conv_w, m_ssd_conv_b, m_ssd_dt_bias, m_ssd_a_log, m_ssd_d, m_ssd_norm_g, m_pool_w, m_pool_scale, m_w_out, m_ffn_norm_g, m_ffn_w_up, m_ffn_conv_w, m_ffn_conv_b, m_ffn_w_down, m_ple_norm_g, m_ple_w_gate, m_ple_w_proj, m_final_norm_g, v_mix_norm_g, v_w_in, v_ssd_conv_w, v_ssd_conv_b, v_ssd_dt_bias, v_ssd_a_log, v_ssd_d, v_ssd_norm_g, v_pool_w, v_pool_scale, v_w_out, v_ffn_norm_g, v_ffn_w_up, v_ffn_conv_w, v_ffn_conv_b, v_ffn_w_down, v_ple_norm_g, v_ple_w_gate, v_ple_w_proj, v_final_norm_g):
    w = dict(mix_norm_g=mix_norm_g, w_in=w_in, ssd_conv_w=ssd_conv_w, ssd_conv_b=ssd_conv_b, ssd_dt_bias=ssd_dt_bias, ssd_a_log=ssd_a_log, ssd_d=ssd_d, ssd_norm_g=ssd_norm_g, pool_w=pool_w, pool_scale=pool_scale, w_out=w_out, ffn_norm_g=ffn_norm_g, ffn_w_up=ffn_w_up, ffn_conv_w=ffn_conv_w, ffn_conv_b=ffn_conv_b, ffn_w_down=ffn_w_down, ple_norm_g=ple_norm_g, ple_w_gate=ple_w_gate, ple_w_proj=ple_w_proj, final_norm_g=final_norm_g)
    m = dict(mix_norm_g=m_mix_norm_g, w_in=m_w_in, ssd_conv_w=m_ssd_conv_w, ssd_conv_b=m_ssd_conv_b, ssd_dt_bias=m_ssd_dt_bias, ssd_a_log=m_ssd_a_log, ssd_d=m_ssd_d, ssd_norm_g=m_ssd_norm_g, pool_w=m_pool_w, pool_scale=m_pool_scale, w_out=m_w_out, ffn_norm_g=m_ffn_norm_g, ffn_w_up=m_ffn_w_up, ffn_conv_w=m_ffn_conv_w, ffn_conv_b=m_ffn_conv_b, ffn_w_down=m_ffn_w_down, ple_norm_g=m_ple_norm_g, ple_w_gate=m_ple_w_gate, ple_w_proj=m_ple_w_proj, final_norm_g=m_final_norm_g)
    v = dict(mix_norm_g=v_mix_norm_g, w_in=v_w_in, ssd_conv_w=v_ssd_conv_w, ssd_conv_b=v_ssd_conv_b, ssd_dt_bias=v_ssd_dt_bias, ssd_a_log=v_ssd_a_log, ssd_d=v_ssd_d, ssd_norm_g=v_ssd_norm_g, pool_w=v_pool_w, pool_scale=v_pool_scale, w_out=v_w_out, ffn_norm_g=v_ffn_norm_g, ffn_w_up=v_ffn_w_up, ffn_conv_w=v_ffn_conv_w, ffn_conv_b=v_ffn_conv_b, ffn_w_down=v_ffn_w_down, ple_norm_g=v_ple_norm_g, ple_w_gate=v_ple_w_gate, ple_w_proj=v_ple_w_proj, final_norm_g=v_final_norm_g)
    return _train_step(x, p, loss_target, w, m, v)
```
